```python
import math
import jax, jax.numpy as jnp
from jax import lax
import numpy as np

D_MODEL = 2048
BATCH = 8
SEQ = 2048
DEPTH = 2

CHUNK = 64
D_FF = 5632
CONV_K = 4
RMS_EPS = 1e-6

GDN_HEADS = 6
GDN_DK = 128
GDN_DV = 128
GDN_QK = GDN_HEADS * GDN_DK
GDN_V = GDN_HEADS * GDN_DV
GDN_QKV = 2 * GDN_QK + GDN_V
GDN_IN = GDN_QKV + GDN_V + 2 * GDN_HEADS

SSD_HEADS = 12
SSD_HEAD_DIM = 64
SSD_GROUPS = 2
SSD_STATE = 128
SSD_HPG = SSD_HEADS // SSD_GROUPS
SSD_INNER = SSD_HEADS * SSD_HEAD_DIM
SSD_BC = SSD_GROUPS * SSD_STATE
SSD_CONV_DIM = SSD_INNER + 2 * SSD_BC
SSD_IN = SSD_INNER + SSD_CONV_DIM + SSD_HEADS

S5_WIDTH = 512
S5_GROUP = 16
S5_GROUPS = S5_WIDTH // S5_GROUP
S5_STATE = 64

D_MIX = GDN_V + SSD_INNER + S5_WIDTH
P_IN = GDN_IN + SSD_IN + S5_WIDTH

kernel_name = "hybrid_gdn_ssd_s5_macaron"


def rms_norm(x, w):
    xf = x.astype(jnp.float32)
    y = xf * lax.rsqrt(jnp.mean(xf * xf, axis=-1, keepdims=True) + RMS_EPS)
    return (y * w.astype(jnp.float32)).astype(x.dtype)


def l2_norm(x):
    return x * lax.rsqrt(jnp.sum(x * x, axis=-1, keepdims=True) + RMS_EPS)


def causal_dwconv(x, w):
    k = w.shape[0]
    seqlen = x.shape[1]
    xp = jnp.pad(x, ((0, 0), (k - 1, 0), (0, 0)))
    out = xp[:, 0:seqlen] * w[0]
    for i in range(1, k):
        out = out + xp[:, i:i + seqlen] * w[i]
    return out


def swiglu(x, w_gate, w_up, w_down):
    return (jax.nn.silu(x @ w_gate) * (x @ w_up)) @ w_down


def causal_decay(G):
    mask = jnp.tril(jnp.ones((CHUNK, CHUNK), dtype=bool))
    diff = G[..., :, None] - G[..., None, :]
    return jnp.where(mask, jnp.exp(jnp.where(mask, diff, 0.0)), 0.0)


def gated_deltanet(proj, conv_w, a_log, dt_bias, norm_w):
    f32 = jnp.float32
    bsz, seqlen, _ = proj.shape
    n = seqlen // CHUNK
    qkv, z, b_raw, a_raw = jnp.split(proj, [GDN_QKV, GDN_QKV + GDN_V, GDN_QKV + GDN_V + GDN_HEADS], axis=-1)
    qkv = jax.nn.silu(causal_dwconv(qkv, conv_w)).astype(f32)
    q, k, v = jnp.split(qkv, [GDN_QK, 2 * GDN_QK], axis=-1)
    q = l2_norm(q.reshape(bsz, seqlen, GDN_HEADS, GDN_DK)) * (GDN_DK ** -0.5)
    k = l2_norm(k.reshape(bsz, seqlen, GDN_HEADS, GDN_DK))
    v = v.reshape(bsz, seqlen, GDN_HEADS, GDN_DV)
    beta = jax.nn.sigmoid(b_raw.astype(f32))
    g = -jnp.exp(a_log.astype(f32)) * jax.nn.softplus(a_raw.astype(f32) + dt_bias.astype(f32))

    def chunks4(t):
        return t.reshape(bsz, n, CHUNK, GDN_HEADS, -1).transpose(0, 3, 1, 2, 4)

    def chunks3(t):
        return t.reshape(bsz, n, CHUNK, GDN_HEADS).transpose(0, 3, 1, 2)

    q, k, v = chunks4(q), chunks4(k), chunks4(v)
    beta, G = chunks3(beta), jnp.cumsum(chunks3(g), axis=-1)
    decay = causal_decay(G)
    kb = k * beta[..., None]
    strict = jnp.tril(jnp.ones((CHUNK, CHUNK), dtype=bool), k=-1)
    lower = jnp.where(strict, jnp.einsum('bhnid,bhnjd->bhnij', kb, k) * decay, 0.0)
    eye = jnp.eye(CHUNK, dtype=f32)
    T = lax.linalg.triangular_solve(lower + eye, jnp.broadcast_to(eye, lower.shape),
                                    left_side=True, lower=True, unit_diagonal=True)
    u = T @ (v * beta[..., None])
    w = T @ (kb * jnp.exp(G)[..., None])
    attn = jnp.einsum('bhnid,bhnjd->bhnij', q, k) * decay
    q_dec = q * jnp.exp(G)[..., None]
    G_last = G[..., -1]
    k_dec = k * jnp.exp(G_last[..., None] - G)[..., None]

    def step(S, xs):
        u_c, w_c, attn_c, q_c, k_c, gl = xs
        v_new = u_c - jnp.einsum('bhid,bhde->bhie', w_c, S)
        o = jnp.einsum('bhid,bhde->bhie', q_c, S) + jnp.einsum('bhij,bhje->bhie', attn_c, v_new)
        S = S * jnp.exp(gl)[..., None, None] + jnp.einsum('bhid,bhie->bhde', k_c, v_new)
        return S, o

    xs = tuple(jnp.moveaxis(t, 2, 0) for t in (u, w, attn, q_dec, k_dec, G_last))
    S0 = jnp.zeros((bsz, GDN_HEADS, GDN_DK, GDN_DV), f32)
    _, o = lax.scan(step, S0, xs)
    o = o.transpose(1, 0, 3, 2, 4).reshape(bsz, seqlen, GDN_HEADS, GDN_DV)
    z = z.astype(f32).reshape(bsz, seqlen, GDN_HEADS, GDN_DV)
    o = rms_norm(o, norm_w) * jax.nn.silu(z)
    return o.reshape(bsz, seqlen, GDN_V)


def mamba2_ssd(proj, conv_w, conv_b, a_log, dt_bias, d_skip, norm_w):
    f32 = jnp.float32
    bsz, seqlen, _ = proj.shape
    n = seqlen // CHUNK
    z, xbc, dt_raw = jnp.split(proj, [SSD_INNER, SSD_INNER + SSD_CONV_DIM], axis=-1)
    xbc = jax.nn.silu(causal_dwconv(xbc, conv_w) + conv_b).astype(f32)
    xs, Bm, Cm = jnp.split(xbc, [SSD_INNER, SSD_INNER + SSD_BC], axis=-1)
    dt = jax.nn.softplus(dt_raw.astype(f32) + dt_bias.astype(f32))
    A = -jnp.exp(a_log.astype(f32)).reshape(SSD_GROUPS, SSD_HPG)
    x = xs.reshape(bsz, n, CHUNK, SSD_GROUPS, SSD_HPG, SSD_HEAD_DIM)
    dtc = dt.reshape(bsz, n, CHUNK, SSD_GROUPS, SSD_HPG)
    Bc = Bm.reshape(bsz, n, CHUNK, SSD_GROUPS, SSD_STATE)
    Cc = Cm.reshape(bsz, n, CHUNK, SSD_GROUPS, SSD_STATE)
    acum = jnp.cumsum((dtc * A).transpose(0, 1, 3, 4, 2), axis=-1)
    Lm = causal_decay(acum)
    xdt = x * dtc[..., None]
    CB = jnp.einsum('bnigk,bnjgk->bngij', Cc, Bc)
    y_diag = jnp.einsum('bnghij,bnjghp->bnighp', CB[:, :, :, None] * Lm, xdt)
    decay_states = jnp.exp(acum[..., -1:] - acum).transpose(0, 1, 4, 2, 3)
    states = jnp.einsum('bnjgk,bnjghp->bnghpk', Bc, xdt * decay_states[..., None])
    chunk_decay = jnp.exp(acum[..., -1])

    def step(S, inp):
        st, dec = inp
        return S * dec[..., None, None] + st, S

    S0 = jnp.zeros((bsz, SSD_GROUPS, SSD_HPG, SSD_HEAD_DIM, SSD_STATE), f32)
    _, S_prev = lax.scan(step, S0, (jnp.moveaxis(states, 1, 0), jnp.moveaxis(chunk_decay, 1, 0)))
    S_prev = jnp.moveaxis(S_prev, 0, 1)
    y_off = jnp.einsum('bnigk,bnghpk->bnighp', Cc, S_prev) * \
        jnp.exp(acum).transpose(0, 1, 4, 2, 3)[..., None]
    y = (y_diag + y_off).reshape(bsz, seqlen, SSD_HEADS, SSD_HEAD_DIM)
    y = y + d_skip.astype(f32)[:, None] * xs.reshape(bsz, seqlen, SSD_HEADS, SSD_HEAD_DIM)
    y = y.reshape(bsz, seqlen, SSD_INNER) * jax.nn.silu(z.astype(f32))
    y = rms_norm(y.reshape(bsz, seqlen, SSD_GROUPS, SSD_INNER // SSD_GROUPS),
                 norm_w.reshape(SSD_GROUPS, SSD_INNER // SSD_GROUPS))
    return y.reshape(bsz, seqlen, SSD_INNER)


def s5_mixer(u, a_re, a_im, b_re, b_im, c_re, c_im, d_skip, log_dt, glu_w, glu_b):
    f32 = jnp.float32
    bsz, seqlen, _ = u.shape
    uf = u.astype(f32)
    ug = uf.reshape(bsz, seqlen, S5_GROUPS, S5_GROUP)
    a_re, a_im = a_re.astype(f32), a_im.astype(f32)
    b_re, b_im = b_re.astype(f32), b_im.astype(f32)
    delta = jnp.exp(log_dt.astype(f32))[:, None]
    mag = jnp.exp(a_re * delta)
    ab_re, ab_im = mag * jnp.cos(a_im * delta), mag * jnp.sin(a_im * delta)
    den = a_re * a_re + a_im * a_im
    p_re, p_im = ab_re - 1.0, ab_im
    f_re = (p_re * a_re + p_im * a_im) / den
    f_im = (p_im * a_re - p_re * a_im) / den
    bb_re = f_re[..., None] * b_re - f_im[..., None] * b_im
    bb_im = f_re[..., None] * b_im + f_im[..., None] * b_re
    bu_re = jnp.einsum('blgi,gni->blgn', ug, bb_re)
    bu_im = jnp.einsum('blgi,gni->blgn', ug, bb_im)
    shape = bu_re.shape
    elems = (jnp.broadcast_to(ab_re, shape), jnp.broadcast_to(ab_im, shape), bu_re, bu_im)

    def combine(e1, e2):
        a1r, a1i, b1r, b1i = e1
        a2r, a2i, b2r, b2i = e2
        return (a2r * a1r - a2i * a1i, a2r * a1i + a2i * a1r,
                a2r * b1r - a2i * b1i + b2r, a2r * b1i + a2i * b1r + b2i)

    _, _, h_re, h_im = lax.associative_scan(combine, elems, axis=1)
    y = jnp.einsum('blgn,gin->blgi', h_re, c_re.astype(f32)) - \
        jnp.einsum('blgn,gin->blgi', h_im, c_im.astype(f32))
    y = y.reshape(bsz, seqlen, S5_WIDTH) + d_skip.astype(f32) * uf
    g = jax.nn.gelu(y)
    return g * jax.nn.sigmoid(g @ glu_w.astype(f32) + glu_b.astype(f32))


def setup_inputs(seed: int = 0) -> dict:
    key = jax.random.key(seed)
    ks = iter(jax.random.split(key, 48))
    f32 = jnp.float32
    L = DEPTH

    def nrm(shape, scale):
        return jax.random.normal(next(ks), shape, f32) * scale

    def gain(shape):
        return 1.0 + nrm(shape, 0.02)

    def unif(shape, lo, hi):
        return jax.random.uniform(next(ks), shape, f32, lo, hi)

    def dt_bias(shape):
        dt = jnp.exp(unif(shape, math.log(1e-3), math.log(1e-1)))
        return dt + jnp.log(-jnp.expm1(-dt))

    return {
        'x': nrm((BATCH, SEQ, D_MODEL), 1.0),
        'ffn1_norm': gain((L, D_MODEL)),
        'ffn1_w_gate': nrm((L, D_MODEL, D_FF), D_MODEL ** -0.5),
        'ffn1_w_up': nrm((L, D_MODEL, D_FF), D_MODEL ** -0.5),
        'ffn1_w_down': nrm((L, D_FF, D_MODEL), D_FF ** -0.5),
        'mix_norm': gain((L, D_MODEL)),
        'w_in': nrm((L, D_MODEL, P_IN), D_MODEL ** -0.5),
        'gdn_conv_w': nrm((L, CONV_K, GDN_QKV), CONV_K ** -0.5),
        'gdn_a_log': jnp.log(unif((L, GDN_HEADS), 1.0, 16.0)),
        'gdn_dt_bias': dt_bias((L, GDN_HEADS)),
        'gdn_norm': gain((L, GDN_DV)),
        'ssd_conv_w': nrm((L, CONV_K, SSD_CONV_DIM), CONV_K ** -0.5),
        'ssd_conv_b': nrm((L, SSD_CONV_DIM), 0.01),
        'ssd_a_log': jnp.log(unif((L, SSD_HEADS), 1.0, 16.0)),
        'ssd_dt_bias': dt_bias((L, SSD_HEADS)),
        'ssd_d': gain((L, SSD_HEADS)),
        'ssd_norm': gain((L, SSD_INNER)),
        's5_a_re': -0.5 + nrm((L, S5_GROUPS, S5_STATE), 0.01),
        's5_a_im': jnp.pi * jnp.arange(S5_STATE, dtype=f32) + nrm((L, S5_GROUPS, S5_STATE), 0.01),
        's5_b_re': nrm((L, S5_GROUPS, S5_STATE, S5_GROUP), (2.0 * S5_GROUP) ** -0.5),
        's5_b_im': nrm((L, S5_GROUPS, S5_STATE, S5_GROUP), (2.0 * S5_GROUP) ** -0.5),
        's5_c_re': nrm((L, S5_GROUPS, S5_GROUP, S5_STATE), S5_STATE ** -0.5),
        's5_c_im': nrm((L, S5_GROUPS, S5_GROUP, S5_STATE), S5_STATE ** -0.5),
        's5_d': nrm((L, S5_WIDTH), 1.0),
        's5_log_dt': unif((L, S5_GROUPS), math.log(1e-3), math.log(1e-1)),
        's5_glu_w': nrm((L, S5_WIDTH, S5_WIDTH), S5_WIDTH ** -0.5),
        's5_glu_b': nrm((L, S5_WIDTH), 0.01),
        'w_out': nrm((L, D_MIX, D_MODEL), D_MIX ** -0.5),
        'ffn2_norm': gain((L, D_MODEL)),
        'ffn2_w_gate': nrm((L, D_MODEL, D_FF), D_MODEL ** -0.5),
        'ffn2_w_up': nrm((L, D_MODEL, D_FF), D_MODEL ** -0.5),
        'ffn2_w_down': nrm((L, D_FF, D_MODEL), D_FF ** -0.5),
        'final_norm': gain((D_MODEL,)),
    }


def reference(x, ffn1_norm, ffn1_w_gate, ffn1_w_up, ffn1_w_down, mix_norm, w_in,
              gdn_conv_w, gdn_a_log, gdn_dt_bias, gdn_norm,
              ssd_conv_w, ssd_conv_b, ssd_a_log, ssd_dt_bias, ssd_d, ssd_norm,
              s5_a_re, s5_a_im, s5_b_re, s5_b_im, s5_c_re, s5_c_im, s5_d, s5_log_dt,
              s5_glu_w, s5_glu_b, w_out, ffn2_norm, ffn2_w_gate, ffn2_w_up, ffn2_w_down,
              final_norm):
    h = x
    for i in range(DEPTH):
        h = h + 0.5 * swiglu(rms_norm(h, ffn1_norm[i]), ffn1_w_gate[i], ffn1_w_up[i], ffn1_w_down[i])
        u = rms_norm(h, mix_norm[i])
        proj = u @ w_in[i]
        p_gdn, p_ssd, p_s5 = jnp.split(proj, [GDN_IN, GDN_IN + SSD_IN], axis=-1)
        o_gdn = gated_deltanet(p_gdn, gdn_conv_w[i], gdn_a_log[i], gdn_dt_bias[i], gdn_norm[i])
        o_ssd = mamba2_ssd(p_ssd, ssd_conv_w[i], ssd_conv_b[i], ssd_a_log[i], ssd_dt_bias[i],
                           ssd_d[i], ssd_norm[i])
        o_s5 = s5_mixer(p_s5, s5_a_re[i], s5_a_im[i], s5_b_re[i], s5_b_im[i], s5_c_re[i], s5_c_im[i],
                        s5_d[i], s5_log_dt[i], s5_glu_w[i], s5_glu_b[i])
        mixed = jnp.concatenate([o_gdn, o_ssd, o_s5], axis=-1).astype(h.dtype)
        h = h + mixed @ w_out[i]
        h = h + 0.5 * swiglu(rms_norm(h, ffn2_norm[i]), ffn2_w_gate[i], ffn2_w_up[i], ffn2_w_down[i])
    return rms_norm(h, final_norm)
```

```python
import functools
import math

import jax
import jax.numpy as jnp
from jax import lax
from jax.experimental import pallas as pl
from jax.experimental.pallas import tpu as pltpu

F32 = jnp.float32
BF16 = jnp.bfloat16

RMS_EPS = 1e-6
CHUNK = 64
CONV_K = 4
D_MODEL = 2048
D_FF = 5632

GDN_HEADS = 6
GDN_D = 128
GDN_QK = GDN_HEADS * GDN_D
GDN_QKV = 3 * GDN_QK

SSD_HEADS = 12
SSD_P = 64
SSD_GROUPS = 2
SSD_N = 128
SSD_HPG = SSD_HEADS // SSD_GROUPS
SSD_GW = SSD_HPG * SSD_P
SSD_INNER = SSD_HEADS * SSD_P

S5_WIDTH = 512
S5_GROUP = 16
S5_GROUPS = 32
S5_STATE = 64
S5_CH = S5_GROUPS * S5_STATE

LANE = 128

COL_SSD_Z = 0
COL_SSD_X = 768
COL_S5_U = 1536
COL_SSD_B = 2048
COL_SSD_C = 2304
COL_SSD_DT = 2560
COL_GDN_QKV = 2688
COL_GDN_Z = 4992
COL_GDN_GATE = 5760
P_PAD = 6144

VMEM_LIMIT = 60 * 1024 * 1024

NN = (((1,), (0,)), ((), ()))
NT = (((1,), (1,)), ((), ()))
TN = (((0,), (0,)), ((), ()))


def _dot(a, b, dims=NN):
    return lax.dot_general(a, b, dims, preferred_element_type=F32)


def _hi_lo(a):
    hi = a.astype(BF16)
    lo = (a - hi.astype(F32)).astype(BF16)
    return hi, lo


def _mm(a, b, dims=NN, passes=3):
    if passes == 1:
        return _dot(a.astype(BF16), b.astype(BF16), dims)
    ah, al = _hi_lo(a)
    bh, bl = _hi_lo(b)
    return _dot(ah, bh, dims) + (_dot(ah, bl, dims) + _dot(al, bh, dims))


def _mm01(m01, x, dims=NN):
    x1 = x.astype(BF16)
    r = x - x1.astype(F32)
    x2 = r.astype(BF16)
    x3 = (r - x2.astype(F32)).astype(BF16)
    return _dot(m01, x1, dims) + (_dot(m01, x2, dims) + _dot(m01, x3, dims))


def _mm01_t(x, m01, dims=NN):
    x1 = x.astype(BF16)
    r = x - x1.astype(F32)
    x2 = r.astype(BF16)
    x3 = (r - x2.astype(F32)).astype(BF16)
    return _dot(x1, m01, dims) + (_dot(x2, m01, dims) + _dot(x3, m01, dims))


def _sigmoid(x):
    return 1.0 / (1.0 + jnp.exp(-x))


def _silu(x):
    return x * _sigmoid(x)


def _softplus(x):
    return jnp.maximum(x, 0.0) + jnp.log1p(jnp.exp(-jnp.abs(x)))


def _iota(shape, dim):
    return lax.broadcasted_iota(jnp.int32, shape, dim)


def _causal_conv(x, cw_ref):
    acc = None
    for i in range(CONV_K):
        shift = CONV_K - 1 - i
        xs = x if shift == 0 else pltpu.roll(x, shift, axis=0)
        term = xs[8:] * cw_ref[i:i + 1, :]
        acc = term if acc is None else acc + term
    return acc


def _ffn_kernel(x_ref, nw_ref, wg_ref, wu_ref, wd_ref, fw_ref, o_ref, xn_ref, *, final_norm):
    j = pl.program_id(1)

    @pl.when(j == 0)
    def _():
        x = x_ref[...]
        ms = jnp.mean(x * x, axis=-1, keepdims=True)
        xn_ref[...] = (x * lax.rsqrt(ms + RMS_EPS) * nw_ref[...]).astype(BF16)
        o_ref[...] = x

    xn = xn_ref[...]
    g = _dot(xn, wg_ref[...])
    u = _dot(xn, wu_ref[...])
    a = (0.5 * _silu(g) * u).astype(BF16)
    o_ref[...] += _dot(a, wd_ref[...])

    if final_norm:
        @pl.when(j == pl.num_programs(1) - 1)
        def _():
            h = o_ref[...]
            ms = jnp.mean(h * h, axis=-1, keepdims=True)
            o_ref[...] = h * lax.rsqrt(ms + RMS_EPS) * fw_ref[...]


def _ffn(x, nw, wg, wu, wd, fw, *, final_norm, tm=512, tf=512):
    t, d = x.shape
    f = wg.shape[1]
    return pl.pallas_call(
        functools.partial(_ffn_kernel, final_norm=final_norm),
        grid=(t // tm, f // tf),
        in_specs=[
            pl.BlockSpec((tm, d), lambda i, j: (i, 0)),
            pl.BlockSpec((1, d), lambda i, j: (0, 0)),
            pl.BlockSpec((d, tf), lambda i, j: (0, j)),
            pl.BlockSpec((d, tf), lambda i, j: (0, j)),
            pl.BlockSpec((tf, d), lambda i, j: (j, 0)),
            pl.BlockSpec((1, d), lambda i, j: (0, 0)),
        ],
        out_specs=pl.BlockSpec((tm, d), lambda i, j: (i, 0)),
        out_shape=jax.ShapeDtypeStruct((t, d), F32),
        scratch_shapes=[pltpu.VMEM((tm, d), BF16)],
        compiler_params=pltpu.CompilerParams(
            dimension_semantics=("parallel", "arbitrary"), vmem_limit_bytes=VMEM_LIMIT),
        name="ffn",
    )(x, nw.reshape(1, d), wg, wu, wd, fw.reshape(1, d))


def _inproj_kernel(x_ref, nw_ref, w_ref, o_ref, xn_ref):
    @pl.when(pl.program_id(1) == 0)
    def _():
        x = x_ref[...]
        ms = jnp.mean(x * x, axis=-1, keepdims=True)
        xn_ref[...] = (x * lax.rsqrt(ms + RMS_EPS) * nw_ref[...]).astype(BF16)

    o_ref[...] = _dot(xn_ref[...], w_ref[...])


def _inproj(x, nw, w, *, tm=512, tn=1024):
    t, d = x.shape
    n = w.shape[1]
    return pl.pallas_call(
        _inproj_kernel,
        grid=(t // tm, n // tn),
        in_specs=[
            pl.BlockSpec((tm, d), lambda i, j: (i, 0)),
            pl.BlockSpec((1, d), lambda i, j: (0, 0)),
            pl.BlockSpec((d, tn), lambda i, j: (0, j)),
        ],
        out_specs=pl.BlockSpec((tm, tn), lambda i, j: (i, j)),
        out_shape=jax.ShapeDtypeStruct((t, n), F32),
        scratch_shapes=[pltpu.VMEM((tm, d), BF16)],
        compiler_params=pltpu.CompilerParams(
            dimension_semantics=("parallel", "arbitrary"), vmem_limit_bytes=VMEM_LIMIT),
        name="inproj",
    )(x, nw.reshape(1, d), w)


def _outproj_kernel(h_ref, a_ref, b_ref, c_ref, wa_ref, wb_ref, wc_ref, o_ref):
    acc = _dot(a_ref[...], wa_ref[...])
    acc += _dot(b_ref[...], wb_ref[...])
    acc += _dot(c_ref[...], wc_ref[...])
    o_ref[...] = h_ref[...] + acc


def _outproj(h, oa, ob, oc, wa, wb, wc, *, tm=512):
    t, d = h.shape
    row = lambda i: (i, 0)
    fixed = lambda i: (0, 0)
    return pl.pallas_call(
        _outproj_kernel,
        grid=(t // tm,),
        in_specs=[
            pl.BlockSpec((tm, d), row),
            pl.BlockSpec((tm, oa.shape[1]), row),
            pl.BlockSpec((tm, ob.shape[1]), row),
            pl.BlockSpec((tm, oc.shape[1]), row),
            pl.BlockSpec(wa.shape, fixed),
            pl.BlockSpec(wb.shape, fixed),
            pl.BlockSpec(wc.shape, fixed),
        ],
        out_specs=pl.BlockSpec((tm, d), row),
        out_shape=jax.ShapeDtypeStruct((t, d), F32),
        compiler_params=pltpu.CompilerParams(
            dimension_semantics=("parallel",), vmem_limit_bytes=VMEM_LIMIT),
        name="outproj",
    )(h, oa, ob, oc, wa, wb, wc)


def _gdn_kernel(alog_ref, dtb_ref,
                q_ref, k_ref, v_ref, z_ref, gate_ref,
                cwq_ref, cwk_ref, cwv_ref, nw_ref,
                o_ref,
                xp_ref, u_ref, w_ref, qd_ref, kd_ref, at_ref, eg_ref, s_ref,
                *, passes):
    h = pl.program_id(1)
    seq = q_ref.shape[0]
    n_chunks = seq // CHUNK
    c = CHUNK

    for idx, src in enumerate((q_ref, k_ref, v_ref)):
        xp_ref[idx, 0:8, :] = jnp.zeros((8, LANE), F32)
        xp_ref[idx, 8:8 + seq, :] = src[...]

    neg_a = -jnp.exp(jnp.full((c, LANE), alog_ref[h], F32))
    dt_bias = jnp.full((c, LANE), dtb_ref[h], F32)

    row = _iota((c, c), 0)
    col = _iota((c, c), 1)
    tril = row >= col
    strict = row > col
    ltri = tril.astype(BF16)
    ones = jnp.ones((c, c), BF16)
    eye = (row == col).astype(F32)
    lane = _iota((c, LANE), 1)

    def conv(idx, cw_ref, r0):
        return _silu(_causal_conv(xp_ref[idx, pl.ds(r0, c + 8), :], cw_ref))

    def prep(ci, carry):
        r0 = pl.multiple_of(ci * c, c)
        q = conv(0, cwq_ref, r0)
        k = conv(1, cwk_ref, r0)
        v = conv(2, cwv_ref, r0)
        qn = q * lax.rsqrt(jnp.sum(q * q, axis=-1, keepdims=True) + RMS_EPS) * (GDN_D ** -0.5)
        kn = k * lax.rsqrt(jnp.sum(k * k, axis=-1, keepdims=True) + RMS_EPS)
        gt = gate_ref[pl.ds(r0, c), :]
        b_raw = jnp.sum(jnp.where(lane == h, gt, 0.0), axis=-1, keepdims=True)
        a_raw = jnp.sum(jnp.where(lane == h + GDN_HEADS, gt, 0.0), axis=-1, keepdims=True)
        beta = _sigmoid(b_raw)
        g_b = neg_a * _softplus(jnp.broadcast_to(a_raw, (c, LANE)) + dt_bias)
        gi = _mm01(ltri, g_b)
        gj = _mm01(ones, jnp.where(row <= col, g_b[:, :c], 0.0))
        decay = jnp.where(tril, jnp.exp(jnp.where(tril, gi[:, :c] - gj, 0.0)), 0.0)
        exp_g = jnp.exp(gi)
        g_last = jnp.broadcast_to(gi[c - 1:c, :], (c, LANE))
        kb = kn * beta
        lower = jnp.where(strict, _mm(kb, kn, NT, passes) * decay, 0.0)
        npow = -lower
        tinv = eye + npow
        for _ in range(5):
            npow = _mm(npow, npow, NN, passes)
            tinv = tinv + _mm(tinv, npow, NN, passes)
        uw = _mm(tinv, jnp.concatenate([v * beta, kb * exp_g], axis=1), NN, passes)
        u_ref[pl.ds(r0, c), :] = uw[:, :LANE]
        w_ref[pl.ds(r0, c), :] = uw[:, LANE:]
        at_ref[pl.ds(r0, c), :] = jnp.where(tril, _mm(qn, kn, NT, passes) * decay, 0.0)
        qd_ref[pl.ds(r0, c), :] = qn * exp_g
        kd_ref[pl.ds(r0, c), :] = kn * jnp.exp(g_last - gi)
        eg_ref[pl.ds(pl.multiple_of(ci * 8, 8), 8), :] = jnp.exp(g_last[0:8, :])
        return carry

    lax.fori_loop(0, n_chunks, prep, 0)

    s_ref[...] = jnp.zeros((GDN_D, GDN_D), F32)
    norm_w = nw_ref[...]

    def scan(ci, carry):
        r0 = pl.multiple_of(ci * c, c)
        s = s_ref[...]
        v_new = u_ref[pl.ds(r0, c), :] - _mm(w_ref[pl.ds(r0, c), :], s, NN, passes)
        o = _mm(qd_ref[pl.ds(r0, c), :], s, NN, passes) + _mm(at_ref[pl.ds(r0, c), :], v_new, NN, passes)
        eg = eg_ref[pl.ds(pl.multiple_of(ci * 8, 8), 8), :]
        s_ref[...] = s * jnp.broadcast_to(eg[0:1, :], (GDN_D, GDN_D)) + \
            _mm(kd_ref[pl.ds(r0, c), :], v_new, TN, passes)
        ms = jnp.mean(o * o, axis=-1, keepdims=True)
        y = o * lax.rsqrt(ms + RMS_EPS) * norm_w * _silu(z_ref[pl.ds(r0, c), :])
        o_ref[pl.ds(r0, c), :] = y.astype(o_ref.dtype)
        return carry

    lax.fori_loop(0, n_chunks, scan, 0)


def _gdn(proj, bsz, seq, conv_w, a_log, dt_bias, norm_w, *, passes=3):
    t = proj.shape[0]
    q0 = COL_GDN_QKV // LANE
    z0 = COL_GDN_Z // LANE
    g0 = COL_GDN_GATE // LANE
    blk = lambda off: pl.BlockSpec((seq, LANE), lambda b, h, off=off: (b, off + h))
    cw = lambda off: pl.BlockSpec((CONV_K, LANE), lambda b, h, off=off: (0, off + h))
    smem = pl.BlockSpec(memory_space=pltpu.SMEM)
    return pl.pallas_call(
        functools.partial(_gdn_kernel, passes=passes),
        grid=(bsz, GDN_HEADS),
        in_specs=[
            smem, smem,
            blk(q0), blk(q0 + GDN_HEADS), blk(q0 + 2 * GDN_HEADS), blk(z0),
            pl.BlockSpec((seq, LANE), lambda b, h: (b, g0)),
            cw(0), cw(GDN_HEADS), cw(2 * GDN_HEADS),
            pl.BlockSpec((1, LANE), lambda b, h: (0, 0)),
        ],
        out_specs=pl.BlockSpec((seq, LANE), lambda b, h: (b, h)),
        out_shape=jax.ShapeDtypeStruct((t, GDN_QK), BF16),
        scratch_shapes=[
            pltpu.VMEM((3, seq + 8, LANE), F32),
            pltpu.VMEM((seq, LANE), F32),
            pltpu.VMEM((seq, LANE), F32),
            pltpu.VMEM((seq, LANE), F32),
            pltpu.VMEM((seq, LANE), F32),
            pltpu.VMEM((seq, CHUNK), F32),
            pltpu.VMEM((seq // CHUNK * 8, LANE), F32),
            pltpu.VMEM((GDN_D, GDN_D), F32),
        ],
        compiler_params=pltpu.CompilerParams(
            dimension_semantics=("parallel", "arbitrary"), vmem_limit_bytes=VMEM_LIMIT),
        name="gdn",
    )(a_log, dt_bias, proj, proj, proj, proj, proj, conv_w, conv_w, conv_w,
      norm_w.reshape(1, LANE))


def _ssd_kernel(z_ref, x_ref, b_ref, c_ref, dt_ref,
                cwx_ref, cwb_ref, cwc_ref,
                cbx_ref, cbb_ref, cbc_ref,
                alog_ref, dtb_ref, dsk_ref, nw_ref,
                o_ref,
                xp_ref, bp_ref, cp_ref, s_ref, *, passes):
    grp = pl.program_id(1)
    seq = x_ref.shape[0]
    n_chunks = seq // CHUNK
    c = CHUNK
    gw = SSD_GW

    for ref, src in ((xp_ref, x_ref), (bp_ref, b_ref), (cp_ref, c_ref)):
        ref[0:8, :] = jnp.zeros((8, ref.shape[1]), F32)
        ref[8:8 + seq, :] = src[...]

    row = _iota((c, c), 0)
    col = _iota((c, c), 1)
    ltri = (row >= col).astype(BF16)
    ones = jnp.ones((c, c), BF16)
    rowt = _iota((c, gw), 0)
    colt = jnp.bitwise_and(_iota((c, gw), 1), c - 1)
    tril_t = rowt >= colt
    upper_t = rowt <= colt
    expand = (_iota((LANE, gw), 0) == grp * SSD_HPG + lax.shift_right_logical(_iota((LANE, gw), 1), 6)).astype(BF16)
    last_row = (_iota((c, gw), 0) == c - 1)
    ones_cn = jnp.ones((c, SSD_N), BF16)

    neg_a = -jnp.exp(alog_ref[...])
    dt_bias = dtb_ref[...]
    d_skip = dsk_ref[...]
    norm_w = nw_ref[...]

    def conv(p_ref, cw_ref, cb_ref, r0):
        return _silu(_causal_conv(p_ref[pl.ds(r0, c + 8), :], cw_ref) + cb_ref[...])

    s_ref[...] = jnp.zeros((gw, SSD_N), F32)

    def body(ci, carry):
        r0 = pl.multiple_of(ci * c, c)
        xs = conv(xp_ref, cwx_ref, cbx_ref, r0)
        bm = conv(bp_ref, cwb_ref, cbb_ref, r0)
        cm = conv(cp_ref, cwc_ref, cbc_ref, r0)
        dt = _softplus(_mm01_t(dt_ref[pl.ds(r0, c), :], expand) + dt_bias)
        a = dt * neg_a
        acum = _mm01(ltri, a)
        acum_j = _mm01(ones, jnp.where(upper_t, a, 0.0))
        lm = jnp.where(tril_t, jnp.exp(jnp.where(tril_t, acum - acum_j, 0.0)), 0.0)
        xdt = xs * dt
        cb = _mm(cm, bm, NT, passes)
        y_parts = []
        for hh in range(SSD_HPG):
            sl = slice(hh * SSD_P, (hh + 1) * SSD_P)
            y_parts.append(_mm(cb * lm[:, sl], xdt[:, sl], NN, passes))
        y_diag = jnp.concatenate(y_parts, axis=1)
        a_last = jnp.broadcast_to(acum[c - 1:c, :], (c, gw))
        states = _mm(xdt * jnp.exp(a_last - acum), bm, TN, passes)
        dec_col = jnp.exp(_mm01_t(jnp.where(last_row, acum, 0.0), ones_cn, TN))
        s_prev = s_ref[...]
        y_off = _mm(cm, s_prev, NT, passes) * jnp.exp(acum)
        s_ref[...] = s_prev * dec_col + states
        y = y_diag + y_off + d_skip * xs
        y = y * _silu(z_ref[pl.ds(r0, c), :])
        ms = jnp.mean(y * y, axis=-1, keepdims=True)
        o_ref[pl.ds(r0, c), :] = (y * lax.rsqrt(ms + RMS_EPS) * norm_w).astype(o_ref.dtype)
        return carry

    lax.fori_loop(0, n_chunks, body, 0)


def _ssd(proj, bsz, seq, conv_w, conv_b, a_log, dt_bias, d_skip, norm_w, *, passes=3):
    t = proj.shape[0]
    gw = SSD_GW
    rep = lambda v: jnp.repeat(v.astype(F32), SSD_P).reshape(1, SSD_INNER)
    cb = conv_b.reshape(1, -1)
    wide = lambda off: pl.BlockSpec((seq, gw), lambda b, g, off=off: (b, off + g))
    lane = lambda off: pl.BlockSpec((seq, LANE), lambda b, g, off=off: (b, off + g))
    chan = pl.BlockSpec((1, gw), lambda b, g: (0, g))
    return pl.pallas_call(
        functools.partial(_ssd_kernel, passes=passes),
        grid=(bsz, SSD_GROUPS),
        in_specs=[
            wide(COL_SSD_Z // gw), wide(COL_SSD_X // gw),
            lane(COL_SSD_B // LANE), lane(COL_SSD_C // LANE),
            pl.BlockSpec((seq, LANE), lambda b, g: (b, COL_SSD_DT // LANE)),
            pl.BlockSpec((CONV_K, gw), lambda b, g: (0, g)),
            pl.BlockSpec((CONV_K, LANE), lambda b, g: (0, SSD_INNER // LANE + g)),
            pl.BlockSpec((CONV_K, LANE), lambda b, g: (0, SSD_INNER // LANE + SSD_GROUPS + g)),
            pl.BlockSpec((1, gw), lambda b, g: (0, g)),
            pl.BlockSpec((1, LANE), lambda b, g: (0, SSD_INNER // LANE + g)),
            pl.BlockSpec((1, LANE), lambda b, g: (0, SSD_INNER // LANE + SSD_GROUPS + g)),
            chan, chan, chan, chan,
        ],
        out_specs=pl.BlockSpec((seq, gw), lambda b, g: (b, g)),
        out_shape=jax.ShapeDtypeStruct((t, SSD_INNER), BF16),
        scratch_shapes=[
            pltpu.VMEM((seq + 8, gw), F32),
            pltpu.VMEM((seq + 8, LANE), F32),
            pltpu.VMEM((seq + 8, LANE), F32),
            pltpu.VMEM((gw, SSD_N), F32),
        ],
        compiler_params=pltpu.CompilerParams(
            dimension_semantics=("parallel", "arbitrary"), vmem_limit_bytes=VMEM_LIMIT),
        name="ssd",
    )(proj, proj, proj, proj, proj, conv_w, conv_w, conv_w, cb, cb, cb,
      rep(a_log), rep(dt_bias), rep(d_skip), norm_w.reshape(1, SSD_INNER))


def _s5_kernel(u_ref, wb_ref, wc_ref, are_ref, aim_ref, dsk_ref, gw_ref, gb_ref,
               o_ref, bu_ref, h_ref):
    bsz, ts, width = u_ref.shape
    nt = S5_CH // LANE
    slab = 4

    @pl.when(pl.program_id(0) == 0)
    def _():
        h_ref[...] = jnp.zeros(h_ref.shape, F32)

    u = u_ref[...].reshape(bsz * ts, width)
    u_bf = u.astype(BF16)
    for kk in range(2 * nt // slab):
        res = _dot(u_bf, wb_ref[:, kk * slab * LANE:(kk + 1) * slab * LANE])
        for k in range(slab):
            bu_ref[kk * slab + k] = res[:, k * LANE:(k + 1) * LANE]

    def step(t, carry):
        rows = pl.ds(t, bsz, stride=ts)
        new_re, new_im = [], []
        for k in range(nt):
            h_re, h_im = carry[k], carry[nt + k]
            a_re = are_ref[:, k * LANE:(k + 1) * LANE]
            a_im = aim_ref[:, k * LANE:(k + 1) * LANE]
            n_re = a_re * h_re - a_im * h_im + bu_ref[k, rows, :]
            n_im = a_re * h_im + a_im * h_re + bu_ref[nt + k, rows, :]
            bu_ref[k, rows, :] = n_re
            bu_ref[nt + k, rows, :] = n_im
            new_re.append(n_re)
            new_im.append(n_im)
        return tuple(new_re + new_im)

    h_last = lax.fori_loop(0, ts, step, tuple(h_ref[k] for k in range(2 * nt)))
    for k in range(2 * nt):
        h_ref[k] = h_last[k]

    y = dsk_ref[...] * u
    for kk in range(2 * nt // slab):
        hs = jnp.concatenate([bu_ref[kk * slab + k] for k in range(slab)], axis=1)
        y += _dot(hs.astype(BF16), wc_ref[kk * slab * LANE:(kk + 1) * slab * LANE, :])
    g = 0.5 * y * (1.0 + jnp.tanh(math.sqrt(2.0 / math.pi) * (y + 0.044715 * (y * y * y))))
    out = g * _sigmoid(_dot(g.astype(BF16), gw_ref[...]) + gb_ref[...])
    o_ref[...] = out.reshape(bsz, ts, width).astype(o_ref.dtype)


def _s5(proj, bsz, seq, wb, wc, a_re, a_im, d_skip, glu_w, glu_b, *, ts=128):
    proj3 = proj.reshape(bsz, seq, proj.shape[1])
    fixed = lambda shape: pl.BlockSpec(shape, lambda k: (0,) * len(shape))
    out = pl.pallas_call(
        _s5_kernel,
        grid=(seq // ts,),
        in_specs=[
            pl.BlockSpec((bsz, ts, S5_WIDTH), lambda k: (0, k, COL_S5_U // S5_WIDTH)),
            fixed(wb.shape), fixed(wc.shape),
            fixed((1, S5_CH)), fixed((1, S5_CH)),
            fixed((1, S5_WIDTH)), fixed(glu_w.shape), fixed((1, S5_WIDTH)),
        ],
        out_specs=pl.BlockSpec((bsz, ts, S5_WIDTH), lambda k: (0, k, 0)),
        out_shape=jax.ShapeDtypeStruct((bsz, seq, S5_WIDTH), BF16),
        scratch_shapes=[
            pltpu.VMEM((2 * S5_CH // LANE, bsz * ts, LANE), F32),
            pltpu.VMEM((2 * S5_CH // LANE, bsz, LANE), F32),
        ],
        compiler_params=pltpu.CompilerParams(
            dimension_semantics=("arbitrary",), vmem_limit_bytes=VMEM_LIMIT),
        name="s5",
    )(proj3, wb, wc, a_re.reshape(1, S5_CH), a_im.reshape(1, S5_CH),
      d_skip.reshape(1, S5_WIDTH), glu_w, glu_b.reshape(1, S5_WIDTH))
    return out.reshape(bsz * seq, S5_WIDTH)


def _s5_params(a_re, a_im, b_re, b_im, c_re, c_im, log_dt):
    delta = jnp.exp(log_dt)[:, None]
    mag = jnp.exp(a_re * delta)
    ab_re, ab_im = mag * jnp.cos(a_im * delta), mag * jnp.sin(a_im * delta)
    den = a_re * a_re + a_im * a_im
    p_re, p_im = ab_re - 1.0, ab_im
    f_re = (p_re * a_re + p_im * a_im) / den
    f_im = (p_im * a_re - p_re * a_im) / den
    bb_re = f_re[..., None] * b_re - f_im[..., None] * b_im
    bb_im = f_re[..., None] * b_im + f_im[..., None] * b_re
    eye = jnp.eye(S5_GROUPS, dtype=F32)
    emb_b = lambda bb: jnp.einsum('gnj,gh->gjhn', bb, eye).reshape(S5_WIDTH, S5_CH)
    wb = jnp.concatenate([emb_b(bb_re), emb_b(bb_im)], axis=1)
    emb_c = lambda cc: jnp.einsum('gin,gh->gnhi', cc, eye).reshape(S5_CH, S5_WIDTH)
    wc = jnp.concatenate([emb_c(c_re), -emb_c(c_im)], axis=0)
    return ab_re.reshape(-1), ab_im.reshape(-1), wb.astype(BF16), wc.astype(BF16)


def _pack_w_in(w_in):
    d = w_in.shape[0]
    gdn_in = GDN_QKV + GDN_QK + 2 * GDN_HEADS
    ssd_conv = SSD_INNER + 2 * SSD_GROUPS * SSD_N
    g_qkv = w_in[:, 0:GDN_QKV]
    g_z = w_in[:, GDN_QKV:GDN_QKV + GDN_QK]
    g_gate = w_in[:, GDN_QKV + GDN_QK:gdn_in]
    s0 = gdn_in
    s_z = w_in[:, s0:s0 + SSD_INNER]
    s_x = w_in[:, s0 + SSD_INNER:s0 + 2 * SSD_INNER]
    s_b = w_in[:, s0 + 2 * SSD_INNER:s0 + 2 * SSD_INNER + SSD_GROUPS * SSD_N]
    s_c = w_in[:, s0 + 2 * SSD_INNER + SSD_GROUPS * SSD_N:s0 + SSD_INNER + ssd_conv]
    s_dt = w_in[:, s0 + SSD_INNER + ssd_conv:s0 + SSD_INNER + ssd_conv + SSD_HEADS]
    u0 = s0 + SSD_INNER + ssd_conv + SSD_HEADS
    s5_u = w_in[:, u0:u0 + S5_WIDTH]
    pad = lambda n: jnp.zeros((d, n), w_in.dtype)
    cols = [s_z, s_x, s5_u, s_b, s_c, s_dt, pad(LANE - SSD_HEADS), g_qkv, g_z,
            g_gate, pad(LANE - 2 * GDN_HEADS), pad(P_PAD - (COL_GDN_GATE + LANE))]
    return jnp.concatenate(cols, axis=1).astype(BF16)


def kernel(x, ffn1_norm, ffn1_w_gate, ffn1_w_up, ffn1_w_down, mix_norm, w_in,
           gdn_conv_w, gdn_a_log, gdn_dt_bias, gdn_norm,
           ssd_conv_w, ssd_conv_b, ssd_a_log, ssd_dt_bias, ssd_d, ssd_norm,
           s5_a_re, s5_a_im, s5_b_re, s5_b_im, s5_c_re, s5_c_im, s5_d, s5_log_dt,
           s5_glu_w, s5_glu_b, w_out, ffn2_norm, ffn2_w_gate, ffn2_w_up, ffn2_w_down,
           final_norm):
    bsz, seq, d = x.shape
    depth = w_in.shape[0]
    h = x.reshape(bsz * seq, d)
    for i in range(depth):
        h = _ffn(h, ffn1_norm[i], ffn1_w_gate[i].astype(BF16), ffn1_w_up[i].astype(BF16),
                 ffn1_w_down[i].astype(BF16), final_norm, final_norm=False)
        proj = _inproj(h, mix_norm[i], _pack_w_in(w_in[i]))
        o_gdn = _gdn(proj, bsz, seq, gdn_conv_w[i], gdn_a_log[i], gdn_dt_bias[i], gdn_norm[i])
        o_ssd = _ssd(proj, bsz, seq, ssd_conv_w[i], ssd_conv_b[i], ssd_a_log[i], ssd_dt_bias[i],
                     ssd_d[i], ssd_norm[i])
        ab_re, ab_im, wb, wc = _s5_params(s5_a_re[i], s5_a_im[i], s5_b_re[i], s5_b_im[i],
                                          s5_c_re[i], s5_c_im[i], s5_log_dt[i])
        o_s5 = _s5(proj, bsz, seq, wb, wc, ab_re, ab_im, s5_d[i],
                   s5_glu_w[i].astype(BF16), s5_glu_b[i])
        wo = w_out[i].astype(BF16)
        h = _outproj(h, o_gdn, o_ssd, o_s5, wo[0:GDN_QK], wo[GDN_QK:GDN_QK + SSD_INNER],
                     wo[GDN_QK + SSD_INNER:])
        h = _ffn(h, ffn2_norm[i], ffn2_w_gate[i].astype(BF16), ffn2_w_up[i].astype(BF16),
                 ffn2_w_down[i].astype(BF16), final_norm, final_norm=(i == depth - 1))
    return h.reshape(bsz, seq, d)
```

```python
import functools
import math

import jax
import jax.numpy as jnp
from jax import lax
from jax.experimental import pallas as pl
from jax.experimental.pallas import tpu as pltpu

F32 = jnp.float32
BF16 = jnp.bfloat16

RMS_EPS = 1e-6
CHUNK = 64
CONV_K = 4
D_MODEL = 2048
D_FF = 5632

GDN_HEADS = 6
GDN_D = 128
GDN_QK = GDN_HEADS * GDN_D
GDN_QKV = 3 * GDN_QK

SSD_HEADS = 12
SSD_P = 64
SSD_GROUPS = 2
SSD_N = 128
SSD_HPG = SSD_HEADS // SSD_GROUPS
SSD_GW = SSD_HPG * SSD_P
SSD_INNER = SSD_HEADS * SSD_P

S5_WIDTH = 512
S5_GROUP = 16
S5_GROUPS = 32
S5_STATE = 64
S5_CH = S5_GROUPS * S5_STATE

LANE = 128

COL_SSD_Z = 0
COL_SSD_X = 768
COL_S5_U = 1536
COL_SSD_B = 2048
COL_SSD_C = 2304
COL_SSD_DT = 2560
COL_GDN_QKV = 2688
COL_GDN_Z = 4992
COL_GDN_GATE = 5760
P_PAD = 6144

VMEM_LIMIT = 60 * 1024 * 1024

NN = (((1,), (0,)), ((), ()))
NT = (((1,), (1,)), ((), ()))
TN = (((0,), (0,)), ((), ()))


def _dot(a, b, dims=NN):
    return lax.dot_general(a, b, dims, preferred_element_type=F32)


def _hi_lo(a):
    hi = a.astype(BF16)
    lo = (a - hi.astype(F32)).astype(BF16)
    return hi, lo


def _mm(a, b, dims=NN, passes=3):
    if passes == 1:
        return _dot(a.astype(BF16), b.astype(BF16), dims)
    ah, al = _hi_lo(a)
    bh, bl = _hi_lo(b)
    return _dot(ah, bh, dims) + (_dot(ah, bl, dims) + _dot(al, bh, dims))


def _mm01(m01, x, dims=NN):
    x1 = x.astype(BF16)
    r = x - x1.astype(F32)
    x2 = r.astype(BF16)
    x3 = (r - x2.astype(F32)).astype(BF16)
    return _dot(m01, x1, dims) + (_dot(m01, x2, dims) + _dot(m01, x3, dims))


def _mm01_t(x, m01, dims=NN):
    x1 = x.astype(BF16)
    r = x - x1.astype(F32)
    x2 = r.astype(BF16)
    x3 = (r - x2.astype(F32)).astype(BF16)
    return _dot(x1, m01, dims) + (_dot(x2, m01, dims) + _dot(x3, m01, dims))


def _sigmoid(x):
    return 1.0 / (1.0 + jnp.exp(-x))


def _silu(x):
    return x * _sigmoid(x)


def _softplus(x):
    return jnp.maximum(x, 0.0) + jnp.log1p(jnp.exp(-jnp.abs(x)))


def _iota(shape, dim):
    return lax.broadcasted_iota(jnp.int32, shape, dim)


def _causal_conv(x, cw_ref):
    acc = None
    for i in range(CONV_K):
        shift = CONV_K - 1 - i
        xs = x if shift == 0 else pltpu.roll(x, shift, axis=0)
        term = xs[8:] * cw_ref[i:i + 1, :]
        acc = term if acc is None else acc + term
    return acc


def _ffn_kernel(x_ref, nw_ref, wg_ref, wu_ref, wd_ref, fw_ref, o_ref, xn_ref, *, final_norm):
    j = pl.program_id(1)

    @pl.when(j == 0)
    def _():
        x = x_ref[...]
        ms = jnp.mean(x * x, axis=-1, keepdims=True)
        xn_ref[...] = (x * lax.rsqrt(ms + RMS_EPS) * nw_ref[...]).astype(BF16)
        o_ref[...] = x

    xn = xn_ref[...]
    g = _dot(xn, wg_ref[...])
    u = _dot(xn, wu_ref[...])
    a = (0.5 * _silu(g) * u).astype(BF16)
    o_ref[...] += _dot(a, wd_ref[...])

    if final_norm:
        @pl.when(j == pl.num_programs(1) - 1)
        def _():
            h = o_ref[...]
            ms = jnp.mean(h * h, axis=-1, keepdims=True)
            o_ref[...] = h * lax.rsqrt(ms + RMS_EPS) * fw_ref[...]


def _ffn(x, nw, wg, wu, wd, fw, *, final_norm, tm=512, tf=512):
    t, d = x.shape
    f = wg.shape[1]
    return pl.pallas_call(
        functools.partial(_ffn_kernel, final_norm=final_norm),
        grid=(t // tm, f // tf),
        in_specs=[
            pl.BlockSpec((tm, d), lambda i, j: (i, 0)),
            pl.BlockSpec((1, d), lambda i, j: (0, 0)),
            pl.BlockSpec((d, tf), lambda i, j: (0, j)),
            pl.BlockSpec((d, tf), lambda i, j: (0, j)),
            pl.BlockSpec((tf, d), lambda i, j: (j, 0)),
            pl.BlockSpec((1, d), lambda i, j: (0, 0)),
        ],
        out_specs=pl.BlockSpec((tm, d), lambda i, j: (i, 0)),
        out_shape=jax.ShapeDtypeStruct((t, d), F32),
        scratch_shapes=[pltpu.VMEM((tm, d), BF16)],
        compiler_params=pltpu.CompilerParams(
            dimension_semantics=("parallel", "arbitrary"), vmem_limit_bytes=VMEM_LIMIT),
        name="ffn",
    )(x, nw.reshape(1, d), wg, wu, wd, fw.reshape(1, d))


def _inproj_kernel(x_ref, nw_ref, w_ref, o_ref, u5_ref, xn_ref, *, tn):
    j = pl.program_id(1)

    @pl.when(j == 0)
    def _():
        x = x_ref[...]
        ms = jnp.mean(x * x, axis=-1, keepdims=True)
        xn_ref[...] = (x * lax.rsqrt(ms + RMS_EPS) * nw_ref[...]).astype(BF16)

    res = _dot(xn_ref[...], w_ref[...])
    o_ref[...] = res

    @pl.when(j == COL_S5_U // tn)
    def _():
        off = COL_S5_U % tn
        u5_ref[...] = res[:, off:off + S5_WIDTH]


def _inproj(x, nw, w, bsz, seq, *, tm=512, tn=1024):
    t, d = x.shape
    n = w.shape[1]
    nseg = seq // tm
    return pl.pallas_call(
        functools.partial(_inproj_kernel, tn=tn),
        grid=(t // tm, n // tn),
        in_specs=[
            pl.BlockSpec((tm, d), lambda i, j: (i, 0)),
            pl.BlockSpec((1, d), lambda i, j: (0, 0)),
            pl.BlockSpec((d, tn), lambda i, j: (0, j)),
        ],
        out_specs=[
            pl.BlockSpec((tm, tn), lambda i, j: (i, j)),
            pl.BlockSpec((tm, S5_WIDTH), lambda i, j: (i % nseg, i // nseg)),
        ],
        out_shape=[jax.ShapeDtypeStruct((t, n), F32),
                   jax.ShapeDtypeStruct((seq, bsz * S5_WIDTH), F32)],
        scratch_shapes=[pltpu.VMEM((tm, d), BF16)],
        compiler_params=pltpu.CompilerParams(
            dimension_semantics=("parallel", "arbitrary"), vmem_limit_bytes=VMEM_LIMIT),
        name="inproj",
    )(x, nw.reshape(1, d), w)


def _outproj_kernel(h_ref, a_ref, b_ref, c_ref, wa_ref, wb_ref, wc_ref, o_ref):
    acc = _dot(a_ref[...], wa_ref[...])
    acc += _dot(b_ref[...], wb_ref[...])
    acc += _dot(c_ref[...], wc_ref[...])
    o_ref[...] = h_ref[...] + acc


def _outproj(h, oa, ob, oc, wa, wb, wc, *, tm=512):
    t, d = h.shape
    nseg = oc.shape[0] // tm
    row = lambda i: (i, 0)
    fixed = lambda i: (0, 0)
    return pl.pallas_call(
        _outproj_kernel,
        grid=(t // tm,),
        in_specs=[
            pl.BlockSpec((tm, d), row),
            pl.BlockSpec((tm, oa.shape[1]), row),
            pl.BlockSpec((tm, ob.shape[1]), row),
            pl.BlockSpec((tm, S5_WIDTH), lambda i: (i % nseg, i // nseg)),
            pl.BlockSpec(wa.shape, fixed),
            pl.BlockSpec(wb.shape, fixed),
            pl.BlockSpec(wc.shape, fixed),
        ],
        out_specs=pl.BlockSpec((tm, d), row),
        out_shape=jax.ShapeDtypeStruct((t, d), F32),
        compiler_params=pltpu.CompilerParams(
            dimension_semantics=("parallel",), vmem_limit_bytes=VMEM_LIMIT),
        name="outproj",
    )(h, oa, ob, oc, wa, wb, wc)


def _chunk_rows(x_ref, lanes, ci, c):
    r0 = ci * c
    x = x_ref[pl.ds(pl.multiple_of(jnp.maximum(r0 - 8, 0), 8), c + 8), lanes]
    first = jnp.where(_iota(x.shape, 0) >= 8, pltpu.roll(x, 8, axis=0), 0.0)
    return jnp.where(ci == 0, first, x)


def _gdn_kernel(alog_ref, dtb_ref,
                q_ref, k_ref, v_ref, z_ref, gate_ref,
                cwq_ref, cwk_ref, cwv_ref, nw_ref,
                o_ref,
                u_ref, w_ref, qd_ref, kd_ref, at_ref, eg_ref,
                *, hb, p_inv, p_mix, unroll):
    hg = pl.program_id(1)
    seq = q_ref.shape[0]
    n_chunks = seq // CHUNK
    c = CHUNK
    mix_dt = BF16 if p_mix == 1 else F32

    row = _iota((c, c), 0)
    col = _iota((c, c), 1)
    tril = row >= col
    strict = row > col
    ltri = tril.astype(BF16)
    ones = jnp.ones((c, c), BF16)
    eye = (row == col).astype(F32)
    lane = _iota((c, LANE), 1)

    def prep(trip, carry):
        items = [(trip * unroll + cc, hh) for cc in range(unroll) for hh in range(hb)]
        lanes = [slice(hh * LANE, (hh + 1) * LANE) for _, hh in items]
        rows = [pl.ds(pl.multiple_of(ci * c, c), c) for ci, _ in items]

        def stage_a(ci, hh, ln, rw):
            h = hg * hb + hh
            neg_a = -jnp.exp(jnp.full((c, LANE), alog_ref[h], F32))
            dt_bias = jnp.full((c, LANE), dtb_ref[h], F32)
            q = _silu(_causal_conv(_chunk_rows(q_ref, ln, ci, c), cwq_ref.at[:, ln]))
            k = _silu(_causal_conv(_chunk_rows(k_ref, ln, ci, c), cwk_ref.at[:, ln]))
            v = _silu(_causal_conv(_chunk_rows(v_ref, ln, ci, c), cwv_ref.at[:, ln]))
            qn = q * lax.rsqrt(jnp.sum(q * q, axis=-1, keepdims=True) + RMS_EPS) * (GDN_D ** -0.5)
            kn = k * lax.rsqrt(jnp.sum(k * k, axis=-1, keepdims=True) + RMS_EPS)
            gt = gate_ref[rw, :]
            b_raw = jnp.sum(jnp.where(lane == h, gt, 0.0), axis=-1, keepdims=True)
            a_raw = jnp.sum(jnp.where(lane == h + GDN_HEADS, gt, 0.0), axis=-1, keepdims=True)
            beta = _sigmoid(b_raw)
            g_b = neg_a * _softplus(jnp.broadcast_to(a_raw, (c, LANE)) + dt_bias)
            return qn, kn, v, beta, g_b

        sa = [stage_a(ci, hh, ln, rw) for (ci, hh), ln, rw in zip(items, lanes, rows)]
        qn = [x[0] for x in sa]
        kn = [x[1] for x in sa]
        v = [x[2] for x in sa]
        beta = [x[3] for x in sa]
        g_b = [x[4] for x in sa]
        gi = [_mm01(ltri, g) for g in g_b]
        gj = [_mm01(ones, jnp.where(row <= col, g[:, :c], 0.0)) for g in g_b]
        kb = [k * b for k, b in zip(kn, beta)]
        kk = [_mm(a, b, NT, p_inv) for a, b in zip(kb, kn)]
        qk = [_mm(a, b, NT, p_mix) for a, b in zip(qn, kn)]
        decay = [jnp.where(tril, jnp.exp(jnp.where(tril, a[:, :c] - b, 0.0)), 0.0)
                 for a, b in zip(gi, gj)]
        npow = [-jnp.where(strict, a * d, 0.0) for a, d in zip(kk, decay)]
        tinv = [eye + n for n in npow]
        for _ in range(5):
            npow = [_mm(n, n, NN, p_inv) for n in npow]
            tinv = [t + _mm(t, n, NN, p_inv) for t, n in zip(tinv, npow)]
        exp_g = [jnp.exp(g) for g in gi]
        rhs = [jnp.concatenate([vv * b, k * e], axis=1) for vv, b, k, e in zip(v, beta, kb, exp_g)]
        uw = [_mm(t, r, NN, p_inv) for t, r in zip(tinv, rhs)]
        for i, ((ci, hh), rw) in enumerate(zip(items, rows)):
            g_last = jnp.broadcast_to(gi[i][c - 1:c, :], (c, LANE))
            u_ref[hh, rw, :] = uw[i][:, :LANE]
            w_ref[hh, rw, :] = uw[i][:, LANE:].astype(mix_dt)
            at_ref[hh, rw, :] = jnp.where(tril, qk[i] * decay[i], 0.0).astype(mix_dt)
            qd_ref[hh, rw, :] = (qn[i] * exp_g[i]).astype(mix_dt)
            kd_ref[hh, rw, :] = (kn[i] * jnp.exp(g_last - gi[i])).astype(mix_dt)
            eg_ref[hh, pl.ds(pl.multiple_of(ci * 8, 8), 8), :] = jnp.exp(g_last[0:8, :])
        return carry

    lax.fori_loop(0, n_chunks // unroll, prep, 0)

    norm_w = nw_ref[...]

    def scan(ci, states):
        rw = pl.ds(pl.multiple_of(ci * c, c), c)
        lhs = [jnp.concatenate([w_ref[hh, rw, :], qd_ref[hh, rw, :]], axis=0) for hh in range(hb)]
        ws_qs = [_mm(a, s, NN, p_mix) for a, s in zip(lhs, states)]
        v_new = [u_ref[hh, rw, :] - ws_qs[hh][:c] for hh in range(hb)]
        kv = [_mm(kd_ref[hh, rw, :], v_new[hh], TN, p_mix) for hh in range(hb)]
        av = [_mm(at_ref[hh, rw, :], v_new[hh], NN, p_mix) for hh in range(hb)]
        new_states = []
        for hh in range(hb):
            eg = eg_ref[hh, pl.ds(pl.multiple_of(ci * 8, 8), 8), :]
            new_states.append(states[hh] * jnp.broadcast_to(eg[0:1, :], (GDN_D, GDN_D)) + kv[hh])
            o = ws_qs[hh][c:] + av[hh]
            ms = jnp.mean(o * o, axis=-1, keepdims=True)
            ln = slice(hh * LANE, (hh + 1) * LANE)
            y = o * lax.rsqrt(ms + RMS_EPS) * norm_w * _silu(z_ref[rw, ln])
            o_ref[rw, ln] = y.astype(o_ref.dtype)
        return tuple(new_states)

    lax.fori_loop(0, n_chunks, scan, tuple(jnp.zeros((GDN_D, GDN_D), F32) for _ in range(hb)))


def _gdn(proj, bsz, seq, conv_w, a_log, dt_bias, norm_w, *, hb=3, p_inv=1, p_mix=1, unroll=2):
    t = proj.shape[0]
    wd = hb * LANE
    mix_dt = BF16 if p_mix == 1 else F32
    q0 = COL_GDN_QKV // wd
    z0 = COL_GDN_Z // wd
    g0 = COL_GDN_GATE // LANE
    ng = GDN_HEADS // hb
    blk = lambda off: pl.BlockSpec((seq, wd), lambda b, h, off=off: (b, off + h))
    cw = lambda off: pl.BlockSpec((CONV_K, wd), lambda b, h, off=off: (0, off + h))
    smem = pl.BlockSpec(memory_space=pltpu.SMEM)
    return pl.pallas_call(
        functools.partial(_gdn_kernel, hb=hb, p_inv=p_inv, p_mix=p_mix, unroll=unroll),
        grid=(bsz, ng),
        in_specs=[
            smem, smem,
            blk(q0), blk(q0 + ng), blk(q0 + 2 * ng), blk(z0),
            pl.BlockSpec((seq, LANE), lambda b, h: (b, g0)),
            cw(0), cw(ng), cw(2 * ng),
            pl.BlockSpec((1, LANE), lambda b, h: (0, 0)),
        ],
        out_specs=pl.BlockSpec((seq, wd), lambda b, h: (b, h)),
        out_shape=jax.ShapeDtypeStruct((t, GDN_QK), BF16),
        scratch_shapes=[
            pltpu.VMEM((hb, seq, LANE), F32),
            pltpu.VMEM((hb, seq, LANE), mix_dt),
            pltpu.VMEM((hb, seq, LANE), mix_dt),
            pltpu.VMEM((hb, seq, LANE), mix_dt),
            pltpu.VMEM((hb, seq, CHUNK), mix_dt),
            pltpu.VMEM((hb, seq // CHUNK * 8, LANE), F32),
        ],
        compiler_params=pltpu.CompilerParams(
            dimension_semantics=("parallel", "arbitrary"), vmem_limit_bytes=VMEM_LIMIT),
        name="gdn",
    )(a_log, dt_bias, proj, proj, proj, proj, proj, conv_w, conv_w, conv_w,
      norm_w.reshape(1, LANE))


def _ssd_kernel(z_ref, x_ref, b_ref, c_ref, dt_ref,
                cwx_ref, cwb_ref, cwc_ref,
                cbx_ref, cbb_ref, cbc_ref,
                alog_ref, dtb_ref, dsk_ref, nw_ref,
                o_ref,
                s_ref, *, passes, unroll):
    grp = pl.program_id(1)
    seq = x_ref.shape[0]
    n_chunks = seq // CHUNK
    c = CHUNK
    gw = SSD_GW

    row = _iota((c, c), 0)
    col = _iota((c, c), 1)
    ltri = (row >= col).astype(BF16)
    ones = jnp.ones((c, c), BF16)
    rowt = _iota((c, gw), 0)
    colt = jnp.bitwise_and(_iota((c, gw), 1), c - 1)
    tril_t = rowt >= colt
    upper_t = rowt <= colt
    expand = (_iota((LANE, gw), 0) == grp * SSD_HPG + lax.shift_right_logical(_iota((LANE, gw), 1), 6)).astype(BF16)
    last_row = (_iota((c, gw), 0) == c - 1)
    ones_cn = jnp.ones((c, SSD_N), BF16)

    neg_a = -jnp.exp(alog_ref[...])
    dt_bias = dtb_ref[...]
    d_skip = dsk_ref[...]
    norm_w = nw_ref[...]

    def conv(x_ref_, cw_ref, cb_ref, ci):
        return _silu(_causal_conv(_chunk_rows(x_ref_, slice(None), ci, c), cw_ref) + cb_ref[...])

    s_ref[...] = jnp.zeros((gw, SSD_N), F32)
    heads = [slice(hh * SSD_P, (hh + 1) * SSD_P) for hh in range(SSD_HPG)]

    def body(trip, carry):
        cis = [trip * unroll + cc for cc in range(unroll)]
        rows = [pl.ds(pl.multiple_of(ci * c, c), c) for ci in cis]
        xs = [conv(x_ref, cwx_ref, cbx_ref, ci) for ci in cis]
        bm = [conv(b_ref, cwb_ref, cbb_ref, ci) for ci in cis]
        cm = [conv(c_ref, cwc_ref, cbc_ref, ci) for ci in cis]
        dt = [_softplus(_mm01_t(dt_ref[rw, :], expand) + dt_bias) for rw in rows]
        a = [d * neg_a for d in dt]
        acum = [_mm01(ltri, x) for x in a]
        acum_j = [_mm01(ones, jnp.where(upper_t, x, 0.0)) for x in a]
        cb = [_mm(x, y, NT, passes) for x, y in zip(cm, bm)]
        lm = [jnp.where(tril_t, jnp.exp(jnp.where(tril_t, x - y, 0.0)), 0.0)
              for x, y in zip(acum, acum_j)]
        xdt = [x * d for x, d in zip(xs, dt)]
        y_diag = [jnp.concatenate([_mm(cb[i] * lm[i][:, sl], xdt[i][:, sl], NN, passes)
                                   for sl in heads], axis=1) for i in range(unroll)]
        a_last = [jnp.broadcast_to(x[c - 1:c, :], (c, gw)) for x in acum]
        states = [_mm(xdt[i] * jnp.exp(a_last[i] - acum[i]), bm[i], TN, passes)
                  for i in range(unroll)]
        dec_col = [jnp.exp(_mm01_t(jnp.where(last_row, x, 0.0), ones_cn, TN)) for x in acum]
        s_prev = s_ref[...]
        for i in range(unroll):
            y_off = _mm(cm[i], s_prev, NT, passes) * jnp.exp(acum[i])
            s_prev = s_prev * dec_col[i] + states[i]
            y = y_diag[i] + y_off + d_skip * xs[i]
            y = y * _silu(z_ref[rows[i], :])
            ms = jnp.mean(y * y, axis=-1, keepdims=True)
            o_ref[rows[i], :] = (y * lax.rsqrt(ms + RMS_EPS) * norm_w).astype(o_ref.dtype)
        s_ref[...] = s_prev
        return carry

    lax.fori_loop(0, n_chunks // unroll, body, 0)


def _ssd(proj, bsz, seq, conv_w, conv_b, a_log, dt_bias, d_skip, norm_w, *, passes=1, unroll=2):
    t = proj.shape[0]
    gw = SSD_GW
    rep = lambda v: jnp.repeat(v.astype(F32), SSD_P).reshape(1, SSD_INNER)
    cb = conv_b.reshape(1, -1)
    wide = lambda off: pl.BlockSpec((seq, gw), lambda b, g, off=off: (b, off + g))
    lane = lambda off: pl.BlockSpec((seq, LANE), lambda b, g, off=off: (b, off + g))
    chan = pl.BlockSpec((1, gw), lambda b, g: (0, g))
    return pl.pallas_call(
        functools.partial(_ssd_kernel, passes=passes, unroll=unroll),
        grid=(bsz, SSD_GROUPS),
        in_specs=[
            wide(COL_SSD_Z // gw), wide(COL_SSD_X // gw),
            lane(COL_SSD_B // LANE), lane(COL_SSD_C // LANE),
            pl.BlockSpec((seq, LANE), lambda b, g: (b, COL_SSD_DT // LANE)),
            pl.BlockSpec((CONV_K, gw), lambda b, g: (0, g)),
            pl.BlockSpec((CONV_K, LANE), lambda b, g: (0, SSD_INNER // LANE + g)),
            pl.BlockSpec((CONV_K, LANE), lambda b, g: (0, SSD_INNER // LANE + SSD_GROUPS + g)),
            pl.BlockSpec((1, gw), lambda b, g: (0, g)),
            pl.BlockSpec((1, LANE), lambda b, g: (0, SSD_INNER // LANE + g)),
            pl.BlockSpec((1, LANE), lambda b, g: (0, SSD_INNER // LANE + SSD_GROUPS + g)),
            chan, chan, chan, chan,
        ],
        out_specs=pl.BlockSpec((seq, gw), lambda b, g: (b, g)),
        out_shape=jax.ShapeDtypeStruct((t, SSD_INNER), BF16),
        scratch_shapes=[pltpu.VMEM((gw, SSD_N), F32)],
        compiler_params=pltpu.CompilerParams(
            dimension_semantics=("parallel", "arbitrary"), vmem_limit_bytes=VMEM_LIMIT),
        name="ssd",
    )(proj, proj, proj, proj, proj, conv_w, conv_w, conv_w, cb, cb, cb,
      rep(a_log), rep(dt_bias), rep(d_skip), norm_w.reshape(1, SSD_INNER))


def _s5_kernel(u_ref, wb_ref, wc_ref, are_ref, aim_ref, dsk_ref, gw_ref, gb_ref,
               o_ref, bu_ref, h_ref, *, bsz):
    ts = u_ref.shape[0] // bsz
    nt = S5_CH // LANE
    nb = S5_WIDTH // LANE
    per = nt // nb
    half = per * LANE

    @pl.when(pl.program_id(0) == 0)
    def _():
        h_ref[...] = jnp.zeros(h_ref.shape, F32)

    u = u_ref[...]
    u_bf = u.astype(BF16)
    for m in range(nb):
        res = _dot(u_bf[:, m * LANE:(m + 1) * LANE], wb_ref[m])
        for k in range(per):
            bu_ref[m * per + k] = res[:, k * LANE:(k + 1) * LANE]
            bu_ref[nt + m * per + k] = res[:, half + k * LANE:half + (k + 1) * LANE]

    def step(t, carry):
        rows = pl.ds(pl.multiple_of(t * bsz, bsz), bsz)
        new_re, new_im = [], []
        for k in range(nt):
            h_re, h_im = carry[k], carry[nt + k]
            a_re = are_ref[:, k * LANE:(k + 1) * LANE]
            a_im = aim_ref[:, k * LANE:(k + 1) * LANE]
            n_re = a_re * h_re - a_im * h_im + bu_ref[k, rows, :]
            n_im = a_re * h_im + a_im * h_re + bu_ref[nt + k, rows, :]
            bu_ref[k, rows, :] = n_re
            bu_ref[nt + k, rows, :] = n_im
            new_re.append(n_re)
            new_im.append(n_im)
        return tuple(new_re + new_im)

    h_last = lax.fori_loop(0, ts, step, tuple(h_ref[k] for k in range(2 * nt)))
    for k in range(2 * nt):
        h_ref[k] = h_last[k]

    y_parts = []
    for m in range(nb):
        hs = jnp.concatenate([bu_ref[m * per + k] for k in range(per)] +
                             [bu_ref[nt + m * per + k] for k in range(per)], axis=1)
        y_parts.append(_dot(hs.astype(BF16), wc_ref[m]))
    y = jnp.concatenate(y_parts, axis=1) + dsk_ref[...] * u
    g = 0.5 * y * (1.0 + jnp.tanh(math.sqrt(2.0 / math.pi) * (y + 0.044715 * (y * y * y))))
    out = g * _sigmoid(_dot(g.astype(BF16), gw_ref[...]) + gb_ref[...])
    o_ref[...] = out.astype(o_ref.dtype)


def _s5(u_tb, bsz, wb, wc, a_re, a_im, d_skip, glu_w, glu_b, *, ts=128):
    rows = u_tb.shape[0]
    blk = ts * bsz
    fixed = lambda shape: pl.BlockSpec(shape, lambda k: (0,) * len(shape))
    return pl.pallas_call(
        functools.partial(_s5_kernel, bsz=bsz),
        grid=(rows // blk,),
        in_specs=[
            pl.BlockSpec((blk, S5_WIDTH), lambda k: (k, 0)),
            fixed(wb.shape), fixed(wc.shape),
            fixed((1, S5_CH)), fixed((1, S5_CH)),
            fixed((1, S5_WIDTH)), fixed(glu_w.shape), fixed((1, S5_WIDTH)),
        ],
        out_specs=pl.BlockSpec((blk, S5_WIDTH), lambda k: (k, 0)),
        out_shape=jax.ShapeDtypeStruct((rows, S5_WIDTH), BF16),
        scratch_shapes=[
            pltpu.VMEM((2 * S5_CH // LANE, blk, LANE), F32),
            pltpu.VMEM((2 * S5_CH // LANE, bsz, LANE), F32),
        ],
        compiler_params=pltpu.CompilerParams(
            dimension_semantics=("arbitrary",), vmem_limit_bytes=VMEM_LIMIT),
        name="s5",
    )(u_tb, wb, wc, a_re.reshape(1, S5_CH), a_im.reshape(1, S5_CH),
      d_skip.reshape(1, S5_WIDTH), glu_w, glu_b.reshape(1, S5_WIDTH))


def _s5_params(a_re, a_im, b_re, b_im, c_re, c_im, log_dt):
    delta = jnp.exp(log_dt)[:, None]
    mag = jnp.exp(a_re * delta)
    ab_re, ab_im = mag * jnp.cos(a_im * delta), mag * jnp.sin(a_im * delta)
    den = a_re * a_re + a_im * a_im
    p_re, p_im = ab_re - 1.0, ab_im
    f_re = (p_re * a_re + p_im * a_im) / den
    f_im = (p_im * a_re - p_re * a_im) / den
    bb_re = f_re[..., None] * b_re - f_im[..., None] * b_im
    bb_im = f_re[..., None] * b_im + f_im[..., None] * b_re
    gpb = LANE // S5_GROUP
    nb = S5_GROUPS // gpb
    eye = jnp.eye(gpb, dtype=F32)
    emb_b = lambda bb: jnp.einsum('mgnj,gh->mgjhn', bb.reshape(nb, gpb, S5_STATE, S5_GROUP),
                                  eye).reshape(nb, LANE, gpb * S5_STATE)
    wb = jnp.concatenate([emb_b(bb_re), emb_b(bb_im)], axis=2)
    emb_c = lambda cc: jnp.einsum('mgin,gh->mgnhi', cc.reshape(nb, gpb, S5_GROUP, S5_STATE),
                                  eye).reshape(nb, gpb * S5_STATE, LANE)
    wc = jnp.concatenate([emb_c(c_re), -emb_c(c_im)], axis=1)
    return ab_re.reshape(-1), ab_im.reshape(-1), wb.astype(BF16), wc.astype(BF16)


def _pack_w_in(w_in):
    d = w_in.shape[0]
    w_in = w_in.astype(BF16)
    gdn_in = GDN_QKV + GDN_QK + 2 * GDN_HEADS
    ssd_conv = SSD_INNER + 2 * SSD_GROUPS * SSD_N
    g_qkv = w_in[:, 0:GDN_QKV]
    g_z = w_in[:, GDN_QKV:GDN_QKV + GDN_QK]
    g_gate = w_in[:, GDN_QKV + GDN_QK:gdn_in]
    s0 = gdn_in
    s_z = w_in[:, s0:s0 + SSD_INNER]
    s_x = w_in[:, s0 + SSD_INNER:s0 + 2 * SSD_INNER]
    s_b = w_in[:, s0 + 2 * SSD_INNER:s0 + 2 * SSD_INNER + SSD_GROUPS * SSD_N]
    s_c = w_in[:, s0 + 2 * SSD_INNER + SSD_GROUPS * SSD_N:s0 + SSD_INNER + ssd_conv]
    s_dt = w_in[:, s0 + SSD_INNER + ssd_conv:s0 + SSD_INNER + ssd_conv + SSD_HEADS]
    u0 = s0 + SSD_INNER + ssd_conv + SSD_HEADS
    s5_u = w_in[:, u0:u0 + S5_WIDTH]
    pad = lambda n: jnp.zeros((d, n), w_in.dtype)
    cols = [s_z, s_x, s5_u, s_b, s_c, s_dt, pad(LANE - SSD_HEADS), g_qkv, g_z,
            g_gate, pad(LANE - 2 * GDN_HEADS), pad(P_PAD - (COL_GDN_GATE + LANE))]
    return jnp.concatenate(cols, axis=1)


def kernel(x, ffn1_norm, ffn1_w_gate, ffn1_w_up, ffn1_w_down, mix_norm, w_in,
           gdn_conv_w, gdn_a_log, gdn_dt_bias, gdn_norm,
           ssd_conv_w, ssd_conv_b, ssd_a_log, ssd_dt_bias, ssd_d, ssd_norm,
           s5_a_re, s5_a_im, s5_b_re, s5_b_im, s5_c_re, s5_c_im, s5_d, s5_log_dt,
           s5_glu_w, s5_glu_b, w_out, ffn2_norm, ffn2_w_gate, ffn2_w_up, ffn2_w_down,
           final_norm):
    bsz, seq, d = x.shape
    depth = w_in.shape[0]
    h = x.reshape(bsz * seq, d)
    for i in range(depth):
        h = _ffn(h, ffn1_norm[i], ffn1_w_gate[i].astype(BF16), ffn1_w_up[i].astype(BF16),
                 ffn1_w_down[i].astype(BF16), final_norm, final_norm=False)
        proj, u_s5 = _inproj(h, mix_norm[i], _pack_w_in(w_in[i]), bsz, seq)
        o_gdn = _gdn(proj, bsz, seq, gdn_conv_w[i], gdn_a_log[i], gdn_dt_bias[i], gdn_norm[i])
        o_ssd = _ssd(proj, bsz, seq, ssd_conv_w[i], ssd_conv_b[i], ssd_a_log[i], ssd_dt_bias[i],
                     ssd_d[i], ssd_norm[i])
        ab_re, ab_im, wb, wc = _s5_params(s5_a_re[i], s5_a_im[i], s5_b_re[i], s5_b_im[i],
                                          s5_c_re[i], s5_c_im[i], s5_log_dt[i])
        o_s5 = _s5(u_s5.reshape(seq * bsz, S5_WIDTH), bsz, wb, wc, ab_re, ab_im, s5_d[i],
                   s5_glu_w[i].astype(BF16), s5_glu_b[i])
        wo = w_out[i].astype(BF16)
        h = _outproj(h, o_gdn, o_ssd, o_s5.reshape(seq, bsz * S5_WIDTH), wo[0:GDN_QK],
                     wo[GDN_QK:GDN_QK + SSD_INNER], wo[GDN_QK + SSD_INNER:])
        h = _ffn(h, ffn2_norm[i], ffn2_w_gate[i].astype(BF16), ffn2_w_up[i].astype(BF16),
                 ffn2_w_down[i].astype(BF16), final_norm, final_norm=(i == depth - 1))
    return h.reshape(bsz, seq, d)
```

```python
import functools
import math

import jax
import jax.numpy as jnp
from jax import lax
from jax.experimental import pallas as pl
from jax.experimental.pallas import tpu as pltpu

F32 = jnp.float32
BF16 = jnp.bfloat16

RMS_EPS = 1e-6
CHUNK = 64
CONV_K = 4
D_MODEL = 2048
D_FF = 5632

GDN_HEADS = 6
GDN_D = 128
GDN_QK = GDN_HEADS * GDN_D
GDN_QKV = 3 * GDN_QK

SSD_HEADS = 12
SSD_P = 64
SSD_GROUPS = 2
SSD_N = 128
SSD_HPG = SSD_HEADS // SSD_GROUPS
SSD_GW = SSD_HPG * SSD_P
SSD_INNER = SSD_HEADS * SSD_P

S5_WIDTH = 512
S5_GROUP = 16
S5_GROUPS = 32
S5_STATE = 64
S5_CH = S5_GROUPS * S5_STATE

LANE = 128

COL_SSD_Z = 0
COL_SSD_X = 768
COL_S5_U = 1536
COL_SSD_B = 2048
COL_SSD_C = 2304
COL_SSD_DT = 2560
COL_GDN_QKV = 2688
COL_GDN_Z = 4992
COL_GDN_GATE = 5760
P_PAD = 6144

VMEM_LIMIT = 60 * 1024 * 1024

NN = (((1,), (0,)), ((), ()))
NT = (((1,), (1,)), ((), ()))
TN = (((0,), (0,)), ((), ()))


def _dot(a, b, dims=NN):
    return lax.dot_general(a, b, dims, preferred_element_type=F32)


def _hi_lo(a):
    hi = a.astype(BF16)
    lo = (a - hi.astype(F32)).astype(BF16)
    return hi, lo


def _mm(a, b, dims=NN, passes=3):
    if passes == 1:
        return _dot(a.astype(BF16), b.astype(BF16), dims)
    ah, al = _hi_lo(a)
    bh, bl = _hi_lo(b)
    return _dot(ah, bh, dims) + (_dot(ah, bl, dims) + _dot(al, bh, dims))


def _mm01(m01, x, dims=NN):
    hi, lo = _hi_lo(x)
    return _dot(m01, hi, dims) + _dot(m01, lo, dims)


def _mm01_t(x, m01, dims=NN):
    hi, lo = _hi_lo(x)
    return _dot(hi, m01, dims) + _dot(lo, m01, dims)


def _sigmoid(x):
    return 1.0 / (1.0 + jnp.exp(-x))


def _silu(x):
    return x * _sigmoid(x)


def _softplus(x):
    return jnp.maximum(x, 0.0) + jnp.log1p(jnp.exp(-jnp.abs(x)))


def _iota(shape, dim):
    return lax.broadcasted_iota(jnp.int32, shape, dim)


def _causal_conv(x, cw_ref):
    acc = None
    for i in range(CONV_K):
        shift = CONV_K - 1 - i
        xs = x if shift == 0 else pltpu.roll(x, shift, axis=0)
        term = xs[8:] * cw_ref[i:i + 1, :]
        acc = term if acc is None else acc + term
    return acc


def _ffn_kernel(x_ref, nw_ref, wg_ref, wu_ref, wd_ref, fw_ref, o_ref, xn_ref, *, final_norm):
    j = pl.program_id(1)

    @pl.when(j == 0)
    def _():
        x = x_ref[...]
        ms = jnp.mean(x * x, axis=-1, keepdims=True)
        xn_ref[...] = (x * lax.rsqrt(ms + RMS_EPS) * nw_ref[...]).astype(BF16)
        o_ref[...] = x

    xn = xn_ref[...]
    g = _dot(xn, wg_ref[...])
    u = _dot(xn, wu_ref[...])
    a = (0.5 * _silu(g) * u).astype(BF16)
    o_ref[...] += _dot(a, wd_ref[...])

    if final_norm:
        @pl.when(j == pl.num_programs(1) - 1)
        def _():
            h = o_ref[...]
            ms = jnp.mean(h * h, axis=-1, keepdims=True)
            o_ref[...] = h * lax.rsqrt(ms + RMS_EPS) * fw_ref[...]


def _ffn(x, nw, wg, wu, wd, fw, *, final_norm, out_batched=None, tm=1024, tf=512):
    d = x.shape[-1]
    t = x.size // d
    f = wg.shape[1]
    if x.ndim == 3:
        nseg = x.shape[1] // tm
        x_spec = pl.BlockSpec((None, tm, d), lambda i, j: (i // nseg, i % nseg, 0))
    else:
        x_spec = pl.BlockSpec((tm, d), lambda i, j: (i, 0))
    if out_batched is None:
        out_spec = pl.BlockSpec((tm, d), lambda i, j: (i, 0))
        out_shape = jax.ShapeDtypeStruct((t, d), F32)
    else:
        oseg = out_batched[1] // tm
        out_spec = pl.BlockSpec((None, tm, d), lambda i, j: (i // oseg, i % oseg, 0))
        out_shape = jax.ShapeDtypeStruct((*out_batched, d), F32)
    return pl.pallas_call(
        functools.partial(_ffn_kernel, final_norm=final_norm),
        grid=(t // tm, f // tf),
        in_specs=[
            x_spec,
            pl.BlockSpec((1, d), lambda i, j: (0, 0)),
            pl.BlockSpec((d, tf), lambda i, j: (0, j)),
            pl.BlockSpec((d, tf), lambda i, j: (0, j)),
            pl.BlockSpec((tf, d), lambda i, j: (j, 0)),
            pl.BlockSpec((1, d), lambda i, j: (0, 0)),
        ],
        out_specs=out_spec,
        out_shape=out_shape,
        scratch_shapes=[pltpu.VMEM((tm, d), BF16)],
        compiler_params=pltpu.CompilerParams(
            dimension_semantics=("parallel", "arbitrary"), vmem_limit_bytes=VMEM_LIMIT),
        name="ffn",
    )(x, nw.reshape(1, d), wg, wu, wd, fw.reshape(1, d))


def _inproj_kernel(x_ref, nw_ref, w_ref, o_ref, u5_ref, xn_ref, *, tn):
    j = pl.program_id(1)

    @pl.when(j == 0)
    def _():
        x = x_ref[...]
        ms = jnp.mean(x * x, axis=-1, keepdims=True)
        xn_ref[...] = (x * lax.rsqrt(ms + RMS_EPS) * nw_ref[...]).astype(BF16)

    res = _dot(xn_ref[...], w_ref[...])
    o_ref[...] = res

    @pl.when(j == COL_S5_U // tn)
    def _():
        off = COL_S5_U % tn
        u5_ref[...] = res[:, off:off + S5_WIDTH]


def _inproj(x, nw, w, bsz, seq, *, tm=1024, tn=1024):
    t, d = x.shape
    n = w.shape[1]
    nseg = seq // tm
    return pl.pallas_call(
        functools.partial(_inproj_kernel, tn=tn),
        grid=(t // tm, n // tn),
        in_specs=[
            pl.BlockSpec((tm, d), lambda i, j: (i, 0)),
            pl.BlockSpec((1, d), lambda i, j: (0, 0)),
            pl.BlockSpec((d, tn), lambda i, j: (0, j)),
        ],
        out_specs=[
            pl.BlockSpec((tm, tn), lambda i, j: (i, j)),
            pl.BlockSpec((tm, S5_WIDTH), lambda i, j: (i % nseg, i // nseg)),
        ],
        out_shape=[jax.ShapeDtypeStruct((t, n), F32),
                   jax.ShapeDtypeStruct((seq, bsz * S5_WIDTH), F32)],
        scratch_shapes=[pltpu.VMEM((tm, d), BF16)],
        compiler_params=pltpu.CompilerParams(
            dimension_semantics=("parallel", "arbitrary"), vmem_limit_bytes=VMEM_LIMIT),
        name="inproj",
    )(x, nw.reshape(1, d), w)


def _outproj_kernel(h_ref, a_ref, b_ref, c_ref, wa_ref, wb_ref, wc_ref, o_ref):
    acc = _dot(a_ref[...], wa_ref[...])
    acc += _dot(b_ref[...], wb_ref[...])
    acc += _dot(c_ref[...], wc_ref[...])
    o_ref[...] = h_ref[...] + acc


def _outproj(h, oa, ob, oc, wa, wb, wc, *, tm=512):
    t, d = h.shape
    nseg = oc.shape[0] // tm
    row = lambda i: (i, 0)
    fixed = lambda i: (0, 0)
    return pl.pallas_call(
        _outproj_kernel,
        grid=(t // tm,),
        in_specs=[
            pl.BlockSpec((tm, d), row),
            pl.BlockSpec((tm, oa.shape[1]), row),
            pl.BlockSpec((tm, ob.shape[1]), row),
            pl.BlockSpec((tm, S5_WIDTH), lambda i: (i % nseg, i // nseg)),
            pl.BlockSpec(wa.shape, fixed),
            pl.BlockSpec(wb.shape, fixed),
            pl.BlockSpec(wc.shape, fixed),
        ],
        out_specs=pl.BlockSpec((tm, d), row),
        out_shape=jax.ShapeDtypeStruct((t, d), F32),
        compiler_params=pltpu.CompilerParams(
            dimension_semantics=("parallel",), vmem_limit_bytes=VMEM_LIMIT),
        name="outproj",
    )(h, oa, ob, oc, wa, wb, wc)


def _chunk_rows(x_ref, lanes, ci, c):
    r0 = ci * c
    x = x_ref[pl.ds(pl.multiple_of(jnp.maximum(r0 - 8, 0), 8), c + 8), lanes]
    first = jnp.where(_iota(x.shape, 0) >= 8, pltpu.roll(x, 8, axis=0), 0.0)
    return jnp.where(ci == 0, first, x)


def _gdn_kernel(alog_ref, dtb_ref,
                q_ref, k_ref, v_ref, z_ref, gate_ref,
                cwq_ref, cwk_ref, cwv_ref, nw_ref,
                o_ref,
                u_ref, w_ref, qd_ref, kd_ref, at_ref, eg_ref,
                *, hb, p_inv, p_mix, unroll):
    hg = pl.program_id(1)
    seq = q_ref.shape[0]
    n_chunks = seq // CHUNK
    c = CHUNK
    mix_dt = BF16 if p_mix == 1 else F32

    row = _iota((c, c), 0)
    col = _iota((c, c), 1)
    tril = row >= col
    strict = row > col
    ltri = tril.astype(BF16)
    ones = jnp.ones((c, c), BF16)
    eye = (row == col).astype(F32)
    lane = _iota((c, LANE), 1)

    def prep(trip, carry):
        items = [(trip * unroll + cc, hh) for cc in range(unroll) for hh in range(hb)]
        lanes = [slice(hh * LANE, (hh + 1) * LANE) for _, hh in items]
        rows = [pl.ds(pl.multiple_of(ci * c, c), c) for ci, _ in items]

        def stage_a(ci, hh, ln, rw):
            h = hg * hb + hh
            neg_a = -jnp.exp(jnp.full((c, LANE), alog_ref[h], F32))
            dt_bias = jnp.full((c, LANE), dtb_ref[h], F32)
            q = _silu(_causal_conv(_chunk_rows(q_ref, ln, ci, c), cwq_ref.at[:, ln]))
            k = _silu(_causal_conv(_chunk_rows(k_ref, ln, ci, c), cwk_ref.at[:, ln]))
            v = _silu(_causal_conv(_chunk_rows(v_ref, ln, ci, c), cwv_ref.at[:, ln]))
            qn = q * lax.rsqrt(jnp.sum(q * q, axis=-1, keepdims=True) + RMS_EPS) * (GDN_D ** -0.5)
            kn = k * lax.rsqrt(jnp.sum(k * k, axis=-1, keepdims=True) + RMS_EPS)
            gt = gate_ref[rw, :]
            b_raw = jnp.sum(jnp.where(lane == h, gt, 0.0), axis=-1, keepdims=True)
            a_raw = jnp.sum(jnp.where(lane == h + GDN_HEADS, gt, 0.0), axis=-1, keepdims=True)
            beta = _sigmoid(b_raw)
            g_b = neg_a * _softplus(jnp.broadcast_to(a_raw, (c, LANE)) + dt_bias)
            return qn, kn, v, beta, g_b

        sa = [stage_a(ci, hh, ln, rw) for (ci, hh), ln, rw in zip(items, lanes, rows)]
        qn = [x[0] for x in sa]
        kn = [x[1] for x in sa]
        v = [x[2] for x in sa]
        beta = [x[3] for x in sa]
        g_b = [x[4] for x in sa]
        gi = [_mm01(ltri, g) for g in g_b]
        gj = [_mm01(ones, jnp.where(row <= col, g[:, :c], 0.0)) for g in g_b]
        kb = [k * b for k, b in zip(kn, beta)]
        kk = [_mm(a, b, NT, p_inv) for a, b in zip(kb, kn)]
        qk = [_mm(a, b, NT, p_mix) for a, b in zip(qn, kn)]
        decay = [jnp.where(tril, jnp.exp(jnp.where(tril, a[:, :c] - b, 0.0)), 0.0)
                 for a, b in zip(gi, gj)]
        npow = [-jnp.where(strict, a * d, 0.0) for a, d in zip(kk, decay)]
        tinv = [eye + n for n in npow]
        for _ in range(5):
            npow = [_mm(n, n, NN, p_inv) for n in npow]
            tinv = [t + _mm(t, n, NN, p_inv) for t, n in zip(tinv, npow)]
        exp_g = [jnp.exp(g) for g in gi]
        rhs = [jnp.concatenate([vv * b, k * e], axis=1) for vv, b, k, e in zip(v, beta, kb, exp_g)]
        uw = [_mm(t, r, NN, p_inv) for t, r in zip(tinv, rhs)]
        for i, ((ci, hh), rw) in enumerate(zip(items, rows)):
            g_last = jnp.broadcast_to(gi[i][c - 1:c, :], (c, LANE))
            u_ref[hh, rw, :] = uw[i][:, :LANE]
            w_ref[hh, rw, :] = uw[i][:, LANE:].astype(mix_dt)
            at_ref[hh, rw, :] = jnp.where(tril, qk[i] * decay[i], 0.0).astype(mix_dt)
            qd_ref[hh, rw, :] = (qn[i] * exp_g[i]).astype(mix_dt)
            kd_ref[hh, rw, :] = (kn[i] * jnp.exp(g_last - gi[i])).astype(mix_dt)
            eg_ref[hh, pl.ds(pl.multiple_of(ci * 8, 8), 8), :] = jnp.exp(g_last[0:8, :])
        return carry

    lax.fori_loop(0, n_chunks // unroll, prep, 0)

    norm_w = nw_ref[...]

    def scan(ci, states):
        rw = pl.ds(pl.multiple_of(ci * c, c), c)
        lhs = [jnp.concatenate([w_ref[hh, rw, :], qd_ref[hh, rw, :]], axis=0) for hh in range(hb)]
        ws_qs = [_mm(a, s, NN, p_mix) for a, s in zip(lhs, states)]
        v_new = [u_ref[hh, rw, :] - ws_qs[hh][:c] for hh in range(hb)]
        kv = [_mm(kd_ref[hh, rw, :], v_new[hh], TN, p_mix) for hh in range(hb)]
        av = [_mm(at_ref[hh, rw, :], v_new[hh], NN, p_mix) for hh in range(hb)]
        new_states = []
        for hh in range(hb):
            eg = eg_ref[hh, pl.ds(pl.multiple_of(ci * 8, 8), 8), :]
            new_states.append(states[hh] * jnp.broadcast_to(eg[0:1, :], (GDN_D, GDN_D)) + kv[hh])
            o = ws_qs[hh][c:] + av[hh]
            ms = jnp.mean(o * o, axis=-1, keepdims=True)
            ln = slice(hh * LANE, (hh + 1) * LANE)
            y = o * lax.rsqrt(ms + RMS_EPS) * norm_w * _silu(z_ref[rw, ln])
            o_ref[rw, ln] = y.astype(o_ref.dtype)
        return tuple(new_states)

    lax.fori_loop(0, n_chunks, scan, tuple(jnp.zeros((GDN_D, GDN_D), F32) for _ in range(hb)))


def _gdn(proj, bsz, seq, conv_w, a_log, dt_bias, norm_w, *, hb=3, p_inv=1, p_mix=1, unroll=4):
    t = proj.shape[0]
    wd = hb * LANE
    mix_dt = BF16 if p_mix == 1 else F32
    q0 = COL_GDN_QKV // wd
    z0 = COL_GDN_Z // wd
    g0 = COL_GDN_GATE // LANE
    ng = GDN_HEADS // hb
    blk = lambda off: pl.BlockSpec((seq, wd), lambda b, h, off=off: (b, off + h))
    cw = lambda off: pl.BlockSpec((CONV_K, wd), lambda b, h, off=off: (0, off + h))
    smem = pl.BlockSpec(memory_space=pltpu.SMEM)
    return pl.pallas_call(
        functools.partial(_gdn_kernel, hb=hb, p_inv=p_inv, p_mix=p_mix, unroll=unroll),
        grid=(bsz, ng),
        in_specs=[
            smem, smem,
            blk(q0), blk(q0 + ng), blk(q0 + 2 * ng), blk(z0),
            pl.BlockSpec((seq, LANE), lambda b, h: (b, g0)),
            cw(0), cw(ng), cw(2 * ng),
            pl.BlockSpec((1, LANE), lambda b, h: (0, 0)),
        ],
        out_specs=pl.BlockSpec((seq, wd), lambda b, h: (b, h)),
        out_shape=jax.ShapeDtypeStruct((t, GDN_QK), BF16),
        scratch_shapes=[
            pltpu.VMEM((hb, seq, LANE), F32),
            pltpu.VMEM((hb, seq, LANE), mix_dt),
            pltpu.VMEM((hb, seq, LANE), mix_dt),
            pltpu.VMEM((hb, seq, LANE), mix_dt),
            pltpu.VMEM((hb, seq, CHUNK), mix_dt),
            pltpu.VMEM((hb, seq // CHUNK * 8, LANE), F32),
        ],
        compiler_params=pltpu.CompilerParams(
            dimension_semantics=("parallel", "arbitrary"), vmem_limit_bytes=VMEM_LIMIT),
        name="gdn",
    )(a_log, dt_bias, proj, proj, proj, proj, proj, conv_w, conv_w, conv_w,
      norm_w.reshape(1, LANE))


def _ssd_kernel(z_ref, x_ref, b_ref, c_ref, dt_ref,
                cwx_ref, cwb_ref, cwc_ref,
                cbx_ref, cbb_ref, cbc_ref,
                alog_ref, dtb_ref, dsk_ref, nw_ref,
                o_ref,
                s_ref, *, passes, unroll):
    grp = pl.program_id(1)
    seq = x_ref.shape[0]
    n_chunks = seq // CHUNK
    c = CHUNK
    gw = SSD_GW

    row = _iota((c, c), 0)
    col = _iota((c, c), 1)
    ltri = (row >= col).astype(BF16)
    ones = jnp.ones((c, c), BF16)
    rowt = _iota((c, gw), 0)
    colt = jnp.bitwise_and(_iota((c, gw), 1), c - 1)
    tril_t = rowt >= colt
    upper_t = rowt <= colt
    expand = (_iota((LANE, gw), 0) == grp * SSD_HPG + lax.shift_right_logical(_iota((LANE, gw), 1), 6)).astype(BF16)
    last_row = (_iota((c, gw), 0) == c - 1)
    ones_cn = jnp.ones((c, SSD_N), BF16)

    neg_a = -jnp.exp(alog_ref[...])
    dt_bias = dtb_ref[...]
    d_skip = dsk_ref[...]
    norm_w = nw_ref[...]

    def conv(x_ref_, cw_ref, cb_ref, ci):
        return _silu(_causal_conv(_chunk_rows(x_ref_, slice(None), ci, c), cw_ref) + cb_ref[...])

    s_ref[...] = jnp.zeros((gw, SSD_N), F32)
    heads = [slice(hh * SSD_P, (hh + 1) * SSD_P) for hh in range(SSD_HPG)]

    def body(trip, carry):
        cis = [trip * unroll + cc for cc in range(unroll)]
        rows = [pl.ds(pl.multiple_of(ci * c, c), c) for ci in cis]
        xs = [conv(x_ref, cwx_ref, cbx_ref, ci) for ci in cis]
        bm = [conv(b_ref, cwb_ref, cbb_ref, ci) for ci in cis]
        cm = [conv(c_ref, cwc_ref, cbc_ref, ci) for ci in cis]
        dt = [_softplus(_mm01_t(dt_ref[rw, :], expand) + dt_bias) for rw in rows]
        a = [d * neg_a for d in dt]
        acum = [_mm01(ltri, x) for x in a]
        acum_j = [_mm01(ones, jnp.where(upper_t, x, 0.0)) for x in a]
        cb = [_mm(x, y, NT, passes) for x, y in zip(cm, bm)]
        lm = [jnp.where(tril_t, jnp.exp(jnp.where(tril_t, x - y, 0.0)), 0.0)
              for x, y in zip(acum, acum_j)]
        xdt = [x * d for x, d in zip(xs, dt)]
        y_diag = [jnp.concatenate([_mm(cb[i] * lm[i][:, sl], xdt[i][:, sl], NN, passes)
                                   for sl in heads], axis=1) for i in range(unroll)]
        a_last = [jnp.broadcast_to(x[c - 1:c, :], (c, gw)) for x in acum]
        states = [_mm(xdt[i] * jnp.exp(a_last[i] - acum[i]), bm[i], TN, passes)
                  for i in range(unroll)]
        dec_col = [jnp.exp(_mm01_t(jnp.where(last_row, x, 0.0), ones_cn, TN)) for x in acum]
        s_prev = s_ref[...]
        for i in range(unroll):
            y_off = _mm(cm[i], s_prev, NT, passes) * jnp.exp(acum[i])
            s_prev = s_prev * dec_col[i] + states[i]
            y = y_diag[i] + y_off + d_skip * xs[i]
            y = y * _silu(z_ref[rows[i], :])
            ms = jnp.mean(y * y, axis=-1, keepdims=True)
            o_ref[rows[i], :] = (y * lax.rsqrt(ms + RMS_EPS) * norm_w).astype(o_ref.dtype)
        s_ref[...] = s_prev
        return carry

    lax.fori_loop(0, n_chunks // unroll, body, 0)


def _ssd(proj, bsz, seq, conv_w, conv_b, a_log, dt_bias, d_skip, norm_w, *, passes=1, unroll=4):
    t = proj.shape[0]
    gw = SSD_GW
    rep = lambda v: jnp.repeat(v.astype(F32), SSD_P).reshape(1, SSD_INNER)
    cb = conv_b.reshape(1, -1)
    wide = lambda off: pl.BlockSpec((seq, gw), lambda b, g, off=off: (b, off + g))
    lane = lambda off: pl.BlockSpec((seq, LANE), lambda b, g, off=off: (b, off + g))
    chan = pl.BlockSpec((1, gw), lambda b, g: (0, g))
    return pl.pallas_call(
        functools.partial(_ssd_kernel, passes=passes, unroll=unroll),
        grid=(bsz, SSD_GROUPS),
        in_specs=[
            wide(COL_SSD_Z // gw), wide(COL_SSD_X // gw),
            lane(COL_SSD_B // LANE), lane(COL_SSD_C // LANE),
            pl.BlockSpec((seq, LANE), lambda b, g: (b, COL_SSD_DT // LANE)),
            pl.BlockSpec((CONV_K, gw), lambda b, g: (0, g)),
            pl.BlockSpec((CONV_K, LANE), lambda b, g: (0, SSD_INNER // LANE + g)),
            pl.BlockSpec((CONV_K, LANE), lambda b, g: (0, SSD_INNER // LANE + SSD_GROUPS + g)),
            pl.BlockSpec((1, gw), lambda b, g: (0, g)),
            pl.BlockSpec((1, LANE), lambda b, g: (0, SSD_INNER // LANE + g)),
            pl.BlockSpec((1, LANE), lambda b, g: (0, SSD_INNER // LANE + SSD_GROUPS + g)),
            chan, chan, chan, chan,
        ],
        out_specs=pl.BlockSpec((seq, gw), lambda b, g: (b, g)),
        out_shape=jax.ShapeDtypeStruct((t, SSD_INNER), BF16),
        scratch_shapes=[pltpu.VMEM((gw, SSD_N), F32)],
        compiler_params=pltpu.CompilerParams(
            dimension_semantics=("parallel", "arbitrary"), vmem_limit_bytes=VMEM_LIMIT),
        name="ssd",
    )(proj, proj, proj, proj, proj, conv_w, conv_w, conv_w, cb, cb, cb,
      rep(a_log), rep(dt_bias), rep(d_skip), norm_w.reshape(1, SSD_INNER))


def _s5_kernel(u_ref, wb_ref, wc_ref, are_ref, aim_ref, dsk_ref, gw_ref, gb_ref,
               o_ref, bu_ref, h_ref, *, bsz):
    ts = u_ref.shape[0] // bsz
    nt = S5_CH // LANE
    nb = S5_WIDTH // LANE
    per = nt // nb
    half = per * LANE

    @pl.when(pl.program_id(0) == 0)
    def _():
        h_ref[...] = jnp.zeros(h_ref.shape, F32)

    u = u_ref[...]
    u_bf = u.astype(BF16)
    for m in range(nb):
        res = _dot(u_bf[:, m * LANE:(m + 1) * LANE], wb_ref[m])
        for k in range(per):
            bu_ref[m * per + k] = res[:, k * LANE:(k + 1) * LANE]
            bu_ref[nt + m * per + k] = res[:, half + k * LANE:half + (k + 1) * LANE]

    def step(t, carry):
        rows = pl.ds(pl.multiple_of(t * bsz, bsz), bsz)
        new_re, new_im = [], []
        for k in range(nt):
            h_re, h_im = carry[k], carry[nt + k]
            a_re = are_ref[:, k * LANE:(k + 1) * LANE]
            a_im = aim_ref[:, k * LANE:(k + 1) * LANE]
            n_re = a_re * h_re - a_im * h_im + bu_ref[k, rows, :]
            n_im = a_re * h_im + a_im * h_re + bu_ref[nt + k, rows, :]
            bu_ref[k, rows, :] = n_re
            bu_ref[nt + k, rows, :] = n_im
            new_re.append(n_re)
            new_im.append(n_im)
        return tuple(new_re + new_im)

    h_last = lax.fori_loop(0, ts, step, tuple(h_ref[k] for k in range(2 * nt)))
    for k in range(2 * nt):
        h_ref[k] = h_last[k]

    y_parts = []
    for m in range(nb):
        hs = jnp.concatenate([bu_ref[m * per + k] for k in range(per)] +
                             [bu_ref[nt + m * per + k] for k in range(per)], axis=1)
        y_parts.append(_dot(hs.astype(BF16), wc_ref[m]))
    y = jnp.concatenate(y_parts, axis=1) + dsk_ref[...] * u
    g = 0.5 * y * (1.0 + jnp.tanh(math.sqrt(2.0 / math.pi) * (y + 0.044715 * (y * y * y))))
    out = g * _sigmoid(_dot(g.astype(BF16), gw_ref[...]) + gb_ref[...])
    o_ref[...] = out.astype(o_ref.dtype)


def _s5(u_tb, bsz, wb, wc, a_re, a_im, d_skip, glu_w, glu_b, *, ts=128):
    rows = u_tb.shape[0]
    blk = ts * bsz
    fixed = lambda shape: pl.BlockSpec(shape, lambda k: (0,) * len(shape))
    return pl.pallas_call(
        functools.partial(_s5_kernel, bsz=bsz),
        grid=(rows // blk,),
        in_specs=[
            pl.BlockSpec((blk, S5_WIDTH), lambda k: (k, 0)),
            fixed(wb.shape), fixed(wc.shape),
            fixed((1, S5_CH)), fixed((1, S5_CH)),
            fixed((1, S5_WIDTH)), fixed(glu_w.shape), fixed((1, S5_WIDTH)),
        ],
        out_specs=pl.BlockSpec((blk, S5_WIDTH), lambda k: (k, 0)),
        out_shape=jax.ShapeDtypeStruct((rows, S5_WIDTH), BF16),
        scratch_shapes=[
            pltpu.VMEM((2 * S5_CH // LANE, blk, LANE), F32),
            pltpu.VMEM((2 * S5_CH // LANE, bsz, LANE), F32),
        ],
        compiler_params=pltpu.CompilerParams(
            dimension_semantics=("arbitrary",), vmem_limit_bytes=VMEM_LIMIT),
        name="s5",
    )(u_tb, wb, wc, a_re.reshape(1, S5_CH), a_im.reshape(1, S5_CH),
      d_skip.reshape(1, S5_WIDTH), glu_w, glu_b.reshape(1, S5_WIDTH))


def _s5_params(a_re, a_im, b_re, b_im, c_re, c_im, log_dt):
    delta = jnp.exp(log_dt)[:, None]
    mag = jnp.exp(a_re * delta)
    ab_re, ab_im = mag * jnp.cos(a_im * delta), mag * jnp.sin(a_im * delta)
    den = a_re * a_re + a_im * a_im
    p_re, p_im = ab_re - 1.0, ab_im
    f_re = (p_re * a_re + p_im * a_im) / den
    f_im = (p_im * a_re - p_re * a_im) / den
    bb_re = f_re[..., None] * b_re - f_im[..., None] * b_im
    bb_im = f_re[..., None] * b_im + f_im[..., None] * b_re
    gpb = LANE // S5_GROUP
    nb = S5_GROUPS // gpb
    eye = jnp.eye(gpb, dtype=F32)
    emb_b = lambda bb: jnp.einsum('mgnj,gh->mgjhn', bb.reshape(nb, gpb, S5_STATE, S5_GROUP),
                                  eye).reshape(nb, LANE, gpb * S5_STATE)
    wb = jnp.concatenate([emb_b(bb_re), emb_b(bb_im)], axis=2)
    emb_c = lambda cc: jnp.einsum('mgin,gh->mgnhi', cc.reshape(nb, gpb, S5_GROUP, S5_STATE),
                                  eye).reshape(nb, gpb * S5_STATE, LANE)
    wc = jnp.concatenate([emb_c(c_re), -emb_c(c_im)], axis=1)
    return ab_re.reshape(-1), ab_im.reshape(-1), wb.astype(BF16), wc.astype(BF16)


def _pack_w_in(w_in):
    d = w_in.shape[0]
    w_in = w_in.astype(BF16)
    gdn_in = GDN_QKV + GDN_QK + 2 * GDN_HEADS
    ssd_conv = SSD_INNER + 2 * SSD_GROUPS * SSD_N
    g_qkv = w_in[:, 0:GDN_QKV]
    g_z = w_in[:, GDN_QKV:GDN_QKV + GDN_QK]
    g_gate = w_in[:, GDN_QKV + GDN_QK:gdn_in]
    s0 = gdn_in
    s_z = w_in[:, s0:s0 + SSD_INNER]
    s_x = w_in[:, s0 + SSD_INNER:s0 + 2 * SSD_INNER]
    s_b = w_in[:, s0 + 2 * SSD_INNER:s0 + 2 * SSD_INNER + SSD_GROUPS * SSD_N]
    s_c = w_in[:, s0 + 2 * SSD_INNER + SSD_GROUPS * SSD_N:s0 + SSD_INNER + ssd_conv]
    s_dt = w_in[:, s0 + SSD_INNER + ssd_conv:s0 + SSD_INNER + ssd_conv + SSD_HEADS]
    u0 = s0 + SSD_INNER + ssd_conv + SSD_HEADS
    s5_u = w_in[:, u0:u0 + S5_WIDTH]
    pad = lambda n: jnp.zeros((d, n), w_in.dtype)
    cols = [s_z, s_x, s5_u, s_b, s_c, s_dt, pad(LANE - SSD_HEADS), g_qkv, g_z,
            g_gate, pad(LANE - 2 * GDN_HEADS), pad(P_PAD - (COL_GDN_GATE + LANE))]
    return jnp.concatenate(cols, axis=1)


def kernel(x, ffn1_norm, ffn1_w_gate, ffn1_w_up, ffn1_w_down, mix_norm, w_in,
           gdn_conv_w, gdn_a_log, gdn_dt_bias, gdn_norm,
           ssd_conv_w, ssd_conv_b, ssd_a_log, ssd_dt_bias, ssd_d, ssd_norm,
           s5_a_re, s5_a_im, s5_b_re, s5_b_im, s5_c_re, s5_c_im, s5_d, s5_log_dt,
           s5_glu_w, s5_glu_b, w_out, ffn2_norm, ffn2_w_gate, ffn2_w_up, ffn2_w_down,
           final_norm):
    bsz, seq, d = x.shape
    depth = w_in.shape[0]
    h = x
    for i in range(depth):
        h = _ffn(h, ffn1_norm[i], ffn1_w_gate[i].astype(BF16), ffn1_w_up[i].astype(BF16),
                 ffn1_w_down[i].astype(BF16), final_norm, final_norm=False)
        proj, u_s5 = _inproj(h, mix_norm[i], _pack_w_in(w_in[i]), bsz, seq)
        o_gdn = _gdn(proj, bsz, seq, gdn_conv_w[i], gdn_a_log[i], gdn_dt_bias[i], gdn_norm[i])
        o_ssd = _ssd(proj, bsz, seq, ssd_conv_w[i], ssd_conv_b[i], ssd_a_log[i], ssd_dt_bias[i],
                     ssd_d[i], ssd_norm[i])
        ab_re, ab_im, wb, wc = _s5_params(s5_a_re[i], s5_a_im[i], s5_b_re[i], s5_b_im[i],
                                          s5_c_re[i], s5_c_im[i], s5_log_dt[i])
        o_s5 = _s5(u_s5.reshape(seq * bsz, S5_WIDTH), bsz, wb, wc, ab_re, ab_im, s5_d[i],
                   s5_glu_w[i].astype(BF16), s5_glu_b[i])
        wo = w_out[i].astype(BF16)
        h = _outproj(h, o_gdn, o_ssd, o_s5.reshape(seq, bsz * S5_WIDTH), wo[0:GDN_QK],
                     wo[GDN_QK:GDN_QK + SSD_INNER], wo[GDN_QK + SSD_INNER:])
        last = i == depth - 1
        h = _ffn(h, ffn2_norm[i], ffn2_w_gate[i].astype(BF16), ffn2_w_up[i].astype(BF16),
                 ffn2_w_down[i].astype(BF16), final_norm, final_norm=last,
                 out_batched=(bsz, seq) if last else None)
    return h
```

```python
import functools
import math

import jax
import jax.numpy as jnp
from jax import lax
from jax.experimental import pallas as pl
from jax.experimental.pallas import tpu as pltpu

F32 = jnp.float32
BF16 = jnp.bfloat16

RMS_EPS = 1e-6
CHUNK = 64
CONV_K = 4
D_MODEL = 2048
D_FF = 5632

GDN_HEADS = 6
GDN_D = 128
GDN_QK = GDN_HEADS * GDN_D
GDN_QKV = 3 * GDN_QK

SSD_HEADS = 12
SSD_P = 64
SSD_GROUPS = 2
SSD_N = 128
SSD_HPG = SSD_HEADS // SSD_GROUPS
SSD_GW = SSD_HPG * SSD_P
SSD_INNER = SSD_HEADS * SSD_P

S5_WIDTH = 512
S5_GROUP = 16
S5_GROUPS = 32
S5_STATE = 64
S5_CH = S5_GROUPS * S5_STATE

LANE = 128

COL_SSD_Z = 0
COL_SSD_X = 768
COL_S5_U = 1536
COL_SSD_B = 2048
COL_SSD_C = 2304
COL_SSD_DT = 2560
COL_GDN_QKV = 2688
COL_GDN_Z = 4992
COL_GDN_GATE = 5760
P_PAD = 6144

VMEM_LIMIT = 60 * 1024 * 1024

NN = (((1,), (0,)), ((), ()))
NT = (((1,), (1,)), ((), ()))
TN = (((0,), (0,)), ((), ()))


def _dot(a, b, dims=NN):
    return lax.dot_general(a, b, dims, preferred_element_type=F32)


def _hi_lo(a):
    hi = a.astype(BF16)
    lo = (a - hi.astype(F32)).astype(BF16)
    return hi, lo


def _mm(a, b, dims=NN, passes=3):
    if passes == 1:
        return _dot(a.astype(BF16), b.astype(BF16), dims)
    ah, al = _hi_lo(a)
    bh, bl = _hi_lo(b)
    return _dot(ah, bh, dims) + (_dot(ah, bl, dims) + _dot(al, bh, dims))


def _mm01(m01, x, dims=NN):
    hi, lo = _hi_lo(x)
    return _dot(m01, hi, dims) + _dot(m01, lo, dims)


def _mm01_t(x, m01, dims=NN):
    hi, lo = _hi_lo(x)
    return _dot(hi, m01, dims) + _dot(lo, m01, dims)


def _sigmoid(x):
    return 1.0 / (1.0 + jnp.exp(-x))


def _silu(x):
    return x * _sigmoid(x)


def _softplus(x):
    return jnp.maximum(x, 0.0) + jnp.log1p(jnp.exp(-jnp.abs(x)))


def _iota(shape, dim):
    return lax.broadcasted_iota(jnp.int32, shape, dim)


def _causal_conv(x, cw_ref):
    acc = None
    for i in range(CONV_K):
        shift = CONV_K - 1 - i
        xs = x if shift == 0 else pltpu.roll(x, shift, axis=0)
        term = xs[8:] * cw_ref[i:i + 1, :]
        acc = term if acc is None else acc + term
    return acc


def _ffn_kernel(x_ref, nw_ref, wg_ref, wu_ref, wd_ref, fw_ref, o_ref, xn_ref, *, final_norm):
    j = pl.program_id(1)

    @pl.when(j == 0)
    def _():
        x = x_ref[...]
        ms = jnp.mean(x * x, axis=-1, keepdims=True)
        xn_ref[...] = (x * lax.rsqrt(ms + RMS_EPS) * nw_ref[...]).astype(BF16)
        o_ref[...] = x

    xn = xn_ref[...]
    g = _dot(xn, wg_ref[...].astype(BF16))
    u = _dot(xn, wu_ref[...].astype(BF16))
    a = (0.5 * _silu(g) * u).astype(BF16)
    o_ref[...] += _dot(a, wd_ref[...].astype(BF16))

    if final_norm:
        @pl.when(j == pl.num_programs(1) - 1)
        def _():
            h = o_ref[...]
            ms = jnp.mean(h * h, axis=-1, keepdims=True)
            o_ref[...] = h * lax.rsqrt(ms + RMS_EPS) * fw_ref[...]


def _ffn(x, nw, wg, wu, wd, fw, layer, *, final_norm, out_batched=None, tm=1024, tf=256):
    d = x.shape[-1]
    t = x.size // d
    f = wg.shape[2]
    if x.ndim == 3:
        nseg = x.shape[1] // tm
        x_spec = pl.BlockSpec((None, tm, d), lambda i, j: (i // nseg, i % nseg, 0))
    else:
        x_spec = pl.BlockSpec((tm, d), lambda i, j: (i, 0))
    if out_batched is None:
        out_spec = pl.BlockSpec((tm, d), lambda i, j: (i, 0))
        out_shape = jax.ShapeDtypeStruct((t, d), F32)
    else:
        oseg = out_batched[1] // tm
        out_spec = pl.BlockSpec((None, tm, d), lambda i, j: (i // oseg, i % oseg, 0))
        out_shape = jax.ShapeDtypeStruct((*out_batched, d), F32)
    return pl.pallas_call(
        functools.partial(_ffn_kernel, final_norm=final_norm),
        grid=(t // tm, f // tf),
        in_specs=[
            x_spec,
            pl.BlockSpec((1, d), lambda i, j: (0, 0)),
            pl.BlockSpec((None, d, tf), lambda i, j: (layer, 0, j)),
            pl.BlockSpec((None, d, tf), lambda i, j: (layer, 0, j)),
            pl.BlockSpec((None, tf, d), lambda i, j: (layer, j, 0)),
            pl.BlockSpec((1, d), lambda i, j: (0, 0)),
        ],
        out_specs=out_spec,
        out_shape=out_shape,
        scratch_shapes=[pltpu.VMEM((tm, d), BF16)],
        compiler_params=pltpu.CompilerParams(
            dimension_semantics=("parallel", "arbitrary"), vmem_limit_bytes=VMEM_LIMIT),
        name="ffn",
    )(x, nw.reshape(1, d), wg, wu, wd, fw.reshape(1, d))


def _inproj_kernel(x_ref, nw_ref, w_ref, o_ref, u5_ref, xn_ref, *, tn):
    j = pl.program_id(1)

    @pl.when(j == 0)
    def _():
        x = x_ref[...]
        ms = jnp.mean(x * x, axis=-1, keepdims=True)
        xn_ref[...] = (x * lax.rsqrt(ms + RMS_EPS) * nw_ref[...]).astype(BF16)

    res = _dot(xn_ref[...], w_ref[...])
    o_ref[...] = res

    @pl.when(j == COL_S5_U // tn)
    def _():
        off = COL_S5_U % tn
        u5_ref[...] = res[:, off:off + S5_WIDTH]


def _inproj(x, nw, w, bsz, seq, *, tm=1024, tn=1024):
    t, d = x.shape
    n = w.shape[1]
    nseg = seq // tm
    return pl.pallas_call(
        functools.partial(_inproj_kernel, tn=tn),
        grid=(t // tm, n // tn),
        in_specs=[
            pl.BlockSpec((tm, d), lambda i, j: (i, 0)),
            pl.BlockSpec((1, d), lambda i, j: (0, 0)),
            pl.BlockSpec((d, tn), lambda i, j: (0, j)),
        ],
        out_specs=[
            pl.BlockSpec((tm, tn), lambda i, j: (i, j)),
            pl.BlockSpec((tm, S5_WIDTH), lambda i, j: (i % nseg, i // nseg)),
        ],
        out_shape=[jax.ShapeDtypeStruct((t, n), F32),
                   jax.ShapeDtypeStruct((seq, bsz * S5_WIDTH), F32)],
        scratch_shapes=[pltpu.VMEM((tm, d), BF16)],
        compiler_params=pltpu.CompilerParams(
            dimension_semantics=("parallel", "arbitrary"), vmem_limit_bytes=VMEM_LIMIT),
        name="inproj",
    )(x, nw.reshape(1, d), w)


def _outproj_kernel(h_ref, a_ref, b_ref, c_ref, wa_ref, wb_ref, wc_ref, o_ref):
    acc = _dot(a_ref[...], wa_ref[...])
    acc += _dot(b_ref[...], wb_ref[...])
    acc += _dot(c_ref[...], wc_ref[...])
    o_ref[...] = h_ref[...] + acc


def _outproj(h, oa, ob, oc, wa, wb, wc, *, tm=512):
    t, d = h.shape
    nseg = oc.shape[0] // tm
    row = lambda i: (i, 0)
    fixed = lambda i: (0, 0)
    return pl.pallas_call(
        _outproj_kernel,
        grid=(t // tm,),
        in_specs=[
            pl.BlockSpec((tm, d), row),
            pl.BlockSpec((tm, oa.shape[1]), row),
            pl.BlockSpec((tm, ob.shape[1]), row),
            pl.BlockSpec((tm, S5_WIDTH), lambda i: (i % nseg, i // nseg)),
            pl.BlockSpec(wa.shape, fixed),
            pl.BlockSpec(wb.shape, fixed),
            pl.BlockSpec(wc.shape, fixed),
        ],
        out_specs=pl.BlockSpec((tm, d), row),
        out_shape=jax.ShapeDtypeStruct((t, d), F32),
        compiler_params=pltpu.CompilerParams(
            dimension_semantics=("parallel",), vmem_limit_bytes=VMEM_LIMIT),
        name="outproj",
    )(h, oa, ob, oc, wa, wb, wc)


def _chunk_rows(x_ref, lanes, ci, c):
    r0 = ci * c
    x = x_ref[pl.ds(pl.multiple_of(jnp.maximum(r0 - 8, 0), 8), c + 8), lanes]
    first = jnp.where(_iota(x.shape, 0) >= 8, pltpu.roll(x, 8, axis=0), 0.0)
    return jnp.where(ci == 0, first, x)


def _gdn_kernel(alog_ref, dtb_ref,
                q_ref, k_ref, v_ref, z_ref, gate_ref,
                cwq_ref, cwk_ref, cwv_ref, nw_ref,
                o_ref,
                u_ref, w_ref, qd_ref, kd_ref, at_ref, eg_ref,
                *, hb, p_inv, p_mix, unroll):
    hg = pl.program_id(1)
    seq = q_ref.shape[0]
    n_chunks = seq // CHUNK
    c = CHUNK
    mix_dt = BF16 if p_mix == 1 else F32

    row = _iota((c, c), 0)
    col = _iota((c, c), 1)
    tril = row >= col
    strict = row > col
    ltri = tril.astype(BF16)
    ones = jnp.ones((c, c), BF16)
    eye = (row == col).astype(F32)
    lane = _iota((c, LANE), 1)

    def prep(trip, carry):
        items = [(trip * unroll + cc, hh) for cc in range(unroll) for hh in range(hb)]
        lanes = [slice(hh * LANE, (hh + 1) * LANE) for _, hh in items]
        rows = [pl.ds(pl.multiple_of(ci * c, c), c) for ci, _ in items]

        def stage_a(ci, hh, ln, rw):
            h = hg * hb + hh
            neg_a = -jnp.exp(jnp.full((c, LANE), alog_ref[h], F32))
            dt_bias = jnp.full((c, LANE), dtb_ref[h], F32)
            q = _silu(_causal_conv(_chunk_rows(q_ref, ln, ci, c), cwq_ref.at[:, ln]))
            k = _silu(_causal_conv(_chunk_rows(k_ref, ln, ci, c), cwk_ref.at[:, ln]))
            v = _silu(_causal_conv(_chunk_rows(v_ref, ln, ci, c), cwv_ref.at[:, ln]))
            qn = q * lax.rsqrt(jnp.sum(q * q, axis=-1, keepdims=True) + RMS_EPS) * (GDN_D ** -0.5)
            kn = k * lax.rsqrt(jnp.sum(k * k, axis=-1, keepdims=True) + RMS_EPS)
            gt = gate_ref[rw, :]
            b_raw = jnp.sum(jnp.where(lane == h, gt, 0.0), axis=-1, keepdims=True)
            a_raw = jnp.sum(jnp.where(lane == h + GDN_HEADS, gt, 0.0), axis=-1, keepdims=True)
            beta = _sigmoid(b_raw)
            g_b = neg_a * _softplus(jnp.broadcast_to(a_raw, (c, LANE)) + dt_bias)
            return qn, kn, v, beta, g_b

        sa = [stage_a(ci, hh, ln, rw) for (ci, hh), ln, rw in zip(items, lanes, rows)]
        qn = [x[0] for x in sa]
        kn = [x[1] for x in sa]
        v = [x[2] for x in sa]
        beta = [x[3] for x in sa]
        g_b = [x[4] for x in sa]
        gi = [_mm01(ltri, g) for g in g_b]
        gj = [_mm01(ones, jnp.where(row <= col, g[:, :c], 0.0)) for g in g_b]
        kb = [k * b for k, b in zip(kn, beta)]
        kk = [_mm(a, b, NT, p_inv) for a, b in zip(kb, kn)]
        qk = [_mm(a, b, NT, p_mix) for a, b in zip(qn, kn)]
        decay = [jnp.where(tril, jnp.exp(jnp.where(tril, a[:, :c] - b, 0.0)), 0.0)
                 for a, b in zip(gi, gj)]
        npow = [-jnp.where(strict, a * d, 0.0) for a, d in zip(kk, decay)]
        tinv = [eye + n for n in npow]
        for _ in range(5):
            npow = [_mm(n, n, NN, p_inv) for n in npow]
            tinv = [t + _mm(t, n, NN, p_inv) for t, n in zip(tinv, npow)]
        exp_g = [jnp.exp(g) for g in gi]
        rhs = [jnp.concatenate([vv * b, k * e], axis=1) for vv, b, k, e in zip(v, beta, kb, exp_g)]
        uw = [_mm(t, r, NN, p_inv) for t, r in zip(tinv, rhs)]
        for i, ((ci, hh), rw) in enumerate(zip(items, rows)):
            g_last = jnp.broadcast_to(gi[i][c - 1:c, :], (c, LANE))
            u_ref[hh, rw, :] = uw[i][:, :LANE]
            w_ref[hh, rw, :] = uw[i][:, LANE:].astype(mix_dt)
            at_ref[hh, rw, :] = jnp.where(tril, qk[i] * decay[i], 0.0).astype(mix_dt)
            qd_ref[hh, rw, :] = (qn[i] * exp_g[i]).astype(mix_dt)
            kd_ref[hh, rw, :] = (kn[i] * jnp.exp(g_last - gi[i])).astype(mix_dt)
            eg_ref[hh, pl.ds(pl.multiple_of(ci * 8, 8), 8), :] = jnp.exp(g_last[0:8, :])
        return carry

    lax.fori_loop(0, n_chunks // unroll, prep, 0)

    norm_w = nw_ref[...]

    def scan(ci, states):
        rw = pl.ds(pl.multiple_of(ci * c, c), c)
        lhs = [jnp.concatenate([w_ref[hh, rw, :], qd_ref[hh, rw, :]], axis=0) for hh in range(hb)]
        ws_qs = [_mm(a, s, NN, p_mix) for a, s in zip(lhs, states)]
        v_new = [u_ref[hh, rw, :] - ws_qs[hh][:c] for hh in range(hb)]
        kv = [_mm(kd_ref[hh, rw, :], v_new[hh], TN, p_mix) for hh in range(hb)]
        av = [_mm(at_ref[hh, rw, :], v_new[hh], NN, p_mix) for hh in range(hb)]
        new_states = []
        for hh in range(hb):
            eg = eg_ref[hh, pl.ds(pl.multiple_of(ci * 8, 8), 8), :]
            new_states.append(states[hh] * jnp.broadcast_to(eg[0:1, :], (GDN_D, GDN_D)) + kv[hh])
            o = ws_qs[hh][c:] + av[hh]
            ms = jnp.mean(o * o, axis=-1, keepdims=True)
            ln = slice(hh * LANE, (hh + 1) * LANE)
            y = o * lax.rsqrt(ms + RMS_EPS) * norm_w * _silu(z_ref[rw, ln])
            o_ref[rw, ln] = y.astype(o_ref.dtype)
        return tuple(new_states)

    lax.fori_loop(0, n_chunks, scan, tuple(jnp.zeros((GDN_D, GDN_D), F32) for _ in range(hb)))


def _gdn(proj, bsz, seq, conv_w, a_log, dt_bias, norm_w, *, hb=3, p_inv=1, p_mix=1, unroll=4):
    t = proj.shape[0]
    wd = hb * LANE
    mix_dt = BF16 if p_mix == 1 else F32
    q0 = COL_GDN_QKV // wd
    z0 = COL_GDN_Z // wd
    g0 = COL_GDN_GATE // LANE
    ng = GDN_HEADS // hb
    blk = lambda off: pl.BlockSpec((seq, wd), lambda b, h, off=off: (b, off + h))
    cw = lambda off: pl.BlockSpec((CONV_K, wd), lambda b, h, off=off: (0, off + h))
    smem = pl.BlockSpec(memory_space=pltpu.SMEM)
    return pl.pallas_call(
        functools.partial(_gdn_kernel, hb=hb, p_inv=p_inv, p_mix=p_mix, unroll=unroll),
        grid=(bsz, ng),
        in_specs=[
            smem, smem,
            blk(q0), blk(q0 + ng), blk(q0 + 2 * ng), blk(z0),
            pl.BlockSpec((seq, LANE), lambda b, h: (b, g0)),
            cw(0), cw(ng), cw(2 * ng),
            pl.BlockSpec((1, LANE), lambda b, h: (0, 0)),
        ],
        out_specs=pl.BlockSpec((seq, wd), lambda b, h: (b, h)),
        out_shape=jax.ShapeDtypeStruct((t, GDN_QK), BF16),
        scratch_shapes=[
            pltpu.VMEM((hb, seq, LANE), F32),
            pltpu.VMEM((hb, seq, LANE), mix_dt),
            pltpu.VMEM((hb, seq, LANE), mix_dt),
            pltpu.VMEM((hb, seq, LANE), mix_dt),
            pltpu.VMEM((hb, seq, CHUNK), mix_dt),
            pltpu.VMEM((hb, seq // CHUNK * 8, LANE), F32),
        ],
        compiler_params=pltpu.CompilerParams(
            dimension_semantics=("parallel", "arbitrary"), vmem_limit_bytes=VMEM_LIMIT),
        name="gdn",
    )(a_log, dt_bias, proj, proj, proj, proj, proj, conv_w, conv_w, conv_w,
      norm_w.reshape(1, LANE))


def _ssd_kernel(z_ref, x_ref, b_ref, c_ref, dt_ref,
                cwx_ref, cwb_ref, cwc_ref,
                cbx_ref, cbb_ref, cbc_ref,
                alog_ref, dtb_ref, dsk_ref, nw_ref,
                o_ref,
                s_ref, *, passes, unroll):
    grp = pl.program_id(1)
    seq = x_ref.shape[0]
    n_chunks = seq // CHUNK
    c = CHUNK
    gw = SSD_GW

    row = _iota((c, c), 0)
    col = _iota((c, c), 1)
    ltri = (row >= col).astype(BF16)
    ones = jnp.ones((c, c), BF16)
    rowt = _iota((c, gw), 0)
    colt = jnp.bitwise_and(_iota((c, gw), 1), c - 1)
    tril_t = rowt >= colt
    upper_t = rowt <= colt
    expand = (_iota((LANE, gw), 0) == grp * SSD_HPG + lax.shift_right_logical(_iota((LANE, gw), 1), 6)).astype(BF16)
    last_row = (_iota((c, gw), 0) == c - 1)
    ones_cn = jnp.ones((c, SSD_N), BF16)

    neg_a = -jnp.exp(alog_ref[...])
    dt_bias = dtb_ref[...]
    d_skip = dsk_ref[...]
    norm_w = nw_ref[...]

    def conv(x_ref_, cw_ref, cb_ref, ci):
        return _silu(_causal_conv(_chunk_rows(x_ref_, slice(None), ci, c), cw_ref) + cb_ref[...])

    s_ref[...] = jnp.zeros((gw, SSD_N), F32)
    heads = [slice(hh * SSD_P, (hh + 1) * SSD_P) for hh in range(SSD_HPG)]

    def body(trip, carry):
        cis = [trip * unroll + cc for cc in range(unroll)]
        rows = [pl.ds(pl.multiple_of(ci * c, c), c) for ci in cis]
        xs = [conv(x_ref, cwx_ref, cbx_ref, ci) for ci in cis]
        bm = [conv(b_ref, cwb_ref, cbb_ref, ci) for ci in cis]
        cm = [conv(c_ref, cwc_ref, cbc_ref, ci) for ci in cis]
        dt = [_softplus(_mm01_t(dt_ref[rw, :], expand) + dt_bias) for rw in rows]
        a = [d * neg_a for d in dt]
        acum = [_mm01(ltri, x) for x in a]
        acum_j = [_mm01(ones, jnp.where(upper_t, x, 0.0)) for x in a]
        cb = [_mm(x, y, NT, passes) for x, y in zip(cm, bm)]
        lm = [jnp.where(tril_t, jnp.exp(jnp.where(tril_t, x - y, 0.0)), 0.0)
              for x, y in zip(acum, acum_j)]
        xdt = [x * d for x, d in zip(xs, dt)]
        y_diag = [jnp.concatenate([_mm(cb[i] * lm[i][:, sl], xdt[i][:, sl], NN, passes)
                                   for sl in heads], axis=1) for i in range(unroll)]
        a_last = [jnp.broadcast_to(x[c - 1:c, :], (c, gw)) for x in acum]
        states = [_mm(xdt[i] * jnp.exp(a_last[i] - acum[i]), bm[i], TN, passes)
                  for i in range(unroll)]
        dec_col = [jnp.exp(_mm01_t(jnp.where(last_row, x, 0.0), ones_cn, TN)) for x in acum]
        s_prev = s_ref[...]
        for i in range(unroll):
            y_off = _mm(cm[i], s_prev, NT, passes) * jnp.exp(acum[i])
            s_prev = s_prev * dec_col[i] + states[i]
            y = y_diag[i] + y_off + d_skip * xs[i]
            y = y * _silu(z_ref[rows[i], :])
            ms = jnp.mean(y * y, axis=-1, keepdims=True)
            o_ref[rows[i], :] = (y * lax.rsqrt(ms + RMS_EPS) * norm_w).astype(o_ref.dtype)
        s_ref[...] = s_prev
        return carry

    lax.fori_loop(0, n_chunks // unroll, body, 0)


def _ssd(proj, bsz, seq, conv_w, conv_b, a_log, dt_bias, d_skip, norm_w, *, passes=1, unroll=4):
    t = proj.shape[0]
    gw = SSD_GW
    rep = lambda v: jnp.repeat(v.astype(F32), SSD_P).reshape(1, SSD_INNER)
    cb = conv_b.reshape(1, -1)
    wide = lambda off: pl.BlockSpec((seq, gw), lambda b, g, off=off: (b, off + g))
    lane = lambda off: pl.BlockSpec((seq, LANE), lambda b, g, off=off: (b, off + g))
    chan = pl.BlockSpec((1, gw), lambda b, g: (0, g))
    return pl.pallas_call(
        functools.partial(_ssd_kernel, passes=passes, unroll=unroll),
        grid=(bsz, SSD_GROUPS),
        in_specs=[
            wide(COL_SSD_Z // gw), wide(COL_SSD_X // gw),
            lane(COL_SSD_B // LANE), lane(COL_SSD_C // LANE),
            pl.BlockSpec((seq, LANE), lambda b, g: (b, COL_SSD_DT // LANE)),
            pl.BlockSpec((CONV_K, gw), lambda b, g: (0, g)),
            pl.BlockSpec((CONV_K, LANE), lambda b, g: (0, SSD_INNER // LANE + g)),
            pl.BlockSpec((CONV_K, LANE), lambda b, g: (0, SSD_INNER // LANE + SSD_GROUPS + g)),
            pl.BlockSpec((1, gw), lambda b, g: (0, g)),
            pl.BlockSpec((1, LANE), lambda b, g: (0, SSD_INNER // LANE + g)),
            pl.BlockSpec((1, LANE), lambda b, g: (0, SSD_INNER // LANE + SSD_GROUPS + g)),
            chan, chan, chan, chan,
        ],
        out_specs=pl.BlockSpec((seq, gw), lambda b, g: (b, g)),
        out_shape=jax.ShapeDtypeStruct((t, SSD_INNER), BF16),
        scratch_shapes=[pltpu.VMEM((gw, SSD_N), F32)],
        compiler_params=pltpu.CompilerParams(
            dimension_semantics=("parallel", "arbitrary"), vmem_limit_bytes=VMEM_LIMIT),
        name="ssd",
    )(proj, proj, proj, proj, proj, conv_w, conv_w, conv_w, cb, cb, cb,
      rep(a_log), rep(dt_bias), rep(d_skip), norm_w.reshape(1, SSD_INNER))


def _s5_kernel(u_ref, wb_ref, wc_ref, are_ref, aim_ref, dsk_ref, gw_ref, gb_ref,
               o_ref, bu_ref, h_ref, *, bsz):
    ts = u_ref.shape[0] // bsz
    nt = S5_CH // LANE
    nb = S5_WIDTH // LANE
    per = nt // nb
    half = per * LANE

    @pl.when(pl.program_id(0) == 0)
    def _():
        h_ref[...] = jnp.zeros(h_ref.shape, F32)

    u = u_ref[...]
    u_bf = u.astype(BF16)
    for m in range(nb):
        res = _dot(u_bf[:, m * LANE:(m + 1) * LANE], wb_ref[m])
        for k in range(per):
            bu_ref[m * per + k] = res[:, k * LANE:(k + 1) * LANE]
            bu_ref[nt + m * per + k] = res[:, half + k * LANE:half + (k + 1) * LANE]

    def step(t, carry):
        rows = pl.ds(pl.multiple_of(t * bsz, bsz), bsz)
        new_re, new_im = [], []
        for k in range(nt):
            h_re, h_im = carry[k], carry[nt + k]
            a_re = are_ref[:, k * LANE:(k + 1) * LANE]
            a_im = aim_ref[:, k * LANE:(k + 1) * LANE]
            n_re = a_re * h_re - a_im * h_im + bu_ref[k, rows, :]
            n_im = a_re * h_im + a_im * h_re + bu_ref[nt + k, rows, :]
            bu_ref[k, rows, :] = n_re
            bu_ref[nt + k, rows, :] = n_im
            new_re.append(n_re)
            new_im.append(n_im)
        return tuple(new_re + new_im)

    h_last = lax.fori_loop(0, ts, step, tuple(h_ref[k] for k in range(2 * nt)))
    for k in range(2 * nt):
        h_ref[k] = h_last[k]

    y_parts = []
    for m in range(nb):
        hs = jnp.concatenate([bu_ref[m * per + k] for k in range(per)] +
                             [bu_ref[nt + m * per + k] for k in range(per)], axis=1)
        y_parts.append(_dot(hs.astype(BF16), wc_ref[m]))
    y = jnp.concatenate(y_parts, axis=1) + dsk_ref[...] * u
    g = 0.5 * y * (1.0 + jnp.tanh(math.sqrt(2.0 / math.pi) * (y + 0.044715 * (y * y * y))))
    out = g * _sigmoid(_dot(g.astype(BF16), gw_ref[...]) + gb_ref[...])
    o_ref[...] = out.astype(o_ref.dtype)


def _s5(u_tb, bsz, wb, wc, a_re, a_im, d_skip, glu_w, glu_b, *, ts=128):
    rows = u_tb.shape[0]
    blk = ts * bsz
    fixed = lambda shape: pl.BlockSpec(shape, lambda k: (0,) * len(shape))
    return pl.pallas_call(
        functools.partial(_s5_kernel, bsz=bsz),
        grid=(rows // blk,),
        in_specs=[
            pl.BlockSpec((blk, S5_WIDTH), lambda k: (k, 0)),
            fixed(wb.shape), fixed(wc.shape),
            fixed((1, S5_CH)), fixed((1, S5_CH)),
            fixed((1, S5_WIDTH)), fixed(glu_w.shape), fixed((1, S5_WIDTH)),
        ],
        out_specs=pl.BlockSpec((blk, S5_WIDTH), lambda k: (k, 0)),
        out_shape=jax.ShapeDtypeStruct((rows, S5_WIDTH), BF16),
        scratch_shapes=[
            pltpu.VMEM((2 * S5_CH // LANE, blk, LANE), F32),
            pltpu.VMEM((2 * S5_CH // LANE, bsz, LANE), F32),
        ],
        compiler_params=pltpu.CompilerParams(
            dimension_semantics=("arbitrary",), vmem_limit_bytes=VMEM_LIMIT),
        name="s5",
    )(u_tb, wb, wc, a_re.reshape(1, S5_CH), a_im.reshape(1, S5_CH),
      d_skip.reshape(1, S5_WIDTH), glu_w, glu_b.reshape(1, S5_WIDTH))


def _s5_params(a_re, a_im, b_re, b_im, c_re, c_im, log_dt):
    delta = jnp.exp(log_dt)[:, None]
    mag = jnp.exp(a_re * delta)
    ab_re, ab_im = mag * jnp.cos(a_im * delta), mag * jnp.sin(a_im * delta)
    den = a_re * a_re + a_im * a_im
    p_re, p_im = ab_re - 1.0, ab_im
    f_re = (p_re * a_re + p_im * a_im) / den
    f_im = (p_im * a_re - p_re * a_im) / den
    bb_re = f_re[..., None] * b_re - f_im[..., None] * b_im
    bb_im = f_re[..., None] * b_im + f_im[..., None] * b_re
    gpb = LANE // S5_GROUP
    nb = S5_GROUPS // gpb
    eye = jnp.eye(gpb, dtype=F32)
    emb_b = lambda bb: jnp.einsum('mgnj,gh->mgjhn', bb.reshape(nb, gpb, S5_STATE, S5_GROUP),
                                  eye).reshape(nb, LANE, gpb * S5_STATE)
    wb = jnp.concatenate([emb_b(bb_re), emb_b(bb_im)], axis=2)
    emb_c = lambda cc: jnp.einsum('mgin,gh->mgnhi', cc.reshape(nb, gpb, S5_GROUP, S5_STATE),
                                  eye).reshape(nb, gpb * S5_STATE, LANE)
    wc = jnp.concatenate([emb_c(c_re), -emb_c(c_im)], axis=1)
    return ab_re.reshape(-1), ab_im.reshape(-1), wb.astype(BF16), wc.astype(BF16)


def _pack_w_in(w_in):
    d = w_in.shape[0]
    w_in = w_in.astype(BF16)
    gdn_in = GDN_QKV + GDN_QK + 2 * GDN_HEADS
    ssd_conv = SSD_INNER + 2 * SSD_GROUPS * SSD_N
    g_qkv = w_in[:, 0:GDN_QKV]
    g_z = w_in[:, GDN_QKV:GDN_QKV + GDN_QK]
    g_gate = w_in[:, GDN_QKV + GDN_QK:gdn_in]
    s0 = gdn_in
    s_z = w_in[:, s0:s0 + SSD_INNER]
    s_x = w_in[:, s0 + SSD_INNER:s0 + 2 * SSD_INNER]
    s_b = w_in[:, s0 + 2 * SSD_INNER:s0 + 2 * SSD_INNER + SSD_GROUPS * SSD_N]
    s_c = w_in[:, s0 + 2 * SSD_INNER + SSD_GROUPS * SSD_N:s0 + SSD_INNER + ssd_conv]
    s_dt = w_in[:, s0 + SSD_INNER + ssd_conv:s0 + SSD_INNER + ssd_conv + SSD_HEADS]
    u0 = s0 + SSD_INNER + ssd_conv + SSD_HEADS
    s5_u = w_in[:, u0:u0 + S5_WIDTH]
    pad = lambda n: jnp.zeros((d, n), w_in.dtype)
    cols = [s_z, s_x, s5_u, s_b, s_c, s_dt, pad(LANE - SSD_HEADS), g_qkv, g_z,
            g_gate, pad(LANE - 2 * GDN_HEADS), pad(P_PAD - (COL_GDN_GATE + LANE))]
    return jnp.concatenate(cols, axis=1)


def kernel(x, ffn1_norm, ffn1_w_gate, ffn1_w_up, ffn1_w_down, mix_norm, w_in,
           gdn_conv_w, gdn_a_log, gdn_dt_bias, gdn_norm,
           ssd_conv_w, ssd_conv_b, ssd_a_log, ssd_dt_bias, ssd_d, ssd_norm,
           s5_a_re, s5_a_im, s5_b_re, s5_b_im, s5_c_re, s5_c_im, s5_d, s5_log_dt,
           s5_glu_w, s5_glu_b, w_out, ffn2_norm, ffn2_w_gate, ffn2_w_up, ffn2_w_down,
           final_norm):
    bsz, seq, d = x.shape
    depth = w_in.shape[0]
    h = x
    for i in range(depth):
        h = _ffn(h, ffn1_norm[i], ffn1_w_gate, ffn1_w_up, ffn1_w_down, final_norm, i,
                 final_norm=False)
        proj, u_s5 = _inproj(h, mix_norm[i], _pack_w_in(w_in[i]), bsz, seq)
        o_gdn = _gdn(proj, bsz, seq, gdn_conv_w[i], gdn_a_log[i], gdn_dt_bias[i], gdn_norm[i])
        o_ssd = _ssd(proj, bsz, seq, ssd_conv_w[i], ssd_conv_b[i], ssd_a_log[i], ssd_dt_bias[i],
                     ssd_d[i], ssd_norm[i])
        ab_re, ab_im, wb, wc = _s5_params(s5_a_re[i], s5_a_im[i], s5_b_re[i], s5_b_im[i],
                                          s5_c_re[i], s5_c_im[i], s5_log_dt[i])
        o_s5 = _s5(u_s5.reshape(seq * bsz, S5_WIDTH), bsz, wb, wc, ab_re, ab_im, s5_d[i],
                   s5_glu_w[i].astype(BF16), s5_glu_b[i])
        wo = w_out[i].astype(BF16)
        h = _outproj(h, o_gdn, o_ssd, o_s5.reshape(seq, bsz * S5_WIDTH), wo[0:GDN_QK],
                     wo[GDN_QK:GDN_QK + SSD_INNER], wo[GDN_QK + SSD_INNER:])
        last = i == depth - 1
        h = _ffn(h, ffn2_norm[i], ffn2_w_gate, ffn2_w_up, ffn2_w_down, final_norm, i,
                 final_norm=last, out_batched=(bsz, seq) if last else None)
    return h
```

```python
import functools
import math

import jax
import jax.numpy as jnp
from jax import lax
from jax.experimental import pallas as pl
from jax.experimental.pallas import tpu as pltpu

F32 = jnp.float32
BF16 = jnp.bfloat16

RMS_EPS = 1e-6
CHUNK = 64
CONV_K = 4
D_MODEL = 2048
D_FF = 5632

GDN_HEADS = 6
GDN_D = 128
GDN_QK = GDN_HEADS * GDN_D
GDN_QKV = 3 * GDN_QK

SSD_HEADS = 12
SSD_P = 64
SSD_GROUPS = 2
SSD_N = 128
SSD_HPG = SSD_HEADS // SSD_GROUPS
SSD_GW = SSD_HPG * SSD_P
SSD_INNER = SSD_HEADS * SSD_P

S5_WIDTH = 512
S5_GROUP = 16
S5_GROUPS = 32
S5_STATE = 64
S5_CH = S5_GROUPS * S5_STATE

LANE = 128

COL_SSD_Z = 0
COL_SSD_X = 768
COL_S5_U = 1536
COL_SSD_B = 2048
COL_SSD_C = 2304
COL_SSD_DT = 2560
COL_GDN_QKV = 2688
COL_GDN_Z = 4992
COL_GDN_GATE = 5760
P_PAD = 6144

VMEM_LIMIT = 60 * 1024 * 1024

NN = (((1,), (0,)), ((), ()))
NT = (((1,), (1,)), ((), ()))
TN = (((0,), (0,)), ((), ()))


def _dot(a, b, dims=NN):
    return lax.dot_general(a, b, dims, preferred_element_type=F32)


def _hi_lo(a):
    hi = a.astype(BF16)
    lo = (a - hi.astype(F32)).astype(BF16)
    return hi, lo


def _mm(a, b, dims=NN, passes=3):
    if passes == 1:
        return _dot(a.astype(BF16), b.astype(BF16), dims)
    ah, al = _hi_lo(a)
    bh, bl = _hi_lo(b)
    return _dot(ah, bh, dims) + (_dot(ah, bl, dims) + _dot(al, bh, dims))


def _mm01(m01, x, dims=NN):
    hi, lo = _hi_lo(x)
    return _dot(m01, hi, dims) + _dot(m01, lo, dims)


def _mm01_t(x, m01, dims=NN):
    hi, lo = _hi_lo(x)
    return _dot(hi, m01, dims) + _dot(lo, m01, dims)


def _sigmoid(x):
    return 1.0 / (1.0 + jnp.exp(-x))


def _silu(x):
    return x * _sigmoid(x)


def _softplus(x):
    return jnp.maximum(x, 0.0) + jnp.log1p(jnp.exp(-jnp.abs(x)))


def _iota(shape, dim):
    return lax.broadcasted_iota(jnp.int32, shape, dim)


def _causal_conv(x, cw_ref):
    acc = None
    for i in range(CONV_K):
        shift = CONV_K - 1 - i
        xs = x if shift == 0 else pltpu.roll(x, shift, axis=0)
        term = xs[8:] * cw_ref[i:i + 1, :]
        acc = term if acc is None else acc + term
    return acc


def _ffn_kernel(x_ref, nw_ref, wg_ref, wu_ref, wd_ref, fw_ref, o_ref, xn_ref, *, final_norm):
    j = pl.program_id(1)

    @pl.when(j == 0)
    def _():
        x = x_ref[...]
        ms = jnp.mean(x * x, axis=-1, keepdims=True)
        xn_ref[...] = (x * lax.rsqrt(ms + RMS_EPS) * nw_ref[...]).astype(BF16)
        o_ref[...] = x

    xn = xn_ref[...]
    g = _dot(xn, wg_ref[...])
    u = _dot(xn, wu_ref[...])
    a = (0.5 * _silu(g) * u).astype(BF16)
    o_ref[...] += _dot(a, wd_ref[...])

    if final_norm:
        @pl.when(j == pl.num_programs(1) - 1)
        def _():
            h = o_ref[...]
            ms = jnp.mean(h * h, axis=-1, keepdims=True)
            o_ref[...] = h * lax.rsqrt(ms + RMS_EPS) * fw_ref[...]


def _cast_kernel(x_ref, o_ref):
    o_ref[...] = x_ref[...].astype(o_ref.dtype)


def _to_bf16(w, *, rows=256):
    nl, r, c = w.shape
    spec = pl.BlockSpec((None, rows, c), lambda l, i: (l, i, 0))
    return pl.pallas_call(
        _cast_kernel,
        grid=(nl, r // rows),
        in_specs=[spec],
        out_specs=spec,
        out_shape=jax.ShapeDtypeStruct(w.shape, BF16),
        compiler_params=pltpu.CompilerParams(
            dimension_semantics=("parallel", "parallel"), vmem_limit_bytes=VMEM_LIMIT),
        name="to_bf16",
    )(w)


def _ffn(x, nw, wg, wu, wd, fw, layer, *, final_norm, out_batched=None, tm=1024, tf=512):
    d = x.shape[-1]
    t = x.size // d
    f = wg.shape[2]
    if x.ndim == 3:
        nseg = x.shape[1] // tm
        x_spec = pl.BlockSpec((None, tm, d), lambda i, j: (i // nseg, i % nseg, 0))
    else:
        x_spec = pl.BlockSpec((tm, d), lambda i, j: (i, 0))
    if out_batched is None:
        out_spec = pl.BlockSpec((tm, d), lambda i, j: (i, 0))
        out_shape = jax.ShapeDtypeStruct((t, d), F32)
    else:
        oseg = out_batched[1] // tm
        out_spec = pl.BlockSpec((None, tm, d), lambda i, j: (i // oseg, i % oseg, 0))
        out_shape = jax.ShapeDtypeStruct((*out_batched, d), F32)
    return pl.pallas_call(
        functools.partial(_ffn_kernel, final_norm=final_norm),
        grid=(t // tm, f // tf),
        in_specs=[
            x_spec,
            pl.BlockSpec((1, d), lambda i, j: (0, 0)),
            pl.BlockSpec((None, d, tf), lambda i, j: (layer, 0, j)),
            pl.BlockSpec((None, d, tf), lambda i, j: (layer, 0, j)),
            pl.BlockSpec((None, tf, d), lambda i, j: (layer, j, 0)),
            pl.BlockSpec((1, d), lambda i, j: (0, 0)),
        ],
        out_specs=out_spec,
        out_shape=out_shape,
        scratch_shapes=[pltpu.VMEM((tm, d), BF16)],
        compiler_params=pltpu.CompilerParams(
            dimension_semantics=("parallel", "arbitrary"), vmem_limit_bytes=VMEM_LIMIT),
        name="ffn",
    )(x, nw.reshape(1, d), wg, wu, wd, fw.reshape(1, d))


def _inproj_kernel(x_ref, nw_ref, w_ref, o_ref, u5_ref, xn_ref, *, tn):
    j = pl.program_id(1)

    @pl.when(j == 0)
    def _():
        x = x_ref[...]
        ms = jnp.mean(x * x, axis=-1, keepdims=True)
        xn_ref[...] = (x * lax.rsqrt(ms + RMS_EPS) * nw_ref[...]).astype(BF16)

    res = _dot(xn_ref[...], w_ref[...])
    o_ref[...] = res

    @pl.when(j == COL_S5_U // tn)
    def _():
        off = COL_S5_U % tn
        u5_ref[...] = res[:, off:off + S5_WIDTH]


def _inproj(x, nw, w, bsz, seq, *, tm=1024, tn=1024):
    t, d = x.shape
    n = w.shape[1]
    nseg = seq // tm
    return pl.pallas_call(
        functools.partial(_inproj_kernel, tn=tn),
        grid=(t // tm, n // tn),
        in_specs=[
            pl.BlockSpec((tm, d), lambda i, j: (i, 0)),
            pl.BlockSpec((1, d), lambda i, j: (0, 0)),
            pl.BlockSpec((d, tn), lambda i, j: (0, j)),
        ],
        out_specs=[
            pl.BlockSpec((tm, tn), lambda i, j: (i, j)),
            pl.BlockSpec((tm, S5_WIDTH), lambda i, j: (i % nseg, i // nseg)),
        ],
        out_shape=[jax.ShapeDtypeStruct((t, n), F32),
                   jax.ShapeDtypeStruct((seq, bsz * S5_WIDTH), F32)],
        scratch_shapes=[pltpu.VMEM((tm, d), BF16)],
        compiler_params=pltpu.CompilerParams(
            dimension_semantics=("parallel", "arbitrary"), vmem_limit_bytes=VMEM_LIMIT),
        name="inproj",
    )(x, nw.reshape(1, d), w)


def _outproj_kernel(h_ref, a_ref, b_ref, c_ref, wa_ref, wb_ref, wc_ref, o_ref):
    acc = _dot(a_ref[...], wa_ref[...])
    acc += _dot(b_ref[...], wb_ref[...])
    acc += _dot(c_ref[...], wc_ref[...])
    o_ref[...] = h_ref[...] + acc


def _outproj(h, oa, ob, oc, wa, wb, wc, *, tm=512):
    t, d = h.shape
    nseg = oc.shape[0] // tm
    row = lambda i: (i, 0)
    fixed = lambda i: (0, 0)
    return pl.pallas_call(
        _outproj_kernel,
        grid=(t // tm,),
        in_specs=[
            pl.BlockSpec((tm, d), row),
            pl.BlockSpec((tm, oa.shape[1]), row),
            pl.BlockSpec((tm, ob.shape[1]), row),
            pl.BlockSpec((tm, S5_WIDTH), lambda i: (i % nseg, i // nseg)),
            pl.BlockSpec(wa.shape, fixed),
            pl.BlockSpec(wb.shape, fixed),
            pl.BlockSpec(wc.shape, fixed),
        ],
        out_specs=pl.BlockSpec((tm, d), row),
        out_shape=jax.ShapeDtypeStruct((t, d), F32),
        compiler_params=pltpu.CompilerParams(
            dimension_semantics=("parallel",), vmem_limit_bytes=VMEM_LIMIT),
        name="outproj",
    )(h, oa, ob, oc, wa, wb, wc)


def _chunk_rows(x_ref, lanes, ci, c):
    r0 = ci * c
    x = x_ref[pl.ds(pl.multiple_of(jnp.maximum(r0 - 8, 0), 8), c + 8), lanes]
    first = jnp.where(_iota(x.shape, 0) >= 8, pltpu.roll(x, 8, axis=0), 0.0)
    return jnp.where(ci == 0, first, x)


def _gdn_kernel(alog_ref, dtb_ref,
                q_ref, k_ref, v_ref, z_ref, gate_ref,
                cwq_ref, cwk_ref, cwv_ref, nw_ref,
                o_ref,
                u_ref, w_ref, qd_ref, kd_ref, at_ref, eg_ref,
                *, hb, p_inv, p_mix, unroll):
    hg = pl.program_id(1)
    seq = q_ref.shape[0]
    n_chunks = seq // CHUNK
    c = CHUNK
    mix_dt = BF16 if p_mix == 1 else F32

    row = _iota((c, c), 0)
    col = _iota((c, c), 1)
    tril = row >= col
    strict = row > col
    ltri = tril.astype(BF16)
    ones = jnp.ones((c, c), BF16)
    eye = (row == col).astype(F32)
    lane = _iota((c, LANE), 1)

    def prep(trip, carry):
        items = [(trip * unroll + cc, hh) for cc in range(unroll) for hh in range(hb)]
        lanes = [slice(hh * LANE, (hh + 1) * LANE) for _, hh in items]
        rows = [pl.ds(pl.multiple_of(ci * c, c), c) for ci, _ in items]

        def stage_a(ci, hh, ln, rw):
            h = hg * hb + hh
            neg_a = -jnp.exp(jnp.full((c, LANE), alog_ref[h], F32))
            dt_bias = jnp.full((c, LANE), dtb_ref[h], F32)
            q = _silu(_causal_conv(_chunk_rows(q_ref, ln, ci, c), cwq_ref.at[:, ln]))
            k = _silu(_causal_conv(_chunk_rows(k_ref, ln, ci, c), cwk_ref.at[:, ln]))
            v = _silu(_causal_conv(_chunk_rows(v_ref, ln, ci, c), cwv_ref.at[:, ln]))
            qn = q * lax.rsqrt(jnp.sum(q * q, axis=-1, keepdims=True) + RMS_EPS) * (GDN_D ** -0.5)
            kn = k * lax.rsqrt(jnp.sum(k * k, axis=-1, keepdims=True) + RMS_EPS)
            gt = gate_ref[rw, :]
            b_raw = jnp.sum(jnp.where(lane == h, gt, 0.0), axis=-1, keepdims=True)
            a_raw = jnp.sum(jnp.where(lane == h + GDN_HEADS, gt, 0.0), axis=-1, keepdims=True)
            beta = _sigmoid(b_raw)
            g_b = neg_a * _softplus(jnp.broadcast_to(a_raw, (c, LANE)) + dt_bias)
            return qn, kn, v, beta, g_b

        sa = [stage_a(ci, hh, ln, rw) for (ci, hh), ln, rw in zip(items, lanes, rows)]
        qn = [x[0] for x in sa]
        kn = [x[1] for x in sa]
        v = [x[2] for x in sa]
        beta = [x[3] for x in sa]
        g_b = [x[4] for x in sa]
        gi = [_mm01(ltri, g) for g in g_b]
        gj = [_mm01(ones, jnp.where(row <= col, g[:, :c], 0.0)) for g in g_b]
        kb = [k * b for k, b in zip(kn, beta)]
        kk = [_mm(a, b, NT, p_inv) for a, b in zip(kb, kn)]
        qk = [_mm(a, b, NT, p_mix) for a, b in zip(qn, kn)]
        decay = [jnp.where(tril, jnp.exp(jnp.where(tril, a[:, :c] - b, 0.0)), 0.0)
                 for a, b in zip(gi, gj)]
        npow = [-jnp.where(strict, a * d, 0.0) for a, d in zip(kk, decay)]
        tinv = [eye + n for n in npow]
        for _ in range(5):
            npow = [_mm(n, n, NN, p_inv) for n in npow]
            tinv = [t + _mm(t, n, NN, p_inv) for t, n in zip(tinv, npow)]
        exp_g = [jnp.exp(g) for g in gi]
        rhs = [jnp.concatenate([vv * b, k * e], axis=1) for vv, b, k, e in zip(v, beta, kb, exp_g)]
        uw = [_mm(t, r, NN, p_inv) for t, r in zip(tinv, rhs)]
        for i, ((ci, hh), rw) in enumerate(zip(items, rows)):
            g_last = jnp.broadcast_to(gi[i][c - 1:c, :], (c, LANE))
            u_ref[hh, rw, :] = uw[i][:, :LANE]
            w_ref[hh, rw, :] = uw[i][:, LANE:].astype(mix_dt)
            at_ref[hh, rw, :] = jnp.where(tril, qk[i] * decay[i], 0.0).astype(mix_dt)
            qd_ref[hh, rw, :] = (qn[i] * exp_g[i]).astype(mix_dt)
            kd_ref[hh, rw, :] = (kn[i] * jnp.exp(g_last - gi[i])).astype(mix_dt)
            eg_ref[hh, pl.ds(pl.multiple_of(ci * 8, 8), 8), :] = jnp.exp(g_last[0:8, :])
        return carry

    lax.fori_loop(0, n_chunks // unroll, prep, 0)

    norm_w = nw_ref[...]

    def scan(ci, states):
        rw = pl.ds(pl.multiple_of(ci * c, c), c)
        lhs = [jnp.concatenate([w_ref[hh, rw, :], qd_ref[hh, rw, :]], axis=0) for hh in range(hb)]
        ws_qs = [_mm(a, s, NN, p_mix) for a, s in zip(lhs, states)]
        v_new = [u_ref[hh, rw, :] - ws_qs[hh][:c] for hh in range(hb)]
        kv = [_mm(kd_ref[hh, rw, :], v_new[hh], TN, p_mix) for hh in range(hb)]
        av = [_mm(at_ref[hh, rw, :], v_new[hh], NN, p_mix) for hh in range(hb)]
        new_states = []
        for hh in range(hb):
            eg = eg_ref[hh, pl.ds(pl.multiple_of(ci * 8, 8), 8), :]
            new_states.append(states[hh] * jnp.broadcast_to(eg[0:1, :], (GDN_D, GDN_D)) + kv[hh])
            o = ws_qs[hh][c:] + av[hh]
            ms = jnp.mean(o * o, axis=-1, keepdims=True)
            ln = slice(hh * LANE, (hh + 1) * LANE)
            y = o * lax.rsqrt(ms + RMS_EPS) * norm_w * _silu(z_ref[rw, ln])
            o_ref[rw, ln] = y.astype(o_ref.dtype)
        return tuple(new_states)

    lax.fori_loop(0, n_chunks, scan, tuple(jnp.zeros((GDN_D, GDN_D), F32) for _ in range(hb)))


def _gdn(proj, bsz, seq, conv_w, a_log, dt_bias, norm_w, *, hb=3, p_inv=1, p_mix=1, unroll=4):
    t = proj.shape[0]
    wd = hb * LANE
    mix_dt = BF16 if p_mix == 1 else F32
    q0 = COL_GDN_QKV // wd
    z0 = COL_GDN_Z // wd
    g0 = COL_GDN_GATE // LANE
    ng = GDN_HEADS // hb
    blk = lambda off: pl.BlockSpec((seq, wd), lambda b, h, off=off: (b, off + h))
    cw = lambda off: pl.BlockSpec((CONV_K, wd), lambda b, h, off=off: (0, off + h))
    smem = pl.BlockSpec(memory_space=pltpu.SMEM)
    return pl.pallas_call(
        functools.partial(_gdn_kernel, hb=hb, p_inv=p_inv, p_mix=p_mix, unroll=unroll),
        grid=(bsz, ng),
        in_specs=[
            smem, smem,
            blk(q0), blk(q0 + ng), blk(q0 + 2 * ng), blk(z0),
            pl.BlockSpec((seq, LANE), lambda b, h: (b, g0)),
            cw(0), cw(ng), cw(2 * ng),
            pl.BlockSpec((1, LANE), lambda b, h: (0, 0)),
        ],
        out_specs=pl.BlockSpec((seq, wd), lambda b, h: (b, h)),
        out_shape=jax.ShapeDtypeStruct((t, GDN_QK), BF16),
        scratch_shapes=[
            pltpu.VMEM((hb, seq, LANE), F32),
            pltpu.VMEM((hb, seq, LANE), mix_dt),
            pltpu.VMEM((hb, seq, LANE), mix_dt),
            pltpu.VMEM((hb, seq, LANE), mix_dt),
            pltpu.VMEM((hb, seq, CHUNK), mix_dt),
            pltpu.VMEM((hb, seq // CHUNK * 8, LANE), F32),
        ],
        compiler_params=pltpu.CompilerParams(
            dimension_semantics=("parallel", "arbitrary"), vmem_limit_bytes=VMEM_LIMIT),
        name="gdn",
    )(a_log, dt_bias, proj, proj, proj, proj, proj, conv_w, conv_w, conv_w,
      norm_w.reshape(1, LANE))


def _ssd_kernel(z_ref, x_ref, b_ref, c_ref, dt_ref,
                cwx_ref, cwb_ref, cwc_ref,
                cbx_ref, cbb_ref, cbc_ref,
                alog_ref, dtb_ref, dsk_ref, nw_ref,
                o_ref,
                s_ref, *, passes, unroll):
    grp = pl.program_id(1)
    seq = x_ref.shape[0]
    n_chunks = seq // CHUNK
    c = CHUNK
    gw = SSD_GW

    row = _iota((c, c), 0)
    col = _iota((c, c), 1)
    ltri = (row >= col).astype(BF16)
    ones = jnp.ones((c, c), BF16)
    rowt = _iota((c, gw), 0)
    colt = jnp.bitwise_and(_iota((c, gw), 1), c - 1)
    tril_t = rowt >= colt
    upper_t = rowt <= colt
    expand = (_iota((LANE, gw), 0) == grp * SSD_HPG + lax.shift_right_logical(_iota((LANE, gw), 1), 6)).astype(BF16)
    last_row = (_iota((c, gw), 0) == c - 1)
    ones_cn = jnp.ones((c, SSD_N), BF16)

    neg_a = -jnp.exp(alog_ref[...])
    dt_bias = dtb_ref[...]
    d_skip = dsk_ref[...]
    norm_w = nw_ref[...]

    def conv(x_ref_, cw_ref, cb_ref, ci):
        return _silu(_causal_conv(_chunk_rows(x_ref_, slice(None), ci, c), cw_ref) + cb_ref[...])

    s_ref[...] = jnp.zeros((gw, SSD_N), F32)
    heads = [slice(hh * SSD_P, (hh + 1) * SSD_P) for hh in range(SSD_HPG)]

    def body(trip, carry):
        cis = [trip * unroll + cc for cc in range(unroll)]
        rows = [pl.ds(pl.multiple_of(ci * c, c), c) for ci in cis]
        xs = [conv(x_ref, cwx_ref, cbx_ref, ci) for ci in cis]
        bm = [conv(b_ref, cwb_ref, cbb_ref, ci) for ci in cis]
        cm = [conv(c_ref, cwc_ref, cbc_ref, ci) for ci in cis]
        dt = [_softplus(_mm01_t(dt_ref[rw, :], expand) + dt_bias) for rw in rows]
        a = [d * neg_a for d in dt]
        acum = [_mm01(ltri, x) for x in a]
        acum_j = [_mm01(ones, jnp.where(upper_t, x, 0.0)) for x in a]
        cb = [_mm(x, y, NT, passes) for x, y in zip(cm, bm)]
        lm = [jnp.where(tril_t, jnp.exp(jnp.where(tril_t, x - y, 0.0)), 0.0)
              for x, y in zip(acum, acum_j)]
        xdt = [x * d for x, d in zip(xs, dt)]
        y_diag = [jnp.concatenate([_mm(cb[i] * lm[i][:, sl], xdt[i][:, sl], NN, passes)
                                   for sl in heads], axis=1) for i in range(unroll)]
        a_last = [jnp.broadcast_to(x[c - 1:c, :], (c, gw)) for x in acum]
        states = [_mm(xdt[i] * jnp.exp(a_last[i] - acum[i]), bm[i], TN, passes)
                  for i in range(unroll)]
        dec_col = [jnp.exp(_mm01_t(jnp.where(last_row, x, 0.0), ones_cn, TN)) for x in acum]
        s_prev = s_ref[...]
        for i in range(unroll):
            y_off = _mm(cm[i], s_prev, NT, passes) * jnp.exp(acum[i])
            s_prev = s_prev * dec_col[i] + states[i]
            y = y_diag[i] + y_off + d_skip * xs[i]
            y = y * _silu(z_ref[rows[i], :])
            ms = jnp.mean(y * y, axis=-1, keepdims=True)
            o_ref[rows[i], :] = (y * lax.rsqrt(ms + RMS_EPS) * norm_w).astype(o_ref.dtype)
        s_ref[...] = s_prev
        return carry

    lax.fori_loop(0, n_chunks // unroll, body, 0)


def _ssd(proj, bsz, seq, conv_w, conv_b, a_log, dt_bias, d_skip, norm_w, *, passes=1, unroll=4):
    t = proj.shape[0]
    gw = SSD_GW
    rep = lambda v: jnp.repeat(v.astype(F32), SSD_P).reshape(1, SSD_INNER)
    cb = conv_b.reshape(1, -1)
    wide = lambda off: pl.BlockSpec((seq, gw), lambda b, g, off=off: (b, off + g))
    lane = lambda off: pl.BlockSpec((seq, LANE), lambda b, g, off=off: (b, off + g))
    chan = pl.BlockSpec((1, gw), lambda b, g: (0, g))
    return pl.pallas_call(
        functools.partial(_ssd_kernel, passes=passes, unroll=unroll),
        grid=(bsz, SSD_GROUPS),
        in_specs=[
            wide(COL_SSD_Z // gw), wide(COL_SSD_X // gw),
            lane(COL_SSD_B // LANE), lane(COL_SSD_C // LANE),
            pl.BlockSpec((seq, LANE), lambda b, g: (b, COL_SSD_DT // LANE)),
            pl.BlockSpec((CONV_K, gw), lambda b, g: (0, g)),
            pl.BlockSpec((CONV_K, LANE), lambda b, g: (0, SSD_INNER // LANE + g)),
            pl.BlockSpec((CONV_K, LANE), lambda b, g: (0, SSD_INNER // LANE + SSD_GROUPS + g)),
            pl.BlockSpec((1, gw), lambda b, g: (0, g)),
            pl.BlockSpec((1, LANE), lambda b, g: (0, SSD_INNER // LANE + g)),
            pl.BlockSpec((1, LANE), lambda b, g: (0, SSD_INNER // LANE + SSD_GROUPS + g)),
            chan, chan, chan, chan,
        ],
        out_specs=pl.BlockSpec((seq, gw), lambda b, g: (b, g)),
        out_shape=jax.ShapeDtypeStruct((t, SSD_INNER), BF16),
        scratch_shapes=[pltpu.VMEM((gw, SSD_N), F32)],
        compiler_params=pltpu.CompilerParams(
            dimension_semantics=("parallel", "arbitrary"), vmem_limit_bytes=VMEM_LIMIT),
        name="ssd",
    )(proj, proj, proj, proj, proj, conv_w, conv_w, conv_w, cb, cb, cb,
      rep(a_log), rep(dt_bias), rep(d_skip), norm_w.reshape(1, SSD_INNER))


def _s5_kernel(u_ref, wb_ref, wc_ref, are_ref, aim_ref, dsk_ref, gw_ref, gb_ref,
               o_ref, bu_ref, h_ref, *, bsz):
    ts = u_ref.shape[0] // bsz
    nt = S5_CH // LANE
    nb = S5_WIDTH // LANE
    per = nt // nb
    half = per * LANE

    @pl.when(pl.program_id(0) == 0)
    def _():
        h_ref[...] = jnp.zeros(h_ref.shape, F32)

    u = u_ref[...]
    u_bf = u.astype(BF16)
    for m in range(nb):
        res = _dot(u_bf[:, m * LANE:(m + 1) * LANE], wb_ref[m])
        for k in range(per):
            bu_ref[m * per + k] = res[:, k * LANE:(k + 1) * LANE]
            bu_ref[nt + m * per + k] = res[:, half + k * LANE:half + (k + 1) * LANE]

    def step(t, carry):
        rows = pl.ds(pl.multiple_of(t * bsz, bsz), bsz)
        new_re, new_im = [], []
        for k in range(nt):
            h_re, h_im = carry[k], carry[nt + k]
            a_re = are_ref[:, k * LANE:(k + 1) * LANE]
            a_im = aim_ref[:, k * LANE:(k + 1) * LANE]
            n_re = a_re * h_re - a_im * h_im + bu_ref[k, rows, :]
            n_im = a_re * h_im + a_im * h_re + bu_ref[nt + k, rows, :]
            bu_ref[k, rows, :] = n_re
            bu_ref[nt + k, rows, :] = n_im
            new_re.append(n_re)
            new_im.append(n_im)
        return tuple(new_re + new_im)

    h_last = lax.fori_loop(0, ts, step, tuple(h_ref[k] for k in range(2 * nt)))
    for k in range(2 * nt):
        h_ref[k] = h_last[k]

    y_parts = []
    for m in range(nb):
        hs = jnp.concatenate([bu_ref[m * per + k] for k in range(per)] +
                             [bu_ref[nt + m * per + k] for k in range(per)], axis=1)
        y_parts.append(_dot(hs.astype(BF16), wc_ref[m]))
    y = jnp.concatenate(y_parts, axis=1) + dsk_ref[...] * u
    g = 0.5 * y * (1.0 + jnp.tanh(math.sqrt(2.0 / math.pi) * (y + 0.044715 * (y * y * y))))
    out = g * _sigmoid(_dot(g.astype(BF16), gw_ref[...]) + gb_ref[...])
    o_ref[...] = out.astype(o_ref.dtype)


def _s5(u_tb, bsz, wb, wc, a_re, a_im, d_skip, glu_w, glu_b, *, ts=128):
    rows = u_tb.shape[0]
    blk = ts * bsz
    fixed = lambda shape: pl.BlockSpec(shape, lambda k: (0,) * len(shape))
    return pl.pallas_call(
        functools.partial(_s5_kernel, bsz=bsz),
        grid=(rows // blk,),
        in_specs=[
            pl.BlockSpec((blk, S5_WIDTH), lambda k: (k, 0)),
            fixed(wb.shape), fixed(wc.shape),
            fixed((1, S5_CH)), fixed((1, S5_CH)),
            fixed((1, S5_WIDTH)), fixed(glu_w.shape), fixed((1, S5_WIDTH)),
        ],
        out_specs=pl.BlockSpec((blk, S5_WIDTH), lambda k: (k, 0)),
        out_shape=jax.ShapeDtypeStruct((rows, S5_WIDTH), BF16),
        scratch_shapes=[
            pltpu.VMEM((2 * S5_CH // LANE, blk, LANE), F32),
            pltpu.VMEM((2 * S5_CH // LANE, bsz, LANE), F32),
        ],
        compiler_params=pltpu.CompilerParams(
            dimension_semantics=("arbitrary",), vmem_limit_bytes=VMEM_LIMIT),
        name="s5",
    )(u_tb, wb, wc, a_re.reshape(1, S5_CH), a_im.reshape(1, S5_CH),
      d_skip.reshape(1, S5_WIDTH), glu_w, glu_b.reshape(1, S5_WIDTH))


def _s5_params(a_re, a_im, b_re, b_im, c_re, c_im, log_dt):
    delta = jnp.exp(log_dt)[:, None]
    mag = jnp.exp(a_re * delta)
    ab_re, ab_im = mag * jnp.cos(a_im * delta), mag * jnp.sin(a_im * delta)
    den = a_re * a_re + a_im * a_im
    p_re, p_im = ab_re - 1.0, ab_im
    f_re = (p_re * a_re + p_im * a_im) / den
    f_im = (p_im * a_re - p_re * a_im) / den
    bb_re = f_re[..., None] * b_re - f_im[..., None] * b_im
    bb_im = f_re[..., None] * b_im + f_im[..., None] * b_re
    gpb = LANE // S5_GROUP
    nb = S5_GROUPS // gpb
    eye = jnp.eye(gpb, dtype=F32)
    emb_b = lambda bb: jnp.einsum('mgnj,gh->mgjhn', bb.reshape(nb, gpb, S5_STATE, S5_GROUP),
                                  eye).reshape(nb, LANE, gpb * S5_STATE)
    wb = jnp.concatenate([emb_b(bb_re), emb_b(bb_im)], axis=2)
    emb_c = lambda cc: jnp.einsum('mgin,gh->mgnhi', cc.reshape(nb, gpb, S5_GROUP, S5_STATE),
                                  eye).reshape(nb, gpb * S5_STATE, LANE)
    wc = jnp.concatenate([emb_c(c_re), -emb_c(c_im)], axis=1)
    return ab_re.reshape(-1), ab_im.reshape(-1), wb.astype(BF16), wc.astype(BF16)


def _pack_w_in(w_in):
    d = w_in.shape[0]
    w_in = w_in.astype(BF16)
    gdn_in = GDN_QKV + GDN_QK + 2 * GDN_HEADS
    ssd_conv = SSD_INNER + 2 * SSD_GROUPS * SSD_N
    g_qkv = w_in[:, 0:GDN_QKV]
    g_z = w_in[:, GDN_QKV:GDN_QKV + GDN_QK]
    g_gate = w_in[:, GDN_QKV + GDN_QK:gdn_in]
    s0 = gdn_in
    s_z = w_in[:, s0:s0 + SSD_INNER]
    s_x = w_in[:, s0 + SSD_INNER:s0 + 2 * SSD_INNER]
    s_b = w_in[:, s0 + 2 * SSD_INNER:s0 + 2 * SSD_INNER + SSD_GROUPS * SSD_N]
    s_c = w_in[:, s0 + 2 * SSD_INNER + SSD_GROUPS * SSD_N:s0 + SSD_INNER + ssd_conv]
    s_dt = w_in[:, s0 + SSD_INNER + ssd_conv:s0 + SSD_INNER + ssd_conv + SSD_HEADS]
    u0 = s0 + SSD_INNER + ssd_conv + SSD_HEADS
    s5_u = w_in[:, u0:u0 + S5_WIDTH]
    pad = lambda n: jnp.zeros((d, n), w_in.dtype)
    cols = [s_z, s_x, s5_u, s_b, s_c, s_dt, pad(LANE - SSD_HEADS), g_qkv, g_z,
            g_gate, pad(LANE - 2 * GDN_HEADS), pad(P_PAD - (COL_GDN_GATE + LANE))]
    return jnp.concatenate(cols, axis=1)


def kernel(x, ffn1_norm, ffn1_w_gate, ffn1_w_up, ffn1_w_down, mix_norm, w_in,
           gdn_conv_w, gdn_a_log, gdn_dt_bias, gdn_norm,
           ssd_conv_w, ssd_conv_b, ssd_a_log, ssd_dt_bias, ssd_d, ssd_norm,
           s5_a_re, s5_a_im, s5_b_re, s5_b_im, s5_c_re, s5_c_im, s5_d, s5_log_dt,
           s5_glu_w, s5_glu_b, w_out, ffn2_norm, ffn2_w_gate, ffn2_w_up, ffn2_w_down,
           final_norm):
    bsz, seq, d = x.shape
    depth = w_in.shape[0]
    h = x
    ffn1_w = [_to_bf16(w) for w in (ffn1_w_gate, ffn1_w_up, ffn1_w_down)]
    ffn2_w = [_to_bf16(w) for w in (ffn2_w_gate, ffn2_w_up, ffn2_w_down)]
    w_in_bf = _to_bf16(w_in)
    for i in range(depth):
        h = _ffn(h, ffn1_norm[i], *ffn1_w, final_norm, i, final_norm=False)
        proj, u_s5 = _inproj(h, mix_norm[i], _pack_w_in(w_in_bf[i]), bsz, seq)
        o_gdn = _gdn(proj, bsz, seq, gdn_conv_w[i], gdn_a_log[i], gdn_dt_bias[i], gdn_norm[i])
        o_ssd = _ssd(proj, bsz, seq, ssd_conv_w[i], ssd_conv_b[i], ssd_a_log[i], ssd_dt_bias[i],
                     ssd_d[i], ssd_norm[i])
        ab_re, ab_im, wb, wc = _s5_params(s5_a_re[i], s5_a_im[i], s5_b_re[i], s5_b_im[i],
                                          s5_c_re[i], s5_c_im[i], s5_log_dt[i])
        o_s5 = _s5(u_s5.reshape(seq * bsz, S5_WIDTH), bsz, wb, wc, ab_re, ab_im, s5_d[i],
                   s5_glu_w[i].astype(BF16), s5_glu_b[i])
        wo = w_out[i].astype(BF16)
        h = _outproj(h, o_gdn, o_ssd, o_s5.reshape(seq, bsz * S5_WIDTH), wo[0:GDN_QK],
                     wo[GDN_QK:GDN_QK + SSD_INNER], wo[GDN_QK + SSD_INNER:])
        last = i == depth - 1
        h = _ffn(h, ffn2_norm[i], *ffn2_w, final_norm, i, final_norm=last,
                 out_batched=(bsz, seq) if last else None)
    return h
```

```python
import functools
import math

import jax
import jax.numpy as jnp
from jax import lax
from jax.experimental import pallas as pl
from jax.experimental.pallas import tpu as pltpu

F32 = jnp.float32
BF16 = jnp.bfloat16

RMS_EPS = 1e-6
CHUNK = 64
CONV_K = 4
D_MODEL = 2048
D_FF = 5632

GDN_HEADS = 6
GDN_D = 128
GDN_QK = GDN_HEADS * GDN_D
GDN_QKV = 3 * GDN_QK

SSD_HEADS = 12
SSD_P = 64
SSD_GROUPS = 2
SSD_N = 128
SSD_HPG = SSD_HEADS // SSD_GROUPS
SSD_GW = SSD_HPG * SSD_P
SSD_INNER = SSD_HEADS * SSD_P

S5_WIDTH = 512
S5_GROUP = 16
S5_GROUPS = 32
S5_STATE = 64
S5_CH = S5_GROUPS * S5_STATE

LANE = 128

COL_SSD_Z = 0
COL_SSD_X = 768
COL_S5_U = 1536
COL_SSD_B = 2048
COL_SSD_C = 2304
COL_SSD_DT = 2560
COL_GDN_QKV = 2688
COL_GDN_Z = 4992
COL_GDN_GATE = 5760
P_PAD = 6144

VMEM_LIMIT = 60 * 1024 * 1024

NN = (((1,), (0,)), ((), ()))
NT = (((1,), (1,)), ((), ()))
TN = (((0,), (0,)), ((), ()))


def _dot(a, b, dims=NN):
    return lax.dot_general(a, b, dims, preferred_element_type=F32)


def _hi_lo(a):
    hi = a.astype(BF16)
    lo = (a - hi.astype(F32)).astype(BF16)
    return hi, lo


def _mm(a, b, dims=NN, passes=3):
    if passes == 1:
        return _dot(a.astype(BF16), b.astype(BF16), dims)
    ah, al = _hi_lo(a)
    bh, bl = _hi_lo(b)
    return _dot(ah, bh, dims) + (_dot(ah, bl, dims) + _dot(al, bh, dims))


def _mm01(m01, x, dims=NN):
    hi, lo = _hi_lo(x)
    return _dot(m01, hi, dims) + _dot(m01, lo, dims)


def _mm01_t(x, m01, dims=NN):
    hi, lo = _hi_lo(x)
    return _dot(hi, m01, dims) + _dot(lo, m01, dims)


def _sigmoid(x):
    return 1.0 / (1.0 + jnp.exp(-x))


def _silu(x):
    return x * _sigmoid(x)


def _softplus(x):
    return jnp.maximum(x, 0.0) + jnp.log1p(jnp.exp(-jnp.abs(x)))


def _iota(shape, dim):
    return lax.broadcasted_iota(jnp.int32, shape, dim)


def _causal_conv(x, cw_ref):
    acc = None
    for i in range(CONV_K):
        shift = CONV_K - 1 - i
        xs = x if shift == 0 else pltpu.roll(x, shift, axis=0)
        term = xs[8:] * cw_ref[i:i + 1, :]
        acc = term if acc is None else acc + term
    return acc


def _ffn_kernel(x_ref, nw_ref, wg_ref, wu_ref, wd_ref, fw_ref, o_ref, xn_ref, *, final_norm):
    j = pl.program_id(1)

    @pl.when(j == 0)
    def _():
        x = x_ref[...]
        ms = jnp.mean(x * x, axis=-1, keepdims=True)
        xn_ref[...] = (x * lax.rsqrt(ms + RMS_EPS) * nw_ref[...]).astype(BF16)
        o_ref[...] = x

    xn = xn_ref[...]
    g = _dot(xn, wg_ref[...])
    u = _dot(xn, wu_ref[...])
    a = (0.5 * _silu(g) * u).astype(BF16)
    o_ref[...] += _dot(a, wd_ref[...])

    if final_norm:
        @pl.when(j == pl.num_programs(1) - 1)
        def _():
            h = o_ref[...]
            ms = jnp.mean(h * h, axis=-1, keepdims=True)
            o_ref[...] = h * lax.rsqrt(ms + RMS_EPS) * fw_ref[...]


def _cast_kernel(x_ref, o_ref):
    o_ref[...] = x_ref[...].astype(o_ref.dtype)


def _to_bf16(w, *, rows=256):
    nl, r, c = w.shape
    spec = pl.BlockSpec((None, rows, c), lambda l, i: (l, i, 0))
    return pl.pallas_call(
        _cast_kernel,
        grid=(nl, r // rows),
        in_specs=[spec],
        out_specs=spec,
        out_shape=jax.ShapeDtypeStruct(w.shape, BF16),
        compiler_params=pltpu.CompilerParams(
            dimension_semantics=("parallel", "parallel"), vmem_limit_bytes=VMEM_LIMIT),
        name="to_bf16",
    )(w)


def _ffn(x, nw, wg, wu, wd, fw, layer, *, final_norm, out_batched=None, tm=1024, tf=512):
    d = x.shape[-1]
    t = x.size // d
    f = wg.shape[2]
    if x.ndim == 3:
        nseg = x.shape[1] // tm
        x_spec = pl.BlockSpec((None, tm, d), lambda i, j: (i // nseg, i % nseg, 0))
    else:
        x_spec = pl.BlockSpec((tm, d), lambda i, j: (i, 0))
    if out_batched is None:
        out_spec = pl.BlockSpec((tm, d), lambda i, j: (i, 0))
        out_shape = jax.ShapeDtypeStruct((t, d), F32)
    else:
        oseg = out_batched[1] // tm
        out_spec = pl.BlockSpec((None, tm, d), lambda i, j: (i // oseg, i % oseg, 0))
        out_shape = jax.ShapeDtypeStruct((*out_batched, d), F32)
    return pl.pallas_call(
        functools.partial(_ffn_kernel, final_norm=final_norm),
        grid=(t // tm, f // tf),
        in_specs=[
            x_spec,
            pl.BlockSpec((1, d), lambda i, j: (0, 0)),
            pl.BlockSpec((None, d, tf), lambda i, j: (layer, 0, j)),
            pl.BlockSpec((None, d, tf), lambda i, j: (layer, 0, j)),
            pl.BlockSpec((None, tf, d), lambda i, j: (layer, j, 0)),
            pl.BlockSpec((1, d), lambda i, j: (0, 0)),
        ],
        out_specs=out_spec,
        out_shape=out_shape,
        scratch_shapes=[pltpu.VMEM((tm, d), BF16)],
        compiler_params=pltpu.CompilerParams(
            dimension_semantics=("parallel", "arbitrary"), vmem_limit_bytes=VMEM_LIMIT),
        name="ffn",
    )(x, nw.reshape(1, d), wg, wu, wd, fw.reshape(1, d))


def _inproj_kernel(x_ref, nw_ref, w_ref, o_ref, u5_ref, xn_ref, *, tn):
    j = pl.program_id(1)

    @pl.when(j == 0)
    def _():
        x = x_ref[...]
        ms = jnp.mean(x * x, axis=-1, keepdims=True)
        xn_ref[...] = (x * lax.rsqrt(ms + RMS_EPS) * nw_ref[...]).astype(BF16)

    res = _dot(xn_ref[...], w_ref[...])
    o_ref[...] = res

    @pl.when(j == COL_S5_U // tn)
    def _():
        off = COL_S5_U % tn
        u5_ref[...] = res[:, off:off + S5_WIDTH]


def _inproj(x, nw, w, bsz, seq, *, tm=1024, tn=1024):
    t, d = x.shape
    n = w.shape[1]
    nseg = seq // tm
    return pl.pallas_call(
        functools.partial(_inproj_kernel, tn=tn),
        grid=(t // tm, n // tn),
        in_specs=[
            pl.BlockSpec((tm, d), lambda i, j: (i, 0)),
            pl.BlockSpec((1, d), lambda i, j: (0, 0)),
            pl.BlockSpec((d, tn), lambda i, j: (0, j)),
        ],
        out_specs=[
            pl.BlockSpec((tm, tn), lambda i, j: (i, j)),
            pl.BlockSpec((tm, S5_WIDTH), lambda i, j: (i % nseg, i // nseg)),
        ],
        out_shape=[jax.ShapeDtypeStruct((t, n), F32),
                   jax.ShapeDtypeStruct((seq, bsz * S5_WIDTH), F32)],
        scratch_shapes=[pltpu.VMEM((tm, d), BF16)],
        compiler_params=pltpu.CompilerParams(
            dimension_semantics=("parallel", "arbitrary"), vmem_limit_bytes=VMEM_LIMIT),
        name="inproj",
    )(x, nw.reshape(1, d), w)


def _outproj_kernel(h_ref, a_ref, b_ref, c_ref, wa_ref, wb_ref, wc_ref, o_ref):
    acc = _dot(a_ref[...], wa_ref[...])
    acc += _dot(b_ref[...], wb_ref[...])
    acc += _dot(c_ref[...], wc_ref[...])
    o_ref[...] = h_ref[...] + acc


def _outproj(h, oa, ob, oc, wa, wb, wc, *, tm=512):
    t, d = h.shape
    nseg = oc.shape[0] // tm
    row = lambda i: (i, 0)
    fixed = lambda i: (0, 0)
    return pl.pallas_call(
        _outproj_kernel,
        grid=(t // tm,),
        in_specs=[
            pl.BlockSpec((tm, d), row),
            pl.BlockSpec((tm, oa.shape[1]), row),
            pl.BlockSpec((tm, ob.shape[1]), row),
            pl.BlockSpec((tm, S5_WIDTH), lambda i: (i % nseg, i // nseg)),
            pl.BlockSpec(wa.shape, fixed),
            pl.BlockSpec(wb.shape, fixed),
            pl.BlockSpec(wc.shape, fixed),
        ],
        out_specs=pl.BlockSpec((tm, d), row),
        out_shape=jax.ShapeDtypeStruct((t, d), F32),
        compiler_params=pltpu.CompilerParams(
            dimension_semantics=("parallel",), vmem_limit_bytes=VMEM_LIMIT),
        name="outproj",
    )(h, oa, ob, oc, wa, wb, wc)


def _chunk_rows(x_ref, lanes, ci, c):
    r0 = ci * c
    cur = x_ref[pl.ds(pl.multiple_of(r0, c), c), lanes]
    prev = x_ref[pl.ds(pl.multiple_of(jnp.maximum(r0 - 8, 0), 8), 8), lanes]
    return jnp.concatenate([jnp.where(ci == 0, 0.0, prev), cur], axis=0)


def _gdn_kernel(alog_ref, dtb_ref,
                q_ref, k_ref, v_ref, z_ref, gate_ref,
                cwq_ref, cwk_ref, cwv_ref, nw_ref,
                o_ref,
                u_ref, w_ref, qd_ref, kd_ref, at_ref, eg_ref,
                *, hb, p_inv, p_mix, unroll):
    hg = pl.program_id(1)
    seq = q_ref.shape[0]
    n_chunks = seq // CHUNK
    c = CHUNK
    mix_dt = BF16 if p_mix == 1 else F32

    row = _iota((c, c), 0)
    col = _iota((c, c), 1)
    tril = row >= col
    strict = row > col
    ltri = tril.astype(BF16)
    ones = jnp.ones((c, c), BF16)
    eye = (row == col).astype(F32)
    lane = _iota((c, LANE), 1)

    def prep_stages(trip):
        items = [(trip * unroll + cc, hh) for cc in range(unroll) for hh in range(hb)]
        lanes = [slice(hh * LANE, (hh + 1) * LANE) for _, hh in items]
        rows = [pl.ds(pl.multiple_of(ci * c, c), c) for ci, _ in items]

        def stage_a(ci, hh, ln, rw):
            h = hg * hb + hh
            neg_a = -jnp.exp(jnp.full((c, LANE), alog_ref[h], F32))
            dt_bias = jnp.full((c, LANE), dtb_ref[h], F32)
            q = _silu(_causal_conv(_chunk_rows(q_ref, ln, ci, c), cwq_ref.at[:, ln]))
            k = _silu(_causal_conv(_chunk_rows(k_ref, ln, ci, c), cwk_ref.at[:, ln]))
            v = _silu(_causal_conv(_chunk_rows(v_ref, ln, ci, c), cwv_ref.at[:, ln]))
            qn = q * lax.rsqrt(jnp.sum(q * q, axis=-1, keepdims=True) + RMS_EPS) * (GDN_D ** -0.5)
            kn = k * lax.rsqrt(jnp.sum(k * k, axis=-1, keepdims=True) + RMS_EPS)
            gt = gate_ref[rw, :]
            b_raw = jnp.sum(jnp.where(lane == h, gt, 0.0), axis=-1, keepdims=True)
            a_raw = jnp.sum(jnp.where(lane == h + GDN_HEADS, gt, 0.0), axis=-1, keepdims=True)
            beta = _sigmoid(b_raw)
            g_b = neg_a * _softplus(jnp.broadcast_to(a_raw, (c, LANE)) + dt_bias)
            return qn, kn, v, beta, g_b

        sa = [stage_a(ci, hh, ln, rw) for (ci, hh), ln, rw in zip(items, lanes, rows)]
        qn = [x[0] for x in sa]
        kn = [x[1] for x in sa]
        v = [x[2] for x in sa]
        beta = [x[3] for x in sa]
        g_b = [x[4] for x in sa]
        yield
        gi = [_mm01(ltri, g) for g in g_b]
        gj = [_mm01(ones, jnp.where(row <= col, g[:, :c], 0.0)) for g in g_b]
        yield
        kb = [k * b for k, b in zip(kn, beta)]
        kk = [_mm(a, b, NT, p_inv) for a, b in zip(kb, kn)]
        qk = [_mm(a, b, NT, p_mix) for a, b in zip(qn, kn)]
        decay = [jnp.where(tril, jnp.exp(jnp.where(tril, a[:, :c] - b, 0.0)), 0.0)
                 for a, b in zip(gi, gj)]
        npow = [-jnp.where(strict, a * d, 0.0) for a, d in zip(kk, decay)]
        tinv = [eye + n for n in npow]
        for _ in range(5):
            yield
            npow = [_mm(n, n, NN, p_inv) for n in npow]
            tinv = [t + _mm(t, n, NN, p_inv) for t, n in zip(tinv, npow)]
        yield
        exp_g = [jnp.exp(g) for g in gi]
        rhs = [jnp.concatenate([vv * b, k * e], axis=1) for vv, b, k, e in zip(v, beta, kb, exp_g)]
        uw = [_mm(t, r, NN, p_inv) for t, r in zip(tinv, rhs)]
        for i, ((ci, hh), rw) in enumerate(zip(items, rows)):
            g_last = jnp.broadcast_to(gi[i][c - 1:c, :], (c, LANE))
            u_ref[hh, rw, :] = uw[i][:, :LANE]
            w_ref[hh, rw, :] = uw[i][:, LANE:].astype(mix_dt)
            at_ref[hh, rw, :] = jnp.where(tril, qk[i] * decay[i], 0.0).astype(mix_dt)
            qd_ref[hh, rw, :] = (qn[i] * exp_g[i]).astype(mix_dt)
            kd_ref[hh, rw, :] = (kn[i] * jnp.exp(g_last - gi[i])).astype(mix_dt)
            eg_ref[hh, pl.ds(pl.multiple_of(ci * 8, 8), 8), :] = jnp.exp(g_last[0:8, :])

    norm_w = nw_ref[...]

    def scan_stages(trip, states, final):
        for cc in range(unroll):
            ci = trip * unroll + cc
            rw = pl.ds(pl.multiple_of(ci * c, c), c)
            lhs = [jnp.concatenate([w_ref[hh, rw, :], qd_ref[hh, rw, :]], axis=0)
                   for hh in range(hb)]
            ws_qs = [_mm(a, s, NN, p_mix) for a, s in zip(lhs, states)]
            yield
            v_new = [u_ref[hh, rw, :] - ws_qs[hh][:c] for hh in range(hb)]
            kv = [_mm(kd_ref[hh, rw, :], v_new[hh], TN, p_mix) for hh in range(hb)]
            av = [_mm(at_ref[hh, rw, :], v_new[hh], NN, p_mix) for hh in range(hb)]
            new_states = []
            for hh in range(hb):
                eg = eg_ref[hh, pl.ds(pl.multiple_of(ci * 8, 8), 8), :]
                new_states.append(states[hh] * jnp.broadcast_to(eg[0:1, :], (GDN_D, GDN_D)) + kv[hh])
                o = ws_qs[hh][c:] + av[hh]
                ms = jnp.mean(o * o, axis=-1, keepdims=True)
                ln = slice(hh * LANE, (hh + 1) * LANE)
                y = o * lax.rsqrt(ms + RMS_EPS) * norm_w * _silu(z_ref[rw, ln])
                o_ref[rw, ln] = y.astype(o_ref.dtype)
            states = new_states
            yield
        final.extend(states)

    def drain(gen):
        for _ in gen:
            pass

    def fused(trip, states):
        final = []
        scan_gen = scan_stages(trip - 1, states, final)
        for _ in prep_stages(trip):
            next(scan_gen, None)
        drain(scan_gen)
        return tuple(final)

    n_trips = n_chunks // unroll
    drain(prep_stages(0))
    states = lax.fori_loop(1, n_trips, fused,
                           tuple(jnp.zeros((GDN_D, GDN_D), F32) for _ in range(hb)))
    drain(scan_stages(n_trips - 1, states, []))


def _gdn(proj, bsz, seq, conv_w, a_log, dt_bias, norm_w, *, hb=3, p_inv=1, p_mix=1, unroll=4):
    t = proj.shape[0]
    wd = hb * LANE
    mix_dt = BF16 if p_mix == 1 else F32
    q0 = COL_GDN_QKV // wd
    z0 = COL_GDN_Z // wd
    g0 = COL_GDN_GATE // LANE
    ng = GDN_HEADS // hb
    blk = lambda off: pl.BlockSpec((seq, wd), lambda b, h, off=off: (b, off + h))
    cw = lambda off: pl.BlockSpec((CONV_K, wd), lambda b, h, off=off: (0, off + h))
    smem = pl.BlockSpec(memory_space=pltpu.SMEM)
    return pl.pallas_call(
        functools.partial(_gdn_kernel, hb=hb, p_inv=p_inv, p_mix=p_mix, unroll=unroll),
        grid=(bsz, ng),
        in_specs=[
            smem, smem,
            blk(q0), blk(q0 + ng), blk(q0 + 2 * ng), blk(z0),
            pl.BlockSpec((seq, LANE), lambda b, h: (b, g0)),
            cw(0), cw(ng), cw(2 * ng),
            pl.BlockSpec((1, LANE), lambda b, h: (0, 0)),
        ],
        out_specs=pl.BlockSpec((seq, wd), lambda b, h: (b, h)),
        out_shape=jax.ShapeDtypeStruct((t, GDN_QK), BF16),
        scratch_shapes=[
            pltpu.VMEM((hb, seq, LANE), F32),
            pltpu.VMEM((hb, seq, LANE), mix_dt),
            pltpu.VMEM((hb, seq, LANE), mix_dt),
            pltpu.VMEM((hb, seq, LANE), mix_dt),
            pltpu.VMEM((hb, seq, CHUNK), mix_dt),
            pltpu.VMEM((hb, seq // CHUNK * 8, LANE), F32),
        ],
        compiler_params=pltpu.CompilerParams(
            dimension_semantics=("parallel", "arbitrary"), vmem_limit_bytes=VMEM_LIMIT),
        name="gdn",
    )(a_log, dt_bias, proj, proj, proj, proj, proj, conv_w, conv_w, conv_w,
      norm_w.reshape(1, LANE))


def _ssd_kernel(z_ref, x_ref, b_ref, c_ref, dt_ref,
                cwx_ref, cwb_ref, cwc_ref,
                cbx_ref, cbb_ref, cbc_ref,
                alog_ref, dtb_ref, dsk_ref, nw_ref,
                o_ref,
                s_ref, *, passes, unroll):
    grp = pl.program_id(1)
    seq = x_ref.shape[0]
    n_chunks = seq // CHUNK
    c = CHUNK
    gw = SSD_GW

    row = _iota((c, c), 0)
    col = _iota((c, c), 1)
    ltri = (row >= col).astype(BF16)
    ones = jnp.ones((c, c), BF16)
    rowt = _iota((c, gw), 0)
    colt = jnp.bitwise_and(_iota((c, gw), 1), c - 1)
    tril_t = rowt >= colt
    upper_t = rowt <= colt
    expand = (_iota((LANE, gw), 0) == grp * SSD_HPG + lax.shift_right_logical(_iota((LANE, gw), 1), 6)).astype(BF16)
    last_row = (_iota((c, gw), 0) == c - 1)
    ones_cn = jnp.ones((c, SSD_N), BF16)

    neg_a = -jnp.exp(alog_ref[...])
    dt_bias = dtb_ref[...]
    d_skip = dsk_ref[...]
    norm_w = nw_ref[...]

    def conv(x_ref_, cw_ref, cb_ref, ci):
        return _silu(_causal_conv(_chunk_rows(x_ref_, slice(None), ci, c), cw_ref) + cb_ref[...])

    s_ref[...] = jnp.zeros((gw, SSD_N), F32)
    heads = [slice(hh * SSD_P, (hh + 1) * SSD_P) for hh in range(SSD_HPG)]

    def body(trip, carry):
        cis = [trip * unroll + cc for cc in range(unroll)]
        rows = [pl.ds(pl.multiple_of(ci * c, c), c) for ci in cis]
        xs = [conv(x_ref, cwx_ref, cbx_ref, ci) for ci in cis]
        bm = [conv(b_ref, cwb_ref, cbb_ref, ci) for ci in cis]
        cm = [conv(c_ref, cwc_ref, cbc_ref, ci) for ci in cis]
        dt = [_softplus(_mm01_t(dt_ref[rw, :], expand) + dt_bias) for rw in rows]
        a = [d * neg_a for d in dt]
        acum = [_mm01(ltri, x) for x in a]
        acum_j = [_mm01(ones, jnp.where(upper_t, x, 0.0)) for x in a]
        cb = [_mm(x, y, NT, passes) for x, y in zip(cm, bm)]
        lm = [jnp.where(tril_t, jnp.exp(jnp.where(tril_t, x - y, 0.0)), 0.0)
              for x, y in zip(acum, acum_j)]
        xdt = [x * d for x, d in zip(xs, dt)]
        y_diag = [jnp.concatenate([_mm(cb[i] * lm[i][:, sl], xdt[i][:, sl], NN, passes)
                                   for sl in heads], axis=1) for i in range(unroll)]
        a_last = [jnp.broadcast_to(x[c - 1:c, :], (c, gw)) for x in acum]
        states = [_mm(xdt[i] * jnp.exp(a_last[i] - acum[i]), bm[i], TN, passes)
                  for i in range(unroll)]
        dec_col = [jnp.exp(_mm01_t(jnp.where(last_row, x, 0.0), ones_cn, TN)) for x in acum]
        s_prev = s_ref[...]
        for i in range(unroll):
            y_off = _mm(cm[i], s_prev, NT, passes) * jnp.exp(acum[i])
            s_prev = s_prev * dec_col[i] + states[i]
            y = y_diag[i] + y_off + d_skip * xs[i]
            y = y * _silu(z_ref[rows[i], :])
            ms = jnp.mean(y * y, axis=-1, keepdims=True)
            o_ref[rows[i], :] = (y * lax.rsqrt(ms + RMS_EPS) * norm_w).astype(o_ref.dtype)
        s_ref[...] = s_prev
        return carry

    lax.fori_loop(0, n_chunks // unroll, body, 0)


def _ssd(proj, bsz, seq, conv_w, conv_b, a_log, dt_bias, d_skip, norm_w, *, passes=1, unroll=4):
    t = proj.shape[0]
    gw = SSD_GW
    rep = lambda v: jnp.repeat(v.astype(F32), SSD_P).reshape(1, SSD_INNER)
    cb = conv_b.reshape(1, -1)
    wide = lambda off: pl.BlockSpec((seq, gw), lambda b, g, off=off: (b, off + g))
    lane = lambda off: pl.BlockSpec((seq, LANE), lambda b, g, off=off: (b, off + g))
    chan = pl.BlockSpec((1, gw), lambda b, g: (0, g))
    return pl.pallas_call(
        functools.partial(_ssd_kernel, passes=passes, unroll=unroll),
        grid=(bsz, SSD_GROUPS),
        in_specs=[
            wide(COL_SSD_Z // gw), wide(COL_SSD_X // gw),
            lane(COL_SSD_B // LANE), lane(COL_SSD_C // LANE),
            pl.BlockSpec((seq, LANE), lambda b, g: (b, COL_SSD_DT // LANE)),
            pl.BlockSpec((CONV_K, gw), lambda b, g: (0, g)),
            pl.BlockSpec((CONV_K, LANE), lambda b, g: (0, SSD_INNER // LANE + g)),
            pl.BlockSpec((CONV_K, LANE), lambda b, g: (0, SSD_INNER // LANE + SSD_GROUPS + g)),
            pl.BlockSpec((1, gw), lambda b, g: (0, g)),
            pl.BlockSpec((1, LANE), lambda b, g: (0, SSD_INNER // LANE + g)),
            pl.BlockSpec((1, LANE), lambda b, g: (0, SSD_INNER // LANE + SSD_GROUPS + g)),
            chan, chan, chan, chan,
        ],
        out_specs=pl.BlockSpec((seq, gw), lambda b, g: (b, g)),
        out_shape=jax.ShapeDtypeStruct((t, SSD_INNER), BF16),
        scratch_shapes=[pltpu.VMEM((gw, SSD_N), F32)],
        compiler_params=pltpu.CompilerParams(
            dimension_semantics=("parallel", "arbitrary"), vmem_limit_bytes=VMEM_LIMIT),
        name="ssd",
    )(proj, proj, proj, proj, proj, conv_w, conv_w, conv_w, cb, cb, cb,
      rep(a_log), rep(dt_bias), rep(d_skip), norm_w.reshape(1, SSD_INNER))


def _s5_kernel(u_ref, wb_ref, wc_ref, are_ref, aim_ref, dsk_ref, gw_ref, gb_ref,
               o_ref, bu_ref, h_ref, *, bsz):
    ts = u_ref.shape[0] // bsz
    nt = S5_CH // LANE
    nb = S5_WIDTH // LANE
    per = nt // nb
    half = per * LANE

    @pl.when(pl.program_id(0) == 0)
    def _():
        h_ref[...] = jnp.zeros(h_ref.shape, F32)

    u = u_ref[...]
    u_bf = u.astype(BF16)
    for m in range(nb):
        res = _dot(u_bf[:, m * LANE:(m + 1) * LANE], wb_ref[m])
        for k in range(per):
            bu_ref[m * per + k] = res[:, k * LANE:(k + 1) * LANE]
            bu_ref[nt + m * per + k] = res[:, half + k * LANE:half + (k + 1) * LANE]

    def step(t, carry):
        rows = pl.ds(pl.multiple_of(t * bsz, bsz), bsz)
        new_re, new_im = [], []
        for k in range(nt):
            h_re, h_im = carry[k], carry[nt + k]
            a_re = are_ref[:, k * LANE:(k + 1) * LANE]
            a_im = aim_ref[:, k * LANE:(k + 1) * LANE]
            n_re = a_re * h_re - a_im * h_im + bu_ref[k, rows, :]
            n_im = a_re * h_im + a_im * h_re + bu_ref[nt + k, rows, :]
            bu_ref[k, rows, :] = n_re
            bu_ref[nt + k, rows, :] = n_im
            new_re.append(n_re)
            new_im.append(n_im)
        return tuple(new_re + new_im)

    h_last = lax.fori_loop(0, ts, step, tuple(h_ref[k] for k in range(2 * nt)))
    for k in range(2 * nt):
        h_ref[k] = h_last[k]

    y_parts = []
    for m in range(nb):
        hs = jnp.concatenate([bu_ref[m * per + k] for k in range(per)] +
                             [bu_ref[nt + m * per + k] for k in range(per)], axis=1)
        y_parts.append(_dot(hs.astype(BF16), wc_ref[m]))
    y = jnp.concatenate(y_parts, axis=1) + dsk_ref[...] * u
    g = 0.5 * y * (1.0 + jnp.tanh(math.sqrt(2.0 / math.pi) * (y + 0.044715 * (y * y * y))))
    out = g * _sigmoid(_dot(g.astype(BF16), gw_ref[...]) + gb_ref[...])
    o_ref[...] = out.astype(o_ref.dtype)


def _s5(u_tb, bsz, wb, wc, a_re, a_im, d_skip, glu_w, glu_b, *, ts=128):
    rows = u_tb.shape[0]
    blk = ts * bsz
    fixed = lambda shape: pl.BlockSpec(shape, lambda k: (0,) * len(shape))
    return pl.pallas_call(
        functools.partial(_s5_kernel, bsz=bsz),
        grid=(rows // blk,),
        in_specs=[
            pl.BlockSpec((blk, S5_WIDTH), lambda k: (k, 0)),
            fixed(wb.shape), fixed(wc.shape),
            fixed((1, S5_CH)), fixed((1, S5_CH)),
            fixed((1, S5_WIDTH)), fixed(glu_w.shape), fixed((1, S5_WIDTH)),
        ],
        out_specs=pl.BlockSpec((blk, S5_WIDTH), lambda k: (k, 0)),
        out_shape=jax.ShapeDtypeStruct((rows, S5_WIDTH), BF16),
        scratch_shapes=[
            pltpu.VMEM((2 * S5_CH // LANE, blk, LANE), F32),
            pltpu.VMEM((2 * S5_CH // LANE, bsz, LANE), F32),
        ],
        compiler_params=pltpu.CompilerParams(
            dimension_semantics=("arbitrary",), vmem_limit_bytes=VMEM_LIMIT),
        name="s5",
    )(u_tb, wb, wc, a_re.reshape(1, S5_CH), a_im.reshape(1, S5_CH),
      d_skip.reshape(1, S5_WIDTH), glu_w, glu_b.reshape(1, S5_WIDTH))


def _s5_params(a_re, a_im, b_re, b_im, c_re, c_im, log_dt):
    delta = jnp.exp(log_dt)[:, None]
    mag = jnp.exp(a_re * delta)
    ab_re, ab_im = mag * jnp.cos(a_im * delta), mag * jnp.sin(a_im * delta)
    den = a_re * a_re + a_im * a_im
    p_re, p_im = ab_re - 1.0, ab_im
    f_re = (p_re * a_re + p_im * a_im) / den
    f_im = (p_im * a_re - p_re * a_im) / den
    bb_re = f_re[..., None] * b_re - f_im[..., None] * b_im
    bb_im = f_re[..., None] * b_im + f_im[..., None] * b_re
    gpb = LANE // S5_GROUP
    nb = S5_GROUPS // gpb
    eye = jnp.eye(gpb, dtype=F32)
    emb_b = lambda bb: jnp.einsum('mgnj,gh->mgjhn', bb.reshape(nb, gpb, S5_STATE, S5_GROUP),
                                  eye).reshape(nb, LANE, gpb * S5_STATE)
    wb = jnp.concatenate([emb_b(bb_re), emb_b(bb_im)], axis=2)
    emb_c = lambda cc: jnp.einsum('mgin,gh->mgnhi', cc.reshape(nb, gpb, S5_GROUP, S5_STATE),
                                  eye).reshape(nb, gpb * S5_STATE, LANE)
    wc = jnp.concatenate([emb_c(c_re), -emb_c(c_im)], axis=1)
    return ab_re.reshape(-1), ab_im.reshape(-1), wb.astype(BF16), wc.astype(BF16)


def _pack_w_in(w_in):
    d = w_in.shape[0]
    w_in = w_in.astype(BF16)
    gdn_in = GDN_QKV + GDN_QK + 2 * GDN_HEADS
    ssd_conv = SSD_INNER + 2 * SSD_GROUPS * SSD_N
    g_qkv = w_in[:, 0:GDN_QKV]
    g_z = w_in[:, GDN_QKV:GDN_QKV + GDN_QK]
    g_gate = w_in[:, GDN_QKV + GDN_QK:gdn_in]
    s0 = gdn_in
    s_z = w_in[:, s0:s0 + SSD_INNER]
    s_x = w_in[:, s0 + SSD_INNER:s0 + 2 * SSD_INNER]
    s_b = w_in[:, s0 + 2 * SSD_INNER:s0 + 2 * SSD_INNER + SSD_GROUPS * SSD_N]
    s_c = w_in[:, s0 + 2 * SSD_INNER + SSD_GROUPS * SSD_N:s0 + SSD_INNER + ssd_conv]
    s_dt = w_in[:, s0 + SSD_INNER + ssd_conv:s0 + SSD_INNER + ssd_conv + SSD_HEADS]
    u0 = s0 + SSD_INNER + ssd_conv + SSD_HEADS
    s5_u = w_in[:, u0:u0 + S5_WIDTH]
    pad = lambda n: jnp.zeros((d, n), w_in.dtype)
    cols = [s_z, s_x, s5_u, s_b, s_c, s_dt, pad(LANE - SSD_HEADS), g_qkv, g_z,
            g_gate, pad(LANE - 2 * GDN_HEADS), pad(P_PAD - (COL_GDN_GATE + LANE))]
    return jnp.concatenate(cols, axis=1)


def kernel(x, ffn1_norm, ffn1_w_gate, ffn1_w_up, ffn1_w_down, mix_norm, w_in,
           gdn_conv_w, gdn_a_log, gdn_dt_bias, gdn_norm,
           ssd_conv_w, ssd_conv_b, ssd_a_log, ssd_dt_bias, ssd_d, ssd_norm,
           s5_a_re, s5_a_im, s5_b_re, s5_b_im, s5_c_re, s5_c_im, s5_d, s5_log_dt,
           s5_glu_w, s5_glu_b, w_out, ffn2_norm, ffn2_w_gate, ffn2_w_up, ffn2_w_down,
           final_norm):
    bsz, seq, d = x.shape
    depth = w_in.shape[0]
    h = x
    ffn1_w = [_to_bf16(w) for w in (ffn1_w_gate, ffn1_w_up, ffn1_w_down)]
    ffn2_w = [_to_bf16(w) for w in (ffn2_w_gate, ffn2_w_up, ffn2_w_down)]
    w_in_bf = w_in.astype(BF16)
    for i in range(depth):
        h = _ffn(h, ffn1_norm[i], *ffn1_w, final_norm, i, final_norm=False)
        proj, u_s5 = _inproj(h, mix_norm[i], _pack_w_in(w_in_bf[i]), bsz, seq)
        o_gdn = _gdn(proj, bsz, seq, gdn_conv_w[i], gdn_a_log[i], gdn_dt_bias[i], gdn_norm[i])
        o_ssd = _ssd(proj, bsz, seq, ssd_conv_w[i], ssd_conv_b[i], ssd_a_log[i], ssd_dt_bias[i],
                     ssd_d[i], ssd_norm[i])
        ab_re, ab_im, wb, wc = _s5_params(s5_a_re[i], s5_a_im[i], s5_b_re[i], s5_b_im[i],
                                          s5_c_re[i], s5_c_im[i], s5_log_dt[i])
        o_s5 = _s5(u_s5.reshape(seq * bsz, S5_WIDTH), bsz, wb, wc, ab_re, ab_im, s5_d[i],
                   s5_glu_w[i].astype(BF16), s5_glu_b[i])
        wo = w_out[i].astype(BF16)
        h = _outproj(h, o_gdn, o_ssd, o_s5.reshape(seq, bsz * S5_WIDTH), wo[0:GDN_QK],
                     wo[GDN_QK:GDN_QK + SSD_INNER], wo[GDN_QK + SSD_INNER:])
        last = i == depth - 1
        h = _ffn(h, ffn2_norm[i], *ffn2_w, final_norm, i, final_norm=last,
                 out_batched=(bsz, seq) if last else None)
    return h
```

```python
import functools
import math

import jax
import jax.numpy as jnp
from jax import lax
from jax.experimental import pallas as pl
from jax.experimental.pallas import tpu as pltpu

F32 = jnp.float32
BF16 = jnp.bfloat16

RMS_EPS = 1e-6
CHUNK = 64
CONV_K = 4
D_MODEL = 2048
D_FF = 5632

GDN_HEADS = 6
GDN_D = 128
GDN_QK = GDN_HEADS * GDN_D
GDN_QKV = 3 * GDN_QK

SSD_HEADS = 12
SSD_P = 64
SSD_GROUPS = 2
SSD_N = 128
SSD_HPG = SSD_HEADS // SSD_GROUPS
SSD_GW = SSD_HPG * SSD_P
SSD_INNER = SSD_HEADS * SSD_P

S5_WIDTH = 512
S5_GROUP = 16
S5_GROUPS = 32
S5_STATE = 64
S5_CH = S5_GROUPS * S5_STATE

LANE = 128

COL_SSD_Z = 0
COL_SSD_X = 768
COL_S5_U = 1536
COL_SSD_B = 2048
COL_SSD_C = 2304
COL_SSD_DT = 2560
COL_GDN_QKV = 2688
COL_GDN_Z = 4992
COL_GDN_GATE = 5760
P_PAD = 6144

VMEM_LIMIT = 60 * 1024 * 1024

NN = (((1,), (0,)), ((), ()))
NT = (((1,), (1,)), ((), ()))
TN = (((0,), (0,)), ((), ()))


def _dot(a, b, dims=NN):
    return lax.dot_general(a, b, dims, preferred_element_type=F32)


def _hi_lo(a):
    hi = a.astype(BF16)
    lo = (a - hi.astype(F32)).astype(BF16)
    return hi, lo


def _mm(a, b, dims=NN, passes=3):
    if passes == 1:
        return _dot(a.astype(BF16), b.astype(BF16), dims)
    ah, al = _hi_lo(a)
    bh, bl = _hi_lo(b)
    return _dot(ah, bh, dims) + (_dot(ah, bl, dims) + _dot(al, bh, dims))


def _mm01(m01, x, dims=NN):
    hi, lo = _hi_lo(x)
    return _dot(m01, hi, dims) + _dot(m01, lo, dims)


def _mm01_t(x, m01, dims=NN):
    hi, lo = _hi_lo(x)
    return _dot(hi, m01, dims) + _dot(lo, m01, dims)


def _sigmoid(x):
    return 1.0 / (1.0 + jnp.exp(-x))


def _silu(x):
    return x * _sigmoid(x)


def _softplus(x):
    return jnp.maximum(x, 0.0) + jnp.log1p(jnp.exp(-jnp.abs(x)))


def _iota(shape, dim):
    return lax.broadcasted_iota(jnp.int32, shape, dim)


def _causal_conv(x, cw_ref):
    acc = None
    for i in range(CONV_K):
        shift = CONV_K - 1 - i
        xs = x if shift == 0 else pltpu.roll(x, shift, axis=0)
        term = xs[8:] * cw_ref[i:i + 1, :]
        acc = term if acc is None else acc + term
    return acc


def _ffn_kernel(x_ref, nw_ref, wg_ref, wu_ref, wd_ref, fw_ref, o_ref, xn_ref, *, final_norm):
    j = pl.program_id(1)

    @pl.when(j == 0)
    def _():
        x = x_ref[...]
        ms = jnp.mean(x * x, axis=-1, keepdims=True)
        xn_ref[...] = (x * lax.rsqrt(ms + RMS_EPS) * nw_ref[...]).astype(BF16)
        o_ref[...] = x

    xn = xn_ref[...]
    g = _dot(xn, wg_ref[...])
    u = _dot(xn, wu_ref[...])
    a = (0.5 * _silu(g) * u).astype(BF16)
    o_ref[...] += _dot(a, wd_ref[...])

    if final_norm:
        @pl.when(j == pl.num_programs(1) - 1)
        def _():
            h = o_ref[...]
            ms = jnp.mean(h * h, axis=-1, keepdims=True)
            o_ref[...] = h * lax.rsqrt(ms + RMS_EPS) * fw_ref[...]


def _cast_kernel(x_ref, o_ref):
    o_ref[...] = x_ref[...].astype(o_ref.dtype)


def _to_bf16(w, *, rows=256):
    nl, r, c = w.shape
    spec = pl.BlockSpec((None, rows, c), lambda l, i: (l, i, 0))
    return pl.pallas_call(
        _cast_kernel,
        grid=(nl, r // rows),
        in_specs=[spec],
        out_specs=spec,
        out_shape=jax.ShapeDtypeStruct(w.shape, BF16),
        compiler_params=pltpu.CompilerParams(
            dimension_semantics=("parallel", "parallel"), vmem_limit_bytes=VMEM_LIMIT),
        name="to_bf16",
    )(w)


def _ffn(x, nw, wg, wu, wd, fw, layer, *, final_norm, out_batched=None, tm=1024, tf=512):
    d = x.shape[-1]
    t = x.size // d
    f = wg.shape[2]
    if x.ndim == 3:
        nseg = x.shape[1] // tm
        x_spec = pl.BlockSpec((None, tm, d), lambda i, j: (i // nseg, i % nseg, 0))
    else:
        x_spec = pl.BlockSpec((tm, d), lambda i, j: (i, 0))
    if out_batched is None:
        out_spec = pl.BlockSpec((tm, d), lambda i, j: (i, 0))
        out_shape = jax.ShapeDtypeStruct((t, d), F32)
    else:
        oseg = out_batched[1] // tm
        out_spec = pl.BlockSpec((None, tm, d), lambda i, j: (i // oseg, i % oseg, 0))
        out_shape = jax.ShapeDtypeStruct((*out_batched, d), F32)
    return pl.pallas_call(
        functools.partial(_ffn_kernel, final_norm=final_norm),
        grid=(t // tm, f // tf),
        in_specs=[
            x_spec,
            pl.BlockSpec((1, d), lambda i, j: (0, 0)),
            pl.BlockSpec((None, d, tf), lambda i, j: (layer, 0, j)),
            pl.BlockSpec((None, d, tf), lambda i, j: (layer, 0, j)),
            pl.BlockSpec((None, tf, d), lambda i, j: (layer, j, 0)),
            pl.BlockSpec((1, d), lambda i, j: (0, 0)),
        ],
        out_specs=out_spec,
        out_shape=out_shape,
        scratch_shapes=[pltpu.VMEM((tm, d), BF16)],
        compiler_params=pltpu.CompilerParams(
            dimension_semantics=("parallel", "arbitrary"), vmem_limit_bytes=VMEM_LIMIT),
        name="ffn",
    )(x, nw.reshape(1, d), wg, wu, wd, fw.reshape(1, d))


def _inproj_kernel(x_ref, nw_ref, w_ref, o_ref, u5_ref, xn_ref, *, tn):
    j = pl.program_id(1)

    @pl.when(j == 0)
    def _():
        x = x_ref[...]
        ms = jnp.mean(x * x, axis=-1, keepdims=True)
        xn_ref[...] = (x * lax.rsqrt(ms + RMS_EPS) * nw_ref[...]).astype(BF16)

    res = _dot(xn_ref[...], w_ref[...])
    o_ref[...] = res

    @pl.when(j == COL_S5_U // tn)
    def _():
        off = COL_S5_U % tn
        u5_ref[...] = res[:, off:off + S5_WIDTH]


def _inproj(x, nw, w, bsz, seq, *, tm=1024, tn=2048):
    t, d = x.shape
    n = w.shape[1]
    nseg = seq // tm
    return pl.pallas_call(
        functools.partial(_inproj_kernel, tn=tn),
        grid=(t // tm, n // tn),
        in_specs=[
            pl.BlockSpec((tm, d), lambda i, j: (i, 0)),
            pl.BlockSpec((1, d), lambda i, j: (0, 0)),
            pl.BlockSpec((d, tn), lambda i, j: (0, j)),
        ],
        out_specs=[
            pl.BlockSpec((tm, tn), lambda i, j: (i, j)),
            pl.BlockSpec((tm, S5_WIDTH), lambda i, j: (i % nseg, i // nseg)),
        ],
        out_shape=[jax.ShapeDtypeStruct((t, n), F32),
                   jax.ShapeDtypeStruct((seq, bsz * S5_WIDTH), F32)],
        scratch_shapes=[pltpu.VMEM((tm, d), BF16)],
        compiler_params=pltpu.CompilerParams(
            dimension_semantics=("parallel", "arbitrary"), vmem_limit_bytes=VMEM_LIMIT),
        name="inproj",
    )(x, nw.reshape(1, d), w)


def _outproj_kernel(h_ref, a_ref, b_ref, c_ref, wa_ref, wb_ref, wc_ref, o_ref):
    acc = _dot(a_ref[...], wa_ref[...])
    acc += _dot(b_ref[...], wb_ref[...])
    acc += _dot(c_ref[...], wc_ref[...])
    o_ref[...] = h_ref[...] + acc


def _outproj(h, oa, ob, oc, wa, wb, wc, *, tm=512):
    t, d = h.shape
    nseg = oc.shape[0] // tm
    row = lambda i: (i, 0)
    fixed = lambda i: (0, 0)
    return pl.pallas_call(
        _outproj_kernel,
        grid=(t // tm,),
        in_specs=[
            pl.BlockSpec((tm, d), row),
            pl.BlockSpec((tm, oa.shape[1]), row),
            pl.BlockSpec((tm, ob.shape[1]), row),
            pl.BlockSpec((tm, S5_WIDTH), lambda i: (i % nseg, i // nseg)),
            pl.BlockSpec(wa.shape, fixed),
            pl.BlockSpec(wb.shape, fixed),
            pl.BlockSpec(wc.shape, fixed),
        ],
        out_specs=pl.BlockSpec((tm, d), row),
        out_shape=jax.ShapeDtypeStruct((t, d), F32),
        compiler_params=pltpu.CompilerParams(
            dimension_semantics=("parallel",), vmem_limit_bytes=VMEM_LIMIT),
        name="outproj",
    )(h, oa, ob, oc, wa, wb, wc)


def _chunk_rows(x_ref, lanes, ci, c):
    r0 = ci * c
    cur = x_ref[pl.ds(pl.multiple_of(r0, c), c), lanes]
    prev = x_ref[pl.ds(pl.multiple_of(jnp.maximum(r0 - 8, 0), 8), 8), lanes]
    return jnp.concatenate([jnp.where(ci == 0, 0.0, prev), cur], axis=0)


def _gdn_kernel(gpar_ref,
                q_ref, k_ref, v_ref, z_ref, gate_ref,
                cwq_ref, cwk_ref, cwv_ref, nw_ref,
                o_ref,
                u_ref, w_ref, qd_ref, kd_ref, at_ref, eg_ref,
                *, hb, p_inv, p_mix, unroll):
    hg = pl.program_id(1)
    seq = q_ref.shape[0]
    n_chunks = seq // CHUNK
    c = CHUNK
    mix_dt = BF16 if p_mix == 1 else F32

    row = _iota((c, c), 0)
    col = _iota((c, c), 1)
    tril = row >= col
    strict = row > col
    ltri = tril.astype(BF16)
    ones = jnp.ones((c, c), BF16)
    eye = (row == col).astype(F32)
    lane = _iota((c, LANE), 1)
    neg_a = -jnp.exp(gpar_ref[0:1, :])
    dt_bias = gpar_ref[1:2, :]

    def prep_stages(trip):
        cis = [trip * unroll + cc for cc in range(unroll)]
        items = [(cis[cc], hh) for cc in range(unroll) for hh in range(hb)]
        lanes = [slice(hh * LANE, (hh + 1) * LANE) for _, hh in items]
        rows = [pl.ds(pl.multiple_of(ci * c, c), c) for ci, _ in items]

        def chunk_gates(ci):
            gt = gate_ref[pl.ds(pl.multiple_of(ci * c, c), c), :]
            return _sigmoid(gt), neg_a * _softplus(gt + dt_bias)

        gates = [chunk_gates(ci) for ci in cis]

        def stage_a(cc, hh, ln):
            ci = cis[cc]
            h = hg * hb + hh
            q = _silu(_causal_conv(_chunk_rows(q_ref, ln, ci, c), cwq_ref.at[:, ln]))
            k = _silu(_causal_conv(_chunk_rows(k_ref, ln, ci, c), cwk_ref.at[:, ln]))
            v = _silu(_causal_conv(_chunk_rows(v_ref, ln, ci, c), cwv_ref.at[:, ln]))
            qn = q * lax.rsqrt(jnp.sum(q * q, axis=-1, keepdims=True) + RMS_EPS) * (GDN_D ** -0.5)
            kn = k * lax.rsqrt(jnp.sum(k * k, axis=-1, keepdims=True) + RMS_EPS)
            beta_all, g_all = gates[cc]
            beta = jnp.sum(jnp.where(lane == h, beta_all, 0.0), axis=-1, keepdims=True)
            g = jnp.sum(jnp.where(lane == h + GDN_HEADS, g_all, 0.0), axis=-1, keepdims=True)
            return qn, kn, v, beta, jnp.broadcast_to(g, (c, LANE))

        sa = [stage_a(cc, hh, slice(hh * LANE, (hh + 1) * LANE))
              for cc in range(unroll) for hh in range(hb)]
        qn = [x[0] for x in sa]
        kn = [x[1] for x in sa]
        v = [x[2] for x in sa]
        beta = [x[3] for x in sa]
        g_b = [x[4] for x in sa]
        yield
        gi = [_mm01(ltri, g) for g in g_b]
        gj = [_mm01(ones, jnp.where(row <= col, g[:, :c], 0.0)) for g in g_b]
        yield
        kb = [k * b for k, b in zip(kn, beta)]
        kk = [_mm(a, b, NT, p_inv) for a, b in zip(kb, kn)]
        qk = [_mm(a, b, NT, p_mix) for a, b in zip(qn, kn)]
        decay = [jnp.where(tril, jnp.exp(a[:, :c] - b), 0.0) for a, b in zip(gi, gj)]
        npow = [-jnp.where(strict, a * d, 0.0) for a, d in zip(kk, decay)]
        tinv = [eye + n for n in npow]
        for _ in range(5):
            yield
            npow = [_mm(n, n, NN, p_inv) for n in npow]
            tinv = [t + _mm(t, n, NN, p_inv) for t, n in zip(tinv, npow)]
        yield
        exp_g = [jnp.exp(g) for g in gi]
        rhs = [jnp.concatenate([vv * b, k * e], axis=1) for vv, b, k, e in zip(v, beta, kb, exp_g)]
        uw = [_mm(t, r, NN, p_inv) for t, r in zip(tinv, rhs)]
        for i, ((ci, hh), rw) in enumerate(zip(items, rows)):
            g_last = jnp.broadcast_to(gi[i][c - 1:c, :], (c, LANE))
            u_ref[hh, rw, :] = uw[i][:, :LANE]
            w_ref[hh, rw, :] = uw[i][:, LANE:].astype(mix_dt)
            at_ref[hh, rw, :] = jnp.where(tril, qk[i] * decay[i], 0.0).astype(mix_dt)
            qd_ref[hh, rw, :] = (qn[i] * exp_g[i]).astype(mix_dt)
            kd_ref[hh, rw, :] = (kn[i] * jnp.exp(g_last - gi[i])).astype(mix_dt)
            eg_ref[hh, pl.ds(pl.multiple_of(ci * 8, 8), 8), :] = jnp.exp(g_last[0:8, :])

    norm_w = nw_ref[...]

    def scan_stages(trip, states, final):
        for cc in range(unroll):
            ci = trip * unroll + cc
            rw = pl.ds(pl.multiple_of(ci * c, c), c)
            lhs = [jnp.concatenate([w_ref[hh, rw, :], qd_ref[hh, rw, :]], axis=0)
                   for hh in range(hb)]
            ws_qs = [_mm(a, s, NN, p_mix) for a, s in zip(lhs, states)]
            yield
            v_new = [u_ref[hh, rw, :] - ws_qs[hh][:c] for hh in range(hb)]
            kv = [_mm(kd_ref[hh, rw, :], v_new[hh], TN, p_mix) for hh in range(hb)]
            av = [_mm(at_ref[hh, rw, :], v_new[hh], NN, p_mix) for hh in range(hb)]
            new_states = []
            for hh in range(hb):
                eg = eg_ref[hh, pl.ds(pl.multiple_of(ci * 8, 8), 8), :]
                new_states.append(states[hh] * jnp.broadcast_to(eg[0:1, :], (GDN_D, GDN_D)) + kv[hh])
                o = ws_qs[hh][c:] + av[hh]
                ms = jnp.mean(o * o, axis=-1, keepdims=True)
                ln = slice(hh * LANE, (hh + 1) * LANE)
                y = o * lax.rsqrt(ms + RMS_EPS) * norm_w * _silu(z_ref[rw, ln])
                o_ref[rw, ln] = y.astype(o_ref.dtype)
            states = new_states
            yield
        final.extend(states)

    def drain(gen):
        for _ in gen:
            pass

    def fused(trip, states):
        final = []
        scan_gen = scan_stages(trip - 1, states, final)
        for _ in prep_stages(trip):
            next(scan_gen, None)
        drain(scan_gen)
        return tuple(final)

    n_trips = n_chunks // unroll
    drain(prep_stages(0))
    states = lax.fori_loop(1, n_trips, fused,
                           tuple(jnp.zeros((GDN_D, GDN_D), F32) for _ in range(hb)))
    drain(scan_stages(n_trips - 1, states, []))


def _gdn(proj, bsz, seq, conv_w, a_log, dt_bias, norm_w, *, hb=3, p_inv=1, p_mix=1, unroll=4):
    t = proj.shape[0]
    wd = hb * LANE
    mix_dt = BF16 if p_mix == 1 else F32
    q0 = COL_GDN_QKV // wd
    z0 = COL_GDN_Z // wd
    g0 = COL_GDN_GATE // LANE
    ng = GDN_HEADS // hb
    blk = lambda off: pl.BlockSpec((seq, wd), lambda b, h, off=off: (b, off + h))
    cw = lambda off: pl.BlockSpec((CONV_K, wd), lambda b, h, off=off: (0, off + h))
    pad = lambda v: jnp.pad(v.astype(F32), (GDN_HEADS, LANE - 2 * GDN_HEADS))
    gpar = jnp.stack([pad(a_log), pad(dt_bias)])
    return pl.pallas_call(
        functools.partial(_gdn_kernel, hb=hb, p_inv=p_inv, p_mix=p_mix, unroll=unroll),
        grid=(bsz, ng),
        in_specs=[
            pl.BlockSpec((2, LANE), lambda b, h: (0, 0)),
            blk(q0), blk(q0 + ng), blk(q0 + 2 * ng), blk(z0),
            pl.BlockSpec((seq, LANE), lambda b, h: (b, g0)),
            cw(0), cw(ng), cw(2 * ng),
            pl.BlockSpec((1, LANE), lambda b, h: (0, 0)),
        ],
        out_specs=pl.BlockSpec((seq, wd), lambda b, h: (b, h)),
        out_shape=jax.ShapeDtypeStruct((t, GDN_QK), BF16),
        scratch_shapes=[
            pltpu.VMEM((hb, seq, LANE), F32),
            pltpu.VMEM((hb, seq, LANE), mix_dt),
            pltpu.VMEM((hb, seq, LANE), mix_dt),
            pltpu.VMEM((hb, seq, LANE), mix_dt),
            pltpu.VMEM((hb, seq, CHUNK), mix_dt),
            pltpu.VMEM((hb, seq // CHUNK * 8, LANE), F32),
        ],
        compiler_params=pltpu.CompilerParams(
            dimension_semantics=("parallel", "arbitrary"), vmem_limit_bytes=VMEM_LIMIT),
        name="gdn",
    )(gpar, proj, proj, proj, proj, proj, conv_w, conv_w, conv_w, norm_w.reshape(1, LANE))


def _ssd_kernel(z_ref, x_ref, b_ref, c_ref, dt_ref,
                cwx_ref, cwb_ref, cwc_ref,
                cbx_ref, cbb_ref, cbc_ref,
                alog_ref, dtb_ref, dsk_ref, nw_ref,
                o_ref,
                s_ref, *, passes, unroll):
    grp = pl.program_id(1)
    seq = x_ref.shape[0]
    n_chunks = seq // CHUNK
    c = CHUNK
    gw = SSD_GW

    row = _iota((c, c), 0)
    col = _iota((c, c), 1)
    ltri = (row >= col).astype(BF16)
    ones = jnp.ones((c, c), BF16)
    rowt = _iota((c, gw), 0)
    colt = jnp.bitwise_and(_iota((c, gw), 1), c - 1)
    tril_t = rowt >= colt
    upper_t = rowt <= colt
    expand = (_iota((LANE, gw), 0) == grp * SSD_HPG + lax.shift_right_logical(_iota((LANE, gw), 1), 6)).astype(BF16)
    last_row = (_iota((8, gw), 0) == 7)
    ones_8n = jnp.ones((8, SSD_N), BF16)

    neg_a = -jnp.exp(alog_ref[...])
    dt_bias = dtb_ref[...]
    d_skip = dsk_ref[...]
    norm_w = nw_ref[...]

    def conv(x_ref_, cw_ref, cb_ref, ci):
        return _silu(_causal_conv(_chunk_rows(x_ref_, slice(None), ci, c), cw_ref) + cb_ref[...])

    s_ref[...] = jnp.zeros((gw, SSD_N), F32)
    heads = [slice(hh * SSD_P, (hh + 1) * SSD_P) for hh in range(SSD_HPG)]

    def body(trip, carry):
        cis = [trip * unroll + cc for cc in range(unroll)]
        rows = [pl.ds(pl.multiple_of(ci * c, c), c) for ci in cis]
        xs = [conv(x_ref, cwx_ref, cbx_ref, ci) for ci in cis]
        bm = [conv(b_ref, cwb_ref, cbb_ref, ci) for ci in cis]
        cm = [conv(c_ref, cwc_ref, cbc_ref, ci) for ci in cis]
        dt = [_softplus(_mm01_t(dt_ref[rw, :], expand) + dt_bias) for rw in rows]
        a = [d * neg_a for d in dt]
        acum = [_mm01(ltri, x) for x in a]
        acum_j = [_mm01(ones, jnp.where(upper_t, x, 0.0)) for x in a]
        cb = [_mm(x, y, NT, passes) for x, y in zip(cm, bm)]
        lm = [jnp.where(tril_t, jnp.exp(x - y), 0.0) for x, y in zip(acum, acum_j)]
        xdt = [x * d for x, d in zip(xs, dt)]
        y_diag = [jnp.concatenate([_mm(cb[i] * lm[i][:, sl], xdt[i][:, sl], NN, passes)
                                   for sl in heads], axis=1) for i in range(unroll)]
        a_last = [jnp.broadcast_to(x[c - 1:c, :], (c, gw)) for x in acum]
        states = [_mm(xdt[i] * jnp.exp(a_last[i] - acum[i]), bm[i], TN, passes)
                  for i in range(unroll)]
        dec_col = [_mm01_t(jnp.where(last_row, jnp.exp(x[c - 8:, :]), 0.0), ones_8n, TN)
                   for x in acum]
        s_prev = s_ref[...]
        for i in range(unroll):
            y_off = _mm(cm[i], s_prev, NT, passes) * jnp.exp(acum[i])
            s_prev = s_prev * dec_col[i] + states[i]
            y = y_diag[i] + y_off + d_skip * xs[i]
            y = y * _silu(z_ref[rows[i], :])
            ms = jnp.mean(y * y, axis=-1, keepdims=True)
            o_ref[rows[i], :] = (y * lax.rsqrt(ms + RMS_EPS) * norm_w).astype(o_ref.dtype)
        s_ref[...] = s_prev
        return carry

    lax.fori_loop(0, n_chunks // unroll, body, 0)


def _ssd(proj, bsz, seq, conv_w, conv_b, a_log, dt_bias, d_skip, norm_w, *, passes=1, unroll=4):
    t = proj.shape[0]
    gw = SSD_GW
    rep = lambda v: jnp.repeat(v.astype(F32), SSD_P).reshape(1, SSD_INNER)
    cb = conv_b.reshape(1, -1)
    wide = lambda off: pl.BlockSpec((seq, gw), lambda b, g, off=off: (b, off + g))
    lane = lambda off: pl.BlockSpec((seq, LANE), lambda b, g, off=off: (b, off + g))
    chan = pl.BlockSpec((1, gw), lambda b, g: (0, g))
    return pl.pallas_call(
        functools.partial(_ssd_kernel, passes=passes, unroll=unroll),
        grid=(bsz, SSD_GROUPS),
        in_specs=[
            wide(COL_SSD_Z // gw), wide(COL_SSD_X // gw),
            lane(COL_SSD_B // LANE), lane(COL_SSD_C // LANE),
            pl.BlockSpec((seq, LANE), lambda b, g: (b, COL_SSD_DT // LANE)),
            pl.BlockSpec((CONV_K, gw), lambda b, g: (0, g)),
            pl.BlockSpec((CONV_K, LANE), lambda b, g: (0, SSD_INNER // LANE + g)),
            pl.BlockSpec((CONV_K, LANE), lambda b, g: (0, SSD_INNER // LANE + SSD_GROUPS + g)),
            pl.BlockSpec((1, gw), lambda b, g: (0, g)),
            pl.BlockSpec((1, LANE), lambda b, g: (0, SSD_INNER // LANE + g)),
            pl.BlockSpec((1, LANE), lambda b, g: (0, SSD_INNER // LANE + SSD_GROUPS + g)),
            chan, chan, chan, chan,
        ],
        out_specs=pl.BlockSpec((seq, gw), lambda b, g: (b, g)),
        out_shape=jax.ShapeDtypeStruct((t, SSD_INNER), BF16),
        scratch_shapes=[pltpu.VMEM((gw, SSD_N), F32)],
        compiler_params=pltpu.CompilerParams(
            dimension_semantics=("parallel", "arbitrary"), vmem_limit_bytes=VMEM_LIMIT),
        name="ssd",
    )(proj, proj, proj, proj, proj, conv_w, conv_w, conv_w, cb, cb, cb,
      rep(a_log), rep(dt_bias), rep(d_skip), norm_w.reshape(1, SSD_INNER))


def _s5_kernel(u_ref, wb_ref, wc_ref, are_ref, aim_ref, dsk_ref, gw_ref, gb_ref,
               o_ref, bu_ref, h_ref, *, bsz):
    ts = u_ref.shape[0] // bsz
    nt = S5_CH // LANE
    nb = S5_WIDTH // LANE
    per = nt // nb
    half = per * LANE

    @pl.when(pl.program_id(0) == 0)
    def _():
        h_ref[...] = jnp.zeros(h_ref.shape, F32)

    u = u_ref[...]
    u_bf = u.astype(BF16)
    for m in range(nb):
        res = _dot(u_bf[:, m * LANE:(m + 1) * LANE], wb_ref[m])
        for k in range(per):
            bu_ref[m * per + k] = res[:, k * LANE:(k + 1) * LANE]
            bu_ref[nt + m * per + k] = res[:, half + k * LANE:half + (k + 1) * LANE]

    def step(t, carry):
        rows = pl.ds(pl.multiple_of(t * bsz, bsz), bsz)
        new_re, new_im = [], []
        for k in range(nt):
            h_re, h_im = carry[k], carry[nt + k]
            a_re = are_ref[:, k * LANE:(k + 1) * LANE]
            a_im = aim_ref[:, k * LANE:(k + 1) * LANE]
            n_re = a_re * h_re - a_im * h_im + bu_ref[k, rows, :]
            n_im = a_re * h_im + a_im * h_re + bu_ref[nt + k, rows, :]
            bu_ref[k, rows, :] = n_re
            bu_ref[nt + k, rows, :] = n_im
            new_re.append(n_re)
            new_im.append(n_im)
        return tuple(new_re + new_im)

    h_last = lax.fori_loop(0, ts, step, tuple(h_ref[k] for k in range(2 * nt)))
    for k in range(2 * nt):
        h_ref[k] = h_last[k]

    y_parts = []
    for m in range(nb):
        hs = jnp.concatenate([bu_ref[m * per + k] for k in range(per)] +
                             [bu_ref[nt + m * per + k] for k in range(per)], axis=1)
        y_parts.append(_dot(hs.astype(BF16), wc_ref[m]))
    y = jnp.concatenate(y_parts, axis=1) + dsk_ref[...] * u
    g = 0.5 * y * (1.0 + jnp.tanh(math.sqrt(2.0 / math.pi) * (y + 0.044715 * (y * y * y))))
    out = g * _sigmoid(_dot(g.astype(BF16), gw_ref[...]) + gb_ref[...])
    o_ref[...] = out.astype(o_ref.dtype)


def _s5(u_tb, bsz, wb, wc, a_re, a_im, d_skip, glu_w, glu_b, *, ts=128):
    rows = u_tb.shape[0]
    blk = ts * bsz
    fixed = lambda shape: pl.BlockSpec(shape, lambda k: (0,) * len(shape))
    return pl.pallas_call(
        functools.partial(_s5_kernel, bsz=bsz),
        grid=(rows // blk,),
        in_specs=[
            pl.BlockSpec((blk, S5_WIDTH), lambda k: (k, 0)),
            fixed(wb.shape), fixed(wc.shape),
            fixed((1, S5_CH)), fixed((1, S5_CH)),
            fixed((1, S5_WIDTH)), fixed(glu_w.shape), fixed((1, S5_WIDTH)),
        ],
        out_specs=pl.BlockSpec((blk, S5_WIDTH), lambda k: (k, 0)),
        out_shape=jax.ShapeDtypeStruct((rows, S5_WIDTH), BF16),
        scratch_shapes=[
            pltpu.VMEM((2 * S5_CH // LANE, blk, LANE), F32),
            pltpu.VMEM((2 * S5_CH // LANE, bsz, LANE), F32),
        ],
        compiler_params=pltpu.CompilerParams(
            dimension_semantics=("arbitrary",), vmem_limit_bytes=VMEM_LIMIT),
        name="s5",
    )(u_tb, wb, wc, a_re.reshape(1, S5_CH), a_im.reshape(1, S5_CH),
      d_skip.reshape(1, S5_WIDTH), glu_w, glu_b.reshape(1, S5_WIDTH))


def _s5_params(a_re, a_im, b_re, b_im, c_re, c_im, log_dt):
    delta = jnp.exp(log_dt)[:, None]
    mag = jnp.exp(a_re * delta)
    ab_re, ab_im = mag * jnp.cos(a_im * delta), mag * jnp.sin(a_im * delta)
    den = a_re * a_re + a_im * a_im
    p_re, p_im = ab_re - 1.0, ab_im
    f_re = (p_re * a_re + p_im * a_im) / den
    f_im = (p_im * a_re - p_re * a_im) / den
    bb_re = f_re[..., None] * b_re - f_im[..., None] * b_im
    bb_im = f_re[..., None] * b_im + f_im[..., None] * b_re
    gpb = LANE // S5_GROUP
    nb = S5_GROUPS // gpb
    eye = jnp.eye(gpb, dtype=F32)
    emb_b = lambda bb: jnp.einsum('mgnj,gh->mgjhn', bb.reshape(nb, gpb, S5_STATE, S5_GROUP),
                                  eye).reshape(nb, LANE, gpb * S5_STATE)
    wb = jnp.concatenate([emb_b(bb_re), emb_b(bb_im)], axis=2)
    emb_c = lambda cc: jnp.einsum('mgin,gh->mgnhi', cc.reshape(nb, gpb, S5_GROUP, S5_STATE),
                                  eye).reshape(nb, gpb * S5_STATE, LANE)
    wc = jnp.concatenate([emb_c(c_re), -emb_c(c_im)], axis=1)
    return ab_re.reshape(-1), ab_im.reshape(-1), wb.astype(BF16), wc.astype(BF16)


def _pack_w_in(w_in):
    d = w_in.shape[0]
    w_in = w_in.astype(BF16)
    gdn_in = GDN_QKV + GDN_QK + 2 * GDN_HEADS
    ssd_conv = SSD_INNER + 2 * SSD_GROUPS * SSD_N
    g_qkv = w_in[:, 0:GDN_QKV]
    g_z = w_in[:, GDN_QKV:GDN_QKV + GDN_QK]
    g_gate = w_in[:, GDN_QKV + GDN_QK:gdn_in]
    s0 = gdn_in
    s_z = w_in[:, s0:s0 + SSD_INNER]
    s_x = w_in[:, s0 + SSD_INNER:s0 + 2 * SSD_INNER]
    s_b = w_in[:, s0 + 2 * SSD_INNER:s0 + 2 * SSD_INNER + SSD_GROUPS * SSD_N]
    s_c = w_in[:, s0 + 2 * SSD_INNER + SSD_GROUPS * SSD_N:s0 + SSD_INNER + ssd_conv]
    s_dt = w_in[:, s0 + SSD_INNER + ssd_conv:s0 + SSD_INNER + ssd_conv + SSD_HEADS]
    u0 = s0 + SSD_INNER + ssd_conv + SSD_HEADS
    s5_u = w_in[:, u0:u0 + S5_WIDTH]
    pad = lambda n: jnp.zeros((d, n), w_in.dtype)
    cols = [s_z, s_x, s5_u, s_b, s_c, s_dt, pad(LANE - SSD_HEADS), g_qkv, g_z,
            g_gate, pad(LANE - 2 * GDN_HEADS), pad(P_PAD - (COL_GDN_GATE + LANE))]
    return jnp.concatenate(cols, axis=1)


def kernel(x, ffn1_norm, ffn1_w_gate, ffn1_w_up, ffn1_w_down, mix_norm, w_in,
           gdn_conv_w, gdn_a_log, gdn_dt_bias, gdn_norm,
           ssd_conv_w, ssd_conv_b, ssd_a_log, ssd_dt_bias, ssd_d, ssd_norm,
           s5_a_re, s5_a_im, s5_b_re, s5_b_im, s5_c_re, s5_c_im, s5_d, s5_log_dt,
           s5_glu_w, s5_glu_b, w_out, ffn2_norm, ffn2_w_gate, ffn2_w_up, ffn2_w_down,
           final_norm):
    bsz, seq, d = x.shape
    depth = w_in.shape[0]
    h = x
    ffn1_w = [_to_bf16(w) for w in (ffn1_w_gate, ffn1_w_up, ffn1_w_down)]
    ffn2_w = [_to_bf16(w) for w in (ffn2_w_gate, ffn2_w_up, ffn2_w_down)]
    w_in_bf = w_in.astype(BF16)
    for i in range(depth):
        h = _ffn(h, ffn1_norm[i], *ffn1_w, final_norm, i, final_norm=False)
        proj, u_s5 = _inproj(h, mix_norm[i], _pack_w_in(w_in_bf[i]), bsz, seq)
        o_gdn = _gdn(proj, bsz, seq, gdn_conv_w[i], gdn_a_log[i], gdn_dt_bias[i], gdn_norm[i])
        o_ssd = _ssd(proj, bsz, seq, ssd_conv_w[i], ssd_conv_b[i], ssd_a_log[i], ssd_dt_bias[i],
                     ssd_d[i], ssd_norm[i])
        ab_re, ab_im, wb, wc = _s5_params(s5_a_re[i], s5_a_im[i], s5_b_re[i], s5_b_im[i],
                                          s5_c_re[i], s5_c_im[i], s5_log_dt[i])
        o_s5 = _s5(u_s5.reshape(seq * bsz, S5_WIDTH), bsz, wb, wc, ab_re, ab_im, s5_d[i],
                   s5_glu_w[i].astype(BF16), s5_glu_b[i])
        wo = w_out[i].astype(BF16)
        h = _outproj(h, o_gdn, o_ssd, o_s5.reshape(seq, bsz * S5_WIDTH), wo[0:GDN_QK],
                     wo[GDN_QK:GDN_QK + SSD_INNER], wo[GDN_QK + SSD_INNER:])
        last = i == depth - 1
        h = _ffn(h, ffn2_norm[i], *ffn2_w, final_norm, i, final_norm=last,
                 out_batched=(bsz, seq) if last else None)
    return h
```

```python
import functools
import math

import jax
import jax.numpy as jnp
from jax import lax
from jax.experimental import pallas as pl
from jax.experimental.pallas import tpu as pltpu

F32 = jnp.float32
BF16 = jnp.bfloat16

RMS_EPS = 1e-6
CHUNK = 64
CONV_K = 4
D_MODEL = 2048
D_FF = 5632

GDN_HEADS = 6
GDN_D = 128
GDN_QK = GDN_HEADS * GDN_D
GDN_QKV = 3 * GDN_QK

SSD_HEADS = 12
SSD_P = 64
SSD_GROUPS = 2
SSD_N = 128
SSD_HPG = SSD_HEADS // SSD_GROUPS
SSD_GW = SSD_HPG * SSD_P
SSD_INNER = SSD_HEADS * SSD_P

S5_WIDTH = 512
S5_GROUP = 16
S5_GROUPS = 32
S5_STATE = 64
S5_CH = S5_GROUPS * S5_STATE

LANE = 128

COL_SSD_Z = 0
COL_SSD_X = 768
COL_S5_U = 1536
COL_SSD_B = 2048
COL_SSD_C = 2304
COL_SSD_DT = 2560
COL_GDN_QKV = 2688
COL_GDN_Z = 4992
COL_GDN_GATE = 5760
P_PAD = 6144

VMEM_LIMIT = 60 * 1024 * 1024

NN = (((1,), (0,)), ((), ()))
NT = (((1,), (1,)), ((), ()))
TN = (((0,), (0,)), ((), ()))


def _dot(a, b, dims=NN):
    return lax.dot_general(a, b, dims, preferred_element_type=F32)


def _hi_lo(a):
    hi = a.astype(BF16)
    lo = (a - hi.astype(F32)).astype(BF16)
    return hi, lo


def _mm(a, b, dims=NN, passes=3):
    if passes == 1:
        return _dot(a.astype(BF16), b.astype(BF16), dims)
    ah, al = _hi_lo(a)
    bh, bl = _hi_lo(b)
    return _dot(ah, bh, dims) + (_dot(ah, bl, dims) + _dot(al, bh, dims))


def _mm01(m01, x, dims=NN):
    hi, lo = _hi_lo(x)
    return _dot(m01, hi, dims) + _dot(m01, lo, dims)


def _mm01_t(x, m01, dims=NN):
    hi, lo = _hi_lo(x)
    return _dot(hi, m01, dims) + _dot(lo, m01, dims)


def _sigmoid(x):
    return 1.0 / (1.0 + jnp.exp(-x))


def _silu(x):
    return x * _sigmoid(x)


def _softplus(x):
    return jnp.maximum(x, 0.0) + jnp.log1p(jnp.exp(-jnp.abs(x)))


def _iota(shape, dim):
    return lax.broadcasted_iota(jnp.int32, shape, dim)


def _causal_conv(x, cw_ref):
    acc = None
    for i in range(CONV_K):
        shift = CONV_K - 1 - i
        xs = x if shift == 0 else pltpu.roll(x, shift, axis=0)
        term = xs[8:] * cw_ref[i:i + 1, :]
        acc = term if acc is None else acc + term
    return acc


def _ffn_kernel(x_ref, nw_ref, wg_ref, wu_ref, wd_ref, fw_ref, o_ref, xn_ref, *, final_norm):
    j = pl.program_id(1)

    @pl.when(j == 0)
    def _():
        x = x_ref[...]
        ms = jnp.mean(x * x, axis=-1, keepdims=True)
        xn_ref[...] = (x * lax.rsqrt(ms + RMS_EPS) * nw_ref[...]).astype(BF16)
        o_ref[...] = x

    xn = xn_ref[...]
    g = _dot(xn, wg_ref[...])
    u = _dot(xn, wu_ref[...])
    a = (0.5 * _silu(g) * u).astype(BF16)
    o_ref[...] += _dot(a, wd_ref[...])

    if final_norm:
        @pl.when(j == pl.num_programs(1) - 1)
        def _():
            h = o_ref[...]
            ms = jnp.mean(h * h, axis=-1, keepdims=True)
            o_ref[...] = h * lax.rsqrt(ms + RMS_EPS) * fw_ref[...]


def _cast_kernel(x_ref, o_ref):
    o_ref[...] = x_ref[...].astype(o_ref.dtype)


def _to_bf16(w, *, rows=256):
    nl, r, c = w.shape
    spec = pl.BlockSpec((None, rows, c), lambda l, i: (l, i, 0))
    return pl.pallas_call(
        _cast_kernel,
        grid=(nl, r // rows),
        in_specs=[spec],
        out_specs=spec,
        out_shape=jax.ShapeDtypeStruct(w.shape, BF16),
        compiler_params=pltpu.CompilerParams(
            dimension_semantics=("parallel", "parallel"), vmem_limit_bytes=VMEM_LIMIT),
        name="to_bf16",
    )(w)


def _ffn(x, nw, wg, wu, wd, fw, layer, *, final_norm, out_batched=None, tm=1024, tf=512):
    d = x.shape[-1]
    t = x.size // d
    f = wg.shape[2]
    if x.ndim == 3:
        nseg = x.shape[1] // tm
        x_spec = pl.BlockSpec((None, tm, d), lambda i, j: (i // nseg, i % nseg, 0))
    else:
        x_spec = pl.BlockSpec((tm, d), lambda i, j: (i, 0))
    if out_batched is None:
        out_spec = pl.BlockSpec((tm, d), lambda i, j: (i, 0))
        out_shape = jax.ShapeDtypeStruct((t, d), F32)
    else:
        oseg = out_batched[1] // tm
        out_spec = pl.BlockSpec((None, tm, d), lambda i, j: (i // oseg, i % oseg, 0))
        out_shape = jax.ShapeDtypeStruct((*out_batched, d), F32)
    return pl.pallas_call(
        functools.partial(_ffn_kernel, final_norm=final_norm),
        grid=(t // tm, f // tf),
        in_specs=[
            x_spec,
            pl.BlockSpec((1, d), lambda i, j: (0, 0)),
            pl.BlockSpec((None, d, tf), lambda i, j: (layer, 0, j)),
            pl.BlockSpec((None, d, tf), lambda i, j: (layer, 0, j)),
            pl.BlockSpec((None, tf, d), lambda i, j: (layer, j, 0)),
            pl.BlockSpec((1, d), lambda i, j: (0, 0)),
        ],
        out_specs=out_spec,
        out_shape=out_shape,
        scratch_shapes=[pltpu.VMEM((tm, d), BF16)],
        compiler_params=pltpu.CompilerParams(
            dimension_semantics=("parallel", "arbitrary"), vmem_limit_bytes=VMEM_LIMIT),
        name="ffn",
    )(x, nw.reshape(1, d), wg, wu, wd, fw.reshape(1, d))


def _inproj_kernel(x_ref, nw_ref, w_ref, o_ref, u5_ref, xn_ref, *, tn):
    j = pl.program_id(1)

    @pl.when(j == 0)
    def _():
        x = x_ref[...]
        ms = jnp.mean(x * x, axis=-1, keepdims=True)
        xn_ref[...] = (x * lax.rsqrt(ms + RMS_EPS) * nw_ref[...]).astype(BF16)

    res = _dot(xn_ref[...], w_ref[...])
    o_ref[...] = res

    @pl.when(j == COL_S5_U // tn)
    def _():
        off = COL_S5_U % tn
        u5_ref[...] = res[:, off:off + S5_WIDTH]


def _inproj(x, nw, w, bsz, seq, *, tm=256, tn=P_PAD):
    t, d = x.shape
    n = w.shape[1]
    nseg = seq // tm
    w_mode = dict(pipeline_mode=pl.Buffered(1)) if tn == n else {}
    return pl.pallas_call(
        functools.partial(_inproj_kernel, tn=tn),
        grid=(t // tm, n // tn),
        in_specs=[
            pl.BlockSpec((tm, d), lambda i, j: (i, 0)),
            pl.BlockSpec((1, d), lambda i, j: (0, 0)),
            pl.BlockSpec((d, tn), lambda i, j: (0, j), **w_mode),
        ],
        out_specs=[
            pl.BlockSpec((tm, tn), lambda i, j: (i, j)),
            pl.BlockSpec((tm, S5_WIDTH), lambda i, j: (i % nseg, i // nseg)),
        ],
        out_shape=[jax.ShapeDtypeStruct((t, n), F32),
                   jax.ShapeDtypeStruct((seq, bsz * S5_WIDTH), F32)],
        scratch_shapes=[pltpu.VMEM((tm, d), BF16)],
        compiler_params=pltpu.CompilerParams(
            dimension_semantics=("parallel", "arbitrary"), vmem_limit_bytes=VMEM_LIMIT),
        name="inproj",
    )(x, nw.reshape(1, d), w)


def _outproj_kernel(h_ref, a_ref, b_ref, c_ref, wa_ref, wb_ref, wc_ref, o_ref):
    acc = _dot(a_ref[...], wa_ref[...])
    acc += _dot(b_ref[...], wb_ref[...])
    acc += _dot(c_ref[...], wc_ref[...])
    o_ref[...] = h_ref[...] + acc


def _outproj(h, oa, ob, oc, wa, wb, wc, *, tm=512):
    t, d = h.shape
    nseg = oc.shape[0] // tm
    row = lambda i: (i, 0)
    fixed = lambda i: (0, 0)
    return pl.pallas_call(
        _outproj_kernel,
        grid=(t // tm,),
        in_specs=[
            pl.BlockSpec((tm, d), row),
            pl.BlockSpec((tm, oa.shape[1]), row),
            pl.BlockSpec((tm, ob.shape[1]), row),
            pl.BlockSpec((tm, S5_WIDTH), lambda i: (i % nseg, i // nseg)),
            pl.BlockSpec(wa.shape, fixed),
            pl.BlockSpec(wb.shape, fixed),
            pl.BlockSpec(wc.shape, fixed),
        ],
        out_specs=pl.BlockSpec((tm, d), row),
        out_shape=jax.ShapeDtypeStruct((t, d), F32),
        compiler_params=pltpu.CompilerParams(
            dimension_semantics=("parallel",), vmem_limit_bytes=VMEM_LIMIT),
        name="outproj",
    )(h, oa, ob, oc, wa, wb, wc)


def _chunk_rows(x_ref, lanes, ci, c):
    r0 = ci * c
    cur = x_ref[pl.ds(pl.multiple_of(r0, c), c), lanes]
    prev = x_ref[pl.ds(pl.multiple_of(jnp.maximum(r0 - 8, 0), 8), 8), lanes]
    return jnp.concatenate([jnp.where(ci == 0, 0.0, prev), cur], axis=0)


def _gdn_kernel(gpar_ref,
                q_ref, k_ref, v_ref, z_ref, gate_ref,
                cwq_ref, cwk_ref, cwv_ref, nw_ref,
                o_ref,
                u_ref, w_ref, qd_ref, kd_ref, at_ref, eg_ref,
                *, hb, p_inv, p_mix, unroll):
    hg = pl.program_id(1)
    seq = q_ref.shape[0]
    n_chunks = seq // CHUNK
    c = CHUNK
    mix_dt = BF16 if p_mix == 1 else F32

    row = _iota((c, c), 0)
    col = _iota((c, c), 1)
    tril = row >= col
    strict = row > col
    ltri = tril.astype(BF16)
    ones = jnp.ones((c, c), BF16)
    eye = (row == col).astype(F32)
    lane = _iota((c, LANE), 1)
    neg_a = -jnp.exp(gpar_ref[0:1, :])
    dt_bias = gpar_ref[1:2, :]

    def prep_stages(trip):
        cis = [trip * unroll + cc for cc in range(unroll)]
        items = [(cis[cc], hh) for cc in range(unroll) for hh in range(hb)]
        lanes = [slice(hh * LANE, (hh + 1) * LANE) for _, hh in items]
        rows = [pl.ds(pl.multiple_of(ci * c, c), c) for ci, _ in items]

        def chunk_gates(ci):
            gt = gate_ref[pl.ds(pl.multiple_of(ci * c, c), c), :]
            return _sigmoid(gt), neg_a * _softplus(gt + dt_bias)

        gates = [chunk_gates(ci) for ci in cis]

        def stage_a(cc, hh, ln):
            ci = cis[cc]
            h = hg * hb + hh
            q = _silu(_causal_conv(_chunk_rows(q_ref, ln, ci, c), cwq_ref.at[:, ln]))
            k = _silu(_causal_conv(_chunk_rows(k_ref, ln, ci, c), cwk_ref.at[:, ln]))
            v = _silu(_causal_conv(_chunk_rows(v_ref, ln, ci, c), cwv_ref.at[:, ln]))
            qn = q * lax.rsqrt(jnp.sum(q * q, axis=-1, keepdims=True) + RMS_EPS) * (GDN_D ** -0.5)
            kn = k * lax.rsqrt(jnp.sum(k * k, axis=-1, keepdims=True) + RMS_EPS)
            beta_all, g_all = gates[cc]
            beta = jnp.sum(jnp.where(lane == h, beta_all, 0.0), axis=-1, keepdims=True)
            g = jnp.sum(jnp.where(lane == h + GDN_HEADS, g_all, 0.0), axis=-1, keepdims=True)
            return qn, kn, v, beta, jnp.broadcast_to(g, (c, LANE))

        sa = [stage_a(cc, hh, slice(hh * LANE, (hh + 1) * LANE))
              for cc in range(unroll) for hh in range(hb)]
        qn = [x[0] for x in sa]
        kn = [x[1] for x in sa]
        v = [x[2] for x in sa]
        beta = [x[3] for x in sa]
        g_b = [x[4] for x in sa]
        yield
        gi = [_mm01(ltri, g) for g in g_b]
        gj = [_mm01(ones, jnp.where(row <= col, g[:, :c], 0.0)) for g in g_b]
        yield
        kb = [k * b for k, b in zip(kn, beta)]
        kk = [_mm(a, b, NT, p_inv) for a, b in zip(kb, kn)]
        qk = [_mm(a, b, NT, p_mix) for a, b in zip(qn, kn)]
        decay = [jnp.where(tril, jnp.exp(a[:, :c] - b), 0.0) for a, b in zip(gi, gj)]
        npow = [-jnp.where(strict, a * d, 0.0) for a, d in zip(kk, decay)]
        tinv = [eye + n for n in npow]
        for _ in range(5):
            yield
            npow = [_mm(n, n, NN, p_inv) for n in npow]
            tinv = [t + _mm(t, n, NN, p_inv) for t, n in zip(tinv, npow)]
        yield
        exp_g = [jnp.exp(g) for g in gi]
        rhs = [jnp.concatenate([vv * b, k * e], axis=1) for vv, b, k, e in zip(v, beta, kb, exp_g)]
        uw = [_mm(t, r, NN, p_inv) for t, r in zip(tinv, rhs)]
        for i, ((ci, hh), rw) in enumerate(zip(items, rows)):
            g_last = jnp.broadcast_to(gi[i][c - 1:c, :], (c, LANE))
            u_ref[hh, rw, :] = uw[i][:, :LANE]
            w_ref[hh, rw, :] = uw[i][:, LANE:].astype(mix_dt)
            at_ref[hh, rw, :] = jnp.where(tril, qk[i] * decay[i], 0.0).astype(mix_dt)
            qd_ref[hh, rw, :] = (qn[i] * exp_g[i]).astype(mix_dt)
            kd_ref[hh, rw, :] = (kn[i] * jnp.exp(g_last - gi[i])).astype(mix_dt)
            eg_ref[hh, pl.ds(pl.multiple_of(ci * 8, 8), 8), :] = jnp.exp(g_last[0:8, :])

    norm_w = nw_ref[...]

    def scan_stages(trip, states, final):
        for cc in range(unroll):
            ci = trip * unroll + cc
            rw = pl.ds(pl.multiple_of(ci * c, c), c)
            lhs = [jnp.concatenate([w_ref[hh, rw, :], qd_ref[hh, rw, :]], axis=0)
                   for hh in range(hb)]
            ws_qs = [_mm(a, s, NN, p_mix) for a, s in zip(lhs, states)]
            yield
            v_new = [u_ref[hh, rw, :] - ws_qs[hh][:c] for hh in range(hb)]
            kv = [_mm(kd_ref[hh, rw, :], v_new[hh], TN, p_mix) for hh in range(hb)]
            av = [_mm(at_ref[hh, rw, :], v_new[hh], NN, p_mix) for hh in range(hb)]
            new_states = []
            for hh in range(hb):
                eg = eg_ref[hh, pl.ds(pl.multiple_of(ci * 8, 8), 8), :]
                new_states.append(states[hh] * jnp.broadcast_to(eg[0:1, :], (GDN_D, GDN_D)) + kv[hh])
                o = ws_qs[hh][c:] + av[hh]
                ms = jnp.mean(o * o, axis=-1, keepdims=True)
                ln = slice(hh * LANE, (hh + 1) * LANE)
                y = o * lax.rsqrt(ms + RMS_EPS) * norm_w * _silu(z_ref[rw, ln])
                o_ref[rw, ln] = y.astype(o_ref.dtype)
            states = new_states
            yield
        final.extend(states)

    def drain(gen):
        for _ in gen:
            pass

    def fused(trip, states):
        final = []
        scan_gen = scan_stages(trip - 1, states, final)
        for _ in prep_stages(trip):
            next(scan_gen, None)
        drain(scan_gen)
        return tuple(final)

    n_trips = n_chunks // unroll
    drain(prep_stages(0))
    states = lax.fori_loop(1, n_trips, fused,
                           tuple(jnp.zeros((GDN_D, GDN_D), F32) for _ in range(hb)))
    drain(scan_stages(n_trips - 1, states, []))


def _gdn(proj, bsz, seq, conv_w, a_log, dt_bias, norm_w, *, hb=3, p_inv=1, p_mix=1, unroll=4):
    t = proj.shape[0]
    wd = hb * LANE
    mix_dt = BF16 if p_mix == 1 else F32
    q0 = COL_GDN_QKV // wd
    z0 = COL_GDN_Z // wd
    g0 = COL_GDN_GATE // LANE
    ng = GDN_HEADS // hb
    blk = lambda off: pl.BlockSpec((seq, wd), lambda b, h, off=off: (b, off + h))
    cw = lambda off: pl.BlockSpec((CONV_K, wd), lambda b, h, off=off: (0, off + h))
    pad = lambda v: jnp.pad(v.astype(F32), (GDN_HEADS, LANE - 2 * GDN_HEADS))
    gpar = jnp.stack([pad(a_log), pad(dt_bias)])
    return pl.pallas_call(
        functools.partial(_gdn_kernel, hb=hb, p_inv=p_inv, p_mix=p_mix, unroll=unroll),
        grid=(bsz, ng),
        in_specs=[
            pl.BlockSpec((2, LANE), lambda b, h: (0, 0)),
            blk(q0), blk(q0 + ng), blk(q0 + 2 * ng), blk(z0),
            pl.BlockSpec((seq, LANE), lambda b, h: (b, g0)),
            cw(0), cw(ng), cw(2 * ng),
            pl.BlockSpec((1, LANE), lambda b, h: (0, 0)),
        ],
        out_specs=pl.BlockSpec((seq, wd), lambda b, h: (b, h)),
        out_shape=jax.ShapeDtypeStruct((t, GDN_QK), BF16),
        scratch_shapes=[
            pltpu.VMEM((hb, seq, LANE), F32),
            pltpu.VMEM((hb, seq, LANE), mix_dt),
            pltpu.VMEM((hb, seq, LANE), mix_dt),
            pltpu.VMEM((hb, seq, LANE), mix_dt),
            pltpu.VMEM((hb, seq, CHUNK), mix_dt),
            pltpu.VMEM((hb, seq // CHUNK * 8, LANE), F32),
        ],
        compiler_params=pltpu.CompilerParams(
            dimension_semantics=("parallel", "arbitrary"), vmem_limit_bytes=VMEM_LIMIT),
        name="gdn",
    )(gpar, proj, proj, proj, proj, proj, conv_w, conv_w, conv_w, norm_w.reshape(1, LANE))


def _ssd_kernel(z_ref, x_ref, b_ref, c_ref, dt_ref,
                cwx_ref, cwb_ref, cwc_ref,
                cbx_ref, cbb_ref, cbc_ref,
                alog_ref, dtb_ref, dsk_ref, nw_ref,
                o_ref,
                s_ref, *, passes, unroll):
    grp = pl.program_id(1)
    seq = x_ref.shape[0]
    n_chunks = seq // CHUNK
    c = CHUNK
    gw = SSD_GW

    row = _iota((c, c), 0)
    col = _iota((c, c), 1)
    ltri = (row >= col).astype(BF16)
    ones = jnp.ones((c, c), BF16)
    rowt = _iota((c, gw), 0)
    colt = jnp.bitwise_and(_iota((c, gw), 1), c - 1)
    tril_t = rowt >= colt
    upper_t = rowt <= colt
    expand = (_iota((LANE, gw), 0) == grp * SSD_HPG + lax.shift_right_logical(_iota((LANE, gw), 1), 6)).astype(BF16)
    last_row = (_iota((8, gw), 0) == 7)
    ones_8n = jnp.ones((8, SSD_N), BF16)

    neg_a = -jnp.exp(alog_ref[...])
    dt_bias = dtb_ref[...]
    d_skip = dsk_ref[...]
    norm_w = nw_ref[...]

    def conv(x_ref_, cw_ref, cb_ref, ci):
        return _silu(_causal_conv(_chunk_rows(x_ref_, slice(None), ci, c), cw_ref) + cb_ref[...])

    s_ref[...] = jnp.zeros((gw, SSD_N), F32)
    heads = [slice(hh * SSD_P, (hh + 1) * SSD_P) for hh in range(SSD_HPG)]

    def body(trip, carry):
        cis = [trip * unroll + cc for cc in range(unroll)]
        rows = [pl.ds(pl.multiple_of(ci * c, c), c) for ci in cis]
        xs = [conv(x_ref, cwx_ref, cbx_ref, ci) for ci in cis]
        bm = [conv(b_ref, cwb_ref, cbb_ref, ci) for ci in cis]
        cm = [conv(c_ref, cwc_ref, cbc_ref, ci) for ci in cis]
        dt = [_softplus(_mm01_t(dt_ref[rw, :], expand) + dt_bias) for rw in rows]
        a = [d * neg_a for d in dt]
        acum = [_mm01(ltri, x) for x in a]
        acum_j = [_mm01(ones, jnp.where(upper_t, x, 0.0)) for x in a]
        cb = [_mm(x, y, NT, passes) for x, y in zip(cm, bm)]
        lm = [jnp.where(tril_t, jnp.exp(x - y), 0.0) for x, y in zip(acum, acum_j)]
        xdt = [x * d for x, d in zip(xs, dt)]
        y_diag = [jnp.concatenate([_mm(cb[i] * lm[i][:, sl], xdt[i][:, sl], NN, passes)
                                   for sl in heads], axis=1) for i in range(unroll)]
        a_last = [jnp.broadcast_to(x[c - 1:c, :], (c, gw)) for x in acum]
        states = [_mm(xdt[i] * jnp.exp(a_last[i] - acum[i]), bm[i], TN, passes)
                  for i in range(unroll)]
        dec_col = [_mm01_t(jnp.where(last_row, jnp.exp(x[c - 8:, :]), 0.0), ones_8n, TN)
                   for x in acum]
        s_prev = s_ref[...]
        for i in range(unroll):
            y_off = _mm(cm[i], s_prev, NT, passes) * jnp.exp(acum[i])
            s_prev = s_prev * dec_col[i] + states[i]
            y = y_diag[i] + y_off + d_skip * xs[i]
            y = y * _silu(z_ref[rows[i], :])
            ms = jnp.mean(y * y, axis=-1, keepdims=True)
            o_ref[rows[i], :] = (y * lax.rsqrt(ms + RMS_EPS) * norm_w).astype(o_ref.dtype)
        s_ref[...] = s_prev
        return carry

    lax.fori_loop(0, n_chunks // unroll, body, 0)


def _ssd(proj, bsz, seq, conv_w, conv_b, a_log, dt_bias, d_skip, norm_w, *, passes=1, unroll=4):
    t = proj.shape[0]
    gw = SSD_GW
    rep = lambda v: jnp.repeat(v.astype(F32), SSD_P).reshape(1, SSD_INNER)
    cb = conv_b.reshape(1, -1)
    wide = lambda off: pl.BlockSpec((seq, gw), lambda b, g, off=off: (b, off + g))
    lane = lambda off: pl.BlockSpec((seq, LANE), lambda b, g, off=off: (b, off + g))
    chan = pl.BlockSpec((1, gw), lambda b, g: (0, g))
    return pl.pallas_call(
        functools.partial(_ssd_kernel, passes=passes, unroll=unroll),
        grid=(bsz, SSD_GROUPS),
        in_specs=[
            wide(COL_SSD_Z // gw), wide(COL_SSD_X // gw),
            lane(COL_SSD_B // LANE), lane(COL_SSD_C // LANE),
            pl.BlockSpec((seq, LANE), lambda b, g: (b, COL_SSD_DT // LANE)),
            pl.BlockSpec((CONV_K, gw), lambda b, g: (0, g)),
            pl.BlockSpec((CONV_K, LANE), lambda b, g: (0, SSD_INNER // LANE + g)),
            pl.BlockSpec((CONV_K, LANE), lambda b, g: (0, SSD_INNER // LANE + SSD_GROUPS + g)),
            pl.BlockSpec((1, gw), lambda b, g: (0, g)),
            pl.BlockSpec((1, LANE), lambda b, g: (0, SSD_INNER // LANE + g)),
            pl.BlockSpec((1, LANE), lambda b, g: (0, SSD_INNER // LANE + SSD_GROUPS + g)),
            chan, chan, chan, chan,
        ],
        out_specs=pl.BlockSpec((seq, gw), lambda b, g: (b, g)),
        out_shape=jax.ShapeDtypeStruct((t, SSD_INNER), BF16),
        scratch_shapes=[pltpu.VMEM((gw, SSD_N), F32)],
        compiler_params=pltpu.CompilerParams(
            dimension_semantics=("parallel", "arbitrary"), vmem_limit_bytes=VMEM_LIMIT),
        name="ssd",
    )(proj, proj, proj, proj, proj, conv_w, conv_w, conv_w, cb, cb, cb,
      rep(a_log), rep(dt_bias), rep(d_skip), norm_w.reshape(1, SSD_INNER))


def _s5_kernel(u_ref, wb_ref, wc_ref, are_ref, aim_ref, dsk_ref, gw_ref, gb_ref,
               o_ref, bu_ref, h_ref, *, bsz):
    ts = u_ref.shape[0] // bsz
    nt = S5_CH // LANE
    nb = S5_WIDTH // LANE
    per = nt // nb
    half = per * LANE

    @pl.when(pl.program_id(0) == 0)
    def _():
        h_ref[...] = jnp.zeros(h_ref.shape, F32)

    u = u_ref[...]
    u_bf = u.astype(BF16)
    for m in range(nb):
        res = _dot(u_bf[:, m * LANE:(m + 1) * LANE], wb_ref[m])
        for k in range(per):
            bu_ref[m * per + k] = res[:, k * LANE:(k + 1) * LANE]
            bu_ref[nt + m * per + k] = res[:, half + k * LANE:half + (k + 1) * LANE]

    def step(t, carry):
        rows = pl.ds(pl.multiple_of(t * bsz, bsz), bsz)
        new_re, new_im = [], []
        for k in range(nt):
            h_re, h_im = carry[k], carry[nt + k]
            a_re = are_ref[:, k * LANE:(k + 1) * LANE]
            a_im = aim_ref[:, k * LANE:(k + 1) * LANE]
            n_re = a_re * h_re - a_im * h_im + bu_ref[k, rows, :]
            n_im = a_re * h_im + a_im * h_re + bu_ref[nt + k, rows, :]
            bu_ref[k, rows, :] = n_re
            bu_ref[nt + k, rows, :] = n_im
            new_re.append(n_re)
            new_im.append(n_im)
        return tuple(new_re + new_im)

    h_last = lax.fori_loop(0, ts, step, tuple(h_ref[k] for k in range(2 * nt)))
    for k in range(2 * nt):
        h_ref[k] = h_last[k]

    y_parts = []
    for m in range(nb):
        hs = jnp.concatenate([bu_ref[m * per + k] for k in range(per)] +
                             [bu_ref[nt + m * per + k] for k in range(per)], axis=1)
        y_parts.append(_dot(hs.astype(BF16), wc_ref[m]))
    y = jnp.concatenate(y_parts, axis=1) + dsk_ref[...] * u
    g = 0.5 * y * (1.0 + jnp.tanh(math.sqrt(2.0 / math.pi) * (y + 0.044715 * (y * y * y))))
    out = g * _sigmoid(_dot(g.astype(BF16), gw_ref[...]) + gb_ref[...])
    o_ref[...] = out.astype(o_ref.dtype)


def _s5(u_tb, bsz, wb, wc, a_re, a_im, d_skip, glu_w, glu_b, *, ts=128):
    rows = u_tb.shape[0]
    blk = ts * bsz
    fixed = lambda shape: pl.BlockSpec(shape, lambda k: (0,) * len(shape))
    return pl.pallas_call(
        functools.partial(_s5_kernel, bsz=bsz),
        grid=(rows // blk,),
        in_specs=[
            pl.BlockSpec((blk, S5_WIDTH), lambda k: (k, 0)),
            fixed(wb.shape), fixed(wc.shape),
            fixed((1, S5_CH)), fixed((1, S5_CH)),
            fixed((1, S5_WIDTH)), fixed(glu_w.shape), fixed((1, S5_WIDTH)),
        ],
        out_specs=pl.BlockSpec((blk, S5_WIDTH), lambda k: (k, 0)),
        out_shape=jax.ShapeDtypeStruct((rows, S5_WIDTH), BF16),
        scratch_shapes=[
            pltpu.VMEM((2 * S5_CH // LANE, blk, LANE), F32),
            pltpu.VMEM((2 * S5_CH // LANE, bsz, LANE), F32),
        ],
        compiler_params=pltpu.CompilerParams(
            dimension_semantics=("arbitrary",), vmem_limit_bytes=VMEM_LIMIT),
        name="s5",
    )(u_tb, wb, wc, a_re.reshape(1, S5_CH), a_im.reshape(1, S5_CH),
      d_skip.reshape(1, S5_WIDTH), glu_w, glu_b.reshape(1, S5_WIDTH))


def _s5_params(a_re, a_im, b_re, b_im, c_re, c_im, log_dt):
    delta = jnp.exp(log_dt)[:, None]
    mag = jnp.exp(a_re * delta)
    ab_re, ab_im = mag * jnp.cos(a_im * delta), mag * jnp.sin(a_im * delta)
    den = a_re * a_re + a_im * a_im
    p_re, p_im = ab_re - 1.0, ab_im
    f_re = (p_re * a_re + p_im * a_im) / den
    f_im = (p_im * a_re - p_re * a_im) / den
    bb_re = f_re[..., None] * b_re - f_im[..., None] * b_im
    bb_im = f_re[..., None] * b_im + f_im[..., None] * b_re
    gpb = LANE // S5_GROUP
    nb = S5_GROUPS // gpb
    eye = jnp.eye(gpb, dtype=F32)
    emb_b = lambda bb: jnp.einsum('mgnj,gh->mgjhn', bb.reshape(nb, gpb, S5_STATE, S5_GROUP),
                                  eye).reshape(nb, LANE, gpb * S5_STATE)
    wb = jnp.concatenate([emb_b(bb_re), emb_b(bb_im)], axis=2)
    emb_c = lambda cc: jnp.einsum('mgin,gh->mgnhi', cc.reshape(nb, gpb, S5_GROUP, S5_STATE),
                                  eye).reshape(nb, gpb * S5_STATE, LANE)
    wc = jnp.concatenate([emb_c(c_re), -emb_c(c_im)], axis=1)
    return ab_re.reshape(-1), ab_im.reshape(-1), wb.astype(BF16), wc.astype(BF16)


def _pack_w_in(w_in):
    d = w_in.shape[0]
    w_in = w_in.astype(BF16)
    gdn_in = GDN_QKV + GDN_QK + 2 * GDN_HEADS
    ssd_conv = SSD_INNER + 2 * SSD_GROUPS * SSD_N
    g_qkv = w_in[:, 0:GDN_QKV]
    g_z = w_in[:, GDN_QKV:GDN_QKV + GDN_QK]
    g_gate = w_in[:, GDN_QKV + GDN_QK:gdn_in]
    s0 = gdn_in
    s_z = w_in[:, s0:s0 + SSD_INNER]
    s_x = w_in[:, s0 + SSD_INNER:s0 + 2 * SSD_INNER]
    s_b = w_in[:, s0 + 2 * SSD_INNER:s0 + 2 * SSD_INNER + SSD_GROUPS * SSD_N]
    s_c = w_in[:, s0 + 2 * SSD_INNER + SSD_GROUPS * SSD_N:s0 + SSD_INNER + ssd_conv]
    s_dt = w_in[:, s0 + SSD_INNER + ssd_conv:s0 + SSD_INNER + ssd_conv + SSD_HEADS]
    u0 = s0 + SSD_INNER + ssd_conv + SSD_HEADS
    s5_u = w_in[:, u0:u0 + S5_WIDTH]
    pad = lambda n: jnp.zeros((d, n), w_in.dtype)
    cols = [s_z, s_x, s5_u, s_b, s_c, s_dt, pad(LANE - SSD_HEADS), g_qkv, g_z,
            g_gate, pad(LANE - 2 * GDN_HEADS), pad(P_PAD - (COL_GDN_GATE + LANE))]
    return jnp.concatenate(cols, axis=1)


def kernel(x, ffn1_norm, ffn1_w_gate, ffn1_w_up, ffn1_w_down, mix_norm, w_in,
           gdn_conv_w, gdn_a_log, gdn_dt_bias, gdn_norm,
           ssd_conv_w, ssd_conv_b, ssd_a_log, ssd_dt_bias, ssd_d, ssd_norm,
           s5_a_re, s5_a_im, s5_b_re, s5_b_im, s5_c_re, s5_c_im, s5_d, s5_log_dt,
           s5_glu_w, s5_glu_b, w_out, ffn2_norm, ffn2_w_gate, ffn2_w_up, ffn2_w_down,
           final_norm):
    bsz, seq, d = x.shape
    depth = w_in.shape[0]
    h = x
    ffn1_w = [_to_bf16(w) for w in (ffn1_w_gate, ffn1_w_up, ffn1_w_down)]
    ffn2_w = [_to_bf16(w) for w in (ffn2_w_gate, ffn2_w_up, ffn2_w_down)]
    w_in_bf = w_in.astype(BF16)
    for i in range(depth):
        h = _ffn(h, ffn1_norm[i], *ffn1_w, final_norm, i, final_norm=False)
        proj, u_s5 = _inproj(h, mix_norm[i], _pack_w_in(w_in_bf[i]), bsz, seq)
        o_gdn = _gdn(proj, bsz, seq, gdn_conv_w[i], gdn_a_log[i], gdn_dt_bias[i], gdn_norm[i])
        o_ssd = _ssd(proj, bsz, seq, ssd_conv_w[i], ssd_conv_b[i], ssd_a_log[i], ssd_dt_bias[i],
                     ssd_d[i], ssd_norm[i])
        ab_re, ab_im, wb, wc = _s5_params(s5_a_re[i], s5_a_im[i], s5_b_re[i], s5_b_im[i],
                                          s5_c_re[i], s5_c_im[i], s5_log_dt[i])
        o_s5 = _s5(u_s5.reshape(seq * bsz, S5_WIDTH), bsz, wb, wc, ab_re, ab_im, s5_d[i],
                   s5_glu_w[i].astype(BF16), s5_glu_b[i])
        wo = w_out[i].astype(BF16)
        h = _outproj(h, o_gdn, o_ssd, o_s5.reshape(seq, bsz * S5_WIDTH), wo[0:GDN_QK],
                     wo[GDN_QK:GDN_QK + SSD_INNER], wo[GDN_QK + SSD_INNER:])
        last = i == depth - 1
        h = _ffn(h, ffn2_norm[i], *ffn2_w, final_norm, i, final_norm=last,
                 out_batched=(bsz, seq) if last else None)
    return h
```

```python
import functools
import math

import jax
import jax.numpy as jnp
from jax import lax
from jax.experimental import pallas as pl
from jax.experimental.pallas import tpu as pltpu

F32 = jnp.float32
BF16 = jnp.bfloat16

RMS_EPS = 1e-6
CHUNK = 64
CONV_K = 4
D_MODEL = 2048
D_FF = 5632

GDN_HEADS = 6
GDN_D = 128
GDN_QK = GDN_HEADS * GDN_D
GDN_QKV = 3 * GDN_QK

SSD_HEADS = 12
SSD_P = 64
SSD_GROUPS = 2
SSD_N = 128
SSD_HPG = SSD_HEADS // SSD_GROUPS
SSD_GW = SSD_HPG * SSD_P
SSD_INNER = SSD_HEADS * SSD_P

S5_WIDTH = 512
S5_GROUP = 16
S5_GROUPS = 32
S5_STATE = 64
S5_CH = S5_GROUPS * S5_STATE

LANE = 128

COL_SSD_Z = 0
COL_SSD_X = 768
COL_S5_U = 1536
COL_SSD_B = 2048
COL_SSD_C = 2304
COL_SSD_DT = 2560
COL_GDN_QKV = 2688
COL_GDN_Z = 4992
COL_GDN_GATE = 5760
P_PAD = 5888

VMEM_LIMIT = 60 * 1024 * 1024

NN = (((1,), (0,)), ((), ()))
NT = (((1,), (1,)), ((), ()))
TN = (((0,), (0,)), ((), ()))


def _dot(a, b, dims=NN):
    return lax.dot_general(a, b, dims, preferred_element_type=F32)


def _hi_lo(a):
    hi = a.astype(BF16)
    lo = (a - hi.astype(F32)).astype(BF16)
    return hi, lo


def _mm(a, b, dims=NN, passes=3):
    if passes == 1:
        return _dot(a.astype(BF16), b.astype(BF16), dims)
    ah, al = _hi_lo(a)
    bh, bl = _hi_lo(b)
    return _dot(ah, bh, dims) + (_dot(ah, bl, dims) + _dot(al, bh, dims))


def _mm01(m01, x, dims=NN):
    hi, lo = _hi_lo(x)
    return _dot(m01, hi, dims) + _dot(m01, lo, dims)


def _mm01_t(x, m01, dims=NN):
    hi, lo = _hi_lo(x)
    return _dot(hi, m01, dims) + _dot(lo, m01, dims)


def _sigmoid(x):
    return 1.0 / (1.0 + jnp.exp(-x))


def _silu(x):
    return x * _sigmoid(x)


def _softplus(x):
    return jnp.maximum(x, 0.0) + jnp.log1p(jnp.exp(-jnp.abs(x)))


def _iota(shape, dim):
    return lax.broadcasted_iota(jnp.int32, shape, dim)


def _causal_conv(x, cw_ref):
    acc = None
    for i in range(CONV_K):
        shift = CONV_K - 1 - i
        xs = x if shift == 0 else pltpu.roll(x, shift, axis=0)
        term = xs[8:] * cw_ref[i:i + 1, :]
        acc = term if acc is None else acc + term
    return acc


def _ffn_kernel(x_ref, nw_ref, wg_ref, wu_ref, wd_ref, fw_ref, o_ref, xn_ref, *, final_norm):
    j = pl.program_id(1)

    @pl.when(j == 0)
    def _():
        x = x_ref[...]
        ms = jnp.mean(x * x, axis=-1, keepdims=True)
        xn_ref[...] = (x * lax.rsqrt(ms + RMS_EPS) * nw_ref[...]).astype(BF16)
        o_ref[...] = x

    xn = xn_ref[...]
    g = _dot(xn, wg_ref[...])
    u = _dot(xn, wu_ref[...])
    a = (0.5 * _silu(g) * u).astype(BF16)
    o_ref[...] += _dot(a, wd_ref[...])

    if final_norm:
        @pl.when(j == pl.num_programs(1) - 1)
        def _():
            h = o_ref[...]
            ms = jnp.mean(h * h, axis=-1, keepdims=True)
            o_ref[...] = h * lax.rsqrt(ms + RMS_EPS) * fw_ref[...]


def _cast_kernel(x_ref, o_ref):
    o_ref[...] = x_ref[...].astype(o_ref.dtype)


def _to_bf16(w, *, rows=256):
    nl, r, c = w.shape
    spec = pl.BlockSpec((None, rows, c), lambda l, i: (l, i, 0))
    return pl.pallas_call(
        _cast_kernel,
        grid=(nl, r // rows),
        in_specs=[spec],
        out_specs=spec,
        out_shape=jax.ShapeDtypeStruct(w.shape, BF16),
        compiler_params=pltpu.CompilerParams(
            dimension_semantics=("parallel", "parallel"), vmem_limit_bytes=VMEM_LIMIT),
        name="to_bf16",
    )(w)


def _ffn(x, nw, wg, wu, wd, fw, layer, *, final_norm, out_batched=None, tm=1024, tf=512):
    d = x.shape[-1]
    t = x.size // d
    f = wg.shape[2]
    if x.ndim == 3:
        nseg = x.shape[1] // tm
        x_spec = pl.BlockSpec((None, tm, d), lambda i, j: (i // nseg, i % nseg, 0))
    else:
        x_spec = pl.BlockSpec((tm, d), lambda i, j: (i, 0))
    if out_batched is None:
        out_spec = pl.BlockSpec((tm, d), lambda i, j: (i, 0))
        out_shape = jax.ShapeDtypeStruct((t, d), F32)
    else:
        oseg = out_batched[1] // tm
        out_spec = pl.BlockSpec((None, tm, d), lambda i, j: (i // oseg, i % oseg, 0))
        out_shape = jax.ShapeDtypeStruct((*out_batched, d), F32)
    return pl.pallas_call(
        functools.partial(_ffn_kernel, final_norm=final_norm),
        grid=(t // tm, f // tf),
        in_specs=[
            x_spec,
            pl.BlockSpec((1, d), lambda i, j: (0, 0)),
            pl.BlockSpec((None, d, tf), lambda i, j: (layer, 0, j)),
            pl.BlockSpec((None, d, tf), lambda i, j: (layer, 0, j)),
            pl.BlockSpec((None, tf, d), lambda i, j: (layer, j, 0)),
            pl.BlockSpec((1, d), lambda i, j: (0, 0)),
        ],
        out_specs=out_spec,
        out_shape=out_shape,
        scratch_shapes=[pltpu.VMEM((tm, d), BF16)],
        compiler_params=pltpu.CompilerParams(
            dimension_semantics=("parallel", "arbitrary"), vmem_limit_bytes=VMEM_LIMIT),
        name="ffn",
    )(x, nw.reshape(1, d), wg, wu, wd, fw.reshape(1, d))


def _inproj_kernel(x_ref, nw_ref, w_ref, o_ref, u5_ref, xn_ref, *, tn):
    j = pl.program_id(1)

    @pl.when(j == 0)
    def _():
        x = x_ref[...]
        ms = jnp.mean(x * x, axis=-1, keepdims=True)
        xn_ref[...] = (x * lax.rsqrt(ms + RMS_EPS) * nw_ref[...]).astype(BF16)

    res = _dot(xn_ref[...], w_ref[...])
    o_ref[...] = res

    @pl.when(j == COL_S5_U // tn)
    def _():
        off = COL_S5_U % tn
        u5_ref[...] = res[:, off:off + S5_WIDTH]


def _inproj(x, nw, w, layer, bsz, seq, *, tm=256, tn=P_PAD):
    t, d = x.shape
    n = w.shape[2]
    nseg = seq // tm
    w_mode = dict(pipeline_mode=pl.Buffered(1)) if tn == n else {}
    return pl.pallas_call(
        functools.partial(_inproj_kernel, tn=tn),
        grid=(t // tm, n // tn),
        in_specs=[
            pl.BlockSpec((tm, d), lambda i, j: (i, 0)),
            pl.BlockSpec((1, d), lambda i, j: (0, 0)),
            pl.BlockSpec((None, d, tn), lambda i, j: (layer, 0, j), **w_mode),
        ],
        out_specs=[
            pl.BlockSpec((tm, tn), lambda i, j: (i, j)),
            pl.BlockSpec((tm, S5_WIDTH), lambda i, j: (i % nseg, i // nseg)),
        ],
        out_shape=[jax.ShapeDtypeStruct((t, n), F32),
                   jax.ShapeDtypeStruct((seq, bsz * S5_WIDTH), F32)],
        scratch_shapes=[pltpu.VMEM((tm, d), BF16)],
        compiler_params=pltpu.CompilerParams(
            dimension_semantics=("parallel", "arbitrary"), vmem_limit_bytes=VMEM_LIMIT),
        name="inproj",
    )(x, nw.reshape(1, d), w)


def _outproj_kernel(h_ref, a_ref, b_ref, c_ref, wa_ref, wb_ref, wc_ref, o_ref):
    acc = _dot(a_ref[...], wa_ref[...])
    acc += _dot(b_ref[...], wb_ref[...])
    acc += _dot(c_ref[...], wc_ref[...])
    o_ref[...] = h_ref[...] + acc


def _outproj(h, oa, ob, oc, wa, wb, wc, *, tm=512):
    t, d = h.shape
    nseg = oc.shape[0] // tm
    row = lambda i: (i, 0)
    fixed = lambda i: (0, 0)
    return pl.pallas_call(
        _outproj_kernel,
        grid=(t // tm,),
        in_specs=[
            pl.BlockSpec((tm, d), row),
            pl.BlockSpec((tm, oa.shape[1]), row),
            pl.BlockSpec((tm, ob.shape[1]), row),
            pl.BlockSpec((tm, S5_WIDTH), lambda i: (i % nseg, i // nseg)),
            pl.BlockSpec(wa.shape, fixed),
            pl.BlockSpec(wb.shape, fixed),
            pl.BlockSpec(wc.shape, fixed),
        ],
        out_specs=pl.BlockSpec((tm, d), row),
        out_shape=jax.ShapeDtypeStruct((t, d), F32),
        compiler_params=pltpu.CompilerParams(
            dimension_semantics=("parallel",), vmem_limit_bytes=VMEM_LIMIT),
        name="outproj",
    )(h, oa, ob, oc, wa, wb, wc)


def _chunk_rows(x_ref, lanes, ci, c):
    r0 = ci * c
    cur = x_ref[pl.ds(pl.multiple_of(r0, c), c), lanes]
    prev = x_ref[pl.ds(pl.multiple_of(jnp.maximum(r0 - 8, 0), 8), 8), lanes]
    return jnp.concatenate([jnp.where(ci == 0, 0.0, prev), cur], axis=0)


def _gdn_kernel(gpar_ref,
                q_ref, k_ref, v_ref, z_ref, gate_ref,
                cwq_ref, cwk_ref, cwv_ref, nw_ref,
                o_ref,
                u_ref, w_ref, qd_ref, kd_ref, at_ref, eg_ref,
                *, hb, p_inv, p_mix, unroll):
    hg = pl.program_id(1)
    seq = q_ref.shape[0]
    n_chunks = seq // CHUNK
    c = CHUNK
    mix_dt = BF16 if p_mix == 1 else F32

    row = _iota((c, c), 0)
    col = _iota((c, c), 1)
    tril = row >= col
    strict = row > col
    ltri = tril.astype(BF16)
    ones = jnp.ones((c, c), BF16)
    eye = (row == col).astype(F32)
    lane = _iota((c, LANE), 1)
    neg_a = -jnp.exp(gpar_ref[0:1, :])
    dt_bias = gpar_ref[1:2, :]

    def prep_stages(trip):
        cis = [trip * unroll + cc for cc in range(unroll)]
        items = [(cis[cc], hh) for cc in range(unroll) for hh in range(hb)]
        lanes = [slice(hh * LANE, (hh + 1) * LANE) for _, hh in items]
        rows = [pl.ds(pl.multiple_of(ci * c, c), c) for ci, _ in items]

        def chunk_gates(ci):
            gt = gate_ref[pl.ds(pl.multiple_of(ci * c, c), c), :]
            return _sigmoid(gt), neg_a * _softplus(gt + dt_bias)

        gates = [chunk_gates(ci) for ci in cis]

        def stage_a(cc, hh, ln):
            ci = cis[cc]
            h = hg * hb + hh
            q = _silu(_causal_conv(_chunk_rows(q_ref, ln, ci, c), cwq_ref.at[:, ln]))
            k = _silu(_causal_conv(_chunk_rows(k_ref, ln, ci, c), cwk_ref.at[:, ln]))
            v = _silu(_causal_conv(_chunk_rows(v_ref, ln, ci, c), cwv_ref.at[:, ln]))
            qn = q * lax.rsqrt(jnp.sum(q * q, axis=-1, keepdims=True) + RMS_EPS) * (GDN_D ** -0.5)
            kn = k * lax.rsqrt(jnp.sum(k * k, axis=-1, keepdims=True) + RMS_EPS)
            beta_all, g_all = gates[cc]
            beta = jnp.sum(jnp.where(lane == h, beta_all, 0.0), axis=-1, keepdims=True)
            g = jnp.sum(jnp.where(lane == h + GDN_HEADS, g_all, 0.0), axis=-1, keepdims=True)
            return qn, kn, v, beta, jnp.broadcast_to(g, (c, LANE))

        sa = [stage_a(cc, hh, slice(hh * LANE, (hh + 1) * LANE))
              for cc in range(unroll) for hh in range(hb)]
        qn = [x[0] for x in sa]
        kn = [x[1] for x in sa]
        v = [x[2] for x in sa]
        beta = [x[3] for x in sa]
        g_b = [x[4] for x in sa]
        yield
        gi = [_mm01(ltri, g) for g in g_b]
        gj = [_mm01(ones, jnp.where(row <= col, g[:, :c], 0.0)) for g in g_b]
        yield
        kb = [k * b for k, b in zip(kn, beta)]
        kk = [_mm(a, b, NT, p_inv) for a, b in zip(kb, kn)]
        qk = [_mm(a, b, NT, p_mix) for a, b in zip(qn, kn)]
        decay = [jnp.where(tril, jnp.exp(a[:, :c] - b), 0.0) for a, b in zip(gi, gj)]
        npow = [-jnp.where(strict, a * d, 0.0) for a, d in zip(kk, decay)]
        tinv = [eye + n for n in npow]
        for _ in range(5):
            yield
            npow = [_mm(n, n, NN, p_inv) for n in npow]
            tinv = [t + _mm(t, n, NN, p_inv) for t, n in zip(tinv, npow)]
        yield
        exp_g = [jnp.exp(g) for g in gi]
        rhs = [jnp.concatenate([vv * b, k * e], axis=1) for vv, b, k, e in zip(v, beta, kb, exp_g)]
        uw = [_mm(t, r, NN, p_inv) for t, r in zip(tinv, rhs)]
        for i, ((ci, hh), rw) in enumerate(zip(items, rows)):
            g_last = jnp.broadcast_to(gi[i][c - 1:c, :], (c, LANE))
            u_ref[hh, rw, :] = uw[i][:, :LANE]
            w_ref[hh, rw, :] = uw[i][:, LANE:].astype(mix_dt)
            at_ref[hh, rw, :] = jnp.where(tril, qk[i] * decay[i], 0.0).astype(mix_dt)
            qd_ref[hh, rw, :] = (qn[i] * exp_g[i]).astype(mix_dt)
            kd_ref[hh, rw, :] = (kn[i] * jnp.exp(g_last - gi[i])).astype(mix_dt)
            eg_ref[hh, pl.ds(pl.multiple_of(ci * 8, 8), 8), :] = jnp.exp(g_last[0:8, :])

    norm_w = nw_ref[...]

    def scan_stages(trip, states, final):
        for cc in range(unroll):
            ci = trip * unroll + cc
            rw = pl.ds(pl.multiple_of(ci * c, c), c)
            lhs = [jnp.concatenate([w_ref[hh, rw, :], qd_ref[hh, rw, :]], axis=0)
                   for hh in range(hb)]
            ws_qs = [_mm(a, s, NN, p_mix) for a, s in zip(lhs, states)]
            yield
            v_new = [u_ref[hh, rw, :] - ws_qs[hh][:c] for hh in range(hb)]
            kv = [_mm(kd_ref[hh, rw, :], v_new[hh], TN, p_mix) for hh in range(hb)]
            av = [_mm(at_ref[hh, rw, :], v_new[hh], NN, p_mix) for hh in range(hb)]
            new_states = []
            for hh in range(hb):
                eg = eg_ref[hh, pl.ds(pl.multiple_of(ci * 8, 8), 8), :]
                new_states.append(states[hh] * jnp.broadcast_to(eg[0:1, :], (GDN_D, GDN_D)) + kv[hh])
                o = ws_qs[hh][c:] + av[hh]
                ms = jnp.mean(o * o, axis=-1, keepdims=True)
                ln = slice(hh * LANE, (hh + 1) * LANE)
                y = o * lax.rsqrt(ms + RMS_EPS) * norm_w * _silu(z_ref[rw, ln])
                o_ref[rw, ln] = y.astype(o_ref.dtype)
            states = new_states
            yield
        final.extend(states)

    def drain(gen):
        for _ in gen:
            pass

    def fused(trip, states):
        final = []
        scan_gen = scan_stages(trip - 1, states, final)
        for _ in prep_stages(trip):
            next(scan_gen, None)
        drain(scan_gen)
        return tuple(final)

    n_trips = n_chunks // unroll
    drain(prep_stages(0))
    states = lax.fori_loop(1, n_trips, fused,
                           tuple(jnp.zeros((GDN_D, GDN_D), F32) for _ in range(hb)))
    drain(scan_stages(n_trips - 1, states, []))


def _gdn(proj, bsz, seq, conv_w, a_log, dt_bias, norm_w, *, hb=3, p_inv=1, p_mix=1, unroll=4):
    t = proj.shape[0]
    wd = hb * LANE
    mix_dt = BF16 if p_mix == 1 else F32
    q0 = COL_GDN_QKV // wd
    z0 = COL_GDN_Z // wd
    g0 = COL_GDN_GATE // LANE
    ng = GDN_HEADS // hb
    blk = lambda off: pl.BlockSpec((seq, wd), lambda b, h, off=off: (b, off + h))
    cw = lambda off: pl.BlockSpec((CONV_K, wd), lambda b, h, off=off: (0, off + h))
    pad = lambda v: jnp.pad(v.astype(F32), (GDN_HEADS, LANE - 2 * GDN_HEADS))
    gpar = jnp.stack([pad(a_log), pad(dt_bias)])
    return pl.pallas_call(
        functools.partial(_gdn_kernel, hb=hb, p_inv=p_inv, p_mix=p_mix, unroll=unroll),
        grid=(bsz, ng),
        in_specs=[
            pl.BlockSpec((2, LANE), lambda b, h: (0, 0)),
            blk(q0), blk(q0 + ng), blk(q0 + 2 * ng), blk(z0),
            pl.BlockSpec((seq, LANE), lambda b, h: (b, g0)),
            cw(0), cw(ng), cw(2 * ng),
            pl.BlockSpec((1, LANE), lambda b, h: (0, 0)),
        ],
        out_specs=pl.BlockSpec((seq, wd), lambda b, h: (b, h)),
        out_shape=jax.ShapeDtypeStruct((t, GDN_QK), BF16),
        scratch_shapes=[
            pltpu.VMEM((hb, seq, LANE), F32),
            pltpu.VMEM((hb, seq, LANE), mix_dt),
            pltpu.VMEM((hb, seq, LANE), mix_dt),
            pltpu.VMEM((hb, seq, LANE), mix_dt),
            pltpu.VMEM((hb, seq, CHUNK), mix_dt),
            pltpu.VMEM((hb, seq // CHUNK * 8, LANE), F32),
        ],
        compiler_params=pltpu.CompilerParams(
            dimension_semantics=("parallel", "arbitrary"), vmem_limit_bytes=VMEM_LIMIT),
        name="gdn",
    )(gpar, proj, proj, proj, proj, proj, conv_w, conv_w, conv_w, norm_w.reshape(1, LANE))


def _ssd_kernel(z_ref, x_ref, b_ref, c_ref, dt_ref,
                cwx_ref, cwb_ref, cwc_ref,
                cbx_ref, cbb_ref, cbc_ref,
                alog_ref, dtb_ref, dsk_ref, nw_ref,
                o_ref,
                s_ref, *, passes, unroll):
    grp = pl.program_id(1)
    seq = x_ref.shape[0]
    n_chunks = seq // CHUNK
    c = CHUNK
    gw = SSD_GW

    row = _iota((c, c), 0)
    col = _iota((c, c), 1)
    ltri = (row >= col).astype(BF16)
    ones = jnp.ones((c, c), BF16)
    rowt = _iota((c, gw), 0)
    colt = jnp.bitwise_and(_iota((c, gw), 1), c - 1)
    tril_t = rowt >= colt
    upper_t = rowt <= colt
    expand = (_iota((LANE, gw), 0) == grp * SSD_HPG + lax.shift_right_logical(_iota((LANE, gw), 1), 6)).astype(BF16)
    last_row = (_iota((8, gw), 0) == 7)
    ones_8n = jnp.ones((8, SSD_N), BF16)

    neg_a = -jnp.exp(alog_ref[...])
    dt_bias = dtb_ref[...]
    d_skip = dsk_ref[...]
    norm_w = nw_ref[...]

    def conv(x_ref_, cw_ref, cb_ref, ci):
        return _silu(_causal_conv(_chunk_rows(x_ref_, slice(None), ci, c), cw_ref) + cb_ref[...])

    s_ref[...] = jnp.zeros((gw, SSD_N), F32)
    heads = [slice(hh * SSD_P, (hh + 1) * SSD_P) for hh in range(SSD_HPG)]

    def body(trip, carry):
        cis = [trip * unroll + cc for cc in range(unroll)]
        rows = [pl.ds(pl.multiple_of(ci * c, c), c) for ci in cis]
        xs = [conv(x_ref, cwx_ref, cbx_ref, ci) for ci in cis]
        bm = [conv(b_ref, cwb_ref, cbb_ref, ci) for ci in cis]
        cm = [conv(c_ref, cwc_ref, cbc_ref, ci) for ci in cis]
        dt = [_softplus(_mm01_t(dt_ref[rw, :], expand) + dt_bias) for rw in rows]
        a = [d * neg_a for d in dt]
        acum = [_mm01(ltri, x) for x in a]
        acum_j = [_mm01(ones, jnp.where(upper_t, x, 0.0)) for x in a]
        cb = [_mm(x, y, NT, passes) for x, y in zip(cm, bm)]
        lm = [jnp.where(tril_t, jnp.exp(x - y), 0.0) for x, y in zip(acum, acum_j)]
        xdt = [x * d for x, d in zip(xs, dt)]
        y_diag = [jnp.concatenate([_mm(cb[i] * lm[i][:, sl], xdt[i][:, sl], NN, passes)
                                   for sl in heads], axis=1) for i in range(unroll)]
        a_last = [jnp.broadcast_to(x[c - 1:c, :], (c, gw)) for x in acum]
        states = [_mm(xdt[i] * jnp.exp(a_last[i] - acum[i]), bm[i], TN, passes)
                  for i in range(unroll)]
        dec_col = [_mm01_t(jnp.where(last_row, jnp.exp(x[c - 8:, :]), 0.0), ones_8n, TN)
                   for x in acum]
        s_prev = s_ref[...]
        for i in range(unroll):
            y_off = _mm(cm[i], s_prev, NT, passes) * jnp.exp(acum[i])
            s_prev = s_prev * dec_col[i] + states[i]
            y = y_diag[i] + y_off + d_skip * xs[i]
            y = y * _silu(z_ref[rows[i], :])
            ms = jnp.mean(y * y, axis=-1, keepdims=True)
            o_ref[rows[i], :] = (y * lax.rsqrt(ms + RMS_EPS) * norm_w).astype(o_ref.dtype)
        s_ref[...] = s_prev
        return carry

    lax.fori_loop(0, n_chunks // unroll, body, 0)


def _ssd(proj, bsz, seq, conv_w, conv_b, a_log, dt_bias, d_skip, norm_w, *, passes=1, unroll=4):
    t = proj.shape[0]
    gw = SSD_GW
    rep = lambda v: jnp.repeat(v.astype(F32), SSD_P).reshape(1, SSD_INNER)
    cb = conv_b.reshape(1, -1)
    wide = lambda off: pl.BlockSpec((seq, gw), lambda b, g, off=off: (b, off + g))
    lane = lambda off: pl.BlockSpec((seq, LANE), lambda b, g, off=off: (b, off + g))
    chan = pl.BlockSpec((1, gw), lambda b, g: (0, g))
    return pl.pallas_call(
        functools.partial(_ssd_kernel, passes=passes, unroll=unroll),
        grid=(bsz, SSD_GROUPS),
        in_specs=[
            wide(COL_SSD_Z // gw), wide(COL_SSD_X // gw),
            lane(COL_SSD_B // LANE), lane(COL_SSD_C // LANE),
            pl.BlockSpec((seq, LANE), lambda b, g: (b, COL_SSD_DT // LANE)),
            pl.BlockSpec((CONV_K, gw), lambda b, g: (0, g)),
            pl.BlockSpec((CONV_K, LANE), lambda b, g: (0, SSD_INNER // LANE + g)),
            pl.BlockSpec((CONV_K, LANE), lambda b, g: (0, SSD_INNER // LANE + SSD_GROUPS + g)),
            pl.BlockSpec((1, gw), lambda b, g: (0, g)),
            pl.BlockSpec((1, LANE), lambda b, g: (0, SSD_INNER // LANE + g)),
            pl.BlockSpec((1, LANE), lambda b, g: (0, SSD_INNER // LANE + SSD_GROUPS + g)),
            chan, chan, chan, chan,
        ],
        out_specs=pl.BlockSpec((seq, gw), lambda b, g: (b, g)),
        out_shape=jax.ShapeDtypeStruct((t, SSD_INNER), BF16),
        scratch_shapes=[pltpu.VMEM((gw, SSD_N), F32)],
        compiler_params=pltpu.CompilerParams(
            dimension_semantics=("parallel", "arbitrary"), vmem_limit_bytes=VMEM_LIMIT),
        name="ssd",
    )(proj, proj, proj, proj, proj, conv_w, conv_w, conv_w, cb, cb, cb,
      rep(a_log), rep(dt_bias), rep(d_skip), norm_w.reshape(1, SSD_INNER))


def _s5_kernel(u_ref, wb_ref, wc_ref, are_ref, aim_ref, dsk_ref, gw_ref, gb_ref,
               o_ref, bu_ref, h_ref, *, bsz):
    ts = u_ref.shape[0] // bsz
    nt = S5_CH // LANE
    nb = S5_WIDTH // LANE
    per = nt // nb
    half = per * LANE

    @pl.when(pl.program_id(0) == 0)
    def _():
        h_ref[...] = jnp.zeros(h_ref.shape, F32)

    u = u_ref[...]
    u_bf = u.astype(BF16)
    for m in range(nb):
        res = _dot(u_bf[:, m * LANE:(m + 1) * LANE], wb_ref[m])
        for k in range(per):
            bu_ref[m * per + k] = res[:, k * LANE:(k + 1) * LANE]
            bu_ref[nt + m * per + k] = res[:, half + k * LANE:half + (k + 1) * LANE]

    def step(t, carry):
        rows = pl.ds(pl.multiple_of(t * bsz, bsz), bsz)
        new_re, new_im = [], []
        for k in range(nt):
            h_re, h_im = carry[k], carry[nt + k]
            a_re = are_ref[:, k * LANE:(k + 1) * LANE]
            a_im = aim_ref[:, k * LANE:(k + 1) * LANE]
            n_re = a_re * h_re - a_im * h_im + bu_ref[k, rows, :]
            n_im = a_re * h_im + a_im * h_re + bu_ref[nt + k, rows, :]
            bu_ref[k, rows, :] = n_re
            bu_ref[nt + k, rows, :] = n_im
            new_re.append(n_re)
            new_im.append(n_im)
        return tuple(new_re + new_im)

    h_last = lax.fori_loop(0, ts, step, tuple(h_ref[k] for k in range(2 * nt)))
    for k in range(2 * nt):
        h_ref[k] = h_last[k]

    y_parts = []
    for m in range(nb):
        hs = jnp.concatenate([bu_ref[m * per + k] for k in range(per)] +
                             [bu_ref[nt + m * per + k] for k in range(per)], axis=1)
        y_parts.append(_dot(hs.astype(BF16), wc_ref[m]))
    y = jnp.concatenate(y_parts, axis=1) + dsk_ref[...] * u
    g = 0.5 * y * (1.0 + jnp.tanh(math.sqrt(2.0 / math.pi) * (y + 0.044715 * (y * y * y))))
    out = g * _sigmoid(_dot(g.astype(BF16), gw_ref[...]) + gb_ref[...])
    o_ref[...] = out.astype(o_ref.dtype)


def _s5(u_tb, bsz, wb, wc, a_re, a_im, d_skip, glu_w, glu_b, *, ts=128):
    rows = u_tb.shape[0]
    blk = ts * bsz
    fixed = lambda shape: pl.BlockSpec(shape, lambda k: (0,) * len(shape))
    return pl.pallas_call(
        functools.partial(_s5_kernel, bsz=bsz),
        grid=(rows // blk,),
        in_specs=[
            pl.BlockSpec((blk, S5_WIDTH), lambda k: (k, 0)),
            fixed(wb.shape), fixed(wc.shape),
            fixed((1, S5_CH)), fixed((1, S5_CH)),
            fixed((1, S5_WIDTH)), fixed(glu_w.shape), fixed((1, S5_WIDTH)),
        ],
        out_specs=pl.BlockSpec((blk, S5_WIDTH), lambda k: (k, 0)),
        out_shape=jax.ShapeDtypeStruct((rows, S5_WIDTH), BF16),
        scratch_shapes=[
            pltpu.VMEM((2 * S5_CH // LANE, blk, LANE), F32),
            pltpu.VMEM((2 * S5_CH // LANE, bsz, LANE), F32),
        ],
        compiler_params=pltpu.CompilerParams(
            dimension_semantics=("arbitrary",), vmem_limit_bytes=VMEM_LIMIT),
        name="s5",
    )(u_tb, wb, wc, a_re.reshape(1, S5_CH), a_im.reshape(1, S5_CH),
      d_skip.reshape(1, S5_WIDTH), glu_w, glu_b.reshape(1, S5_WIDTH))


def _s5_params(a_re, a_im, b_re, b_im, c_re, c_im, log_dt):
    delta = jnp.exp(log_dt)[:, None]
    mag = jnp.exp(a_re * delta)
    ab_re, ab_im = mag * jnp.cos(a_im * delta), mag * jnp.sin(a_im * delta)
    den = a_re * a_re + a_im * a_im
    p_re, p_im = ab_re - 1.0, ab_im
    f_re = (p_re * a_re + p_im * a_im) / den
    f_im = (p_im * a_re - p_re * a_im) / den
    bb_re = f_re[..., None] * b_re - f_im[..., None] * b_im
    bb_im = f_re[..., None] * b_im + f_im[..., None] * b_re
    gpb = LANE // S5_GROUP
    nb = S5_GROUPS // gpb
    eye = jnp.eye(gpb, dtype=F32)
    emb_b = lambda bb: jnp.einsum('mgnj,gh->mgjhn', bb.reshape(nb, gpb, S5_STATE, S5_GROUP),
                                  eye).reshape(nb, LANE, gpb * S5_STATE)
    wb = jnp.concatenate([emb_b(bb_re), emb_b(bb_im)], axis=2)
    emb_c = lambda cc: jnp.einsum('mgin,gh->mgnhi', cc.reshape(nb, gpb, S5_GROUP, S5_STATE),
                                  eye).reshape(nb, gpb * S5_STATE, LANE)
    wc = jnp.concatenate([emb_c(c_re), -emb_c(c_im)], axis=1)
    return ab_re.reshape(-1), ab_im.reshape(-1), wb.astype(BF16), wc.astype(BF16)


def _w_in_segments():
    bc = SSD_GROUPS * SSD_N
    g_z = GDN_QKV
    g_gate = g_z + GDN_QK
    s_z = g_gate + 2 * GDN_HEADS
    s_x = s_z + SSD_INNER
    s_b = s_x + SSD_INNER
    s_c = s_b + bc
    s_dt = s_c + bc
    s5_u = s_dt + SSD_HEADS
    return ((0, GDN_QKV, COL_GDN_QKV), (g_z, GDN_QK, COL_GDN_Z),
            (g_gate, 2 * GDN_HEADS, COL_GDN_GATE), (s_z, SSD_INNER, COL_SSD_Z),
            (s_x, SSD_INNER, COL_SSD_X), (s_b, bc, COL_SSD_B), (s_c, bc, COL_SSD_C),
            (s_dt, SSD_HEADS, COL_SSD_DT), (s5_u, S5_WIDTH, COL_S5_U))


def _pack_kernel(x_ref, o_ref):
    for col in (COL_SSD_DT, COL_GDN_GATE):
        o_ref[:, col:col + LANE] = jnp.zeros((o_ref.shape[0], LANE), o_ref.dtype)
    for src, width, dst in _w_in_segments():
        o_ref[:, dst:dst + width] = x_ref[:, src:src + width].astype(o_ref.dtype)


def _pack_w_in(w_in, *, rows=256):
    nl, d, c = w_in.shape
    return pl.pallas_call(
        _pack_kernel,
        grid=(nl, d // rows),
        in_specs=[pl.BlockSpec((None, rows, c), lambda l, i: (l, i, 0))],
        out_specs=pl.BlockSpec((None, rows, P_PAD), lambda l, i: (l, i, 0)),
        out_shape=jax.ShapeDtypeStruct((nl, d, P_PAD), BF16),
        compiler_params=pltpu.CompilerParams(
            dimension_semantics=("parallel", "parallel"), vmem_limit_bytes=VMEM_LIMIT),
        name="pack_w_in",
    )(w_in)


def kernel(x, ffn1_norm, ffn1_w_gate, ffn1_w_up, ffn1_w_down, mix_norm, w_in,
           gdn_conv_w, gdn_a_log, gdn_dt_bias, gdn_norm,
           ssd_conv_w, ssd_conv_b, ssd_a_log, ssd_dt_bias, ssd_d, ssd_norm,
           s5_a_re, s5_a_im, s5_b_re, s5_b_im, s5_c_re, s5_c_im, s5_d, s5_log_dt,
           s5_glu_w, s5_glu_b, w_out, ffn2_norm, ffn2_w_gate, ffn2_w_up, ffn2_w_down,
           final_norm):
    bsz, seq, d = x.shape
    depth = w_in.shape[0]
    h = x
    ffn1_w = [_to_bf16(w) for w in (ffn1_w_gate, ffn1_w_up, ffn1_w_down)]
    ffn2_w = [_to_bf16(w) for w in (ffn2_w_gate, ffn2_w_up, ffn2_w_down)]
    w_in_packed = _pack_w_in(w_in)
    for i in range(depth):
        h = _ffn(h, ffn1_norm[i], *ffn1_w, final_norm, i, final_norm=False)
        proj, u_s5 = _inproj(h, mix_norm[i], w_in_packed, i, bsz, seq)
        o_gdn = _gdn(proj, bsz, seq, gdn_conv_w[i], gdn_a_log[i], gdn_dt_bias[i], gdn_norm[i])
        o_ssd = _ssd(proj, bsz, seq, ssd_conv_w[i], ssd_conv_b[i], ssd_a_log[i], ssd_dt_bias[i],
                     ssd_d[i], ssd_norm[i])
        ab_re, ab_im, wb, wc = _s5_params(s5_a_re[i], s5_a_im[i], s5_b_re[i], s5_b_im[i],
                                          s5_c_re[i], s5_c_im[i], s5_log_dt[i])
        o_s5 = _s5(u_s5.reshape(seq * bsz, S5_WIDTH), bsz, wb, wc, ab_re, ab_im, s5_d[i],
                   s5_glu_w[i].astype(BF16), s5_glu_b[i])
        wo = w_out[i].astype(BF16)
        h = _outproj(h, o_gdn, o_ssd, o_s5.reshape(seq, bsz * S5_WIDTH), wo[0:GDN_QK],
                     wo[GDN_QK:GDN_QK + SSD_INNER], wo[GDN_QK + SSD_INNER:])
        last = i == depth - 1
        h = _ffn(h, ffn2_norm[i], *ffn2_w, final_norm, i, final_norm=last,
                 out_batched=(bsz, seq) if last else None)
    return h
```

```python
import functools
import math

import jax
import jax.numpy as jnp
from jax import lax
from jax.experimental import pallas as pl
from jax.experimental.pallas import tpu as pltpu

F32 = jnp.float32
BF16 = jnp.bfloat16

RMS_EPS = 1e-6
CHUNK = 64
CONV_K = 4
D_MODEL = 2048
D_FF = 5632

GDN_HEADS = 6
GDN_D = 128
GDN_QK = GDN_HEADS * GDN_D
GDN_QKV = 3 * GDN_QK

SSD_HEADS = 12
SSD_P = 64
SSD_GROUPS = 2
SSD_N = 128
SSD_HPG = SSD_HEADS // SSD_GROUPS
SSD_GW = SSD_HPG * SSD_P
SSD_INNER = SSD_HEADS * SSD_P

S5_WIDTH = 512
S5_GROUP = 16
S5_GROUPS = 32
S5_STATE = 64
S5_CH = S5_GROUPS * S5_STATE

LANE = 128

COL_GDN_QKV = 0
COL_GDN_Z = 2304
COL_SSD_Z = 3072
COL_SSD_X = 3840
COL_SSD_B = 4608
COL_SSD_C = 4864
COL_S5_U = 5120
COL_SSD_DT = 5632
COL_GDN_GATE = 5760
P_PAD = 5888

VMEM_LIMIT = 60 * 1024 * 1024

NN = (((1,), (0,)), ((), ()))
NT = (((1,), (1,)), ((), ()))
TN = (((0,), (0,)), ((), ()))


def _dot(a, b, dims=NN):
    return lax.dot_general(a, b, dims, preferred_element_type=F32)


def _hi_lo(a):
    hi = a.astype(BF16)
    lo = (a - hi.astype(F32)).astype(BF16)
    return hi, lo


def _mm(a, b, dims=NN, passes=3):
    if passes == 1:
        return _dot(a.astype(BF16), b.astype(BF16), dims)
    ah, al = _hi_lo(a)
    bh, bl = _hi_lo(b)
    return _dot(ah, bh, dims) + (_dot(ah, bl, dims) + _dot(al, bh, dims))


def _mm01(m01, x, dims=NN):
    hi, lo = _hi_lo(x)
    return _dot(m01, hi, dims) + _dot(m01, lo, dims)


def _mm01_t(x, m01, dims=NN):
    hi, lo = _hi_lo(x)
    return _dot(hi, m01, dims) + _dot(lo, m01, dims)


def _sigmoid(x):
    return 1.0 / (1.0 + jnp.exp(-x))


def _silu(x):
    return x * _sigmoid(x)


def _softplus(x):
    return jnp.maximum(x, 0.0) + jnp.log1p(jnp.exp(-jnp.abs(x)))


def _iota(shape, dim):
    return lax.broadcasted_iota(jnp.int32, shape, dim)


def _causal_conv(x, cw_ref):
    acc = None
    for i in range(CONV_K):
        shift = CONV_K - 1 - i
        xs = x if shift == 0 else pltpu.roll(x, shift, axis=0)
        term = xs[8:] * cw_ref[i:i + 1, :]
        acc = term if acc is None else acc + term
    return acc


def _ffn_kernel(x_ref, nw_ref, wg_ref, wu_ref, wd_ref, fw_ref, o_ref, xn_ref, *, final_norm):
    j = pl.program_id(1)

    @pl.when(j == 0)
    def _():
        x = x_ref[...]
        ms = jnp.mean(x * x, axis=-1, keepdims=True)
        xn_ref[...] = (x * lax.rsqrt(ms + RMS_EPS) * nw_ref[...]).astype(BF16)
        o_ref[...] = x

    xn = xn_ref[...]
    g = _dot(xn, wg_ref[...])
    u = _dot(xn, wu_ref[...])
    a = (0.5 * _silu(g) * u).astype(BF16)
    o_ref[...] += _dot(a, wd_ref[...])

    if final_norm:
        @pl.when(j == pl.num_programs(1) - 1)
        def _():
            h = o_ref[...]
            ms = jnp.mean(h * h, axis=-1, keepdims=True)
            o_ref[...] = h * lax.rsqrt(ms + RMS_EPS) * fw_ref[...]


def _cast_kernel(x_ref, o_ref):
    o_ref[...] = x_ref[...].astype(o_ref.dtype)


def _to_bf16(w, *, rows=256):
    nl, r, c = w.shape
    spec = pl.BlockSpec((None, rows, c), lambda l, i: (l, i, 0))
    return pl.pallas_call(
        _cast_kernel,
        grid=(nl, r // rows),
        in_specs=[spec],
        out_specs=spec,
        out_shape=jax.ShapeDtypeStruct(w.shape, BF16),
        compiler_params=pltpu.CompilerParams(
            dimension_semantics=("parallel", "parallel"), vmem_limit_bytes=VMEM_LIMIT),
        name="to_bf16",
    )(w)


def _ffn(x, nw, wg, wu, wd, fw, layer, *, final_norm, out_batched=None, tm=1024, tf=512):
    d = x.shape[-1]
    t = x.size // d
    f = wg.shape[2]
    if x.ndim == 3:
        nseg = x.shape[1] // tm
        x_spec = pl.BlockSpec((None, tm, d), lambda i, j: (i // nseg, i % nseg, 0))
    else:
        x_spec = pl.BlockSpec((tm, d), lambda i, j: (i, 0))
    if out_batched is None:
        out_spec = pl.BlockSpec((tm, d), lambda i, j: (i, 0))
        out_shape = jax.ShapeDtypeStruct((t, d), F32)
    else:
        oseg = out_batched[1] // tm
        out_spec = pl.BlockSpec((None, tm, d), lambda i, j: (i // oseg, i % oseg, 0))
        out_shape = jax.ShapeDtypeStruct((*out_batched, d), F32)
    return pl.pallas_call(
        functools.partial(_ffn_kernel, final_norm=final_norm),
        grid=(t // tm, f // tf),
        in_specs=[
            x_spec,
            pl.BlockSpec((1, d), lambda i, j: (0, 0)),
            pl.BlockSpec((None, d, tf), lambda i, j: (layer, 0, j)),
            pl.BlockSpec((None, d, tf), lambda i, j: (layer, 0, j)),
            pl.BlockSpec((None, tf, d), lambda i, j: (layer, j, 0)),
            pl.BlockSpec((1, d), lambda i, j: (0, 0)),
        ],
        out_specs=out_spec,
        out_shape=out_shape,
        scratch_shapes=[pltpu.VMEM((tm, d), BF16)],
        compiler_params=pltpu.CompilerParams(
            dimension_semantics=("parallel", "arbitrary"), vmem_limit_bytes=VMEM_LIMIT),
        name="ffn",
    )(x, nw.reshape(1, d), wg, wu, wd, fw.reshape(1, d))


CONV_RANGES = ((COL_SSD_X, SSD_INNER), (COL_SSD_B, SSD_GROUPS * SSD_N),
               (COL_SSD_C, SSD_GROUPS * SSD_N), (COL_GDN_QKV, GDN_QKV))
INPROJ_PIECE = 256


def _inproj_kernel(x_ref, nw_ref, w_ref, cw_ref, cb_ref, o_ref, u5_ref, stage_ref, *, nseg):
    tm = x_ref.shape[0]

    @pl.when(pl.program_id(0) % nseg == 0)
    def _():
        stage_ref[0:8, :] = jnp.zeros((8, stage_ref.shape[1]), F32)

    x = x_ref[...]
    ms = jnp.mean(x * x, axis=-1, keepdims=True)
    xn = (x * lax.rsqrt(ms + RMS_EPS) * nw_ref[...]).astype(BF16)
    starts = list(range(0, o_ref.shape[1], INPROJ_PIECE))
    piece = lambda col: _dot(xn, w_ref[:, col:col + INPROJ_PIECE])
    nxt = piece(starts[0])
    for idx, col in enumerate(starts):
        cols = slice(col, col + INPROJ_PIECE)
        res, nxt = nxt, (piece(starts[idx + 1]) if idx + 1 < len(starts) else None)
        if any(lo <= col < lo + width for lo, width in CONV_RANGES):
            stage_ref[8:, cols] = res
            for r in range(0, tm, CHUNK):
                y = _causal_conv(stage_ref[r:r + CHUNK + 8, cols], cw_ref.at[:, cols])
                o_ref[r:r + CHUNK, cols] = _silu(y + cb_ref[:, cols])
            stage_ref[0:8, cols] = res[tm - 8:, :]
        else:
            o_ref[:, cols] = res
        if COL_S5_U <= col < COL_S5_U + S5_WIDTH:
            u5_ref[:, col - COL_S5_U:col - COL_S5_U + INPROJ_PIECE] = res


def _inproj(x, nw, w, layer, conv_w, conv_b, bsz, seq, *, tm=256):
    t, d = x.shape
    n = w.shape[2]
    nseg = seq // tm
    fixed = lambda i: (0, 0)
    return pl.pallas_call(
        functools.partial(_inproj_kernel, nseg=nseg),
        grid=(t // tm,),
        in_specs=[
            pl.BlockSpec((tm, d), lambda i: (i, 0)),
            pl.BlockSpec((1, d), fixed),
            pl.BlockSpec((None, d, n), lambda i: (layer, 0, 0), pipeline_mode=pl.Buffered(1)),
            pl.BlockSpec((CONV_K, n), fixed),
            pl.BlockSpec((1, n), fixed),
        ],
        out_specs=[
            pl.BlockSpec((tm, n), lambda i: (i, 0)),
            pl.BlockSpec((tm, S5_WIDTH), lambda i: (i % nseg, i // nseg)),
        ],
        out_shape=[jax.ShapeDtypeStruct((t, n), F32),
                   jax.ShapeDtypeStruct((seq, bsz * S5_WIDTH), F32)],
        scratch_shapes=[pltpu.VMEM((tm + 8, n), F32)],
        compiler_params=pltpu.CompilerParams(
            dimension_semantics=("arbitrary",), vmem_limit_bytes=VMEM_LIMIT),
        name="inproj",
    )(x, nw.reshape(1, d), w, conv_w, conv_b)


def _outproj_kernel(h_ref, a_ref, b_ref, c_ref, wa_ref, wb_ref, wc_ref, o_ref):
    acc = _dot(a_ref[...], wa_ref[...])
    acc += _dot(b_ref[...], wb_ref[...])
    acc += _dot(c_ref[...], wc_ref[...])
    o_ref[...] = h_ref[...] + acc


def _outproj(h, oa, ob, oc, wa, wb, wc, *, tm=512):
    t, d = h.shape
    nseg = oc.shape[0] // tm
    row = lambda i: (i, 0)
    fixed = lambda i: (0, 0)
    return pl.pallas_call(
        _outproj_kernel,
        grid=(t // tm,),
        in_specs=[
            pl.BlockSpec((tm, d), row),
            pl.BlockSpec((tm, oa.shape[1]), row),
            pl.BlockSpec((tm, ob.shape[1]), row),
            pl.BlockSpec((tm, S5_WIDTH), lambda i: (i % nseg, i // nseg)),
            pl.BlockSpec(wa.shape, fixed),
            pl.BlockSpec(wb.shape, fixed),
            pl.BlockSpec(wc.shape, fixed),
        ],
        out_specs=pl.BlockSpec((tm, d), row),
        out_shape=jax.ShapeDtypeStruct((t, d), F32),
        compiler_params=pltpu.CompilerParams(
            dimension_semantics=("parallel",), vmem_limit_bytes=VMEM_LIMIT),
        name="outproj",
    )(h, oa, ob, oc, wa, wb, wc)


def _gdn_kernel(gpar_ref,
                q_ref, k_ref, v_ref, z_ref, gate_ref,
                nw_ref,
                o_ref,
                u_ref, w_ref, qd_ref, kd_ref, at_ref, eg_ref,
                *, hb, p_inv, p_mix, unroll):
    hg = pl.program_id(1)
    seq = q_ref.shape[0]
    n_chunks = seq // CHUNK
    c = CHUNK
    mix_dt = BF16 if p_mix == 1 else F32

    row = _iota((c, c), 0)
    col = _iota((c, c), 1)
    tril = row >= col
    strict = row > col
    ltri = tril.astype(BF16)
    ones = jnp.ones((c, c), BF16)
    eye = (row == col).astype(F32)
    lane = _iota((c, LANE), 1)
    neg_a = -jnp.exp(gpar_ref[0:1, :])
    dt_bias = gpar_ref[1:2, :]

    def prep_stages(trip):
        cis = [trip * unroll + cc for cc in range(unroll)]
        items = [(cis[cc], hh) for cc in range(unroll) for hh in range(hb)]
        lanes = [slice(hh * LANE, (hh + 1) * LANE) for _, hh in items]
        rows = [pl.ds(pl.multiple_of(ci * c, c), c) for ci, _ in items]

        def chunk_gates(ci):
            gt = gate_ref[pl.ds(pl.multiple_of(ci * c, c), c), :]
            return _sigmoid(gt), neg_a * _softplus(gt + dt_bias)

        gates = [chunk_gates(ci) for ci in cis]

        def stage_a(cc, hh, ln):
            ci = cis[cc]
            h = hg * hb + hh
            rw = pl.ds(pl.multiple_of(ci * c, c), c)
            q, k, v = q_ref[rw, ln], k_ref[rw, ln], v_ref[rw, ln]
            qn = q * lax.rsqrt(jnp.sum(q * q, axis=-1, keepdims=True) + RMS_EPS) * (GDN_D ** -0.5)
            kn = k * lax.rsqrt(jnp.sum(k * k, axis=-1, keepdims=True) + RMS_EPS)
            beta_all, g_all = gates[cc]
            beta = jnp.sum(jnp.where(lane == h, beta_all, 0.0), axis=-1, keepdims=True)
            g = jnp.sum(jnp.where(lane == h + GDN_HEADS, g_all, 0.0), axis=-1, keepdims=True)
            return qn, kn, v, beta, jnp.broadcast_to(g, (c, LANE))

        sa = [stage_a(cc, hh, slice(hh * LANE, (hh + 1) * LANE))
              for cc in range(unroll) for hh in range(hb)]
        qn = [x[0] for x in sa]
        kn = [x[1] for x in sa]
        v = [x[2] for x in sa]
        beta = [x[3] for x in sa]
        g_b = [x[4] for x in sa]
        yield
        gi = [_mm01(ltri, g) for g in g_b]
        gj = [_mm01(ones, jnp.where(row <= col, g[:, :c], 0.0)) for g in g_b]
        yield
        kb = [k * b for k, b in zip(kn, beta)]
        kk = [_mm(a, b, NT, p_inv) for a, b in zip(kb, kn)]
        qk = [_mm(a, b, NT, p_mix) for a, b in zip(qn, kn)]
        decay = [jnp.where(tril, jnp.exp(a[:, :c] - b), 0.0) for a, b in zip(gi, gj)]
        npow = [-jnp.where(strict, a * d, 0.0) for a, d in zip(kk, decay)]
        tinv = [eye + n for n in npow]
        for _ in range(5):
            yield
            npow = [_mm(n, n, NN, p_inv) for n in npow]
            tinv = [t + _mm(t, n, NN, p_inv) for t, n in zip(tinv, npow)]
        yield
        exp_g = [jnp.exp(g) for g in gi]
        rhs = [jnp.concatenate([vv * b, k * e], axis=1) for vv, b, k, e in zip(v, beta, kb, exp_g)]
        uw = [_mm(t, r, NN, p_inv) for t, r in zip(tinv, rhs)]
        for i, ((ci, hh), rw) in enumerate(zip(items, rows)):
            g_last = jnp.broadcast_to(gi[i][c - 1:c, :], (c, LANE))
            u_ref[hh, rw, :] = uw[i][:, :LANE]
            w_ref[hh, rw, :] = uw[i][:, LANE:].astype(mix_dt)
            at_ref[hh, rw, :] = jnp.where(tril, qk[i] * decay[i], 0.0).astype(mix_dt)
            qd_ref[hh, rw, :] = (qn[i] * exp_g[i]).astype(mix_dt)
            kd_ref[hh, rw, :] = (kn[i] * jnp.exp(g_last - gi[i])).astype(mix_dt)
            eg_ref[hh, pl.ds(pl.multiple_of(ci * 8, 8), 8), :] = jnp.exp(g_last[0:8, :])

    norm_w = nw_ref[...]

    def scan_stages(trip, states, final):
        for cc in range(unroll):
            ci = trip * unroll + cc
            rw = pl.ds(pl.multiple_of(ci * c, c), c)
            lhs = [jnp.concatenate([w_ref[hh, rw, :], qd_ref[hh, rw, :]], axis=0)
                   for hh in range(hb)]
            ws_qs = [_mm(a, s, NN, p_mix) for a, s in zip(lhs, states)]
            yield
            v_new = [u_ref[hh, rw, :] - ws_qs[hh][:c] for hh in range(hb)]
            kv = [_mm(kd_ref[hh, rw, :], v_new[hh], TN, p_mix) for hh in range(hb)]
            av = [_mm(at_ref[hh, rw, :], v_new[hh], NN, p_mix) for hh in range(hb)]
            new_states = []
            for hh in range(hb):
                eg = eg_ref[hh, pl.ds(pl.multiple_of(ci * 8, 8), 8), :]
                new_states.append(states[hh] * jnp.broadcast_to(eg[0:1, :], (GDN_D, GDN_D)) + kv[hh])
                o = ws_qs[hh][c:] + av[hh]
                ms = jnp.mean(o * o, axis=-1, keepdims=True)
                ln = slice(hh * LANE, (hh + 1) * LANE)
                y = o * lax.rsqrt(ms + RMS_EPS) * norm_w * _silu(z_ref[rw, ln])
                o_ref[rw, ln] = y.astype(o_ref.dtype)
            states = new_states
            yield
        final.extend(states)

    def drain(gen):
        for _ in gen:
            pass

    def fused(trip, states):
        final = []
        scan_gen = scan_stages(trip - 1, states, final)
        for _ in prep_stages(trip):
            next(scan_gen, None)
        drain(scan_gen)
        return tuple(final)

    n_trips = n_chunks // unroll
    drain(prep_stages(0))
    states = lax.fori_loop(1, n_trips, fused,
                           tuple(jnp.zeros((GDN_D, GDN_D), F32) for _ in range(hb)))
    drain(scan_stages(n_trips - 1, states, []))


def _gdn(proj, bsz, seq, a_log, dt_bias, norm_w, *, hb=3, p_inv=1, p_mix=1, unroll=4):
    t = proj.shape[0]
    wd = hb * LANE
    mix_dt = BF16 if p_mix == 1 else F32
    q0 = COL_GDN_QKV // wd
    z0 = COL_GDN_Z // wd
    g0 = COL_GDN_GATE // LANE
    ng = GDN_HEADS // hb
    blk = lambda off: pl.BlockSpec((seq, wd), lambda b, h, off=off: (b, off + h))
    pad = lambda v: jnp.pad(v.astype(F32), (GDN_HEADS, LANE - 2 * GDN_HEADS))
    gpar = jnp.stack([pad(a_log), pad(dt_bias)])
    return pl.pallas_call(
        functools.partial(_gdn_kernel, hb=hb, p_inv=p_inv, p_mix=p_mix, unroll=unroll),
        grid=(bsz, ng),
        in_specs=[
            pl.BlockSpec((2, LANE), lambda b, h: (0, 0)),
            blk(q0), blk(q0 + ng), blk(q0 + 2 * ng), blk(z0),
            pl.BlockSpec((seq, LANE), lambda b, h: (b, g0)),
            pl.BlockSpec((1, LANE), lambda b, h: (0, 0)),
        ],
        out_specs=pl.BlockSpec((seq, wd), lambda b, h: (b, h)),
        out_shape=jax.ShapeDtypeStruct((t, GDN_QK), BF16),
        scratch_shapes=[
            pltpu.VMEM((hb, seq, LANE), F32),
            pltpu.VMEM((hb, seq, LANE), mix_dt),
            pltpu.VMEM((hb, seq, LANE), mix_dt),
            pltpu.VMEM((hb, seq, LANE), mix_dt),
            pltpu.VMEM((hb, seq, CHUNK), mix_dt),
            pltpu.VMEM((hb, seq // CHUNK * 8, LANE), F32),
        ],
        compiler_params=pltpu.CompilerParams(
            dimension_semantics=("parallel", "arbitrary"), vmem_limit_bytes=VMEM_LIMIT),
        name="gdn",
    )(gpar, proj, proj, proj, proj, proj, norm_w.reshape(1, LANE))


def _ssd_kernel(z_ref, x_ref, b_ref, c_ref, dt_ref,
                alog_ref, dtb_ref, dsk_ref, nw_ref,
                o_ref,
                s_ref, *, passes, unroll):
    grp = pl.program_id(1)
    seq = x_ref.shape[0]
    n_chunks = seq // CHUNK
    c = CHUNK
    gw = SSD_GW

    row = _iota((c, c), 0)
    col = _iota((c, c), 1)
    ltri = (row >= col).astype(BF16)
    ones = jnp.ones((c, c), BF16)
    rowt = _iota((c, gw), 0)
    colt = jnp.bitwise_and(_iota((c, gw), 1), c - 1)
    tril_t = rowt >= colt
    upper_t = rowt <= colt
    expand = (_iota((LANE, gw), 0) == grp * SSD_HPG + lax.shift_right_logical(_iota((LANE, gw), 1), 6)).astype(BF16)
    last_row = (_iota((8, gw), 0) == 7)
    ones_8n = jnp.ones((8, SSD_N), BF16)

    neg_a = -jnp.exp(alog_ref[...])
    dt_bias = dtb_ref[...]
    d_skip = dsk_ref[...]
    norm_w = nw_ref[...]

    s_ref[...] = jnp.zeros((gw, SSD_N), F32)
    heads = [slice(hh * SSD_P, (hh + 1) * SSD_P) for hh in range(SSD_HPG)]

    def body(trip, carry):
        cis = [trip * unroll + cc for cc in range(unroll)]
        rows = [pl.ds(pl.multiple_of(ci * c, c), c) for ci in cis]
        xs = [x_ref[rw, :] for rw in rows]
        bm = [b_ref[rw, :] for rw in rows]
        cm = [c_ref[rw, :] for rw in rows]
        dt = [_softplus(_mm01_t(dt_ref[rw, :], expand) + dt_bias) for rw in rows]
        a = [d * neg_a for d in dt]
        acum = [_mm01(ltri, x) for x in a]
        acum_j = [_mm01(ones, jnp.where(upper_t, x, 0.0)) for x in a]
        cb = [_mm(x, y, NT, passes) for x, y in zip(cm, bm)]
        lm = [jnp.where(tril_t, jnp.exp(x - y), 0.0) for x, y in zip(acum, acum_j)]
        xdt = [x * d for x, d in zip(xs, dt)]
        y_diag = [jnp.concatenate([_mm(cb[i] * lm[i][:, sl], xdt[i][:, sl], NN, passes)
                                   for sl in heads], axis=1) for i in range(unroll)]
        a_last = [jnp.broadcast_to(x[c - 1:c, :], (c, gw)) for x in acum]
        states = [_mm(xdt[i] * jnp.exp(a_last[i] - acum[i]), bm[i], TN, passes)
                  for i in range(unroll)]
        dec_col = [_mm01_t(jnp.where(last_row, jnp.exp(x[c - 8:, :]), 0.0), ones_8n, TN)
                   for x in acum]
        s_prev = s_ref[...]
        for i in range(unroll):
            y_off = _mm(cm[i], s_prev, NT, passes) * jnp.exp(acum[i])
            s_prev = s_prev * dec_col[i] + states[i]
            y = y_diag[i] + y_off + d_skip * xs[i]
            y = y * _silu(z_ref[rows[i], :])
            ms = jnp.mean(y * y, axis=-1, keepdims=True)
            o_ref[rows[i], :] = (y * lax.rsqrt(ms + RMS_EPS) * norm_w).astype(o_ref.dtype)
        s_ref[...] = s_prev
        return carry

    lax.fori_loop(0, n_chunks // unroll, body, 0)


def _ssd(proj, bsz, seq, a_log, dt_bias, d_skip, norm_w, *, passes=1, unroll=4):
    t = proj.shape[0]
    gw = SSD_GW
    rep = lambda v: jnp.repeat(v.astype(F32), SSD_P).reshape(1, SSD_INNER)
    wide = lambda off: pl.BlockSpec((seq, gw), lambda b, g, off=off: (b, off + g))
    lane = lambda off: pl.BlockSpec((seq, LANE), lambda b, g, off=off: (b, off + g))
    chan = pl.BlockSpec((1, gw), lambda b, g: (0, g))
    return pl.pallas_call(
        functools.partial(_ssd_kernel, passes=passes, unroll=unroll),
        grid=(bsz, SSD_GROUPS),
        in_specs=[
            wide(COL_SSD_Z // gw), wide(COL_SSD_X // gw),
            lane(COL_SSD_B // LANE), lane(COL_SSD_C // LANE),
            pl.BlockSpec((seq, LANE), lambda b, g: (b, COL_SSD_DT // LANE)),
            chan, chan, chan, chan,
        ],
        out_specs=pl.BlockSpec((seq, gw), lambda b, g: (b, g)),
        out_shape=jax.ShapeDtypeStruct((t, SSD_INNER), BF16),
        scratch_shapes=[pltpu.VMEM((gw, SSD_N), F32)],
        compiler_params=pltpu.CompilerParams(
            dimension_semantics=("parallel", "arbitrary"), vmem_limit_bytes=VMEM_LIMIT),
        name="ssd",
    )(proj, proj, proj, proj, proj,
      rep(a_log), rep(dt_bias), rep(d_skip), norm_w.reshape(1, SSD_INNER))


def _s5_kernel(u_ref, wb_ref, wc_ref, are_ref, aim_ref, dsk_ref, gw_ref, gb_ref,
               o_ref, bu_ref, h_ref, *, bsz):
    ts = u_ref.shape[0] // bsz
    nt = S5_CH // LANE
    nb = S5_WIDTH // LANE
    per = nt // nb
    half = per * LANE

    @pl.when(pl.program_id(0) == 0)
    def _():
        h_ref[...] = jnp.zeros(h_ref.shape, F32)

    u = u_ref[...]
    u_bf = u.astype(BF16)
    for m in range(nb):
        res = _dot(u_bf[:, m * LANE:(m + 1) * LANE], wb_ref[m])
        for k in range(per):
            bu_ref[m * per + k] = res[:, k * LANE:(k + 1) * LANE]
            bu_ref[nt + m * per + k] = res[:, half + k * LANE:half + (k + 1) * LANE]

    def step(t, carry):
        rows = pl.ds(pl.multiple_of(t * bsz, bsz), bsz)
        new_re, new_im = [], []
        for k in range(nt):
            h_re, h_im = carry[k], carry[nt + k]
            a_re = are_ref[:, k * LANE:(k + 1) * LANE]
            a_im = aim_ref[:, k * LANE:(k + 1) * LANE]
            n_re = a_re * h_re - a_im * h_im + bu_ref[k, rows, :]
            n_im = a_re * h_im + a_im * h_re + bu_ref[nt + k, rows, :]
            bu_ref[k, rows, :] = n_re
            bu_ref[nt + k, rows, :] = n_im
            new_re.append(n_re)
            new_im.append(n_im)
        return tuple(new_re + new_im)

    h_last = lax.fori_loop(0, ts, step, tuple(h_ref[k] for k in range(2 * nt)))
    for k in range(2 * nt):
        h_ref[k] = h_last[k]

    y_parts = []
    for m in range(nb):
        hs = jnp.concatenate([bu_ref[m * per + k] for k in range(per)] +
                             [bu_ref[nt + m * per + k] for k in range(per)], axis=1)
        y_parts.append(_dot(hs.astype(BF16), wc_ref[m]))
    y = jnp.concatenate(y_parts, axis=1) + dsk_ref[...] * u
    g = 0.5 * y * (1.0 + jnp.tanh(math.sqrt(2.0 / math.pi) * (y + 0.044715 * (y * y * y))))
    out = g * _sigmoid(_dot(g.astype(BF16), gw_ref[...]) + gb_ref[...])
    o_ref[...] = out.astype(o_ref.dtype)


def _s5(u_tb, bsz, wb, wc, a_re, a_im, d_skip, glu_w, glu_b, *, ts=128):
    rows = u_tb.shape[0]
    blk = ts * bsz
    fixed = lambda shape: pl.BlockSpec(shape, lambda k: (0,) * len(shape))
    return pl.pallas_call(
        functools.partial(_s5_kernel, bsz=bsz),
        grid=(rows // blk,),
        in_specs=[
            pl.BlockSpec((blk, S5_WIDTH), lambda k: (k, 0)),
            fixed(wb.shape), fixed(wc.shape),
            fixed((1, S5_CH)), fixed((1, S5_CH)),
            fixed((1, S5_WIDTH)), fixed(glu_w.shape), fixed((1, S5_WIDTH)),
        ],
        out_specs=pl.BlockSpec((blk, S5_WIDTH), lambda k: (k, 0)),
        out_shape=jax.ShapeDtypeStruct((rows, S5_WIDTH), BF16),
        scratch_shapes=[
            pltpu.VMEM((2 * S5_CH // LANE, blk, LANE), F32),
            pltpu.VMEM((2 * S5_CH // LANE, bsz, LANE), F32),
        ],
        compiler_params=pltpu.CompilerParams(
            dimension_semantics=("arbitrary",), vmem_limit_bytes=VMEM_LIMIT),
        name="s5",
    )(u_tb, wb, wc, a_re.reshape(1, S5_CH), a_im.reshape(1, S5_CH),
      d_skip.reshape(1, S5_WIDTH), glu_w, glu_b.reshape(1, S5_WIDTH))


def _s5_params(a_re, a_im, b_re, b_im, c_re, c_im, log_dt):
    delta = jnp.exp(log_dt)[:, None]
    mag = jnp.exp(a_re * delta)
    ab_re, ab_im = mag * jnp.cos(a_im * delta), mag * jnp.sin(a_im * delta)
    den = a_re * a_re + a_im * a_im
    p_re, p_im = ab_re - 1.0, ab_im
    f_re = (p_re * a_re + p_im * a_im) / den
    f_im = (p_im * a_re - p_re * a_im) / den
    bb_re = f_re[..., None] * b_re - f_im[..., None] * b_im
    bb_im = f_re[..., None] * b_im + f_im[..., None] * b_re
    gpb = LANE // S5_GROUP
    nb = S5_GROUPS // gpb
    eye = jnp.eye(gpb, dtype=F32)
    emb_b = lambda bb: jnp.einsum('mgnj,gh->mgjhn', bb.reshape(nb, gpb, S5_STATE, S5_GROUP),
                                  eye).reshape(nb, LANE, gpb * S5_STATE)
    wb = jnp.concatenate([emb_b(bb_re), emb_b(bb_im)], axis=2)
    emb_c = lambda cc: jnp.einsum('mgin,gh->mgnhi', cc.reshape(nb, gpb, S5_GROUP, S5_STATE),
                                  eye).reshape(nb, gpb * S5_STATE, LANE)
    wc = jnp.concatenate([emb_c(c_re), -emb_c(c_im)], axis=1)
    return ab_re.reshape(-1), ab_im.reshape(-1), wb.astype(BF16), wc.astype(BF16)


def _w_in_segments():
    bc = SSD_GROUPS * SSD_N
    g_z = GDN_QKV
    g_gate = g_z + GDN_QK
    s_z = g_gate + 2 * GDN_HEADS
    s_x = s_z + SSD_INNER
    s_b = s_x + SSD_INNER
    s_c = s_b + bc
    s_dt = s_c + bc
    s5_u = s_dt + SSD_HEADS
    return ((0, GDN_QKV, COL_GDN_QKV), (g_z, GDN_QK, COL_GDN_Z),
            (g_gate, 2 * GDN_HEADS, COL_GDN_GATE), (s_z, SSD_INNER, COL_SSD_Z),
            (s_x, SSD_INNER, COL_SSD_X), (s_b, bc, COL_SSD_B), (s_c, bc, COL_SSD_C),
            (s_dt, SSD_HEADS, COL_SSD_DT), (s5_u, S5_WIDTH, COL_S5_U))


def _pack_conv(gdn_w, ssd_w, ssd_b):
    bc = SSD_GROUPS * SSD_N
    pieces = ((COL_GDN_QKV, gdn_w, None),
              (COL_SSD_X, ssd_w[:, :SSD_INNER], ssd_b[:SSD_INNER]),
              (COL_SSD_B, ssd_w[:, SSD_INNER:SSD_INNER + bc], ssd_b[SSD_INNER:SSD_INNER + bc]),
              (COL_SSD_C, ssd_w[:, SSD_INNER + bc:], ssd_b[SSD_INNER + bc:]))
    cw = jnp.zeros((CONV_K, P_PAD), F32)
    cb = jnp.zeros((1, P_PAD), F32)
    for col, w, b in pieces:
        cw = cw.at[:, col:col + w.shape[1]].set(w.astype(F32))
        if b is not None:
            cb = cb.at[0, col:col + b.shape[0]].set(b.astype(F32))
    return cw, cb


def _pack_kernel(x_ref, o_ref):
    for col in (COL_SSD_DT, COL_GDN_GATE):
        o_ref[:, col:col + LANE] = jnp.zeros((o_ref.shape[0], LANE), o_ref.dtype)
    for src, width, dst in _w_in_segments():
        o_ref[:, dst:dst + width] = x_ref[:, src:src + width].astype(o_ref.dtype)


def _pack_w_in(w_in, *, rows=256):
    nl, d, c = w_in.shape
    return pl.pallas_call(
        _pack_kernel,
        grid=(nl, d // rows),
        in_specs=[pl.BlockSpec((None, rows, c), lambda l, i: (l, i, 0))],
        out_specs=pl.BlockSpec((None, rows, P_PAD), lambda l, i: (l, i, 0)),
        out_shape=jax.ShapeDtypeStruct((nl, d, P_PAD), BF16),
        compiler_params=pltpu.CompilerParams(
            dimension_semantics=("parallel", "parallel"), vmem_limit_bytes=VMEM_LIMIT),
        name="pack_w_in",
    )(w_in)


def kernel(x, ffn1_norm, ffn1_w_gate, ffn1_w_up, ffn1_w_down, mix_norm, w_in,
           gdn_conv_w, gdn_a_log, gdn_dt_bias, gdn_norm,
           ssd_conv_w, ssd_conv_b, ssd_a_log, ssd_dt_bias, ssd_d, ssd_norm,
           s5_a_re, s5_a_im, s5_b_re, s5_b_im, s5_c_re, s5_c_im, s5_d, s5_log_dt,
           s5_glu_w, s5_glu_b, w_out, ffn2_norm, ffn2_w_gate, ffn2_w_up, ffn2_w_down,
           final_norm):
    bsz, seq, d = x.shape
    depth = w_in.shape[0]
    h = x
    ffn1_w = [_to_bf16(w) for w in (ffn1_w_gate, ffn1_w_up, ffn1_w_down)]
    ffn2_w = [_to_bf16(w) for w in (ffn2_w_gate, ffn2_w_up, ffn2_w_down)]
    w_in_packed = _pack_w_in(w_in)
    for i in range(depth):
        h = _ffn(h, ffn1_norm[i], *ffn1_w, final_norm, i, final_norm=False)
        conv_w, conv_b = _pack_conv(gdn_conv_w[i], ssd_conv_w[i], ssd_conv_b[i])
        proj, u_s5 = _inproj(h, mix_norm[i], w_in_packed, i, conv_w, conv_b, bsz, seq)
        o_gdn = _gdn(proj, bsz, seq, gdn_a_log[i], gdn_dt_bias[i], gdn_norm[i])
        o_ssd = _ssd(proj, bsz, seq, ssd_a_log[i], ssd_dt_bias[i], ssd_d[i], ssd_norm[i])
        ab_re, ab_im, wb, wc = _s5_params(s5_a_re[i], s5_a_im[i], s5_b_re[i], s5_b_im[i],
                                          s5_c_re[i], s5_c_im[i], s5_log_dt[i])
        o_s5 = _s5(u_s5.reshape(seq * bsz, S5_WIDTH), bsz, wb, wc, ab_re, ab_im, s5_d[i],
                   s5_glu_w[i].astype(BF16), s5_glu_b[i])
        wo = w_out[i].astype(BF16)
        h = _outproj(h, o_gdn, o_ssd, o_s5.reshape(seq, bsz * S5_WIDTH), wo[0:GDN_QK],
                     wo[GDN_QK:GDN_QK + SSD_INNER], wo[GDN_QK + SSD_INNER:])
        last = i == depth - 1
        h = _ffn(h, ffn2_norm[i], *ffn2_w, final_norm, i, final_norm=last,
                 out_batched=(bsz, seq) if last else None)
    return h
```

```python
import functools
import math

import jax
import jax.numpy as jnp
from jax import lax
from jax.experimental import pallas as pl
from jax.experimental.pallas import tpu as pltpu

F32 = jnp.float32
BF16 = jnp.bfloat16

RMS_EPS = 1e-6
CHUNK = 64
CONV_K = 4
D_MODEL = 2048
D_FF = 5632

GDN_HEADS = 6
GDN_D = 128
GDN_QK = GDN_HEADS * GDN_D
GDN_QKV = 3 * GDN_QK

SSD_HEADS = 12
SSD_P = 64
SSD_GROUPS = 2
SSD_N = 128
SSD_HPG = SSD_HEADS // SSD_GROUPS
SSD_GW = SSD_HPG * SSD_P
SSD_INNER = SSD_HEADS * SSD_P

S5_WIDTH = 512
S5_GROUP = 16
S5_GROUPS = 32
S5_STATE = 64
S5_CH = S5_GROUPS * S5_STATE

LANE = 128

COL_GDN_QKV = 0
COL_GDN_Z = 2304
COL_SSD_Z = 3072
COL_SSD_X = 3840
COL_SSD_B = 4608
COL_SSD_C = 4864
COL_S5_U = 5120
COL_SSD_DT = 5632
COL_GDN_GATE = 5760
P_PAD = 5888

VMEM_LIMIT = 60 * 1024 * 1024

NN = (((1,), (0,)), ((), ()))
NT = (((1,), (1,)), ((), ()))
TN = (((0,), (0,)), ((), ()))


def _dot(a, b, dims=NN):
    return lax.dot_general(a, b, dims, preferred_element_type=F32)


def _hi_lo(a):
    hi = a.astype(BF16)
    lo = (a - hi.astype(F32)).astype(BF16)
    return hi, lo


def _mm(a, b, dims=NN, passes=3):
    if passes == 1:
        return _dot(a.astype(BF16), b.astype(BF16), dims)
    ah, al = _hi_lo(a)
    bh, bl = _hi_lo(b)
    return _dot(ah, bh, dims) + (_dot(ah, bl, dims) + _dot(al, bh, dims))


def _mm01(m01, x, dims=NN):
    hi, lo = _hi_lo(x)
    return _dot(m01, hi, dims) + _dot(m01, lo, dims)


def _mm01_t(x, m01, dims=NN):
    hi, lo = _hi_lo(x)
    return _dot(hi, m01, dims) + _dot(lo, m01, dims)


def _sigmoid(x):
    return 1.0 / (1.0 + jnp.exp(-x))


def _silu(x):
    return x * _sigmoid(x)


def _softplus(x):
    return jnp.maximum(x, 0.0) + jnp.log1p(jnp.exp(-jnp.abs(x)))


def _iota(shape, dim):
    return lax.broadcasted_iota(jnp.int32, shape, dim)


def _causal_conv(x, cw_ref):
    acc = None
    for i in range(CONV_K):
        shift = CONV_K - 1 - i
        xs = x if shift == 0 else pltpu.roll(x, shift, axis=0)
        term = xs[8:] * cw_ref[i:i + 1, :]
        acc = term if acc is None else acc + term
    return acc


def _ffn_kernel(x_ref, nw_ref, wg_ref, wu_ref, wd_ref, fw_ref, o_ref, xn_ref, *, final_norm):
    j = pl.program_id(1)

    @pl.when(j == 0)
    def _():
        x = x_ref[...]
        ms = jnp.mean(x * x, axis=-1, keepdims=True)
        xn_ref[...] = (x * lax.rsqrt(ms + RMS_EPS) * nw_ref[...]).astype(BF16)
        o_ref[...] = x

    xn = xn_ref[...]
    g = _dot(xn, wg_ref[...])
    u = _dot(xn, wu_ref[...])
    a = (0.5 * _silu(g) * u).astype(BF16)
    o_ref[...] += _dot(a, wd_ref[...])

    if final_norm:
        @pl.when(j == pl.num_programs(1) - 1)
        def _():
            h = o_ref[...]
            ms = jnp.mean(h * h, axis=-1, keepdims=True)
            o_ref[...] = h * lax.rsqrt(ms + RMS_EPS) * fw_ref[...]


def _cast_kernel(x_ref, o_ref):
    o_ref[...] = x_ref[...].astype(o_ref.dtype)


def _to_bf16(w, *, rows=256):
    nl, r, c = w.shape
    spec = pl.BlockSpec((None, rows, c), lambda l, i: (l, i, 0))
    return pl.pallas_call(
        _cast_kernel,
        grid=(nl, r // rows),
        in_specs=[spec],
        out_specs=spec,
        out_shape=jax.ShapeDtypeStruct(w.shape, BF16),
        compiler_params=pltpu.CompilerParams(
            dimension_semantics=("parallel", "parallel"), vmem_limit_bytes=VMEM_LIMIT),
        name="to_bf16",
    )(w)


def _ffn(x, nw, wg, wu, wd, fw, layer, *, final_norm, out_batched=None, tm=1024, tf=512):
    d = x.shape[-1]
    t = x.size // d
    f = wg.shape[2]
    if x.ndim == 3:
        nseg = x.shape[1] // tm
        x_spec = pl.BlockSpec((None, tm, d), lambda i, j: (i // nseg, i % nseg, 0))
    else:
        x_spec = pl.BlockSpec((tm, d), lambda i, j: (i, 0))
    if out_batched is None:
        out_spec = pl.BlockSpec((tm, d), lambda i, j: (i, 0))
        out_shape = jax.ShapeDtypeStruct((t, d), F32)
    else:
        oseg = out_batched[1] // tm
        out_spec = pl.BlockSpec((None, tm, d), lambda i, j: (i // oseg, i % oseg, 0))
        out_shape = jax.ShapeDtypeStruct((*out_batched, d), F32)
    return pl.pallas_call(
        functools.partial(_ffn_kernel, final_norm=final_norm),
        grid=(t // tm, f // tf),
        in_specs=[
            x_spec,
            pl.BlockSpec((1, d), lambda i, j: (0, 0)),
            pl.BlockSpec((None, d, tf), lambda i, j: (layer, 0, j)),
            pl.BlockSpec((None, d, tf), lambda i, j: (layer, 0, j)),
            pl.BlockSpec((None, tf, d), lambda i, j: (layer, j, 0)),
            pl.BlockSpec((1, d), lambda i, j: (0, 0)),
        ],
        out_specs=out_spec,
        out_shape=out_shape,
        scratch_shapes=[pltpu.VMEM((tm, d), BF16)],
        compiler_params=pltpu.CompilerParams(
            dimension_semantics=("parallel", "arbitrary"), vmem_limit_bytes=VMEM_LIMIT),
        name="ffn",
    )(x, nw.reshape(1, d), wg, wu, wd, fw.reshape(1, d))


CONV_RANGES = ((COL_SSD_X, SSD_INNER), (COL_SSD_B, SSD_GROUPS * SSD_N),
               (COL_SSD_C, SSD_GROUPS * SSD_N), (COL_GDN_QKV, GDN_QKV))
INPROJ_PIECE = 256


def _inproj_kernel(x_ref, nw_ref, w_ref, cw_ref, cb_ref, o_ref, u5_ref, stage_ref, *, nseg):
    tm = x_ref.shape[0]

    @pl.when(pl.program_id(0) % nseg == 0)
    def _():
        stage_ref[:, 0:8, :] = jnp.zeros((stage_ref.shape[0], 8, LANE), F32)

    x = x_ref[...]
    ms = jnp.mean(x * x, axis=-1, keepdims=True)
    xn = (x * lax.rsqrt(ms + RMS_EPS) * nw_ref[...]).astype(BF16)
    starts = list(range(0, o_ref.shape[1], INPROJ_PIECE))
    piece = lambda col: _dot(xn, w_ref[:, col:col + INPROJ_PIECE])
    nxt = piece(starts[0])
    for idx, col in enumerate(starts):
        cols = slice(col, col + INPROJ_PIECE)
        res, nxt = nxt, (piece(starts[idx + 1]) if idx + 1 < len(starts) else None)
        if any(lo <= col < lo + width for lo, width in CONV_RANGES):
            for lt in range(INPROJ_PIECE // LANE):
                ti = col // LANE + lt
                ln = slice(lt * LANE, (lt + 1) * LANE)
                gl = slice(col + lt * LANE, col + (lt + 1) * LANE)
                stage_ref[ti, 8:, :] = res[:, ln]
                acc = None
                for k in range(CONV_K):
                    lo = k + 8 - (CONV_K - 1)
                    term = stage_ref[ti, pl.ds(lo, tm, stride=1), :] * cw_ref[k:k + 1, gl]
                    acc = term if acc is None else acc + term
                stage_ref[ti, 0:8, :] = res[tm - 8:, ln]
                o_ref[:, gl] = _silu(acc + cb_ref[:, gl])
        else:
            o_ref[:, cols] = res
        if COL_S5_U <= col < COL_S5_U + S5_WIDTH:
            u5_ref[:, col - COL_S5_U:col - COL_S5_U + INPROJ_PIECE] = res


def _inproj(x, nw, w, layer, conv_w, conv_b, bsz, seq, *, tm=256):
    t, d = x.shape
    n = w.shape[2]
    nseg = seq // tm
    fixed = lambda i: (0, 0)
    return pl.pallas_call(
        functools.partial(_inproj_kernel, nseg=nseg),
        grid=(t // tm,),
        in_specs=[
            pl.BlockSpec((tm, d), lambda i: (i, 0)),
            pl.BlockSpec((1, d), fixed),
            pl.BlockSpec((None, d, n), lambda i: (layer, 0, 0), pipeline_mode=pl.Buffered(1)),
            pl.BlockSpec((CONV_K, n), fixed),
            pl.BlockSpec((1, n), fixed),
        ],
        out_specs=[
            pl.BlockSpec((tm, n), lambda i: (i, 0)),
            pl.BlockSpec((tm, S5_WIDTH), lambda i: (i % nseg, i // nseg)),
        ],
        out_shape=[jax.ShapeDtypeStruct((t, n), F32),
                   jax.ShapeDtypeStruct((seq, bsz * S5_WIDTH), F32)],
        scratch_shapes=[pltpu.VMEM((n // LANE, tm + 8, LANE), F32)],
        compiler_params=pltpu.CompilerParams(
            dimension_semantics=("arbitrary",), vmem_limit_bytes=VMEM_LIMIT),
        name="inproj",
    )(x, nw.reshape(1, d), w, conv_w, conv_b)


def _outproj_kernel(h_ref, a_ref, b_ref, c_ref, wa_ref, wb_ref, wc_ref, o_ref):
    acc = _dot(a_ref[...], wa_ref[...])
    acc += _dot(b_ref[...], wb_ref[...])
    acc += _dot(c_ref[...], wc_ref[...])
    o_ref[...] = h_ref[...] + acc


def _outproj(h, oa, ob, oc, wa, wb, wc, *, tm=512):
    t, d = h.shape
    nseg = oc.shape[0] // tm
    row = lambda i: (i, 0)
    fixed = lambda i: (0, 0)
    return pl.pallas_call(
        _outproj_kernel,
        grid=(t // tm,),
        in_specs=[
            pl.BlockSpec((tm, d), row),
            pl.BlockSpec((tm, oa.shape[1]), row),
            pl.BlockSpec((tm, ob.shape[1]), row),
            pl.BlockSpec((tm, S5_WIDTH), lambda i: (i % nseg, i // nseg)),
            pl.BlockSpec(wa.shape, fixed),
            pl.BlockSpec(wb.shape, fixed),
            pl.BlockSpec(wc.shape, fixed),
        ],
        out_specs=pl.BlockSpec((tm, d), row),
        out_shape=jax.ShapeDtypeStruct((t, d), F32),
        compiler_params=pltpu.CompilerParams(
            dimension_semantics=("parallel",), vmem_limit_bytes=VMEM_LIMIT),
        name="outproj",
    )(h, oa, ob, oc, wa, wb, wc)


def _gdn_kernel(gpar_ref,
                q_ref, k_ref, v_ref, z_ref, gate_ref,
                nw_ref,
                o_ref,
                u_ref, w_ref, qd_ref, kd_ref, at_ref, eg_ref,
                *, hb, p_inv, p_mix, unroll):
    hg = pl.program_id(1)
    seq = q_ref.shape[0]
    n_chunks = seq // CHUNK
    c = CHUNK
    mix_dt = BF16 if p_mix == 1 else F32

    row = _iota((c, c), 0)
    col = _iota((c, c), 1)
    tril = row >= col
    strict = row > col
    ltri = tril.astype(BF16)
    ones = jnp.ones((c, c), BF16)
    eye = (row == col).astype(F32)
    lane = _iota((c, LANE), 1)
    neg_a = -jnp.exp(gpar_ref[0:1, :])
    dt_bias = gpar_ref[1:2, :]

    def prep_stages(trip):
        cis = [trip * unroll + cc for cc in range(unroll)]
        items = [(cis[cc], hh) for cc in range(unroll) for hh in range(hb)]
        lanes = [slice(hh * LANE, (hh + 1) * LANE) for _, hh in items]
        rows = [pl.ds(pl.multiple_of(ci * c, c), c) for ci, _ in items]

        def chunk_gates(ci):
            gt = gate_ref[pl.ds(pl.multiple_of(ci * c, c), c), :]
            return _sigmoid(gt), neg_a * _softplus(gt + dt_bias)

        gates = [chunk_gates(ci) for ci in cis]

        def stage_a(cc, hh, ln):
            ci = cis[cc]
            h = hg * hb + hh
            rw = pl.ds(pl.multiple_of(ci * c, c), c)
            q, k, v = q_ref[rw, ln], k_ref[rw, ln], v_ref[rw, ln]
            qn = q * lax.rsqrt(jnp.sum(q * q, axis=-1, keepdims=True) + RMS_EPS) * (GDN_D ** -0.5)
            kn = k * lax.rsqrt(jnp.sum(k * k, axis=-1, keepdims=True) + RMS_EPS)
            beta_all, g_all = gates[cc]
            beta = jnp.sum(jnp.where(lane == h, beta_all, 0.0), axis=-1, keepdims=True)
            g = jnp.sum(jnp.where(lane == h + GDN_HEADS, g_all, 0.0), axis=-1, keepdims=True)
            return qn, kn, v, beta, jnp.broadcast_to(g, (c, LANE))

        sa = [stage_a(cc, hh, slice(hh * LANE, (hh + 1) * LANE))
              for cc in range(unroll) for hh in range(hb)]
        qn = [x[0] for x in sa]
        kn = [x[1] for x in sa]
        v = [x[2] for x in sa]
        beta = [x[3] for x in sa]
        g_b = [x[4] for x in sa]
        yield
        gi = [_mm01(ltri, g) for g in g_b]
        gj = [_mm01(ones, jnp.where(row <= col, g[:, :c], 0.0)) for g in g_b]
        yield
        kb = [k * b for k, b in zip(kn, beta)]
        kk = [_mm(a, b, NT, p_inv) for a, b in zip(kb, kn)]
        qk = [_mm(a, b, NT, p_mix) for a, b in zip(qn, kn)]
        decay = [jnp.where(tril, jnp.exp(a[:, :c] - b), 0.0) for a, b in zip(gi, gj)]
        npow = [-jnp.where(strict, a * d, 0.0) for a, d in zip(kk, decay)]
        tinv = [eye + n for n in npow]
        for _ in range(5):
            yield
            npow = [_mm(n, n, NN, p_inv) for n in npow]
            tinv = [t + _mm(t, n, NN, p_inv) for t, n in zip(tinv, npow)]
        yield
        exp_g = [jnp.exp(g) for g in gi]
        rhs = [jnp.concatenate([vv * b, k * e], axis=1) for vv, b, k, e in zip(v, beta, kb, exp_g)]
        uw = [_mm(t, r, NN, p_inv) for t, r in zip(tinv, rhs)]
        for i, ((ci, hh), rw) in enumerate(zip(items, rows)):
            g_last = jnp.broadcast_to(gi[i][c - 1:c, :], (c, LANE))
            u_ref[hh, rw, :] = uw[i][:, :LANE]
            w_ref[hh, rw, :] = uw[i][:, LANE:].astype(mix_dt)
            at_ref[hh, rw, :] = jnp.where(tril, qk[i] * decay[i], 0.0).astype(mix_dt)
            qd_ref[hh, rw, :] = (qn[i] * exp_g[i]).astype(mix_dt)
            kd_ref[hh, rw, :] = (kn[i] * jnp.exp(g_last - gi[i])).astype(mix_dt)
            eg_ref[hh, pl.ds(pl.multiple_of(ci * 8, 8), 8), :] = jnp.exp(g_last[0:8, :])

    norm_w = nw_ref[...]

    def scan_stages(trip, states, final):
        for cc in range(unroll):
            ci = trip * unroll + cc
            rw = pl.ds(pl.multiple_of(ci * c, c), c)
            lhs = [jnp.concatenate([w_ref[hh, rw, :], qd_ref[hh, rw, :]], axis=0)
                   for hh in range(hb)]
            ws_qs = [_mm(a, s, NN, p_mix) for a, s in zip(lhs, states)]
            yield
            v_new = [u_ref[hh, rw, :] - ws_qs[hh][:c] for hh in range(hb)]
            kv = [_mm(kd_ref[hh, rw, :], v_new[hh], TN, p_mix) for hh in range(hb)]
            av = [_mm(at_ref[hh, rw, :], v_new[hh], NN, p_mix) for hh in range(hb)]
            new_states = []
            for hh in range(hb):
                eg = eg_ref[hh, pl.ds(pl.multiple_of(ci * 8, 8), 8), :]
                new_states.append(states[hh] * jnp.broadcast_to(eg[0:1, :], (GDN_D, GDN_D)) + kv[hh])
                o = ws_qs[hh][c:] + av[hh]
                ms = jnp.mean(o * o, axis=-1, keepdims=True)
                ln = slice(hh * LANE, (hh + 1) * LANE)
                y = o * lax.rsqrt(ms + RMS_EPS) * norm_w * _silu(z_ref[rw, ln])
                o_ref[rw, ln] = y.astype(o_ref.dtype)
            states = new_states
            yield
        final.extend(states)

    def drain(gen):
        for _ in gen:
            pass

    def fused(trip, states):
        final = []
        scan_gen = scan_stages(trip - 1, states, final)
        for _ in prep_stages(trip):
            next(scan_gen, None)
        drain(scan_gen)
        return tuple(final)

    n_trips = n_chunks // unroll
    drain(prep_stages(0))
    states = lax.fori_loop(1, n_trips, fused,
                           tuple(jnp.zeros((GDN_D, GDN_D), F32) for _ in range(hb)))
    drain(scan_stages(n_trips - 1, states, []))


def _gdn(proj, bsz, seq, a_log, dt_bias, norm_w, *, hb=3, p_inv=1, p_mix=1, unroll=4):
    t = proj.shape[0]
    wd = hb * LANE
    mix_dt = BF16 if p_mix == 1 else F32
    q0 = COL_GDN_QKV // wd
    z0 = COL_GDN_Z // wd
    g0 = COL_GDN_GATE // LANE
    ng = GDN_HEADS // hb
    blk = lambda off: pl.BlockSpec((seq, wd), lambda b, h, off=off: (b, off + h))
    pad = lambda v: jnp.pad(v.astype(F32), (GDN_HEADS, LANE - 2 * GDN_HEADS))
    gpar = jnp.stack([pad(a_log), pad(dt_bias)])
    return pl.pallas_call(
        functools.partial(_gdn_kernel, hb=hb, p_inv=p_inv, p_mix=p_mix, unroll=unroll),
        grid=(bsz, ng),
        in_specs=[
            pl.BlockSpec((2, LANE), lambda b, h: (0, 0)),
            blk(q0), blk(q0 + ng), blk(q0 + 2 * ng), blk(z0),
            pl.BlockSpec((seq, LANE), lambda b, h: (b, g0)),
            pl.BlockSpec((1, LANE), lambda b, h: (0, 0)),
        ],
        out_specs=pl.BlockSpec((seq, wd), lambda b, h: (b, h)),
        out_shape=jax.ShapeDtypeStruct((t, GDN_QK), BF16),
        scratch_shapes=[
            pltpu.VMEM((hb, seq, LANE), F32),
            pltpu.VMEM((hb, seq, LANE), mix_dt),
            pltpu.VMEM((hb, seq, LANE), mix_dt),
            pltpu.VMEM((hb, seq, LANE), mix_dt),
            pltpu.VMEM((hb, seq, CHUNK), mix_dt),
            pltpu.VMEM((hb, seq // CHUNK * 8, LANE), F32),
        ],
        compiler_params=pltpu.CompilerParams(
            dimension_semantics=("parallel", "arbitrary"), vmem_limit_bytes=VMEM_LIMIT),
        name="gdn",
    )(gpar, proj, proj, proj, proj, proj, norm_w.reshape(1, LANE))


def _ssd_kernel(z_ref, x_ref, b_ref, c_ref, dt_ref,
                alog_ref, dtb_ref, dsk_ref, nw_ref,
                o_ref,
                s_ref, *, passes, unroll):
    grp = pl.program_id(1)
    seq = x_ref.shape[0]
    n_chunks = seq // CHUNK
    c = CHUNK
    gw = SSD_GW

    row = _iota((c, c), 0)
    col = _iota((c, c), 1)
    ltri = (row >= col).astype(BF16)
    ones = jnp.ones((c, c), BF16)
    rowt = _iota((c, gw), 0)
    colt = jnp.bitwise_and(_iota((c, gw), 1), c - 1)
    tril_t = rowt >= colt
    upper_t = rowt <= colt
    expand = (_iota((LANE, gw), 0) == grp * SSD_HPG + lax.shift_right_logical(_iota((LANE, gw), 1), 6)).astype(BF16)
    last_row = (_iota((8, gw), 0) == 7)
    ones_8n = jnp.ones((8, SSD_N), BF16)

    neg_a = -jnp.exp(alog_ref[...])
    dt_bias = dtb_ref[...]
    d_skip = dsk_ref[...]
    norm_w = nw_ref[...]

    s_ref[...] = jnp.zeros((gw, SSD_N), F32)
    heads = [slice(hh * SSD_P, (hh + 1) * SSD_P) for hh in range(SSD_HPG)]

    def body(trip, carry):
        cis = [trip * unroll + cc for cc in range(unroll)]
        rows = [pl.ds(pl.multiple_of(ci * c, c), c) for ci in cis]
        xs = [x_ref[rw, :] for rw in rows]
        bm = [b_ref[rw, :] for rw in rows]
        cm = [c_ref[rw, :] for rw in rows]
        dt = [_softplus(_mm01_t(dt_ref[rw, :], expand) + dt_bias) for rw in rows]
        a = [d * neg_a for d in dt]
        acum = [_mm01(ltri, x) for x in a]
        acum_j = [_mm01(ones, jnp.where(upper_t, x, 0.0)) for x in a]
        cb = [_mm(x, y, NT, passes) for x, y in zip(cm, bm)]
        lm = [jnp.where(tril_t, jnp.exp(x - y), 0.0) for x, y in zip(acum, acum_j)]
        xdt = [x * d for x, d in zip(xs, dt)]
        y_diag = [jnp.concatenate([_mm(cb[i] * lm[i][:, sl], xdt[i][:, sl], NN, passes)
                                   for sl in heads], axis=1) for i in range(unroll)]
        a_last = [jnp.broadcast_to(x[c - 1:c, :], (c, gw)) for x in acum]
        states = [_mm(xdt[i] * jnp.exp(a_last[i] - acum[i]), bm[i], TN, passes)
                  for i in range(unroll)]
        dec_col = [_mm01_t(jnp.where(last_row, jnp.exp(x[c - 8:, :]), 0.0), ones_8n, TN)
                   for x in acum]
        s_prev = s_ref[...]
        for i in range(unroll):
            y_off = _mm(cm[i], s_prev, NT, passes) * jnp.exp(acum[i])
            s_prev = s_prev * dec_col[i] + states[i]
            y = y_diag[i] + y_off + d_skip * xs[i]
            y = y * _silu(z_ref[rows[i], :])
            ms = jnp.mean(y * y, axis=-1, keepdims=True)
            o_ref[rows[i], :] = (y * lax.rsqrt(ms + RMS_EPS) * norm_w).astype(o_ref.dtype)
        s_ref[...] = s_prev
        return carry

    lax.fori_loop(0, n_chunks // unroll, body, 0)


def _ssd(proj, bsz, seq, a_log, dt_bias, d_skip, norm_w, *, passes=1, unroll=4):
    t = proj.shape[0]
    gw = SSD_GW
    rep = lambda v: jnp.repeat(v.astype(F32), SSD_P).reshape(1, SSD_INNER)
    wide = lambda off: pl.BlockSpec((seq, gw), lambda b, g, off=off: (b, off + g))
    lane = lambda off: pl.BlockSpec((seq, LANE), lambda b, g, off=off: (b, off + g))
    chan = pl.BlockSpec((1, gw), lambda b, g: (0, g))
    return pl.pallas_call(
        functools.partial(_ssd_kernel, passes=passes, unroll=unroll),
        grid=(bsz, SSD_GROUPS),
        in_specs=[
            wide(COL_SSD_Z // gw), wide(COL_SSD_X // gw),
            lane(COL_SSD_B // LANE), lane(COL_SSD_C // LANE),
            pl.BlockSpec((seq, LANE), lambda b, g: (b, COL_SSD_DT // LANE)),
            chan, chan, chan, chan,
        ],
        out_specs=pl.BlockSpec((seq, gw), lambda b, g: (b, g)),
        out_shape=jax.ShapeDtypeStruct((t, SSD_INNER), BF16),
        scratch_shapes=[pltpu.VMEM((gw, SSD_N), F32)],
        compiler_params=pltpu.CompilerParams(
            dimension_semantics=("parallel", "arbitrary"), vmem_limit_bytes=VMEM_LIMIT),
        name="ssd",
    )(proj, proj, proj, proj, proj,
      rep(a_log), rep(dt_bias), rep(d_skip), norm_w.reshape(1, SSD_INNER))


def _s5_kernel(u_ref, wb_ref, wc_ref, are_ref, aim_ref, dsk_ref, gw_ref, gb_ref,
               o_ref, bu_ref, h_ref, *, bsz):
    ts = u_ref.shape[0] // bsz
    nt = S5_CH // LANE
    nb = S5_WIDTH // LANE
    per = nt // nb
    half = per * LANE

    @pl.when(pl.program_id(0) == 0)
    def _():
        h_ref[...] = jnp.zeros(h_ref.shape, F32)

    u = u_ref[...]
    u_bf = u.astype(BF16)
    for m in range(nb):
        res = _dot(u_bf[:, m * LANE:(m + 1) * LANE], wb_ref[m])
        for k in range(per):
            bu_ref[m * per + k] = res[:, k * LANE:(k + 1) * LANE]
            bu_ref[nt + m * per + k] = res[:, half + k * LANE:half + (k + 1) * LANE]

    def step(t, carry):
        rows = pl.ds(pl.multiple_of(t * bsz, bsz), bsz)
        new_re, new_im = [], []
        for k in range(nt):
            h_re, h_im = carry[k], carry[nt + k]
            a_re = are_ref[:, k * LANE:(k + 1) * LANE]
            a_im = aim_ref[:, k * LANE:(k + 1) * LANE]
            n_re = a_re * h_re - a_im * h_im + bu_ref[k, rows, :]
            n_im = a_re * h_im + a_im * h_re + bu_ref[nt + k, rows, :]
            bu_ref[k, rows, :] = n_re
            bu_ref[nt + k, rows, :] = n_im
            new_re.append(n_re)
            new_im.append(n_im)
        return tuple(new_re + new_im)

    h_last = lax.fori_loop(0, ts, step, tuple(h_ref[k] for k in range(2 * nt)))
    for k in range(2 * nt):
        h_ref[k] = h_last[k]

    y_parts = []
    for m in range(nb):
        hs = jnp.concatenate([bu_ref[m * per + k] for k in range(per)] +
                             [bu_ref[nt + m * per + k] for k in range(per)], axis=1)
        y_parts.append(_dot(hs.astype(BF16), wc_ref[m]))
    y = jnp.concatenate(y_parts, axis=1) + dsk_ref[...] * u
    g = 0.5 * y * (1.0 + jnp.tanh(math.sqrt(2.0 / math.pi) * (y + 0.044715 * (y * y * y))))
    out = g * _sigmoid(_dot(g.astype(BF16), gw_ref[...]) + gb_ref[...])
    o_ref[...] = out.astype(o_ref.dtype)


def _s5(u_tb, bsz, wb, wc, a_re, a_im, d_skip, glu_w, glu_b, *, ts=128):
    rows = u_tb.shape[0]
    blk = ts * bsz
    fixed = lambda shape: pl.BlockSpec(shape, lambda k: (0,) * len(shape))
    return pl.pallas_call(
        functools.partial(_s5_kernel, bsz=bsz),
        grid=(rows // blk,),
        in_specs=[
            pl.BlockSpec((blk, S5_WIDTH), lambda k: (k, 0)),
            fixed(wb.shape), fixed(wc.shape),
            fixed((1, S5_CH)), fixed((1, S5_CH)),
            fixed((1, S5_WIDTH)), fixed(glu_w.shape), fixed((1, S5_WIDTH)),
        ],
        out_specs=pl.BlockSpec((blk, S5_WIDTH), lambda k: (k, 0)),
        out_shape=jax.ShapeDtypeStruct((rows, S5_WIDTH), BF16),
        scratch_shapes=[
            pltpu.VMEM((2 * S5_CH // LANE, blk, LANE), F32),
            pltpu.VMEM((2 * S5_CH // LANE, bsz, LANE), F32),
        ],
        compiler_params=pltpu.CompilerParams(
            dimension_semantics=("arbitrary",), vmem_limit_bytes=VMEM_LIMIT),
        name="s5",
    )(u_tb, wb, wc, a_re.reshape(1, S5_CH), a_im.reshape(1, S5_CH),
      d_skip.reshape(1, S5_WIDTH), glu_w, glu_b.reshape(1, S5_WIDTH))


def _s5_params(a_re, a_im, b_re, b_im, c_re, c_im, log_dt):
    delta = jnp.exp(log_dt)[:, None]
    mag = jnp.exp(a_re * delta)
    ab_re, ab_im = mag * jnp.cos(a_im * delta), mag * jnp.sin(a_im * delta)
    den = a_re * a_re + a_im * a_im
    p_re, p_im = ab_re - 1.0, ab_im
    f_re = (p_re * a_re + p_im * a_im) / den
    f_im = (p_im * a_re - p_re * a_im) / den
    bb_re = f_re[..., None] * b_re - f_im[..., None] * b_im
    bb_im = f_re[..., None] * b_im + f_im[..., None] * b_re
    gpb = LANE // S5_GROUP
    nb = S5_GROUPS // gpb
    eye = jnp.eye(gpb, dtype=F32)
    emb_b = lambda bb: jnp.einsum('mgnj,gh->mgjhn', bb.reshape(nb, gpb, S5_STATE, S5_GROUP),
                                  eye).reshape(nb, LANE, gpb * S5_STATE)
    wb = jnp.concatenate([emb_b(bb_re), emb_b(bb_im)], axis=2)
    emb_c = lambda cc: jnp.einsum('mgin,gh->mgnhi', cc.reshape(nb, gpb, S5_GROUP, S5_STATE),
                                  eye).reshape(nb, gpb * S5_STATE, LANE)
    wc = jnp.concatenate([emb_c(c_re), -emb_c(c_im)], axis=1)
    return ab_re.reshape(-1), ab_im.reshape(-1), wb.astype(BF16), wc.astype(BF16)


def _w_in_segments():
    bc = SSD_GROUPS * SSD_N
    g_z = GDN_QKV
    g_gate = g_z + GDN_QK
    s_z = g_gate + 2 * GDN_HEADS
    s_x = s_z + SSD_INNER
    s_b = s_x + SSD_INNER
    s_c = s_b + bc
    s_dt = s_c + bc
    s5_u = s_dt + SSD_HEADS
    return ((0, GDN_QKV, COL_GDN_QKV), (g_z, GDN_QK, COL_GDN_Z),
            (g_gate, 2 * GDN_HEADS, COL_GDN_GATE), (s_z, SSD_INNER, COL_SSD_Z),
            (s_x, SSD_INNER, COL_SSD_X), (s_b, bc, COL_SSD_B), (s_c, bc, COL_SSD_C),
            (s_dt, SSD_HEADS, COL_SSD_DT), (s5_u, S5_WIDTH, COL_S5_U))


def _pack_conv(gdn_w, ssd_w, ssd_b):
    bc = SSD_GROUPS * SSD_N
    pieces = ((COL_GDN_QKV, gdn_w, None),
              (COL_SSD_X, ssd_w[:, :SSD_INNER], ssd_b[:SSD_INNER]),
              (COL_SSD_B, ssd_w[:, SSD_INNER:SSD_INNER + bc], ssd_b[SSD_INNER:SSD_INNER + bc]),
              (COL_SSD_C, ssd_w[:, SSD_INNER + bc:], ssd_b[SSD_INNER + bc:]))
    cw = jnp.zeros((CONV_K, P_PAD), F32)
    cb = jnp.zeros((1, P_PAD), F32)
    for col, w, b in pieces:
        cw = cw.at[:, col:col + w.shape[1]].set(w.astype(F32))
        if b is not None:
            cb = cb.at[0, col:col + b.shape[0]].set(b.astype(F32))
    return cw, cb


def _pack_kernel(x_ref, o_ref):
    for col in (COL_SSD_DT, COL_GDN_GATE):
        o_ref[:, col:col + LANE] = jnp.zeros((o_ref.shape[0], LANE), o_ref.dtype)
    for src, width, dst in _w_in_segments():
        o_ref[:, dst:dst + width] = x_ref[:, src:src + width].astype(o_ref.dtype)


def _pack_w_in(w_in, *, rows=256):
    nl, d, c = w_in.shape
    return pl.pallas_call(
        _pack_kernel,
        grid=(nl, d // rows),
        in_specs=[pl.BlockSpec((None, rows, c), lambda l, i: (l, i, 0))],
        out_specs=pl.BlockSpec((None, rows, P_PAD), lambda l, i: (l, i, 0)),
        out_shape=jax.ShapeDtypeStruct((nl, d, P_PAD), BF16),
        compiler_params=pltpu.CompilerParams(
            dimension_semantics=("parallel", "parallel"), vmem_limit_bytes=VMEM_LIMIT),
        name="pack_w_in",
    )(w_in)


def kernel(x, ffn1_norm, ffn1_w_gate, ffn1_w_up, ffn1_w_down, mix_norm, w_in,
           gdn_conv_w, gdn_a_log, gdn_dt_bias, gdn_norm,
           ssd_conv_w, ssd_conv_b, ssd_a_log, ssd_dt_bias, ssd_d, ssd_norm,
           s5_a_re, s5_a_im, s5_b_re, s5_b_im, s5_c_re, s5_c_im, s5_d, s5_log_dt,
           s5_glu_w, s5_glu_b, w_out, ffn2_norm, ffn2_w_gate, ffn2_w_up, ffn2_w_down,
           final_norm):
    bsz, seq, d = x.shape
    depth = w_in.shape[0]
    h = x
    ffn1_w = [_to_bf16(w) for w in (ffn1_w_gate, ffn1_w_up, ffn1_w_down)]
    ffn2_w = [_to_bf16(w) for w in (ffn2_w_gate, ffn2_w_up, ffn2_w_down)]
    w_in_packed = _pack_w_in(w_in)
    for i in range(depth):
        h = _ffn(h, ffn1_norm[i], *ffn1_w, final_norm, i, final_norm=False)
        conv_w, conv_b = _pack_conv(gdn_conv_w[i], ssd_conv_w[i], ssd_conv_b[i])
        proj, u_s5 = _inproj(h, mix_norm[i], w_in_packed, i, conv_w, conv_b, bsz, seq)
        o_gdn = _gdn(proj, bsz, seq, gdn_a_log[i], gdn_dt_bias[i], gdn_norm[i])
        o_ssd = _ssd(proj, bsz, seq, ssd_a_log[i], ssd_dt_bias[i], ssd_d[i], ssd_norm[i])
        ab_re, ab_im, wb, wc = _s5_params(s5_a_re[i], s5_a_im[i], s5_b_re[i], s5_b_im[i],
                                          s5_c_re[i], s5_c_im[i], s5_log_dt[i])
        o_s5 = _s5(u_s5.reshape(seq * bsz, S5_WIDTH), bsz, wb, wc, ab_re, ab_im, s5_d[i],
                   s5_glu_w[i].astype(BF16), s5_glu_b[i])
        wo = w_out[i].astype(BF16)
        h = _outproj(h, o_gdn, o_ssd, o_s5.reshape(seq, bsz * S5_WIDTH), wo[0:GDN_QK],
                     wo[GDN_QK:GDN_QK + SSD_INNER], wo[GDN_QK + SSD_INNER:])
        last = i == depth - 1
        h = _ffn(h, ffn2_norm[i], *ffn2_w, final_norm, i, final_norm=last,
                 out_batched=(bsz, seq) if last else None)
    return h
```

```python
import functools
import math

import jax
import jax.numpy as jnp
from jax import lax
from jax.experimental import pallas as pl
from jax.experimental.pallas import tpu as pltpu

F32 = jnp.float32
BF16 = jnp.bfloat16

RMS_EPS = 1e-6
CHUNK = 64
CONV_K = 4
D_MODEL = 2048
D_FF = 5632

GDN_HEADS = 6
GDN_D = 128
GDN_QK = GDN_HEADS * GDN_D
GDN_QKV = 3 * GDN_QK

SSD_HEADS = 12
SSD_P = 64
SSD_GROUPS = 2
SSD_N = 128
SSD_HPG = SSD_HEADS // SSD_GROUPS
SSD_GW = SSD_HPG * SSD_P
SSD_INNER = SSD_HEADS * SSD_P

S5_WIDTH = 512
S5_GROUP = 16
S5_GROUPS = 32
S5_STATE = 64
S5_CH = S5_GROUPS * S5_STATE

LANE = 128

COL_GDN_QKV = 0
COL_GDN_Z = 2304
COL_SSD_Z = 3072
COL_SSD_X = 3840
COL_SSD_B = 4608
COL_SSD_C = 4864
COL_S5_U = 5120
COL_SSD_DT = 5632
COL_GDN_GATE = 5760
P_PAD = 5888

VMEM_LIMIT = 60 * 1024 * 1024

NN = (((1,), (0,)), ((), ()))
NT = (((1,), (1,)), ((), ()))
TN = (((0,), (0,)), ((), ()))


def _dot(a, b, dims=NN):
    return lax.dot_general(a, b, dims, preferred_element_type=F32)


def _hi_lo(a):
    hi = a.astype(BF16)
    lo = (a - hi.astype(F32)).astype(BF16)
    return hi, lo


def _mm(a, b, dims=NN, passes=3):
    if passes == 1:
        return _dot(a.astype(BF16), b.astype(BF16), dims)
    ah, al = _hi_lo(a)
    bh, bl = _hi_lo(b)
    return _dot(ah, bh, dims) + (_dot(ah, bl, dims) + _dot(al, bh, dims))


def _mm01(m01, x, dims=NN):
    hi, lo = _hi_lo(x)
    return _dot(m01, hi, dims) + _dot(m01, lo, dims)


def _mm01_t(x, m01, dims=NN):
    hi, lo = _hi_lo(x)
    return _dot(hi, m01, dims) + _dot(lo, m01, dims)


def _sigmoid(x):
    return 1.0 / (1.0 + jnp.exp(-x))


def _silu(x):
    return x * _sigmoid(x)


def _softplus(x):
    return jnp.maximum(x, 0.0) + jnp.log1p(jnp.exp(-jnp.abs(x)))


def _iota(shape, dim):
    return lax.broadcasted_iota(jnp.int32, shape, dim)


def _causal_conv(x, cw_ref):
    acc = None
    for i in range(CONV_K):
        shift = CONV_K - 1 - i
        xs = x if shift == 0 else pltpu.roll(x, shift, axis=0)
        term = xs[8:] * cw_ref[i:i + 1, :]
        acc = term if acc is None else acc + term
    return acc


def _ffn_kernel(x_ref, nw_ref, wg_ref, wu_ref, wd_ref, fw_ref, o_ref, xn_ref, *, final_norm):
    j = pl.program_id(1)

    @pl.when(j == 0)
    def _():
        x = x_ref[...]
        ms = jnp.mean(x * x, axis=-1, keepdims=True)
        xn_ref[...] = (x * lax.rsqrt(ms + RMS_EPS) * nw_ref[...]).astype(BF16)
        o_ref[...] = x

    xn = xn_ref[...]
    g = _dot(xn, wg_ref[...])
    u = _dot(xn, wu_ref[...])
    a = (0.5 * _silu(g) * u).astype(BF16)
    o_ref[...] += _dot(a, wd_ref[...])

    if final_norm:
        @pl.when(j == pl.num_programs(1) - 1)
        def _():
            h = o_ref[...]
            ms = jnp.mean(h * h, axis=-1, keepdims=True)
            o_ref[...] = h * lax.rsqrt(ms + RMS_EPS) * fw_ref[...]


def _cast_kernel(x_ref, o_ref):
    o_ref[...] = x_ref[...].astype(o_ref.dtype)


def _to_bf16(w, *, rows=256):
    nl, r, c = w.shape
    spec = pl.BlockSpec((None, rows, c), lambda l, i: (l, i, 0))
    return pl.pallas_call(
        _cast_kernel,
        grid=(nl, r // rows),
        in_specs=[spec],
        out_specs=spec,
        out_shape=jax.ShapeDtypeStruct(w.shape, BF16),
        compiler_params=pltpu.CompilerParams(
            dimension_semantics=("parallel", "parallel"), vmem_limit_bytes=VMEM_LIMIT),
        name="to_bf16",
    )(w)


def _ffn(x, nw, wg, wu, wd, fw, layer, *, final_norm, out_batched=None, tm=1024, tf=512):
    d = x.shape[-1]
    t = x.size // d
    f = wg.shape[2]
    if x.ndim == 3:
        nseg = x.shape[1] // tm
        x_spec = pl.BlockSpec((None, tm, d), lambda i, j: (i // nseg, i % nseg, 0))
    else:
        x_spec = pl.BlockSpec((tm, d), lambda i, j: (i, 0))
    if out_batched is None:
        out_spec = pl.BlockSpec((tm, d), lambda i, j: (i, 0))
        out_shape = jax.ShapeDtypeStruct((t, d), F32)
    else:
        oseg = out_batched[1] // tm
        out_spec = pl.BlockSpec((None, tm, d), lambda i, j: (i // oseg, i % oseg, 0))
        out_shape = jax.ShapeDtypeStruct((*out_batched, d), F32)
    return pl.pallas_call(
        functools.partial(_ffn_kernel, final_norm=final_norm),
        grid=(t // tm, f // tf),
        in_specs=[
            x_spec,
            pl.BlockSpec((1, d), lambda i, j: (0, 0)),
            pl.BlockSpec((None, d, tf), lambda i, j: (layer, 0, j)),
            pl.BlockSpec((None, d, tf), lambda i, j: (layer, 0, j)),
            pl.BlockSpec((None, tf, d), lambda i, j: (layer, j, 0)),
            pl.BlockSpec((1, d), lambda i, j: (0, 0)),
        ],
        out_specs=out_spec,
        out_shape=out_shape,
        scratch_shapes=[pltpu.VMEM((tm, d), BF16)],
        compiler_params=pltpu.CompilerParams(
            dimension_semantics=("parallel", "arbitrary"), vmem_limit_bytes=VMEM_LIMIT),
        name="ffn",
    )(x, nw.reshape(1, d), wg, wu, wd, fw.reshape(1, d))


CONV_RANGES = ((COL_SSD_X, SSD_INNER), (COL_SSD_B, SSD_GROUPS * SSD_N),
               (COL_SSD_C, SSD_GROUPS * SSD_N), (COL_GDN_QKV, GDN_QKV))
INPROJ_PIECE = 256


def _inproj_kernel(x_ref, nw_ref, w_ref, cw_ref, cb_ref, o_ref, u5_ref, stage_ref, *, nseg):
    tm = x_ref.shape[0]

    @pl.when(pl.program_id(0) % nseg == 0)
    def _():
        stage_ref[:, 0:8, :] = jnp.zeros((stage_ref.shape[0], 8, LANE), F32)

    x = x_ref[...]
    ms = jnp.mean(x * x, axis=-1, keepdims=True)
    xn = (x * lax.rsqrt(ms + RMS_EPS) * nw_ref[...]).astype(BF16)
    starts = list(range(0, o_ref.shape[1], INPROJ_PIECE))
    piece = lambda col: _dot(xn, w_ref[:, col:col + INPROJ_PIECE])
    nxt = piece(starts[0])
    for idx, col in enumerate(starts):
        cols = slice(col, col + INPROJ_PIECE)
        res, nxt = nxt, (piece(starts[idx + 1]) if idx + 1 < len(starts) else None)
        if any(lo <= col < lo + width for lo, width in CONV_RANGES):
            for lt in range(INPROJ_PIECE // LANE):
                ti = col // LANE + lt
                ln = slice(lt * LANE, (lt + 1) * LANE)
                gl = slice(col + lt * LANE, col + (lt + 1) * LANE)
                stage_ref[ti, 8:, :] = res[:, ln]
                acc = None
                for k in range(CONV_K):
                    lo = k + 8 - (CONV_K - 1)
                    term = stage_ref[ti, pl.ds(lo, tm, stride=1), :] * cw_ref[k:k + 1, gl]
                    acc = term if acc is None else acc + term
                stage_ref[ti, 0:8, :] = res[tm - 8:, ln]
                o_ref[:, gl] = _silu(acc + cb_ref[:, gl])
        else:
            o_ref[:, cols] = res
        if COL_S5_U <= col < COL_S5_U + S5_WIDTH:
            u5_ref[:, col - COL_S5_U:col - COL_S5_U + INPROJ_PIECE] = res


def _inproj(x, nw, w, layer, conv_w, conv_b, bsz, seq, *, tm=256):
    t, d = x.shape
    n = w.shape[2]
    nseg = seq // tm
    fixed = lambda i: (0, 0)
    return pl.pallas_call(
        functools.partial(_inproj_kernel, nseg=nseg),
        grid=(t // tm,),
        in_specs=[
            pl.BlockSpec((tm, d), lambda i: (i, 0)),
            pl.BlockSpec((1, d), fixed),
            pl.BlockSpec((None, d, n), lambda i: (layer, 0, 0), pipeline_mode=pl.Buffered(1)),
            pl.BlockSpec((CONV_K, n), fixed),
            pl.BlockSpec((1, n), fixed),
        ],
        out_specs=[
            pl.BlockSpec((tm, n), lambda i: (i, 0)),
            pl.BlockSpec((tm, S5_WIDTH), lambda i: (i % nseg, i // nseg)),
        ],
        out_shape=[jax.ShapeDtypeStruct((t, n), F32),
                   jax.ShapeDtypeStruct((seq, bsz * S5_WIDTH), F32)],
        scratch_shapes=[pltpu.VMEM((n // LANE, tm + 8, LANE), F32)],
        compiler_params=pltpu.CompilerParams(
            dimension_semantics=("arbitrary",), vmem_limit_bytes=VMEM_LIMIT),
        name="inproj",
    )(x, nw.reshape(1, d), w, conv_w, conv_b)


def _outproj_kernel(h_ref, a_ref, b_ref, c_ref, wa_ref, wb_ref, wc_ref, o_ref):
    acc = _dot(a_ref[...], wa_ref[...])
    acc += _dot(b_ref[...], wb_ref[...])
    acc += _dot(c_ref[...], wc_ref[...])
    o_ref[...] = h_ref[...] + acc


def _outproj(h, oa, ob, oc, wa, wb, wc, *, tm=512):
    t, d = h.shape
    nseg = oc.shape[0] // tm
    row = lambda i: (i, 0)
    fixed = lambda i: (0, 0)
    return pl.pallas_call(
        _outproj_kernel,
        grid=(t // tm,),
        in_specs=[
            pl.BlockSpec((tm, d), row),
            pl.BlockSpec((tm, oa.shape[1]), row),
            pl.BlockSpec((tm, ob.shape[1]), row),
            pl.BlockSpec((tm, S5_WIDTH), lambda i: (i % nseg, i // nseg)),
            pl.BlockSpec(wa.shape, fixed),
            pl.BlockSpec(wb.shape, fixed),
            pl.BlockSpec(wc.shape, fixed),
        ],
        out_specs=pl.BlockSpec((tm, d), row),
        out_shape=jax.ShapeDtypeStruct((t, d), F32),
        compiler_params=pltpu.CompilerParams(
            dimension_semantics=("parallel",), vmem_limit_bytes=VMEM_LIMIT),
        name="outproj",
    )(h, oa, ob, oc, wa, wb, wc)


def _gdn_kernel(gpar_ref,
                q_ref, k_ref, v_ref, z_ref, gate_ref,
                nw_ref,
                o_ref,
                u_ref, w_ref, qd_ref, kd_ref, at_ref, eg_ref,
                *, hb, p_inv, p_mix, unroll):
    hg = pl.program_id(1)
    seq = q_ref.shape[0]
    n_chunks = seq // CHUNK
    c = CHUNK
    mix_dt = BF16 if p_mix == 1 else F32

    row = _iota((c, c), 0)
    col = _iota((c, c), 1)
    tril = row >= col
    strict = row > col
    ltri = tril.astype(BF16)
    ones = jnp.ones((c, c), BF16)
    eye = (row == col).astype(F32)
    lane = _iota((c, LANE), 1)
    neg_a = -jnp.exp(gpar_ref[0:1, :])
    dt_bias = gpar_ref[1:2, :]

    def prep_stages(trip):
        cis = [trip * unroll + cc for cc in range(unroll)]
        items = [(cis[cc], hh) for cc in range(unroll) for hh in range(hb)]
        lanes = [slice(hh * LANE, (hh + 1) * LANE) for _, hh in items]
        rows = [pl.ds(pl.multiple_of(ci * c, c), c) for ci, _ in items]

        def chunk_gates(ci):
            gt = gate_ref[pl.ds(pl.multiple_of(ci * c, c), c), :]
            return _sigmoid(gt), neg_a * _softplus(gt + dt_bias)

        gates = [chunk_gates(ci) for ci in cis]

        def stage_a(cc, hh, ln):
            ci = cis[cc]
            h = hg * hb + hh
            rw = pl.ds(pl.multiple_of(ci * c, c), c)
            q, k, v = q_ref[rw, ln], k_ref[rw, ln], v_ref[rw, ln]
            qn = q * lax.rsqrt(jnp.sum(q * q, axis=-1, keepdims=True) + RMS_EPS) * (GDN_D ** -0.5)
            kn = k * lax.rsqrt(jnp.sum(k * k, axis=-1, keepdims=True) + RMS_EPS)
            beta_all, g_all = gates[cc]
            beta = jnp.sum(jnp.where(lane == h, beta_all, 0.0), axis=-1, keepdims=True)
            g = jnp.sum(jnp.where(lane == h + GDN_HEADS, g_all, 0.0), axis=-1, keepdims=True)
            return qn, kn, v, beta, jnp.broadcast_to(g, (c, LANE))

        sa = [stage_a(cc, hh, slice(hh * LANE, (hh + 1) * LANE))
              for cc in range(unroll) for hh in range(hb)]
        qn = [x[0] for x in sa]
        kn = [x[1] for x in sa]
        v = [x[2] for x in sa]
        beta = [x[3] for x in sa]
        g_b = [x[4] for x in sa]
        yield
        gi = [_mm01(ltri, g) for g in g_b]
        gj = [_mm01(ones, jnp.where(row <= col, g[:, :c], 0.0)) for g in g_b]
        yield
        kb = [k * b for k, b in zip(kn, beta)]
        kk = [_mm(a, b, NT, p_inv) for a, b in zip(kb, kn)]
        qk = [_mm(a, b, NT, p_mix) for a, b in zip(qn, kn)]
        decay = [jnp.where(tril, jnp.exp(a[:, :c] - b), 0.0) for a, b in zip(gi, gj)]
        npow = [-jnp.where(strict, a * d, 0.0) for a, d in zip(kk, decay)]
        tinv = [eye + n for n in npow]
        for _ in range(5):
            yield
            npow = [_mm(n, n, NN, p_inv) for n in npow]
            tinv = [t + _mm(t, n, NN, p_inv) for t, n in zip(tinv, npow)]
        yield
        exp_g = [jnp.exp(g) for g in gi]
        rhs = [jnp.concatenate([vv * b, k * e], axis=1) for vv, b, k, e in zip(v, beta, kb, exp_g)]
        uw = [_mm(t, r, NN, p_inv) for t, r in zip(tinv, rhs)]
        for i, ((ci, hh), rw) in enumerate(zip(items, rows)):
            g_last = jnp.broadcast_to(gi[i][c - 1:c, :], (c, LANE))
            u_ref[hh, rw, :] = uw[i][:, :LANE]
            w_ref[hh, rw, :] = uw[i][:, LANE:].astype(mix_dt)
            at_ref[hh, rw, :] = jnp.where(tril, qk[i] * decay[i], 0.0).astype(mix_dt)
            qd_ref[hh, rw, :] = (qn[i] * exp_g[i]).astype(mix_dt)
            kd_ref[hh, rw, :] = (kn[i] * jnp.exp(g_last - gi[i])).astype(mix_dt)
            eg_ref[hh, pl.ds(pl.multiple_of(ci * 8, 8), 8), :] = jnp.exp(g_last[0:8, :])

    norm_w = nw_ref[...]

    def scan_stages(trip, states, final):
        for cc in range(unroll):
            ci = trip * unroll + cc
            rw = pl.ds(pl.multiple_of(ci * c, c), c)
            lhs = [jnp.concatenate([w_ref[hh, rw, :], qd_ref[hh, rw, :]], axis=0)
                   for hh in range(hb)]
            ws_qs = [_mm(a, s, NN, p_mix) for a, s in zip(lhs, states)]
            yield
            v_new = [u_ref[hh, rw, :] - ws_qs[hh][:c] for hh in range(hb)]
            kv = [_mm(kd_ref[hh, rw, :], v_new[hh], TN, p_mix) for hh in range(hb)]
            av = [_mm(at_ref[hh, rw, :], v_new[hh], NN, p_mix) for hh in range(hb)]
            new_states = []
            for hh in range(hb):
                eg = eg_ref[hh, pl.ds(pl.multiple_of(ci * 8, 8), 8), :]
                new_states.append(states[hh] * jnp.broadcast_to(eg[0:1, :], (GDN_D, GDN_D)) + kv[hh])
                o = ws_qs[hh][c:] + av[hh]
                ms = jnp.mean(o * o, axis=-1, keepdims=True)
                ln = slice(hh * LANE, (hh + 1) * LANE)
                y = o * lax.rsqrt(ms + RMS_EPS) * norm_w * _silu(z_ref[rw, ln])
                o_ref[rw, ln] = y.astype(o_ref.dtype)
            states = new_states
            yield
        final.extend(states)

    def drain(gen):
        for _ in gen:
            pass

    def fused(trip, states):
        final = []
        scan_gen = scan_stages(trip - 1, states, final)
        for _ in prep_stages(trip):
            next(scan_gen, None)
        drain(scan_gen)
        return tuple(final)

    n_trips = n_chunks // unroll
    drain(prep_stages(0))
    states = lax.fori_loop(1, n_trips, fused,
                           tuple(jnp.zeros((GDN_D, GDN_D), F32) for _ in range(hb)))
    drain(scan_stages(n_trips - 1, states, []))


def _gdn(proj, bsz, seq, a_log, dt_bias, norm_w, *, hb=3, p_inv=1, p_mix=1, unroll=4):
    t = proj.shape[0]
    unroll = math.gcd(unroll, seq // CHUNK)
    wd = hb * LANE
    mix_dt = BF16 if p_mix == 1 else F32
    q0 = COL_GDN_QKV // wd
    z0 = COL_GDN_Z // wd
    g0 = COL_GDN_GATE // LANE
    ng = GDN_HEADS // hb
    blk = lambda off: pl.BlockSpec((seq, wd), lambda b, h, off=off: (b, off + h))
    pad = lambda v: jnp.pad(v.astype(F32), (GDN_HEADS, LANE - 2 * GDN_HEADS))
    gpar = jnp.stack([pad(a_log), pad(dt_bias)])
    return pl.pallas_call(
        functools.partial(_gdn_kernel, hb=hb, p_inv=p_inv, p_mix=p_mix, unroll=unroll),
        grid=(bsz, ng),
        in_specs=[
            pl.BlockSpec((2, LANE), lambda b, h: (0, 0)),
            blk(q0), blk(q0 + ng), blk(q0 + 2 * ng), blk(z0),
            pl.BlockSpec((seq, LANE), lambda b, h: (b, g0)),
            pl.BlockSpec((1, LANE), lambda b, h: (0, 0)),
        ],
        out_specs=pl.BlockSpec((seq, wd), lambda b, h: (b, h)),
        out_shape=jax.ShapeDtypeStruct((t, GDN_QK), BF16),
        scratch_shapes=[
            pltpu.VMEM((hb, seq, LANE), F32),
            pltpu.VMEM((hb, seq, LANE), mix_dt),
            pltpu.VMEM((hb, seq, LANE), mix_dt),
            pltpu.VMEM((hb, seq, LANE), mix_dt),
            pltpu.VMEM((hb, seq, CHUNK), mix_dt),
            pltpu.VMEM((hb, seq // CHUNK * 8, LANE), F32),
        ],
        compiler_params=pltpu.CompilerParams(
            dimension_semantics=("parallel", "arbitrary"), vmem_limit_bytes=VMEM_LIMIT),
        name="gdn",
    )(gpar, proj, proj, proj, proj, proj, norm_w.reshape(1, LANE))


def _ssd_kernel(z_ref, x_ref, b_ref, c_ref, dt_ref,
                alog_ref, dtb_ref, dsk_ref, nw_ref,
                o_ref,
                s_ref, *, passes, unroll):
    grp = pl.program_id(1)
    seq = x_ref.shape[0]
    n_chunks = seq // CHUNK
    c = CHUNK
    gw = SSD_GW

    row = _iota((c, c), 0)
    col = _iota((c, c), 1)
    ltri = (row >= col).astype(BF16)
    ones = jnp.ones((c, c), BF16)
    rowt = _iota((c, gw), 0)
    colt = jnp.bitwise_and(_iota((c, gw), 1), c - 1)
    tril_t = rowt >= colt
    upper_t = rowt <= colt
    expand = (_iota((LANE, gw), 0) == grp * SSD_HPG + lax.shift_right_logical(_iota((LANE, gw), 1), 6)).astype(BF16)
    last_row = (_iota((8, gw), 0) == 7)
    ones_8n = jnp.ones((8, SSD_N), BF16)

    neg_a = -jnp.exp(alog_ref[...])
    dt_bias = dtb_ref[...]
    d_skip = dsk_ref[...]
    norm_w = nw_ref[...]

    s_ref[...] = jnp.zeros((gw, SSD_N), F32)
    heads = [slice(hh * SSD_P, (hh + 1) * SSD_P) for hh in range(SSD_HPG)]

    def body(trip, carry):
        cis = [trip * unroll + cc for cc in range(unroll)]
        rows = [pl.ds(pl.multiple_of(ci * c, c), c) for ci in cis]
        xs = [x_ref[rw, :] for rw in rows]
        bm = [b_ref[rw, :] for rw in rows]
        cm = [c_ref[rw, :] for rw in rows]
        dt = [_softplus(_mm01_t(dt_ref[rw, :], expand) + dt_bias) for rw in rows]
        a = [d * neg_a for d in dt]
        acum = [_mm01(ltri, x) for x in a]
        acum_j = [_mm01(ones, jnp.where(upper_t, x, 0.0)) for x in a]
        cb = [_mm(x, y, NT, passes) for x, y in zip(cm, bm)]
        lm = [jnp.where(tril_t, jnp.exp(x - y), 0.0) for x, y in zip(acum, acum_j)]
        xdt = [x * d for x, d in zip(xs, dt)]
        y_diag = [jnp.concatenate([_mm(cb[i] * lm[i][:, sl], xdt[i][:, sl], NN, passes)
                                   for sl in heads], axis=1) for i in range(unroll)]
        a_last = [jnp.broadcast_to(x[c - 1:c, :], (c, gw)) for x in acum]
        states = [_mm(xdt[i] * jnp.exp(a_last[i] - acum[i]), bm[i], TN, passes)
                  for i in range(unroll)]
        dec_col = [_mm01_t(jnp.where(last_row, jnp.exp(x[c - 8:, :]), 0.0), ones_8n, TN)
                   for x in acum]
        s_prev = s_ref[...]
        for i in range(unroll):
            y_off = _mm(cm[i], s_prev, NT, passes) * jnp.exp(acum[i])
            s_prev = s_prev * dec_col[i] + states[i]
            y = y_diag[i] + y_off + d_skip * xs[i]
            y = y * _silu(z_ref[rows[i], :])
            ms = jnp.mean(y * y, axis=-1, keepdims=True)
            o_ref[rows[i], :] = (y * lax.rsqrt(ms + RMS_EPS) * norm_w).astype(o_ref.dtype)
        s_ref[...] = s_prev
        return carry

    lax.fori_loop(0, n_chunks // unroll, body, 0)


def _ssd(proj, bsz, seq, a_log, dt_bias, d_skip, norm_w, *, passes=1, unroll=8):
    t = proj.shape[0]
    gw = SSD_GW
    unroll = math.gcd(unroll, seq // CHUNK)
    rep = lambda v: jnp.repeat(v.astype(F32), SSD_P).reshape(1, SSD_INNER)
    wide = lambda off: pl.BlockSpec((seq, gw), lambda b, g, off=off: (b, off + g))
    lane = lambda off: pl.BlockSpec((seq, LANE), lambda b, g, off=off: (b, off + g))
    chan = pl.BlockSpec((1, gw), lambda b, g: (0, g))
    return pl.pallas_call(
        functools.partial(_ssd_kernel, passes=passes, unroll=unroll),
        grid=(bsz, SSD_GROUPS),
        in_specs=[
            wide(COL_SSD_Z // gw), wide(COL_SSD_X // gw),
            lane(COL_SSD_B // LANE), lane(COL_SSD_C // LANE),
            pl.BlockSpec((seq, LANE), lambda b, g: (b, COL_SSD_DT // LANE)),
            chan, chan, chan, chan,
        ],
        out_specs=pl.BlockSpec((seq, gw), lambda b, g: (b, g)),
        out_shape=jax.ShapeDtypeStruct((t, SSD_INNER), BF16),
        scratch_shapes=[pltpu.VMEM((gw, SSD_N), F32)],
        compiler_params=pltpu.CompilerParams(
            dimension_semantics=("parallel", "arbitrary"), vmem_limit_bytes=VMEM_LIMIT),
        name="ssd",
    )(proj, proj, proj, proj, proj,
      rep(a_log), rep(dt_bias), rep(d_skip), norm_w.reshape(1, SSD_INNER))


def _s5_kernel(u_ref, wb_ref, wc_ref, are_ref, aim_ref, dsk_ref, gw_ref, gb_ref,
               o_ref, bu_ref, h_ref, *, bsz):
    ts = u_ref.shape[0]
    nt = S5_CH // LANE
    nb = S5_WIDTH // LANE
    per = nt // nb
    half = per * LANE

    @pl.when(pl.program_id(0) == 0)
    def _():
        h_ref[...] = jnp.zeros(h_ref.shape, F32)

    u = u_ref[...].reshape(ts * bsz, S5_WIDTH)
    u_bf = u.astype(BF16)
    for m in range(nb):
        res = _dot(u_bf[:, m * LANE:(m + 1) * LANE], wb_ref[m])
        for k in range(per):
            bu_ref[m * per + k] = res[:, k * LANE:(k + 1) * LANE]
            bu_ref[nt + m * per + k] = res[:, half + k * LANE:half + (k + 1) * LANE]

    def step(t, carry):
        rows = pl.ds(pl.multiple_of(t * bsz, bsz), bsz)
        new_re, new_im = [], []
        for k in range(nt):
            h_re, h_im = carry[k], carry[nt + k]
            a_re = are_ref[:, k * LANE:(k + 1) * LANE]
            a_im = aim_ref[:, k * LANE:(k + 1) * LANE]
            n_re = a_re * h_re - a_im * h_im + bu_ref[k, rows, :]
            n_im = a_re * h_im + a_im * h_re + bu_ref[nt + k, rows, :]
            bu_ref[k, rows, :] = n_re
            bu_ref[nt + k, rows, :] = n_im
            new_re.append(n_re)
            new_im.append(n_im)
        return tuple(new_re + new_im)

    h_last = lax.fori_loop(0, ts, step, tuple(h_ref[k] for k in range(2 * nt)))
    for k in range(2 * nt):
        h_ref[k] = h_last[k]

    y_parts = []
    for m in range(nb):
        hs = jnp.concatenate([bu_ref[m * per + k] for k in range(per)] +
                             [bu_ref[nt + m * per + k] for k in range(per)], axis=1)
        y_parts.append(_dot(hs.astype(BF16), wc_ref[m]))
    y = jnp.concatenate(y_parts, axis=1) + dsk_ref[...] * u
    g = 0.5 * y * (1.0 + jnp.tanh(math.sqrt(2.0 / math.pi) * (y + 0.044715 * (y * y * y))))
    out = g * _sigmoid(_dot(g.astype(BF16), gw_ref[...]) + gb_ref[...])
    o_ref[...] = out.reshape(ts, bsz * S5_WIDTH).astype(o_ref.dtype)


def _s5(u_tb, bsz, wb, wc, a_re, a_im, d_skip, glu_w, glu_b, *, ts=128):
    seq = u_tb.shape[0]
    blk = ts * bsz
    fixed = lambda shape: pl.BlockSpec(shape, lambda k: (0,) * len(shape))
    return pl.pallas_call(
        functools.partial(_s5_kernel, bsz=bsz),
        grid=(seq // ts,),
        in_specs=[
            pl.BlockSpec((ts, bsz * S5_WIDTH), lambda k: (k, 0)),
            fixed(wb.shape), fixed(wc.shape),
            fixed((1, S5_CH)), fixed((1, S5_CH)),
            fixed((1, S5_WIDTH)), fixed(glu_w.shape), fixed((1, S5_WIDTH)),
        ],
        out_specs=pl.BlockSpec((ts, bsz * S5_WIDTH), lambda k: (k, 0)),
        out_shape=jax.ShapeDtypeStruct((seq, bsz * S5_WIDTH), BF16),
        scratch_shapes=[
            pltpu.VMEM((2 * S5_CH // LANE, blk, LANE), F32),
            pltpu.VMEM((2 * S5_CH // LANE, bsz, LANE), F32),
        ],
        compiler_params=pltpu.CompilerParams(
            dimension_semantics=("arbitrary",), vmem_limit_bytes=VMEM_LIMIT),
        name="s5",
    )(u_tb, wb, wc, a_re.reshape(1, S5_CH), a_im.reshape(1, S5_CH),
      d_skip.reshape(1, S5_WIDTH), glu_w, glu_b.reshape(1, S5_WIDTH))


def _s5_params(a_re, a_im, b_re, b_im, c_re, c_im, log_dt):
    delta = jnp.exp(log_dt)[:, None]
    mag = jnp.exp(a_re * delta)
    ab_re, ab_im = mag * jnp.cos(a_im * delta), mag * jnp.sin(a_im * delta)
    den = a_re * a_re + a_im * a_im
    p_re, p_im = ab_re - 1.0, ab_im
    f_re = (p_re * a_re + p_im * a_im) / den
    f_im = (p_im * a_re - p_re * a_im) / den
    bb_re = f_re[..., None] * b_re - f_im[..., None] * b_im
    bb_im = f_re[..., None] * b_im + f_im[..., None] * b_re
    gpb = LANE // S5_GROUP
    nb = S5_GROUPS // gpb
    eye = jnp.eye(gpb, dtype=F32)
    emb_b = lambda bb: jnp.einsum('mgnj,gh->mgjhn', bb.reshape(nb, gpb, S5_STATE, S5_GROUP),
                                  eye).reshape(nb, LANE, gpb * S5_STATE)
    wb = jnp.concatenate([emb_b(bb_re), emb_b(bb_im)], axis=2)
    emb_c = lambda cc: jnp.einsum('mgin,gh->mgnhi', cc.reshape(nb, gpb, S5_GROUP, S5_STATE),
                                  eye).reshape(nb, gpb * S5_STATE, LANE)
    wc = jnp.concatenate([emb_c(c_re), -emb_c(c_im)], axis=1)
    return ab_re.reshape(-1), ab_im.reshape(-1), wb.astype(BF16), wc.astype(BF16)


def _w_in_segments():
    bc = SSD_GROUPS * SSD_N
    g_z = GDN_QKV
    g_gate = g_z + GDN_QK
    s_z = g_gate + 2 * GDN_HEADS
    s_x = s_z + SSD_INNER
    s_b = s_x + SSD_INNER
    s_c = s_b + bc
    s_dt = s_c + bc
    s5_u = s_dt + SSD_HEADS
    return ((0, GDN_QKV, COL_GDN_QKV), (g_z, GDN_QK, COL_GDN_Z),
            (g_gate, 2 * GDN_HEADS, COL_GDN_GATE), (s_z, SSD_INNER, COL_SSD_Z),
            (s_x, SSD_INNER, COL_SSD_X), (s_b, bc, COL_SSD_B), (s_c, bc, COL_SSD_C),
            (s_dt, SSD_HEADS, COL_SSD_DT), (s5_u, S5_WIDTH, COL_S5_U))


def _pack_conv(gdn_w, ssd_w, ssd_b):
    bc = SSD_GROUPS * SSD_N
    pieces = ((COL_GDN_QKV, gdn_w, None),
              (COL_SSD_X, ssd_w[:, :SSD_INNER], ssd_b[:SSD_INNER]),
              (COL_SSD_B, ssd_w[:, SSD_INNER:SSD_INNER + bc], ssd_b[SSD_INNER:SSD_INNER + bc]),
              (COL_SSD_C, ssd_w[:, SSD_INNER + bc:], ssd_b[SSD_INNER + bc:]))
    cw = jnp.zeros((CONV_K, P_PAD), F32)
    cb = jnp.zeros((1, P_PAD), F32)
    for col, w, b in pieces:
        cw = cw.at[:, col:col + w.shape[1]].set(w.astype(F32))
        if b is not None:
            cb = cb.at[0, col:col + b.shape[0]].set(b.astype(F32))
    return cw, cb


def _pack_kernel(x_ref, o_ref):
    for col in (COL_SSD_DT, COL_GDN_GATE):
        o_ref[:, col:col + LANE] = jnp.zeros((o_ref.shape[0], LANE), o_ref.dtype)
    for src, width, dst in _w_in_segments():
        o_ref[:, dst:dst + width] = x_ref[:, src:src + width].astype(o_ref.dtype)


def _pack_w_in(w_in, *, rows=256):
    nl, d, c = w_in.shape
    return pl.pallas_call(
        _pack_kernel,
        grid=(nl, d // rows),
        in_specs=[pl.BlockSpec((None, rows, c), lambda l, i: (l, i, 0))],
        out_specs=pl.BlockSpec((None, rows, P_PAD), lambda l, i: (l, i, 0)),
        out_shape=jax.ShapeDtypeStruct((nl, d, P_PAD), BF16),
        compiler_params=pltpu.CompilerParams(
            dimension_semantics=("parallel", "parallel"), vmem_limit_bytes=VMEM_LIMIT),
        name="pack_w_in",
    )(w_in)


def kernel(x, ffn1_norm, ffn1_w_gate, ffn1_w_up, ffn1_w_down, mix_norm, w_in,
           gdn_conv_w, gdn_a_log, gdn_dt_bias, gdn_norm,
           ssd_conv_w, ssd_conv_b, ssd_a_log, ssd_dt_bias, ssd_d, ssd_norm,
           s5_a_re, s5_a_im, s5_b_re, s5_b_im, s5_c_re, s5_c_im, s5_d, s5_log_dt,
           s5_glu_w, s5_glu_b, w_out, ffn2_norm, ffn2_w_gate, ffn2_w_up, ffn2_w_down,
           final_norm):
    bsz, seq, d = x.shape
    depth = w_in.shape[0]
    h = x
    ffn1_w = [_to_bf16(w) for w in (ffn1_w_gate, ffn1_w_up, ffn1_w_down)]
    ffn2_w = [_to_bf16(w) for w in (ffn2_w_gate, ffn2_w_up, ffn2_w_down)]
    w_in_packed = _pack_w_in(w_in)
    for i in range(depth):
        h = _ffn(h, ffn1_norm[i], *ffn1_w, final_norm, i, final_norm=False)
        conv_w, conv_b = _pack_conv(gdn_conv_w[i], ssd_conv_w[i], ssd_conv_b[i])
        proj, u_s5 = _inproj(h, mix_norm[i], w_in_packed, i, conv_w, conv_b, bsz, seq)
        o_gdn = _gdn(proj, bsz, seq, gdn_a_log[i], gdn_dt_bias[i], gdn_norm[i])
        o_ssd = _ssd(proj, bsz, seq, ssd_a_log[i], ssd_dt_bias[i], ssd_d[i], ssd_norm[i])
        ab_re, ab_im, wb, wc = _s5_params(s5_a_re[i], s5_a_im[i], s5_b_re[i], s5_b_im[i],
                                          s5_c_re[i], s5_c_im[i], s5_log_dt[i])
        o_s5 = _s5(u_s5, bsz, wb, wc, ab_re, ab_im, s5_d[i], s5_glu_w[i].astype(BF16),
                   s5_glu_b[i])
        wo = w_out[i].astype(BF16)
        h = _outproj(h, o_gdn, o_ssd, o_s5, wo[0:GDN_QK], wo[GDN_QK:GDN_QK + SSD_INNER],
                     wo[GDN_QK + SSD_INNER:])
        last = i == depth - 1
        h = _ffn(h, ffn2_norm[i], *ffn2_w, final_norm, i, final_norm=last,
                 out_batched=(bsz, seq) if last else None)
    return h
```

```python
import functools
import math

import jax
import jax.numpy as jnp
from jax import lax
from jax.experimental import pallas as pl
from jax.experimental.pallas import tpu as pltpu

F32 = jnp.float32
BF16 = jnp.bfloat16

RMS_EPS = 1e-6
CHUNK = 64
CONV_K = 4
D_MODEL = 2048
D_FF = 5632

GDN_HEADS = 6
GDN_D = 128
GDN_QK = GDN_HEADS * GDN_D
GDN_QKV = 3 * GDN_QK

SSD_HEADS = 12
SSD_P = 64
SSD_GROUPS = 2
SSD_N = 128
SSD_HPG = SSD_HEADS // SSD_GROUPS
SSD_GW = SSD_HPG * SSD_P
SSD_INNER = SSD_HEADS * SSD_P

S5_WIDTH = 512
S5_GROUP = 16
S5_GROUPS = 32
S5_STATE = 64
S5_CH = S5_GROUPS * S5_STATE

LANE = 128

COL_GDN_QKV = 0
COL_GDN_Z = 2304
COL_SSD_Z = 3072
COL_SSD_X = 3840
COL_SSD_B = 4608
COL_SSD_C = 4864
COL_S5_U = 5120
COL_SSD_DT = 5632
COL_GDN_GATE = 5760
P_PAD = 5888

VMEM_LIMIT = 60 * 1024 * 1024

NN = (((1,), (0,)), ((), ()))
NT = (((1,), (1,)), ((), ()))
TN = (((0,), (0,)), ((), ()))


def _dot(a, b, dims=NN):
    return lax.dot_general(a, b, dims, preferred_element_type=F32)


def _hi_lo(a):
    hi = a.astype(BF16)
    lo = (a - hi.astype(F32)).astype(BF16)
    return hi, lo


def _mm(a, b, dims=NN, passes=3):
    if passes == 1:
        return _dot(a.astype(BF16), b.astype(BF16), dims)
    ah, al = _hi_lo(a)
    bh, bl = _hi_lo(b)
    return _dot(ah, bh, dims) + (_dot(ah, bl, dims) + _dot(al, bh, dims))


def _mm01(m01, x, dims=NN):
    hi, lo = _hi_lo(x)
    return _dot(m01, hi, dims) + _dot(m01, lo, dims)


def _mm01_t(x, m01, dims=NN):
    hi, lo = _hi_lo(x)
    return _dot(hi, m01, dims) + _dot(lo, m01, dims)


def _sigmoid(x):
    return 1.0 / (1.0 + jnp.exp(-x))


def _silu(x):
    return x * _sigmoid(x)


def _softplus(x):
    return jnp.maximum(x, 0.0) + jnp.log1p(jnp.exp(-jnp.abs(x)))


def _iota(shape, dim):
    return lax.broadcasted_iota(jnp.int32, shape, dim)


def _causal_conv(x, cw_ref):
    acc = None
    for i in range(CONV_K):
        shift = CONV_K - 1 - i
        xs = x if shift == 0 else pltpu.roll(x, shift, axis=0)
        term = xs[8:] * cw_ref[i:i + 1, :]
        acc = term if acc is None else acc + term
    return acc


def _ffn_kernel(x_ref, nw_ref, wg_ref, wu_ref, wd_ref, fw_ref, *rest, final_norm, side_cast):
    if side_cast:
        cg_ref, cu_ref, cd_ref, o_ref, og_ref, ou_ref, od_ref, xn_ref = rest
        og_ref[...] = cg_ref[...].astype(BF16)
        ou_ref[...] = cu_ref[...].astype(BF16)
        od_ref[...] = cd_ref[...].astype(BF16)
    else:
        o_ref, xn_ref = rest
    j = pl.program_id(1)

    @pl.when(j == 0)
    def _():
        x = x_ref[...]
        ms = jnp.mean(x * x, axis=-1, keepdims=True)
        xn_ref[...] = (x * lax.rsqrt(ms + RMS_EPS) * nw_ref[...]).astype(BF16)
        o_ref[...] = x

    xn = xn_ref[...]
    g = _dot(xn, wg_ref[...])
    u = _dot(xn, wu_ref[...])
    a = (0.5 * _silu(g) * u).astype(BF16)
    o_ref[...] += _dot(a, wd_ref[...])

    if final_norm:
        @pl.when(j == pl.num_programs(1) - 1)
        def _():
            h = o_ref[...]
            ms = jnp.mean(h * h, axis=-1, keepdims=True)
            o_ref[...] = h * lax.rsqrt(ms + RMS_EPS) * fw_ref[...]


def _cast_kernel(x_ref, o_ref):
    o_ref[...] = x_ref[...].astype(o_ref.dtype)


def _to_bf16(w, layer, *, rows=256):
    _, r, c = w.shape
    return pl.pallas_call(
        _cast_kernel,
        grid=(r // rows,),
        in_specs=[pl.BlockSpec((None, rows, c), lambda i: (layer, i, 0))],
        out_specs=pl.BlockSpec((rows, c), lambda i: (i, 0)),
        out_shape=jax.ShapeDtypeStruct((r, c), BF16),
        compiler_params=pltpu.CompilerParams(
            dimension_semantics=("parallel",), vmem_limit_bytes=VMEM_LIMIT),
        name="to_bf16",
    )(w)


def _ffn(x, nw, wg, wu, wd, fw, *, final_norm, out_batched=None, cast_next=None,
         tm=1024, tf=512):
    d = x.shape[-1]
    t = x.size // d
    f = wg.shape[1]
    ni, nj = t // tm, f // tf
    if x.ndim == 3:
        nseg = x.shape[1] // tm
        x_spec = pl.BlockSpec((None, tm, d), lambda i, j: (i // nseg, i % nseg, 0))
    else:
        x_spec = pl.BlockSpec((tm, d), lambda i, j: (i, 0))
    if out_batched is None:
        out_spec = pl.BlockSpec((tm, d), lambda i, j: (i, 0))
        out_shape = jax.ShapeDtypeStruct((t, d), F32)
    else:
        oseg = out_batched[1] // tm
        out_spec = pl.BlockSpec((None, tm, d), lambda i, j: (i // oseg, i % oseg, 0))
        out_shape = jax.ShapeDtypeStruct((*out_batched, d), F32)
    in_specs = [
        x_spec,
        pl.BlockSpec((1, d), lambda i, j: (0, 0)),
        pl.BlockSpec((d, tf), lambda i, j: (0, j)),
        pl.BlockSpec((d, tf), lambda i, j: (0, j)),
        pl.BlockSpec((tf, d), lambda i, j: (j, 0)),
        pl.BlockSpec((1, d), lambda i, j: (0, 0)),
    ]
    operands = [x, nw.reshape(1, d), wg, wu, wd, fw.reshape(1, d)]
    out_specs, out_shapes = [out_spec], [out_shape]
    if cast_next is not None:
        cg, cu, cd, layer = cast_next
        dr = d // ni
        in_specs += [pl.BlockSpec((None, dr, tf), lambda i, j: (layer, i, j)),
                     pl.BlockSpec((None, dr, tf), lambda i, j: (layer, i, j)),
                     pl.BlockSpec((None, tf, dr), lambda i, j: (layer, j, i))]
        operands += [cg, cu, cd]
        out_specs += [pl.BlockSpec((dr, tf), lambda i, j: (i, j)),
                      pl.BlockSpec((dr, tf), lambda i, j: (i, j)),
                      pl.BlockSpec((tf, dr), lambda i, j: (j, i))]
        out_shapes += [jax.ShapeDtypeStruct((d, f), BF16), jax.ShapeDtypeStruct((d, f), BF16),
                       jax.ShapeDtypeStruct((f, d), BF16)]
    res = pl.pallas_call(
        functools.partial(_ffn_kernel, final_norm=final_norm, side_cast=cast_next is not None),
        grid=(ni, nj),
        in_specs=in_specs,
        out_specs=out_specs,
        out_shape=out_shapes,
        scratch_shapes=[pltpu.VMEM((tm, d), BF16)],
        compiler_params=pltpu.CompilerParams(
            dimension_semantics=("parallel", "arbitrary"), vmem_limit_bytes=VMEM_LIMIT),
        name="ffn",
    )(*operands)
    return (res[0], tuple(res[1:])) if cast_next is not None else res[0]


CONV_RANGES = ((COL_SSD_X, SSD_INNER), (COL_SSD_B, SSD_GROUPS * SSD_N),
               (COL_SSD_C, SSD_GROUPS * SSD_N), (COL_GDN_QKV, GDN_QKV))
INPROJ_PIECE = 256


def _inproj_kernel(x_ref, nw_ref, w_ref, cw_ref, cb_ref, o_ref, u5_ref, stage_ref, *, nseg):
    tm = x_ref.shape[0]

    @pl.when(pl.program_id(0) % nseg == 0)
    def _():
        stage_ref[:, 0:8, :] = jnp.zeros((stage_ref.shape[0], 8, LANE), F32)

    x = x_ref[...]
    ms = jnp.mean(x * x, axis=-1, keepdims=True)
    xn = (x * lax.rsqrt(ms + RMS_EPS) * nw_ref[...]).astype(BF16)
    starts = list(range(0, o_ref.shape[1], INPROJ_PIECE))
    piece = lambda col: _dot(xn, w_ref[:, col:col + INPROJ_PIECE])
    nxt = piece(starts[0])
    for idx, col in enumerate(starts):
        cols = slice(col, col + INPROJ_PIECE)
        res, nxt = nxt, (piece(starts[idx + 1]) if idx + 1 < len(starts) else None)
        if any(lo <= col < lo + width for lo, width in CONV_RANGES):
            for lt in range(INPROJ_PIECE // LANE):
                ti = col // LANE + lt
                ln = slice(lt * LANE, (lt + 1) * LANE)
                gl = slice(col + lt * LANE, col + (lt + 1) * LANE)
                stage_ref[ti, 8:, :] = res[:, ln]
                acc = None
                for k in range(CONV_K):
                    lo = k + 8 - (CONV_K - 1)
                    term = stage_ref[ti, pl.ds(lo, tm, stride=1), :] * cw_ref[k:k + 1, gl]
                    acc = term if acc is None else acc + term
                stage_ref[ti, 0:8, :] = res[tm - 8:, ln]
                o_ref[:, gl] = _silu(acc + cb_ref[:, gl])
        else:
            o_ref[:, cols] = res
        if COL_S5_U <= col < COL_S5_U + S5_WIDTH:
            u5_ref[:, col - COL_S5_U:col - COL_S5_U + INPROJ_PIECE] = res


def _inproj(x, nw, w, layer, conv_w, conv_b, bsz, seq, *, tm=256):
    t, d = x.shape
    n = w.shape[2]
    nseg = seq // tm
    fixed = lambda i: (0, 0)
    return pl.pallas_call(
        functools.partial(_inproj_kernel, nseg=nseg),
        grid=(t // tm,),
        in_specs=[
            pl.BlockSpec((tm, d), lambda i: (i, 0)),
            pl.BlockSpec((1, d), fixed),
            pl.BlockSpec((None, d, n), lambda i: (layer, 0, 0), pipeline_mode=pl.Buffered(1)),
            pl.BlockSpec((CONV_K, n), fixed),
            pl.BlockSpec((1, n), fixed),
        ],
        out_specs=[
            pl.BlockSpec((tm, n), lambda i: (i, 0)),
            pl.BlockSpec((tm, S5_WIDTH), lambda i: (i % nseg, i // nseg)),
        ],
        out_shape=[jax.ShapeDtypeStruct((t, n), F32),
                   jax.ShapeDtypeStruct((seq, bsz * S5_WIDTH), F32)],
        scratch_shapes=[pltpu.VMEM((n // LANE, tm + 8, LANE), F32)],
        compiler_params=pltpu.CompilerParams(
            dimension_semantics=("arbitrary",), vmem_limit_bytes=VMEM_LIMIT),
        name="inproj",
    )(x, nw.reshape(1, d), w, conv_w, conv_b)


def _outproj_kernel(h_ref, a_ref, b_ref, c_ref, wa_ref, wb_ref, wc_ref, o_ref):
    acc = _dot(a_ref[...], wa_ref[...])
    acc += _dot(b_ref[...], wb_ref[...])
    acc += _dot(c_ref[...], wc_ref[...])
    o_ref[...] = h_ref[...] + acc


def _outproj(h, oa, ob, oc, wa, wb, wc, *, tm=512):
    t, d = h.shape
    nseg = oc.shape[0] // tm
    row = lambda i: (i, 0)
    fixed = lambda i: (0, 0)
    return pl.pallas_call(
        _outproj_kernel,
        grid=(t // tm,),
        in_specs=[
            pl.BlockSpec((tm, d), row),
            pl.BlockSpec((tm, oa.shape[1]), row),
            pl.BlockSpec((tm, ob.shape[1]), row),
            pl.BlockSpec((tm, S5_WIDTH), lambda i: (i % nseg, i // nseg)),
            pl.BlockSpec(wa.shape, fixed),
            pl.BlockSpec(wb.shape, fixed),
            pl.BlockSpec(wc.shape, fixed),
        ],
        out_specs=pl.BlockSpec((tm, d), row),
        out_shape=jax.ShapeDtypeStruct((t, d), F32),
        compiler_params=pltpu.CompilerParams(
            dimension_semantics=("parallel",), vmem_limit_bytes=VMEM_LIMIT),
        name="outproj",
    )(h, oa, ob, oc, wa, wb, wc)


def _gdn_kernel(gpar_ref,
                q_ref, k_ref, v_ref, z_ref, gate_ref,
                nw_ref,
                o_ref,
                u_ref, w_ref, qd_ref, kd_ref, at_ref, eg_ref,
                *, hb, p_inv, p_mix, unroll):
    hg = pl.program_id(1)
    seq = q_ref.shape[0]
    n_chunks = seq // CHUNK
    c = CHUNK
    mix_dt = BF16 if p_mix == 1 else F32

    row = _iota((c, c), 0)
    col = _iota((c, c), 1)
    tril = row >= col
    strict = row > col
    ltri = tril.astype(BF16)
    ones = jnp.ones((c, c), BF16)
    eye = (row == col).astype(F32)
    lane = _iota((c, LANE), 1)
    neg_a = -jnp.exp(gpar_ref[0:1, :])
    dt_bias = gpar_ref[1:2, :]

    def prep_stages(trip):
        cis = [trip * unroll + cc for cc in range(unroll)]
        items = [(cis[cc], hh) for cc in range(unroll) for hh in range(hb)]
        lanes = [slice(hh * LANE, (hh + 1) * LANE) for _, hh in items]
        rows = [pl.ds(pl.multiple_of(ci * c, c), c) for ci, _ in items]

        def chunk_gates(ci):
            gt = gate_ref[pl.ds(pl.multiple_of(ci * c, c), c), :]
            return _sigmoid(gt), neg_a * _softplus(gt + dt_bias)

        gates = [chunk_gates(ci) for ci in cis]

        def stage_a(cc, hh, ln):
            ci = cis[cc]
            h = hg * hb + hh
            rw = pl.ds(pl.multiple_of(ci * c, c), c)
            q, k, v = q_ref[rw, ln], k_ref[rw, ln], v_ref[rw, ln]
            qn = q * lax.rsqrt(jnp.sum(q * q, axis=-1, keepdims=True) + RMS_EPS) * (GDN_D ** -0.5)
            kn = k * lax.rsqrt(jnp.sum(k * k, axis=-1, keepdims=True) + RMS_EPS)
            beta_all, g_all = gates[cc]
            beta = jnp.sum(jnp.where(lane == h, beta_all, 0.0), axis=-1, keepdims=True)
            g = jnp.sum(jnp.where(lane == h + GDN_HEADS, g_all, 0.0), axis=-1, keepdims=True)
            return qn, kn, v, beta, jnp.broadcast_to(g, (c, LANE))

        sa = [stage_a(cc, hh, slice(hh * LANE, (hh + 1) * LANE))
              for cc in range(unroll) for hh in range(hb)]
        qn = [x[0] for x in sa]
        kn = [x[1] for x in sa]
        v = [x[2] for x in sa]
        beta = [x[3] for x in sa]
        g_b = [x[4] for x in sa]
        yield
        gi = [_mm01(ltri, g) for g in g_b]
        gj = [_mm01(ones, jnp.where(row <= col, g[:, :c], 0.0)) for g in g_b]
        yield
        kb = [k * b for k, b in zip(kn, beta)]
        kk = [_mm(a, b, NT, p_inv) for a, b in zip(kb, kn)]
        qk = [_mm(a, b, NT, p_mix) for a, b in zip(qn, kn)]
        decay = [jnp.where(tril, jnp.exp(a[:, :c] - b), 0.0) for a, b in zip(gi, gj)]
        npow = [-jnp.where(strict, a * d, 0.0) for a, d in zip(kk, decay)]
        tinv = [eye + n for n in npow]
        for _ in range(5):
            yield
            npow = [_mm(n, n, NN, p_inv) for n in npow]
            tinv = [t + _mm(t, n, NN, p_inv) for t, n in zip(tinv, npow)]
        yield
        exp_g = [jnp.exp(g) for g in gi]
        rhs = [jnp.concatenate([vv * b, k * e], axis=1) for vv, b, k, e in zip(v, beta, kb, exp_g)]
        uw = [_mm(t, r, NN, p_inv) for t, r in zip(tinv, rhs)]
        for i, ((ci, hh), rw) in enumerate(zip(items, rows)):
            g_last = jnp.broadcast_to(gi[i][c - 1:c, :], (c, LANE))
            u_ref[hh, rw, :] = uw[i][:, :LANE]
            w_ref[hh, rw, :] = uw[i][:, LANE:].astype(mix_dt)
            at_ref[hh, rw, :] = jnp.where(tril, qk[i] * decay[i], 0.0).astype(mix_dt)
            qd_ref[hh, rw, :] = (qn[i] * exp_g[i]).astype(mix_dt)
            kd_ref[hh, rw, :] = (kn[i] * jnp.exp(g_last - gi[i])).astype(mix_dt)
            eg_ref[hh, pl.ds(pl.multiple_of(ci * 8, 8), 8), :] = jnp.exp(g_last[0:8, :])

    norm_w = nw_ref[...]

    def scan_stages(trip, states, final):
        for cc in range(unroll):
            ci = trip * unroll + cc
            rw = pl.ds(pl.multiple_of(ci * c, c), c)
            lhs = [jnp.concatenate([w_ref[hh, rw, :], qd_ref[hh, rw, :]], axis=0)
                   for hh in range(hb)]
            ws_qs = [_mm(a, s, NN, p_mix) for a, s in zip(lhs, states)]
            yield
            v_new = [u_ref[hh, rw, :] - ws_qs[hh][:c] for hh in range(hb)]
            kv = [_mm(kd_ref[hh, rw, :], v_new[hh], TN, p_mix) for hh in range(hb)]
            av = [_mm(at_ref[hh, rw, :], v_new[hh], NN, p_mix) for hh in range(hb)]
            new_states = []
            for hh in range(hb):
                eg = eg_ref[hh, pl.ds(pl.multiple_of(ci * 8, 8), 8), :]
                new_states.append(states[hh] * jnp.broadcast_to(eg[0:1, :], (GDN_D, GDN_D)) + kv[hh])
                o = ws_qs[hh][c:] + av[hh]
                ms = jnp.mean(o * o, axis=-1, keepdims=True)
                ln = slice(hh * LANE, (hh + 1) * LANE)
                y = o * lax.rsqrt(ms + RMS_EPS) * norm_w * _silu(z_ref[rw, ln])
                o_ref[rw, ln] = y.astype(o_ref.dtype)
            states = new_states
            yield
        final.extend(states)

    def drain(gen):
        for _ in gen:
            pass

    def fused(trip, states):
        final = []
        scan_gen = scan_stages(trip - 1, states, final)
        for _ in prep_stages(trip):
            next(scan_gen, None)
        drain(scan_gen)
        return tuple(final)

    n_trips = n_chunks // unroll
    drain(prep_stages(0))
    states = lax.fori_loop(1, n_trips, fused,
                           tuple(jnp.zeros((GDN_D, GDN_D), F32) for _ in range(hb)))
    drain(scan_stages(n_trips - 1, states, []))


def _gdn(proj, bsz, seq, a_log, dt_bias, norm_w, *, hb=3, p_inv=1, p_mix=1, unroll=4):
    t = proj.shape[0]
    unroll = math.gcd(unroll, seq // CHUNK)
    wd = hb * LANE
    mix_dt = BF16 if p_mix == 1 else F32
    q0 = COL_GDN_QKV // wd
    z0 = COL_GDN_Z // wd
    g0 = COL_GDN_GATE // LANE
    ng = GDN_HEADS // hb
    blk = lambda off: pl.BlockSpec((seq, wd), lambda b, h, off=off: (b, off + h))
    pad = lambda v: jnp.pad(v.astype(F32), (GDN_HEADS, LANE - 2 * GDN_HEADS))
    gpar = jnp.stack([pad(a_log), pad(dt_bias)])
    return pl.pallas_call(
        functools.partial(_gdn_kernel, hb=hb, p_inv=p_inv, p_mix=p_mix, unroll=unroll),
        grid=(bsz, ng),
        in_specs=[
            pl.BlockSpec((2, LANE), lambda b, h: (0, 0)),
            blk(q0), blk(q0 + ng), blk(q0 + 2 * ng), blk(z0),
            pl.BlockSpec((seq, LANE), lambda b, h: (b, g0)),
            pl.BlockSpec((1, LANE), lambda b, h: (0, 0)),
        ],
        out_specs=pl.BlockSpec((seq, wd), lambda b, h: (b, h)),
        out_shape=jax.ShapeDtypeStruct((t, GDN_QK), BF16),
        scratch_shapes=[
            pltpu.VMEM((hb, seq, LANE), F32),
            pltpu.VMEM((hb, seq, LANE), mix_dt),
            pltpu.VMEM((hb, seq, LANE), mix_dt),
            pltpu.VMEM((hb, seq, LANE), mix_dt),
            pltpu.VMEM((hb, seq, CHUNK), mix_dt),
            pltpu.VMEM((hb, seq // CHUNK * 8, LANE), F32),
        ],
        compiler_params=pltpu.CompilerParams(
            dimension_semantics=("parallel", "arbitrary"), vmem_limit_bytes=VMEM_LIMIT),
        name="gdn",
    )(gpar, proj, proj, proj, proj, proj, norm_w.reshape(1, LANE))


def _ssd_kernel(z_ref, x_ref, b_ref, c_ref, dt_ref,
                alog_ref, dtb_ref, dsk_ref, nw_ref,
                o_ref,
                s_ref, *, passes, unroll):
    grp = pl.program_id(1)
    seq = x_ref.shape[0]
    n_chunks = seq // CHUNK
    c = CHUNK
    gw = SSD_GW

    row = _iota((c, c), 0)
    col = _iota((c, c), 1)
    ltri = (row >= col).astype(BF16)
    ones = jnp.ones((c, c), BF16)
    rowt = _iota((c, gw), 0)
    colt = jnp.bitwise_and(_iota((c, gw), 1), c - 1)
    tril_t = rowt >= colt
    upper_t = rowt <= colt
    expand = (_iota((LANE, gw), 0) == grp * SSD_HPG + lax.shift_right_logical(_iota((LANE, gw), 1), 6)).astype(BF16)
    last_row = (_iota((8, gw), 0) == 7)
    ones_8n = jnp.ones((8, SSD_N), BF16)

    neg_a = -jnp.exp(alog_ref[...])
    dt_bias = dtb_ref[...]
    d_skip = dsk_ref[...]
    norm_w = nw_ref[...]

    s_ref[...] = jnp.zeros((gw, SSD_N), F32)
    heads = [slice(hh * SSD_P, (hh + 1) * SSD_P) for hh in range(SSD_HPG)]

    def body(trip, carry):
        cis = [trip * unroll + cc for cc in range(unroll)]
        rows = [pl.ds(pl.multiple_of(ci * c, c), c) for ci in cis]
        xs = [x_ref[rw, :] for rw in rows]
        bm = [b_ref[rw, :] for rw in rows]
        cm = [c_ref[rw, :] for rw in rows]
        dt = [_softplus(_mm01_t(dt_ref[rw, :], expand) + dt_bias) for rw in rows]
        a = [d * neg_a for d in dt]
        acum = [_mm01(ltri, x) for x in a]
        acum_j = [_mm01(ones, jnp.where(upper_t, x, 0.0)) for x in a]
        cb = [_mm(x, y, NT, passes) for x, y in zip(cm, bm)]
        lm = [jnp.where(tril_t, jnp.exp(x - y), 0.0) for x, y in zip(acum, acum_j)]
        xdt = [x * d for x, d in zip(xs, dt)]
        y_diag = [jnp.concatenate([_mm(cb[i] * lm[i][:, sl], xdt[i][:, sl], NN, passes)
                                   for sl in heads], axis=1) for i in range(unroll)]
        a_last = [jnp.broadcast_to(x[c - 1:c, :], (c, gw)) for x in acum]
        states = [_mm(xdt[i] * jnp.exp(a_last[i] - acum[i]), bm[i], TN, passes)
                  for i in range(unroll)]
        dec_col = [_mm01_t(jnp.where(last_row, jnp.exp(x[c - 8:, :]), 0.0), ones_8n, TN)
                   for x in acum]
        s_prev = s_ref[...]
        for i in range(unroll):
            y_off = _mm(cm[i], s_prev, NT, passes) * jnp.exp(acum[i])
            s_prev = s_prev * dec_col[i] + states[i]
            y = y_diag[i] + y_off + d_skip * xs[i]
            y = y * _silu(z_ref[rows[i], :])
            ms = jnp.mean(y * y, axis=-1, keepdims=True)
            o_ref[rows[i], :] = (y * lax.rsqrt(ms + RMS_EPS) * norm_w).astype(o_ref.dtype)
        s_ref[...] = s_prev
        return carry

    lax.fori_loop(0, n_chunks // unroll, body, 0)


def _ssd(proj, bsz, seq, a_log, dt_bias, d_skip, norm_w, *, passes=1, unroll=8):
    t = proj.shape[0]
    gw = SSD_GW
    unroll = math.gcd(unroll, seq // CHUNK)
    rep = lambda v: jnp.repeat(v.astype(F32), SSD_P).reshape(1, SSD_INNER)
    wide = lambda off: pl.BlockSpec((seq, gw), lambda b, g, off=off: (b, off + g))
    lane = lambda off: pl.BlockSpec((seq, LANE), lambda b, g, off=off: (b, off + g))
    chan = pl.BlockSpec((1, gw), lambda b, g: (0, g))
    return pl.pallas_call(
        functools.partial(_ssd_kernel, passes=passes, unroll=unroll),
        grid=(bsz, SSD_GROUPS),
        in_specs=[
            wide(COL_SSD_Z // gw), wide(COL_SSD_X // gw),
            lane(COL_SSD_B // LANE), lane(COL_SSD_C // LANE),
            pl.BlockSpec((seq, LANE), lambda b, g: (b, COL_SSD_DT // LANE)),
            chan, chan, chan, chan,
        ],
        out_specs=pl.BlockSpec((seq, gw), lambda b, g: (b, g)),
        out_shape=jax.ShapeDtypeStruct((t, SSD_INNER), BF16),
        scratch_shapes=[pltpu.VMEM((gw, SSD_N), F32)],
        compiler_params=pltpu.CompilerParams(
            dimension_semantics=("parallel", "arbitrary"), vmem_limit_bytes=VMEM_LIMIT),
        name="ssd",
    )(proj, proj, proj, proj, proj,
      rep(a_log), rep(dt_bias), rep(d_skip), norm_w.reshape(1, SSD_INNER))


def _s5_kernel(u_ref, wb_ref, wc_ref, are_ref, aim_ref, dsk_ref, gw_ref, gb_ref,
               o_ref, bu_ref, h_ref, *, bsz):
    ts = u_ref.shape[0]
    nt = S5_CH // LANE
    nb = S5_WIDTH // LANE
    per = nt // nb
    half = per * LANE

    @pl.when(pl.program_id(0) == 0)
    def _():
        h_ref[...] = jnp.zeros(h_ref.shape, F32)

    u = u_ref[...].reshape(ts * bsz, S5_WIDTH)
    u_bf = u.astype(BF16)
    for m in range(nb):
        res = _dot(u_bf[:, m * LANE:(m + 1) * LANE], wb_ref[m])
        for k in range(per):
            bu_ref[m * per + k] = res[:, k * LANE:(k + 1) * LANE]
            bu_ref[nt + m * per + k] = res[:, half + k * LANE:half + (k + 1) * LANE]

    def step(t, carry):
        rows = pl.ds(pl.multiple_of(t * bsz, bsz), bsz)
        new_re, new_im = [], []
        for k in range(nt):
            h_re, h_im = carry[k], carry[nt + k]
            a_re = are_ref[:, k * LANE:(k + 1) * LANE]
            a_im = aim_ref[:, k * LANE:(k + 1) * LANE]
            n_re = a_re * h_re - a_im * h_im + bu_ref[k, rows, :]
            n_im = a_re * h_im + a_im * h_re + bu_ref[nt + k, rows, :]
            bu_ref[k, rows, :] = n_re
            bu_ref[nt + k, rows, :] = n_im
            new_re.append(n_re)
            new_im.append(n_im)
        return tuple(new_re + new_im)

    h_last = lax.fori_loop(0, ts, step, tuple(h_ref[k] for k in range(2 * nt)))
    for k in range(2 * nt):
        h_ref[k] = h_last[k]

    y_parts = []
    for m in range(nb):
        hs = jnp.concatenate([bu_ref[m * per + k] for k in range(per)] +
                             [bu_ref[nt + m * per + k] for k in range(per)], axis=1)
        y_parts.append(_dot(hs.astype(BF16), wc_ref[m]))
    y = jnp.concatenate(y_parts, axis=1) + dsk_ref[...] * u
    g = 0.5 * y * (1.0 + jnp.tanh(math.sqrt(2.0 / math.pi) * (y + 0.044715 * (y * y * y))))
    out = g * _sigmoid(_dot(g.astype(BF16), gw_ref[...]) + gb_ref[...])
    o_ref[...] = out.reshape(ts, bsz * S5_WIDTH).astype(o_ref.dtype)


def _s5(u_tb, bsz, wb, wc, a_re, a_im, d_skip, glu_w, glu_b, *, ts=128):
    seq = u_tb.shape[0]
    blk = ts * bsz
    fixed = lambda shape: pl.BlockSpec(shape, lambda k: (0,) * len(shape))
    return pl.pallas_call(
        functools.partial(_s5_kernel, bsz=bsz),
        grid=(seq // ts,),
        in_specs=[
            pl.BlockSpec((ts, bsz * S5_WIDTH), lambda k: (k, 0)),
            fixed(wb.shape), fixed(wc.shape),
            fixed((1, S5_CH)), fixed((1, S5_CH)),
            fixed((1, S5_WIDTH)), fixed(glu_w.shape), fixed((1, S5_WIDTH)),
        ],
        out_specs=pl.BlockSpec((ts, bsz * S5_WIDTH), lambda k: (k, 0)),
        out_shape=jax.ShapeDtypeStruct((seq, bsz * S5_WIDTH), BF16),
        scratch_shapes=[
            pltpu.VMEM((2 * S5_CH // LANE, blk, LANE), F32),
            pltpu.VMEM((2 * S5_CH // LANE, bsz, LANE), F32),
        ],
        compiler_params=pltpu.CompilerParams(
            dimension_semantics=("arbitrary",), vmem_limit_bytes=VMEM_LIMIT),
        name="s5",
    )(u_tb, wb, wc, a_re.reshape(1, S5_CH), a_im.reshape(1, S5_CH),
      d_skip.reshape(1, S5_WIDTH), glu_w, glu_b.reshape(1, S5_WIDTH))


def _s5_params(a_re, a_im, b_re, b_im, c_re, c_im, log_dt):
    delta = jnp.exp(log_dt)[:, None]
    mag = jnp.exp(a_re * delta)
    ab_re, ab_im = mag * jnp.cos(a_im * delta), mag * jnp.sin(a_im * delta)
    den = a_re * a_re + a_im * a_im
    p_re, p_im = ab_re - 1.0, ab_im
    f_re = (p_re * a_re + p_im * a_im) / den
    f_im = (p_im * a_re - p_re * a_im) / den
    bb_re = f_re[..., None] * b_re - f_im[..., None] * b_im
    bb_im = f_re[..., None] * b_im + f_im[..., None] * b_re
    gpb = LANE // S5_GROUP
    nb = S5_GROUPS // gpb
    eye = jnp.eye(gpb, dtype=F32)
    emb_b = lambda bb: jnp.einsum('mgnj,gh->mgjhn', bb.reshape(nb, gpb, S5_STATE, S5_GROUP),
                                  eye).reshape(nb, LANE, gpb * S5_STATE)
    wb = jnp.concatenate([emb_b(bb_re), emb_b(bb_im)], axis=2)
    emb_c = lambda cc: jnp.einsum('mgin,gh->mgnhi', cc.reshape(nb, gpb, S5_GROUP, S5_STATE),
                                  eye).reshape(nb, gpb * S5_STATE, LANE)
    wc = jnp.concatenate([emb_c(c_re), -emb_c(c_im)], axis=1)
    return ab_re.reshape(-1), ab_im.reshape(-1), wb.astype(BF16), wc.astype(BF16)


def _w_in_segments():
    bc = SSD_GROUPS * SSD_N
    g_z = GDN_QKV
    g_gate = g_z + GDN_QK
    s_z = g_gate + 2 * GDN_HEADS
    s_x = s_z + SSD_INNER
    s_b = s_x + SSD_INNER
    s_c = s_b + bc
    s_dt = s_c + bc
    s5_u = s_dt + SSD_HEADS
    return ((0, GDN_QKV, COL_GDN_QKV), (g_z, GDN_QK, COL_GDN_Z),
            (g_gate, 2 * GDN_HEADS, COL_GDN_GATE), (s_z, SSD_INNER, COL_SSD_Z),
            (s_x, SSD_INNER, COL_SSD_X), (s_b, bc, COL_SSD_B), (s_c, bc, COL_SSD_C),
            (s_dt, SSD_HEADS, COL_SSD_DT), (s5_u, S5_WIDTH, COL_S5_U))


def _pack_conv(gdn_w, ssd_w, ssd_b):
    bc = SSD_GROUPS * SSD_N
    pieces = ((COL_GDN_QKV, gdn_w, None),
              (COL_SSD_X, ssd_w[:, :SSD_INNER], ssd_b[:SSD_INNER]),
              (COL_SSD_B, ssd_w[:, SSD_INNER:SSD_INNER + bc], ssd_b[SSD_INNER:SSD_INNER + bc]),
              (COL_SSD_C, ssd_w[:, SSD_INNER + bc:], ssd_b[SSD_INNER + bc:]))
    cw = jnp.zeros((CONV_K, P_PAD), F32)
    cb = jnp.zeros((1, P_PAD), F32)
    for col, w, b in pieces:
        cw = cw.at[:, col:col + w.shape[1]].set(w.astype(F32))
        if b is not None:
            cb = cb.at[0, col:col + b.shape[0]].set(b.astype(F32))
    return cw, cb


def _pack_kernel(x_ref, o_ref):
    for col in (COL_SSD_DT, COL_GDN_GATE):
        o_ref[:, col:col + LANE] = jnp.zeros((o_ref.shape[0], LANE), o_ref.dtype)
    for src, width, dst in _w_in_segments():
        o_ref[:, dst:dst + width] = x_ref[:, src:src + width].astype(o_ref.dtype)


def _pack_w_in(w_in, *, rows=256):
    nl, d, c = w_in.shape
    return pl.pallas_call(
        _pack_kernel,
        grid=(nl, d // rows),
        in_specs=[pl.BlockSpec((None, rows, c), lambda l, i: (l, i, 0))],
        out_specs=pl.BlockSpec((None, rows, P_PAD), lambda l, i: (l, i, 0)),
        out_shape=jax.ShapeDtypeStruct((nl, d, P_PAD), BF16),
        compiler_params=pltpu.CompilerParams(
            dimension_semantics=("parallel", "parallel"), vmem_limit_bytes=VMEM_LIMIT),
        name="pack_w_in",
    )(w_in)


def kernel(x, ffn1_norm, ffn1_w_gate, ffn1_w_up, ffn1_w_down, mix_norm, w_in,
           gdn_conv_w, gdn_a_log, gdn_dt_bias, gdn_norm,
           ssd_conv_w, ssd_conv_b, ssd_a_log, ssd_dt_bias, ssd_d, ssd_norm,
           s5_a_re, s5_a_im, s5_b_re, s5_b_im, s5_c_re, s5_c_im, s5_d, s5_log_dt,
           s5_glu_w, s5_glu_b, w_out, ffn2_norm, ffn2_w_gate, ffn2_w_up, ffn2_w_down,
           final_norm):
    bsz, seq, d = x.shape
    depth = w_in.shape[0]
    h = x
    ffn1_f32 = (ffn1_w_gate, ffn1_w_up, ffn1_w_down)
    ffn2_f32 = (ffn2_w_gate, ffn2_w_up, ffn2_w_down)
    w_next = tuple(_to_bf16(w, 0) for w in ffn1_f32)
    w_in_packed = _pack_w_in(w_in)
    for i in range(depth):
        h, w_next = _ffn(h, ffn1_norm[i], *w_next, final_norm, final_norm=False,
                         cast_next=(*ffn2_f32, i))
        conv_w, conv_b = _pack_conv(gdn_conv_w[i], ssd_conv_w[i], ssd_conv_b[i])
        proj, u_s5 = _inproj(h, mix_norm[i], w_in_packed, i, conv_w, conv_b, bsz, seq)
        o_gdn = _gdn(proj, bsz, seq, gdn_a_log[i], gdn_dt_bias[i], gdn_norm[i])
        o_ssd = _ssd(proj, bsz, seq, ssd_a_log[i], ssd_dt_bias[i], ssd_d[i], ssd_norm[i])
        ab_re, ab_im, wb, wc = _s5_params(s5_a_re[i], s5_a_im[i], s5_b_re[i], s5_b_im[i],
                                          s5_c_re[i], s5_c_im[i], s5_log_dt[i])
        o_s5 = _s5(u_s5, bsz, wb, wc, ab_re, ab_im, s5_d[i], s5_glu_w[i].astype(BF16),
                   s5_glu_b[i])
        wo = w_out[i].astype(BF16)
        h = _outproj(h, o_gdn, o_ssd, o_s5, wo[0:GDN_QK], wo[GDN_QK:GDN_QK + SSD_INNER],
                     wo[GDN_QK + SSD_INNER:])
        last = i == depth - 1
        if last:
            h = _ffn(h, ffn2_norm[i], *w_next, final_norm, final_norm=True,
                     out_batched=(bsz, seq))
        else:
            h, w_next = _ffn(h, ffn2_norm[i], *w_next, final_norm, final_norm=False,
                             cast_next=(*ffn1_f32, i + 1))
    return h
```

```python
import functools
import math

import jax
import jax.numpy as jnp
from jax import lax
from jax.experimental import pallas as pl
from jax.experimental.pallas import tpu as pltpu

F32 = jnp.float32
BF16 = jnp.bfloat16

RMS_EPS = 1e-6
CHUNK = 64
CONV_K = 4
D_MODEL = 2048
D_FF = 5632

GDN_HEADS = 6
GDN_D = 128
GDN_QK = GDN_HEADS * GDN_D
GDN_QKV = 3 * GDN_QK

SSD_HEADS = 12
SSD_P = 64
SSD_GROUPS = 2
SSD_N = 128
SSD_HPG = SSD_HEADS // SSD_GROUPS
SSD_GW = SSD_HPG * SSD_P
SSD_INNER = SSD_HEADS * SSD_P

S5_WIDTH = 512
S5_GROUP = 16
S5_GROUPS = 32
S5_STATE = 64
S5_CH = S5_GROUPS * S5_STATE

LANE = 128

COL_GDN_QKV = 0
COL_GDN_Z = 2304
COL_SSD_Z = 3072
COL_SSD_X = 3840
COL_SSD_B = 4608
COL_SSD_C = 4864
COL_S5_U = 5120
COL_SSD_DT = 5632
COL_GDN_GATE = 5760
P_PAD = 5888

VMEM_LIMIT = 60 * 1024 * 1024

NN = (((1,), (0,)), ((), ()))
NT = (((1,), (1,)), ((), ()))
TN = (((0,), (0,)), ((), ()))


def _dot(a, b, dims=NN):
    return lax.dot_general(a, b, dims, preferred_element_type=F32)


def _hi_lo(a):
    hi = a.astype(BF16)
    lo = (a - hi.astype(F32)).astype(BF16)
    return hi, lo


def _mm(a, b, dims=NN):
    return _dot(a.astype(BF16), b.astype(BF16), dims)


def _mm01(m01, x, dims=NN):
    hi, lo = _hi_lo(x)
    return _dot(m01, hi, dims) + _dot(m01, lo, dims)


def _mm01_t(x, m01, dims=NN):
    hi, lo = _hi_lo(x)
    return _dot(hi, m01, dims) + _dot(lo, m01, dims)


def _sigmoid(x):
    return 1.0 / (1.0 + jnp.exp(-x))


def _silu(x):
    return x * _sigmoid(x)


def _softplus(x):
    return jnp.maximum(x, 0.0) + jnp.log1p(jnp.exp(-jnp.abs(x)))


def _iota(shape, dim):
    return lax.broadcasted_iota(jnp.int32, shape, dim)


def _ffn_kernel(x_ref, nw_ref, wg_ref, wu_ref, wd_ref, fw_ref, *rest, final_norm, side_cast):
    if side_cast:
        cg_ref, cu_ref, cd_ref, o_ref, og_ref, ou_ref, od_ref, xn_ref = rest
        og_ref[...] = cg_ref[...].astype(BF16)
        ou_ref[...] = cu_ref[...].astype(BF16)
        od_ref[...] = cd_ref[...].astype(BF16)
    else:
        o_ref, xn_ref = rest
    j = pl.program_id(1)

    @pl.when(j == 0)
    def _():
        x = x_ref[...]
        ms = jnp.mean(x * x, axis=-1, keepdims=True)
        xn_ref[...] = (x * lax.rsqrt(ms + RMS_EPS) * nw_ref[...]).astype(BF16)
        o_ref[...] = x

    xn = xn_ref[...]
    g = _dot(xn, wg_ref[...])
    u = _dot(xn, wu_ref[...])
    a = (0.5 * _silu(g) * u).astype(BF16)
    o_ref[...] += _dot(a, wd_ref[...])

    if final_norm:
        @pl.when(j == pl.num_programs(1) - 1)
        def _():
            h = o_ref[...]
            ms = jnp.mean(h * h, axis=-1, keepdims=True)
            o_ref[...] = h * lax.rsqrt(ms + RMS_EPS) * fw_ref[...]


def _cast_kernel(x_ref, o_ref):
    o_ref[...] = x_ref[...].astype(o_ref.dtype)


def _to_bf16(w, layer, *, rows=256):
    _, r, c = w.shape
    return pl.pallas_call(
        _cast_kernel,
        grid=(r // rows,),
        in_specs=[pl.BlockSpec((None, rows, c), lambda i: (layer, i, 0))],
        out_specs=pl.BlockSpec((rows, c), lambda i: (i, 0)),
        out_shape=jax.ShapeDtypeStruct((r, c), BF16),
        compiler_params=pltpu.CompilerParams(
            dimension_semantics=("parallel",), vmem_limit_bytes=VMEM_LIMIT),
        name="to_bf16",
    )(w)


def _ffn(x, nw, wg, wu, wd, fw, *, final_norm, out_batched=None, cast_next=None,
         tm=1024, tf=512):
    d = x.shape[-1]
    t = x.size // d
    f = wg.shape[1]
    ni, nj = t // tm, f // tf
    if x.ndim == 3:
        nseg = x.shape[1] // tm
        x_spec = pl.BlockSpec((None, tm, d), lambda i, j: (i // nseg, i % nseg, 0))
    else:
        x_spec = pl.BlockSpec((tm, d), lambda i, j: (i, 0))
    if out_batched is None:
        out_spec = pl.BlockSpec((tm, d), lambda i, j: (i, 0))
        out_shape = jax.ShapeDtypeStruct((t, d), F32)
    else:
        oseg = out_batched[1] // tm
        out_spec = pl.BlockSpec((None, tm, d), lambda i, j: (i // oseg, i % oseg, 0))
        out_shape = jax.ShapeDtypeStruct((*out_batched, d), F32)
    in_specs = [
        x_spec,
        pl.BlockSpec((1, d), lambda i, j: (0, 0)),
        pl.BlockSpec((d, tf), lambda i, j: (0, j)),
        pl.BlockSpec((d, tf), lambda i, j: (0, j)),
        pl.BlockSpec((tf, d), lambda i, j: (j, 0)),
        pl.BlockSpec((1, d), lambda i, j: (0, 0)),
    ]
    operands = [x, nw.reshape(1, d), wg, wu, wd, fw.reshape(1, d)]
    out_specs, out_shapes = [out_spec], [out_shape]
    if cast_next is not None:
        cg, cu, cd, layer = cast_next
        dr = d // ni
        assert dr * ni == d and dr % LANE == 0, (d, ni)
        in_specs += [pl.BlockSpec((None, dr, tf), lambda i, j: (layer, i, j)),
                     pl.BlockSpec((None, dr, tf), lambda i, j: (layer, i, j)),
                     pl.BlockSpec((None, tf, dr), lambda i, j: (layer, j, i))]
        operands += [cg, cu, cd]
        out_specs += [pl.BlockSpec((dr, tf), lambda i, j: (i, j)),
                      pl.BlockSpec((dr, tf), lambda i, j: (i, j)),
                      pl.BlockSpec((tf, dr), lambda i, j: (j, i))]
        out_shapes += [jax.ShapeDtypeStruct((d, f), BF16), jax.ShapeDtypeStruct((d, f), BF16),
                       jax.ShapeDtypeStruct((f, d), BF16)]
    res = pl.pallas_call(
        functools.partial(_ffn_kernel, final_norm=final_norm, side_cast=cast_next is not None),
        grid=(ni, nj),
        in_specs=in_specs,
        out_specs=out_specs,
        out_shape=out_shapes,
        scratch_shapes=[pltpu.VMEM((tm, d), BF16)],
        compiler_params=pltpu.CompilerParams(
            dimension_semantics=("parallel", "arbitrary"), vmem_limit_bytes=VMEM_LIMIT),
        name="ffn",
    )(*operands)
    return (res[0], tuple(res[1:])) if cast_next is not None else res[0]


CONV_RANGES = ((COL_SSD_X, SSD_INNER), (COL_SSD_B, SSD_GROUPS * SSD_N),
               (COL_SSD_C, SSD_GROUPS * SSD_N), (COL_GDN_QKV, GDN_QKV))
INPROJ_PIECE = 256


def _inproj_kernel(x_ref, nw_ref, w_ref, cw_ref, cb_ref, o_ref, u5_ref, stage_ref, *, nseg):
    tm = x_ref.shape[0]

    @pl.when(pl.program_id(0) % nseg == 0)
    def _():
        stage_ref[:, 0:8, :] = jnp.zeros((stage_ref.shape[0], 8, LANE), F32)

    x = x_ref[...]
    ms = jnp.mean(x * x, axis=-1, keepdims=True)
    xn = (x * lax.rsqrt(ms + RMS_EPS) * nw_ref[...]).astype(BF16)
    starts = list(range(0, o_ref.shape[1], INPROJ_PIECE))
    piece = lambda col: _dot(xn, w_ref[:, col:col + INPROJ_PIECE])
    nxt = piece(starts[0])
    for idx, col in enumerate(starts):
        cols = slice(col, col + INPROJ_PIECE)
        res, nxt = nxt, (piece(starts[idx + 1]) if idx + 1 < len(starts) else None)
        if any(lo <= col < lo + width for lo, width in CONV_RANGES):
            for lt in range(INPROJ_PIECE // LANE):
                ti = col // LANE + lt
                ln = slice(lt * LANE, (lt + 1) * LANE)
                gl = slice(col + lt * LANE, col + (lt + 1) * LANE)
                stage_ref[ti, 8:, :] = res[:, ln]
                acc = None
                for k in range(CONV_K):
                    lo = k + 8 - (CONV_K - 1)
                    term = stage_ref[ti, pl.ds(lo, tm, stride=1), :] * cw_ref[k:k + 1, gl]
                    acc = term if acc is None else acc + term
                stage_ref[ti, 0:8, :] = res[tm - 8:, ln]
                o_ref[:, gl] = _silu(acc + cb_ref[:, gl])
        else:
            o_ref[:, cols] = res
        if COL_S5_U <= col < COL_S5_U + S5_WIDTH:
            u5_ref[:, col - COL_S5_U:col - COL_S5_U + INPROJ_PIECE] = res


def _inproj(x, nw, w, layer, conv_w, conv_b, bsz, seq, *, tm=256):
    t, d = x.shape
    n = w.shape[2]
    nseg = seq // tm
    assert nseg * tm == seq and n % INPROJ_PIECE == 0, (seq, tm, n)
    fixed = lambda i: (0, 0)
    return pl.pallas_call(
        functools.partial(_inproj_kernel, nseg=nseg),
        grid=(t // tm,),
        in_specs=[
            pl.BlockSpec((tm, d), lambda i: (i, 0)),
            pl.BlockSpec((1, d), fixed),
            pl.BlockSpec((None, d, n), lambda i: (layer, 0, 0), pipeline_mode=pl.Buffered(1)),
            pl.BlockSpec((CONV_K, n), fixed),
            pl.BlockSpec((1, n), fixed),
        ],
        out_specs=[
            pl.BlockSpec((tm, n), lambda i: (i, 0)),
            pl.BlockSpec((tm, S5_WIDTH), lambda i: (i % nseg, i // nseg)),
        ],
        out_shape=[jax.ShapeDtypeStruct((t, n), F32),
                   jax.ShapeDtypeStruct((seq, bsz * S5_WIDTH), F32)],
        scratch_shapes=[pltpu.VMEM((n // LANE, tm + 8, LANE), F32)],
        compiler_params=pltpu.CompilerParams(
            dimension_semantics=("arbitrary",), vmem_limit_bytes=VMEM_LIMIT),
        name="inproj",
    )(x, nw.reshape(1, d), w, conv_w, conv_b)


def _outproj_kernel(h_ref, a_ref, b_ref, c_ref, wa_ref, wb_ref, wc_ref, o_ref):
    acc = _dot(a_ref[...], wa_ref[...])
    acc += _dot(b_ref[...], wb_ref[...])
    acc += _dot(c_ref[...], wc_ref[...])
    o_ref[...] = h_ref[...] + acc


def _outproj(h, oa, ob, oc, wa, wb, wc, *, tm=512):
    t, d = h.shape
    nseg = oc.shape[0] // tm
    row = lambda i: (i, 0)
    fixed = lambda i: (0, 0)
    return pl.pallas_call(
        _outproj_kernel,
        grid=(t // tm,),
        in_specs=[
            pl.BlockSpec((tm, d), row),
            pl.BlockSpec((tm, oa.shape[1]), row),
            pl.BlockSpec((tm, ob.shape[1]), row),
            pl.BlockSpec((tm, S5_WIDTH), lambda i: (i % nseg, i // nseg)),
            pl.BlockSpec(wa.shape, fixed),
            pl.BlockSpec(wb.shape, fixed),
            pl.BlockSpec(wc.shape, fixed),
        ],
        out_specs=pl.BlockSpec((tm, d), row),
        out_shape=jax.ShapeDtypeStruct((t, d), F32),
        compiler_params=pltpu.CompilerParams(
            dimension_semantics=("parallel",), vmem_limit_bytes=VMEM_LIMIT),
        name="outproj",
    )(h, oa, ob, oc, wa, wb, wc)


def _gdn_kernel(gpar_ref,
                q_ref, k_ref, v_ref, z_ref, gate_ref,
                nw_ref,
                o_ref,
                u_ref, w_ref, qd_ref, kd_ref, at_ref, eg_ref,
                *, hb, unroll):
    hg = pl.program_id(1)
    seq = q_ref.shape[0]
    n_chunks = seq // CHUNK
    c = CHUNK

    row = _iota((c, c), 0)
    col = _iota((c, c), 1)
    tril = row >= col
    strict = row > col
    ltri = tril.astype(BF16)
    ones = jnp.ones((c, c), BF16)
    eye = (row == col).astype(F32)
    lane = _iota((c, LANE), 1)
    neg_a = -jnp.exp(gpar_ref[0:1, :])
    dt_bias = gpar_ref[1:2, :]

    def prep_stages(trip):
        cis = [trip * unroll + cc for cc in range(unroll)]
        items = [(cis[cc], hh) for cc in range(unroll) for hh in range(hb)]
        lanes = [slice(hh * LANE, (hh + 1) * LANE) for _, hh in items]
        rows = [pl.ds(pl.multiple_of(ci * c, c), c) for ci, _ in items]

        def chunk_gates(ci):
            gt = gate_ref[pl.ds(pl.multiple_of(ci * c, c), c), :]
            return _sigmoid(gt), neg_a * _softplus(gt + dt_bias)

        gates = [chunk_gates(ci) for ci in cis]

        def stage_a(cc, hh, ln):
            ci = cis[cc]
            h = hg * hb + hh
            rw = pl.ds(pl.multiple_of(ci * c, c), c)
            q, k, v = q_ref[rw, ln], k_ref[rw, ln], v_ref[rw, ln]
            qn = q * lax.rsqrt(jnp.sum(q * q, axis=-1, keepdims=True) + RMS_EPS) * (GDN_D ** -0.5)
            kn = k * lax.rsqrt(jnp.sum(k * k, axis=-1, keepdims=True) + RMS_EPS)
            beta_all, g_all = gates[cc]
            beta = jnp.sum(jnp.where(lane == h, beta_all, 0.0), axis=-1, keepdims=True)
            g = jnp.sum(jnp.where(lane == h + GDN_HEADS, g_all, 0.0), axis=-1, keepdims=True)
            return qn, kn, v, beta, jnp.broadcast_to(g, (c, LANE))

        sa = [stage_a(cc, hh, slice(hh * LANE, (hh + 1) * LANE))
              for cc in range(unroll) for hh in range(hb)]
        qn = [x[0] for x in sa]
        kn = [x[1] for x in sa]
        v = [x[2] for x in sa]
        beta = [x[3] for x in sa]
        g_b = [x[4] for x in sa]
        yield
        gi = [_mm01(ltri, g) for g in g_b]
        gj = [_mm01(ones, jnp.where(row <= col, g[:, :c], 0.0)) for g in g_b]
        yield
        kb = [k * b for k, b in zip(kn, beta)]
        kk = [_mm(a, b, NT) for a, b in zip(kb, kn)]
        qk = [_mm(a, b, NT) for a, b in zip(qn, kn)]
        decay = [jnp.where(tril, jnp.exp(a[:, :c] - b), 0.0) for a, b in zip(gi, gj)]
        npow = [-jnp.where(strict, a * d, 0.0) for a, d in zip(kk, decay)]
        tinv = [eye + n for n in npow]
        for _ in range(5):
            yield
            npow = [_mm(n, n, NN) for n in npow]
            tinv = [t + _mm(t, n, NN) for t, n in zip(tinv, npow)]
        yield
        exp_g = [jnp.exp(g) for g in gi]
        rhs = [jnp.concatenate([vv * b, k * e], axis=1) for vv, b, k, e in zip(v, beta, kb, exp_g)]
        uw = [_mm(t, r, NN) for t, r in zip(tinv, rhs)]
        for i, ((ci, hh), rw) in enumerate(zip(items, rows)):
            g_last = jnp.broadcast_to(gi[i][c - 1:c, :], (c, LANE))
            u_ref[hh, rw, :] = uw[i][:, :LANE]
            w_ref[hh, rw, :] = uw[i][:, LANE:].astype(BF16)
            at_ref[hh, rw, :] = jnp.where(tril, qk[i] * decay[i], 0.0).astype(BF16)
            qd_ref[hh, rw, :] = (qn[i] * exp_g[i]).astype(BF16)
            kd_ref[hh, rw, :] = (kn[i] * jnp.exp(g_last - gi[i])).astype(BF16)
            eg_ref[hh, pl.ds(pl.multiple_of(ci * 8, 8), 8), :] = jnp.exp(g_last[0:8, :])

    norm_w = nw_ref[...]

    def scan_stages(trip, states, final):
        for cc in range(unroll):
            ci = trip * unroll + cc
            rw = pl.ds(pl.multiple_of(ci * c, c), c)
            lhs = [jnp.concatenate([w_ref[hh, rw, :], qd_ref[hh, rw, :]], axis=0)
                   for hh in range(hb)]
            ws_qs = [_mm(a, s, NN) for a, s in zip(lhs, states)]
            yield
            v_new = [u_ref[hh, rw, :] - ws_qs[hh][:c] for hh in range(hb)]
            kv = [_mm(kd_ref[hh, rw, :], v_new[hh], TN) for hh in range(hb)]
            av = [_mm(at_ref[hh, rw, :], v_new[hh], NN) for hh in range(hb)]
            new_states = []
            for hh in range(hb):
                eg = eg_ref[hh, pl.ds(pl.multiple_of(ci * 8, 8), 8), :]
                new_states.append(states[hh] * jnp.broadcast_to(eg[0:1, :], (GDN_D, GDN_D)) + kv[hh])
                o = ws_qs[hh][c:] + av[hh]
                ms = jnp.mean(o * o, axis=-1, keepdims=True)
                ln = slice(hh * LANE, (hh + 1) * LANE)
                y = o * lax.rsqrt(ms + RMS_EPS) * norm_w * _silu(z_ref[rw, ln])
                o_ref[rw, ln] = y.astype(o_ref.dtype)
            states = new_states
            yield
        final.extend(states)

    def drain(gen):
        for _ in gen:
            pass

    def fused(trip, states):
        final = []
        scan_gen = scan_stages(trip - 1, states, final)
        for _ in prep_stages(trip):
            next(scan_gen, None)
        drain(scan_gen)
        return tuple(final)

    n_trips = n_chunks // unroll
    drain(prep_stages(0))
    states = lax.fori_loop(1, n_trips, fused,
                           tuple(jnp.zeros((GDN_D, GDN_D), F32) for _ in range(hb)))
    drain(scan_stages(n_trips - 1, states, []))


def _gdn(proj, bsz, seq, a_log, dt_bias, norm_w, *, hb=3, unroll=4):
    t = proj.shape[0]
    unroll = math.gcd(unroll, seq // CHUNK)
    wd = hb * LANE
    q0 = COL_GDN_QKV // wd
    z0 = COL_GDN_Z // wd
    g0 = COL_GDN_GATE // LANE
    ng = GDN_HEADS // hb
    blk = lambda off: pl.BlockSpec((seq, wd), lambda b, h, off=off: (b, off + h))
    pad = lambda v: jnp.pad(v.astype(F32), (GDN_HEADS, LANE - 2 * GDN_HEADS))
    gpar = jnp.stack([pad(a_log), pad(dt_bias)])
    return pl.pallas_call(
        functools.partial(_gdn_kernel, hb=hb, unroll=unroll),
        grid=(bsz, ng),
        in_specs=[
            pl.BlockSpec((2, LANE), lambda b, h: (0, 0)),
            blk(q0), blk(q0 + ng), blk(q0 + 2 * ng), blk(z0),
            pl.BlockSpec((seq, LANE), lambda b, h: (b, g0)),
            pl.BlockSpec((1, LANE), lambda b, h: (0, 0)),
        ],
        out_specs=pl.BlockSpec((seq, wd), lambda b, h: (b, h)),
        out_shape=jax.ShapeDtypeStruct((t, GDN_QK), BF16),
        scratch_shapes=[
            pltpu.VMEM((hb, seq, LANE), F32),
            pltpu.VMEM((hb, seq, LANE), BF16),
            pltpu.VMEM((hb, seq, LANE), BF16),
            pltpu.VMEM((hb, seq, LANE), BF16),
            pltpu.VMEM((hb, seq, CHUNK), BF16),
            pltpu.VMEM((hb, seq // CHUNK * 8, LANE), F32),
        ],
        compiler_params=pltpu.CompilerParams(
            dimension_semantics=("parallel", "arbitrary"), vmem_limit_bytes=VMEM_LIMIT),
        name="gdn",
    )(gpar, proj, proj, proj, proj, proj, norm_w.reshape(1, LANE))


def _ssd_kernel(z_ref, x_ref, b_ref, c_ref, dt_ref,
                alog_ref, dtb_ref, dsk_ref, nw_ref,
                o_ref,
                s_ref, *, unroll):
    grp = pl.program_id(1)
    seq = x_ref.shape[0]
    n_chunks = seq // CHUNK
    c = CHUNK
    gw = SSD_GW

    row = _iota((c, c), 0)
    col = _iota((c, c), 1)
    ltri = (row >= col).astype(BF16)
    ones = jnp.ones((c, c), BF16)
    rowt = _iota((c, gw), 0)
    colt = jnp.bitwise_and(_iota((c, gw), 1), c - 1)
    tril_t = rowt >= colt
    upper_t = rowt <= colt
    expand = (_iota((LANE, gw), 0) == grp * SSD_HPG + lax.shift_right_logical(_iota((LANE, gw), 1), 6)).astype(BF16)
    last_row = (_iota((8, gw), 0) == 7)
    ones_8n = jnp.ones((8, SSD_N), BF16)

    neg_a = -jnp.exp(alog_ref[...])
    dt_bias = dtb_ref[...]
    d_skip = dsk_ref[...]
    norm_w = nw_ref[...]

    s_ref[...] = jnp.zeros((gw, SSD_N), F32)
    heads = [slice(hh * SSD_P, (hh + 1) * SSD_P) for hh in range(SSD_HPG)]

    def body(trip, carry):
        cis = [trip * unroll + cc for cc in range(unroll)]
        rows = [pl.ds(pl.multiple_of(ci * c, c), c) for ci in cis]
        xs = [x_ref[rw, :] for rw in rows]
        bm = [b_ref[rw, :] for rw in rows]
        cm = [c_ref[rw, :] for rw in rows]
        dt = [_softplus(_mm01_t(dt_ref[rw, :], expand) + dt_bias) for rw in rows]
        a = [d * neg_a for d in dt]
        acum = [_mm01(ltri, x) for x in a]
        acum_j = [_mm01(ones, jnp.where(upper_t, x, 0.0)) for x in a]
        cb = [_mm(x, y, NT) for x, y in zip(cm, bm)]
        lm = [jnp.where(tril_t, jnp.exp(x - y), 0.0) for x, y in zip(acum, acum_j)]
        xdt = [x * d for x, d in zip(xs, dt)]
        y_diag = [jnp.concatenate([_mm(cb[i] * lm[i][:, sl], xdt[i][:, sl], NN)
                                   for sl in heads], axis=1) for i in range(unroll)]
        a_last = [jnp.broadcast_to(x[c - 1:c, :], (c, gw)) for x in acum]
        states = [_mm(xdt[i] * jnp.exp(a_last[i] - acum[i]), bm[i], TN)
                  for i in range(unroll)]
        dec_col = [_mm01_t(jnp.where(last_row, jnp.exp(x[c - 8:, :]), 0.0), ones_8n, TN)
                   for x in acum]
        s_prev = s_ref[...]
        for i in range(unroll):
            y_off = _mm(cm[i], s_prev, NT) * jnp.exp(acum[i])
            s_prev = s_prev * dec_col[i] + states[i]
            y = y_diag[i] + y_off + d_skip * xs[i]
            y = y * _silu(z_ref[rows[i], :])
            ms = jnp.mean(y * y, axis=-1, keepdims=True)
            o_ref[rows[i], :] = (y * lax.rsqrt(ms + RMS_EPS) * norm_w).astype(o_ref.dtype)
        s_ref[...] = s_prev
        return carry

    lax.fori_loop(0, n_chunks // unroll, body, 0)


def _ssd(proj, bsz, seq, a_log, dt_bias, d_skip, norm_w, *, unroll=8):
    t = proj.shape[0]
    gw = SSD_GW
    unroll = math.gcd(unroll, seq // CHUNK)
    rep = lambda v: jnp.repeat(v.astype(F32), SSD_P).reshape(1, SSD_INNER)
    wide = lambda off: pl.BlockSpec((seq, gw), lambda b, g, off=off: (b, off + g))
    lane = lambda off: pl.BlockSpec((seq, LANE), lambda b, g, off=off: (b, off + g))
    chan = pl.BlockSpec((1, gw), lambda b, g: (0, g))
    return pl.pallas_call(
        functools.partial(_ssd_kernel, unroll=unroll),
        grid=(bsz, SSD_GROUPS),
        in_specs=[
            wide(COL_SSD_Z // gw), wide(COL_SSD_X // gw),
            lane(COL_SSD_B // LANE), lane(COL_SSD_C // LANE),
            pl.BlockSpec((seq, LANE), lambda b, g: (b, COL_SSD_DT // LANE)),
            chan, chan, chan, chan,
        ],
        out_specs=pl.BlockSpec((seq, gw), lambda b, g: (b, g)),
        out_shape=jax.ShapeDtypeStruct((t, SSD_INNER), BF16),
        scratch_shapes=[pltpu.VMEM((gw, SSD_N), F32)],
        compiler_params=pltpu.CompilerParams(
            dimension_semantics=("parallel", "arbitrary"), vmem_limit_bytes=VMEM_LIMIT),
        name="ssd",
    )(proj, proj, proj, proj, proj,
      rep(a_log), rep(dt_bias), rep(d_skip), norm_w.reshape(1, SSD_INNER))


def _s5_kernel(u_ref, wb_ref, wc_ref, are_ref, aim_ref, dsk_ref, gw_ref, gb_ref,
               o_ref, bu_ref, h_ref, *, bsz):
    ts = u_ref.shape[0]
    nt = S5_CH // LANE
    nb = S5_WIDTH // LANE
    per = nt // nb
    half = per * LANE

    @pl.when(pl.program_id(0) == 0)
    def _():
        h_ref[...] = jnp.zeros(h_ref.shape, F32)

    u = u_ref[...].reshape(ts * bsz, S5_WIDTH)
    u_bf = u.astype(BF16)
    for m in range(nb):
        res = _dot(u_bf[:, m * LANE:(m + 1) * LANE], wb_ref[m])
        for k in range(per):
            bu_ref[m * per + k] = res[:, k * LANE:(k + 1) * LANE]
            bu_ref[nt + m * per + k] = res[:, half + k * LANE:half + (k + 1) * LANE]

    def step(t, carry):
        rows = pl.ds(pl.multiple_of(t * bsz, bsz), bsz)
        new_re, new_im = [], []
        for k in range(nt):
            h_re, h_im = carry[k], carry[nt + k]
            a_re = are_ref[:, k * LANE:(k + 1) * LANE]
            a_im = aim_ref[:, k * LANE:(k + 1) * LANE]
            n_re = a_re * h_re - a_im * h_im + bu_ref[k, rows, :]
            n_im = a_re * h_im + a_im * h_re + bu_ref[nt + k, rows, :]
            bu_ref[k, rows, :] = n_re
            bu_ref[nt + k, rows, :] = n_im
            new_re.append(n_re)
            new_im.append(n_im)
        return tuple(new_re + new_im)

    h_last = lax.fori_loop(0, ts, step, tuple(h_ref[k] for k in range(2 * nt)))
    for k in range(2 * nt):
        h_ref[k] = h_last[k]

    y_parts = []
    for m in range(nb):
        hs = jnp.concatenate([bu_ref[m * per + k] for k in range(per)] +
                             [bu_ref[nt + m * per + k] for k in range(per)], axis=1)
        y_parts.append(_dot(hs.astype(BF16), wc_ref[m]))
    y = jnp.concatenate(y_parts, axis=1) + dsk_ref[...] * u
    g = 0.5 * y * (1.0 + jnp.tanh(math.sqrt(2.0 / math.pi) * (y + 0.044715 * (y * y * y))))
    out = g * _sigmoid(_dot(g.astype(BF16), gw_ref[...]) + gb_ref[...])
    o_ref[...] = out.reshape(ts, bsz * S5_WIDTH).astype(o_ref.dtype)


def _s5(u_tb, bsz, wb, wc, a_re, a_im, d_skip, glu_w, glu_b, *, ts=128):
    seq = u_tb.shape[0]
    blk = ts * bsz
    fixed = lambda shape: pl.BlockSpec(shape, lambda k: (0,) * len(shape))
    return pl.pallas_call(
        functools.partial(_s5_kernel, bsz=bsz),
        grid=(seq // ts,),
        in_specs=[
            pl.BlockSpec((ts, bsz * S5_WIDTH), lambda k: (k, 0)),
            fixed(wb.shape), fixed(wc.shape),
            fixed((1, S5_CH)), fixed((1, S5_CH)),
            fixed((1, S5_WIDTH)), fixed(glu_w.shape), fixed((1, S5_WIDTH)),
        ],
        out_specs=pl.BlockSpec((ts, bsz * S5_WIDTH), lambda k: (k, 0)),
        out_shape=jax.ShapeDtypeStruct((seq, bsz * S5_WIDTH), BF16),
        scratch_shapes=[
            pltpu.VMEM((2 * S5_CH // LANE, blk, LANE), F32),
            pltpu.VMEM((2 * S5_CH // LANE, bsz, LANE), F32),
        ],
        compiler_params=pltpu.CompilerParams(
            dimension_semantics=("arbitrary",), vmem_limit_bytes=VMEM_LIMIT),
        name="s5",
    )(u_tb, wb, wc, a_re.reshape(1, S5_CH), a_im.reshape(1, S5_CH),
      d_skip.reshape(1, S5_WIDTH), glu_w, glu_b.reshape(1, S5_WIDTH))


def _s5_params(a_re, a_im, b_re, b_im, c_re, c_im, log_dt):
    delta = jnp.exp(log_dt)[:, None]
    mag = jnp.exp(a_re * delta)
    ab_re, ab_im = mag * jnp.cos(a_im * delta), mag * jnp.sin(a_im * delta)
    den = a_re * a_re + a_im * a_im
    p_re, p_im = ab_re - 1.0, ab_im
    f_re = (p_re * a_re + p_im * a_im) / den
    f_im = (p_im * a_re - p_re * a_im) / den
    bb_re = f_re[..., None] * b_re - f_im[..., None] * b_im
    bb_im = f_re[..., None] * b_im + f_im[..., None] * b_re
    gpb = LANE // S5_GROUP
    nb = S5_GROUPS // gpb
    eye = jnp.eye(gpb, dtype=F32)
    emb_b = lambda bb: jnp.einsum('mgnj,gh->mgjhn', bb.reshape(nb, gpb, S5_STATE, S5_GROUP),
                                  eye).reshape(nb, LANE, gpb * S5_STATE)
    wb = jnp.concatenate([emb_b(bb_re), emb_b(bb_im)], axis=2)
    emb_c = lambda cc: jnp.einsum('mgin,gh->mgnhi', cc.reshape(nb, gpb, S5_GROUP, S5_STATE),
                                  eye).reshape(nb, gpb * S5_STATE, LANE)
    wc = jnp.concatenate([emb_c(c_re), -emb_c(c_im)], axis=1)
    return ab_re.reshape(-1), ab_im.reshape(-1), wb.astype(BF16), wc.astype(BF16)


def _w_in_segments():
    bc = SSD_GROUPS * SSD_N
    g_z = GDN_QKV
    g_gate = g_z + GDN_QK
    s_z = g_gate + 2 * GDN_HEADS
    s_x = s_z + SSD_INNER
    s_b = s_x + SSD_INNER
    s_c = s_b + bc
    s_dt = s_c + bc
    s5_u = s_dt + SSD_HEADS
    return ((0, GDN_QKV, COL_GDN_QKV), (g_z, GDN_QK, COL_GDN_Z),
            (g_gate, 2 * GDN_HEADS, COL_GDN_GATE), (s_z, SSD_INNER, COL_SSD_Z),
            (s_x, SSD_INNER, COL_SSD_X), (s_b, bc, COL_SSD_B), (s_c, bc, COL_SSD_C),
            (s_dt, SSD_HEADS, COL_SSD_DT), (s5_u, S5_WIDTH, COL_S5_U))


def _pack_conv(gdn_w, ssd_w, ssd_b):
    bc = SSD_GROUPS * SSD_N
    pieces = ((COL_GDN_QKV, gdn_w, None),
              (COL_SSD_X, ssd_w[:, :SSD_INNER], ssd_b[:SSD_INNER]),
              (COL_SSD_B, ssd_w[:, SSD_INNER:SSD_INNER + bc], ssd_b[SSD_INNER:SSD_INNER + bc]),
              (COL_SSD_C, ssd_w[:, SSD_INNER + bc:], ssd_b[SSD_INNER + bc:]))
    cw = jnp.zeros((CONV_K, P_PAD), F32)
    cb = jnp.zeros((1, P_PAD), F32)
    for col, w, b in pieces:
        cw = cw.at[:, col:col + w.shape[1]].set(w.astype(F32))
        if b is not None:
            cb = cb.at[0, col:col + b.shape[0]].set(b.astype(F32))
    return cw, cb


def _pack_kernel(x_ref, o_ref):
    for col in (COL_SSD_DT, COL_GDN_GATE):
        o_ref[:, col:col + LANE] = jnp.zeros((o_ref.shape[0], LANE), o_ref.dtype)
    for src, width, dst in _w_in_segments():
        o_ref[:, dst:dst + width] = x_ref[:, src:src + width].astype(o_ref.dtype)


def _pack_w_in(w_in, *, rows=256):
    nl, d, c = w_in.shape
    return pl.pallas_call(
        _pack_kernel,
        grid=(nl, d // rows),
        in_specs=[pl.BlockSpec((None, rows, c), lambda l, i: (l, i, 0))],
        out_specs=pl.BlockSpec((None, rows, P_PAD), lambda l, i: (l, i, 0)),
        out_shape=jax.ShapeDtypeStruct((nl, d, P_PAD), BF16),
        compiler_params=pltpu.CompilerParams(
            dimension_semantics=("parallel", "parallel"), vmem_limit_bytes=VMEM_LIMIT),
        name="pack_w_in",
    )(w_in)


def kernel(x, ffn1_norm, ffn1_w_gate, ffn1_w_up, ffn1_w_down, mix_norm, w_in,
           gdn_conv_w, gdn_a_log, gdn_dt_bias, gdn_norm,
           ssd_conv_w, ssd_conv_b, ssd_a_log, ssd_dt_bias, ssd_d, ssd_norm,
           s5_a_re, s5_a_im, s5_b_re, s5_b_im, s5_c_re, s5_c_im, s5_d, s5_log_dt,
           s5_glu_w, s5_glu_b, w_out, ffn2_norm, ffn2_w_gate, ffn2_w_up, ffn2_w_down,
           final_norm):
    bsz, seq, d = x.shape
    depth = w_in.shape[0]
    h = x
    ffn1_f32 = (ffn1_w_gate, ffn1_w_up, ffn1_w_down)
    ffn2_f32 = (ffn2_w_gate, ffn2_w_up, ffn2_w_down)
    w_next = tuple(_to_bf16(w, 0) for w in ffn1_f32)
    w_in_packed = _pack_w_in(w_in)
    for i in range(depth):
        h, w_next = _ffn(h, ffn1_norm[i], *w_next, final_norm, final_norm=False,
                         cast_next=(*ffn2_f32, i))
        conv_w, conv_b = _pack_conv(gdn_conv_w[i], ssd_conv_w[i], ssd_conv_b[i])
        proj, u_s5 = _inproj(h, mix_norm[i], w_in_packed, i, conv_w, conv_b, bsz, seq)
        o_gdn = _gdn(proj, bsz, seq, gdn_a_log[i], gdn_dt_bias[i], gdn_norm[i])
        o_ssd = _ssd(proj, bsz, seq, ssd_a_log[i], ssd_dt_bias[i], ssd_d[i], ssd_norm[i])
        ab_re, ab_im, wb, wc = _s5_params(s5_a_re[i], s5_a_im[i], s5_b_re[i], s5_b_im[i],
                                          s5_c_re[i], s5_c_im[i], s5_log_dt[i])
        o_s5 = _s5(u_s5, bsz, wb, wc, ab_re, ab_im, s5_d[i], s5_glu_w[i].astype(BF16),
                   s5_glu_b[i])
        wo = w_out[i].astype(BF16)
        h = _outproj(h, o_gdn, o_ssd, o_s5, wo[0:GDN_QK], wo[GDN_QK:GDN_QK + SSD_INNER],
                     wo[GDN_QK + SSD_INNER:])
        last = i == depth - 1
        if last:
            h = _ffn(h, ffn2_norm[i], *w_next, final_norm, final_norm=True,
                     out_batched=(bsz, seq))
        else:
            h, w_next = _ffn(h, ffn2_norm[i], *w_next, final_norm, final_norm=False,
                             cast_next=(*ffn1_f32, i + 1))
    return h
```

```python
import functools
import math

import jax
import jax.numpy as jnp
from jax import lax
from jax.experimental import pallas as pl
from jax.experimental.pallas import tpu as pltpu

F32 = jnp.float32
BF16 = jnp.bfloat16

RMS_EPS = 1e-6
CHUNK = 64
CONV_K = 4
D_MODEL = 2048
D_FF = 5632

GDN_HEADS = 6
GDN_D = 128
GDN_QK = GDN_HEADS * GDN_D
GDN_QKV = 3 * GDN_QK

SSD_HEADS = 12
SSD_P = 64
SSD_GROUPS = 2
SSD_N = 128
SSD_HPG = SSD_HEADS // SSD_GROUPS
SSD_GW = SSD_HPG * SSD_P
SSD_INNER = SSD_HEADS * SSD_P

S5_WIDTH = 512
S5_GROUP = 16
S5_GROUPS = 32
S5_STATE = 64
S5_CH = S5_GROUPS * S5_STATE

LANE = 128

COL_GDN_QKV = 0
COL_GDN_Z = 2304
COL_SSD_Z = 3072
COL_SSD_X = 3840
COL_SSD_B = 4608
COL_SSD_C = 4864
COL_S5_U = 5120
COL_SSD_DT = 5632
COL_GDN_GATE = 5760
P_PAD = 5888

VMEM_LIMIT = 60 * 1024 * 1024

NN = (((1,), (0,)), ((), ()))
NT = (((1,), (1,)), ((), ()))
TN = (((0,), (0,)), ((), ()))


def _dot(a, b, dims=NN):
    return lax.dot_general(a, b, dims, preferred_element_type=F32)


def _hi_lo(a):
    hi = a.astype(BF16)
    lo = (a - hi.astype(F32)).astype(BF16)
    return hi, lo


def _mm(a, b, dims=NN):
    return _dot(a.astype(BF16), b.astype(BF16), dims)


def _mm01(m01, x, dims=NN):
    hi, lo = _hi_lo(x)
    return _dot(m01, hi, dims) + _dot(m01, lo, dims)


def _mm01_t(x, m01, dims=NN):
    hi, lo = _hi_lo(x)
    return _dot(hi, m01, dims) + _dot(lo, m01, dims)


def _sigmoid(x):
    return 1.0 / (1.0 + jnp.exp(-x))


def _silu(x):
    return x * _sigmoid(x)


def _softplus(x):
    return jnp.maximum(x, 0.0) + jnp.log1p(jnp.exp(-jnp.abs(x)))


def _iota(shape, dim):
    return lax.broadcasted_iota(jnp.int32, shape, dim)


def _ffn_kernel(x_ref, nw_ref, wg_ref, wu_ref, wd_ref, fw_ref, *rest, final_norm, side_cast):
    if side_cast:
        cg_ref, cu_ref, cd_ref, o_ref, og_ref, ou_ref, od_ref, xn_ref = rest
        og_ref[...] = cg_ref[...].astype(BF16)
        ou_ref[...] = cu_ref[...].astype(BF16)
        od_ref[...] = cd_ref[...].astype(BF16)
    else:
        o_ref, xn_ref = rest
    j = pl.program_id(1)

    @pl.when(j == 0)
    def _():
        x = x_ref[...]
        ms = jnp.mean(x * x, axis=-1, keepdims=True)
        xn_ref[...] = (x * lax.rsqrt(ms + RMS_EPS) * nw_ref[...]).astype(BF16)
        o_ref[...] = x

    xn = xn_ref[...]
    g = _dot(xn, wg_ref[...])
    u = _dot(xn, wu_ref[...])
    a = (0.5 * _silu(g) * u).astype(BF16)
    o_ref[...] += _dot(a, wd_ref[...])

    if final_norm:
        @pl.when(j == pl.num_programs(1) - 1)
        def _():
            h = o_ref[...]
            ms = jnp.mean(h * h, axis=-1, keepdims=True)
            o_ref[...] = h * lax.rsqrt(ms + RMS_EPS) * fw_ref[...]


def _cast_kernel(x_ref, o_ref):
    o_ref[...] = x_ref[...].astype(o_ref.dtype)


def _to_bf16(w, layer, *, rows=256):
    _, r, c = w.shape
    return pl.pallas_call(
        _cast_kernel,
        grid=(r // rows,),
        in_specs=[pl.BlockSpec((None, rows, c), lambda i: (layer, i, 0))],
        out_specs=pl.BlockSpec((rows, c), lambda i: (i, 0)),
        out_shape=jax.ShapeDtypeStruct((r, c), BF16),
        compiler_params=pltpu.CompilerParams(
            dimension_semantics=("parallel",), vmem_limit_bytes=VMEM_LIMIT),
        name="to_bf16",
    )(w)


def _ffn(x, nw, wg, wu, wd, fw, *, final_norm, out_batched=None, cast_next=None,
         tm=1024, tf=512):
    d = x.shape[-1]
    t = x.size // d
    f = wg.shape[1]
    ni, nj = t // tm, f // tf
    if x.ndim == 3:
        nseg = x.shape[1] // tm
        x_spec = pl.BlockSpec((None, tm, d), lambda i, j: (i // nseg, i % nseg, 0))
    else:
        x_spec = pl.BlockSpec((tm, d), lambda i, j: (i, 0))
    if out_batched is None:
        out_spec = pl.BlockSpec((tm, d), lambda i, j: (i, 0))
        out_shape = jax.ShapeDtypeStruct((t, d), F32)
    else:
        oseg = out_batched[1] // tm
        out_spec = pl.BlockSpec((None, tm, d), lambda i, j: (i // oseg, i % oseg, 0))
        out_shape = jax.ShapeDtypeStruct((*out_batched, d), F32)
    in_specs = [
        x_spec,
        pl.BlockSpec((1, d), lambda i, j: (0, 0)),
        pl.BlockSpec((d, tf), lambda i, j: (0, j)),
        pl.BlockSpec((d, tf), lambda i, j: (0, j)),
        pl.BlockSpec((tf, d), lambda i, j: (j, 0)),
        pl.BlockSpec((1, d), lambda i, j: (0, 0)),
    ]
    operands = [x, nw.reshape(1, d), wg, wu, wd, fw.reshape(1, d)]
    out_specs, out_shapes = [out_spec], [out_shape]
    if cast_next is not None:
        cg, cu, cd, layer = cast_next
        dr = d // ni
        assert dr * ni == d and dr % LANE == 0, (d, ni)
        in_specs += [pl.BlockSpec((None, dr, tf), lambda i, j: (layer, i, j)),
                     pl.BlockSpec((None, dr, tf), lambda i, j: (layer, i, j)),
                     pl.BlockSpec((None, tf, dr), lambda i, j: (layer, j, i))]
        operands += [cg, cu, cd]
        out_specs += [pl.BlockSpec((dr, tf), lambda i, j: (i, j)),
                      pl.BlockSpec((dr, tf), lambda i, j: (i, j)),
                      pl.BlockSpec((tf, dr), lambda i, j: (j, i))]
        out_shapes += [jax.ShapeDtypeStruct((d, f), BF16), jax.ShapeDtypeStruct((d, f), BF16),
                       jax.ShapeDtypeStruct((f, d), BF16)]
    res = pl.pallas_call(
        functools.partial(_ffn_kernel, final_norm=final_norm, side_cast=cast_next is not None),
        grid=(ni, nj),
        in_specs=in_specs,
        out_specs=out_specs,
        out_shape=out_shapes,
        scratch_shapes=[pltpu.VMEM((tm, d), BF16)],
        compiler_params=pltpu.CompilerParams(
            dimension_semantics=("parallel", "arbitrary"), vmem_limit_bytes=VMEM_LIMIT),
        name="ffn",
    )(*operands)
    return (res[0], tuple(res[1:])) if cast_next is not None else res[0]


CONV_RANGES = ((COL_SSD_X, SSD_INNER), (COL_SSD_B, SSD_GROUPS * SSD_N),
               (COL_SSD_C, SSD_GROUPS * SSD_N), (COL_GDN_QKV, GDN_QKV))
INPROJ_PIECE = 256


def _inproj_kernel(x_ref, nw_ref, w_ref, cw_ref, cb_ref, o_ref, u5_ref, stage_ref, *, nseg):
    tm = x_ref.shape[0]

    @pl.when(pl.program_id(0) % nseg == 0)
    def _():
        stage_ref[:, 0:8, :] = jnp.zeros((stage_ref.shape[0], 8, LANE), F32)

    x = x_ref[...]
    ms = jnp.mean(x * x, axis=-1, keepdims=True)
    xn = (x * lax.rsqrt(ms + RMS_EPS) * nw_ref[...]).astype(BF16)
    starts = list(range(0, o_ref.shape[1], INPROJ_PIECE))
    piece = lambda col: _dot(xn, w_ref[:, col:col + INPROJ_PIECE])
    nxt = piece(starts[0])
    for idx, col in enumerate(starts):
        cols = slice(col, col + INPROJ_PIECE)
        res, nxt = nxt, (piece(starts[idx + 1]) if idx + 1 < len(starts) else None)
        if any(lo <= col < lo + width for lo, width in CONV_RANGES):
            for lt in range(INPROJ_PIECE // LANE):
                ti = col // LANE + lt
                ln = slice(lt * LANE, (lt + 1) * LANE)
                gl = slice(col + lt * LANE, col + (lt + 1) * LANE)
                stage_ref[ti, 8:, :] = res[:, ln]
                acc = None
                for k in range(CONV_K):
                    lo = k + 8 - (CONV_K - 1)
                    term = stage_ref[ti, pl.ds(lo, tm, stride=1), :] * cw_ref[k:k + 1, gl]
                    acc = term if acc is None else acc + term
                stage_ref[ti, 0:8, :] = res[tm - 8:, ln]
                o_ref[:, gl] = _silu(acc + cb_ref[:, gl])
        else:
            o_ref[:, cols] = res
        if COL_S5_U <= col < COL_S5_U + S5_WIDTH:
            u5_ref[:, col - COL_S5_U:col - COL_S5_U + INPROJ_PIECE] = res


def _inproj(x, nw, w, layer, conv_w, conv_b, bsz, seq, *, tm=256):
    t, d = x.shape
    n = w.shape[2]
    nseg = seq // tm
    assert nseg * tm == seq and n % INPROJ_PIECE == 0, (seq, tm, n)
    fixed = lambda i: (0, 0)
    return pl.pallas_call(
        functools.partial(_inproj_kernel, nseg=nseg),
        grid=(t // tm,),
        in_specs=[
            pl.BlockSpec((tm, d), lambda i: (i, 0)),
            pl.BlockSpec((1, d), fixed),
            pl.BlockSpec((None, d, n), lambda i: (layer, 0, 0), pipeline_mode=pl.Buffered(1)),
            pl.BlockSpec((CONV_K, n), fixed),
            pl.BlockSpec((1, n), fixed),
        ],
        out_specs=[
            pl.BlockSpec((tm, n), lambda i: (i, 0)),
            pl.BlockSpec((tm, S5_WIDTH), lambda i: (i % nseg, i // nseg)),
        ],
        out_shape=[jax.ShapeDtypeStruct((t, n), F32),
                   jax.ShapeDtypeStruct((seq, bsz * S5_WIDTH), F32)],
        scratch_shapes=[pltpu.VMEM((n // LANE, tm + 8, LANE), F32)],
        compiler_params=pltpu.CompilerParams(
            dimension_semantics=("arbitrary",), vmem_limit_bytes=VMEM_LIMIT),
        name="inproj",
    )(x, nw.reshape(1, d), w, conv_w, conv_b)


def _outproj_kernel(h_ref, a_ref, b_ref, c_ref, wa_ref, wb_ref, wc_ref, o_ref):
    acc = _dot(a_ref[...], wa_ref[...])
    acc += _dot(b_ref[...], wb_ref[...])
    acc += _dot(c_ref[...], wc_ref[...])
    o_ref[...] = h_ref[...] + acc


def _outproj(h, oa, ob, oc, wa, wb, wc, *, tm=1024):
    t, d = h.shape
    nseg = oc.shape[0] // tm
    row = lambda i: (i, 0)
    fixed = lambda i: (0, 0)
    return pl.pallas_call(
        _outproj_kernel,
        grid=(t // tm,),
        in_specs=[
            pl.BlockSpec((tm, d), row),
            pl.BlockSpec((tm, oa.shape[1]), row),
            pl.BlockSpec((tm, ob.shape[1]), row),
            pl.BlockSpec((tm, S5_WIDTH), lambda i: (i % nseg, i // nseg)),
            pl.BlockSpec(wa.shape, fixed, pipeline_mode=pl.Buffered(1)),
            pl.BlockSpec(wb.shape, fixed, pipeline_mode=pl.Buffered(1)),
            pl.BlockSpec(wc.shape, fixed, pipeline_mode=pl.Buffered(1)),
        ],
        out_specs=pl.BlockSpec((tm, d), row),
        out_shape=jax.ShapeDtypeStruct((t, d), F32),
        compiler_params=pltpu.CompilerParams(
            dimension_semantics=("parallel",), vmem_limit_bytes=VMEM_LIMIT),
        name="outproj",
    )(h, oa, ob, oc, wa, wb, wc)


def _gdn_kernel(gpar_ref,
                q_ref, k_ref, v_ref, z_ref, gate_ref,
                nw_ref,
                o_ref,
                u_ref, w_ref, qd_ref, kd_ref, at_ref, eg_ref,
                *, hb, unroll):
    hg = pl.program_id(1)
    seq = q_ref.shape[0]
    n_chunks = seq // CHUNK
    c = CHUNK

    row = _iota((c, c), 0)
    col = _iota((c, c), 1)
    tril = row >= col
    strict = row > col
    ltri = tril.astype(BF16)
    ones = jnp.ones((c, c), BF16)
    eye = (row == col).astype(F32)
    lane = _iota((c, LANE), 1)
    neg_a = -jnp.exp(gpar_ref[0:1, :])
    dt_bias = gpar_ref[1:2, :]

    def prep_stages(trip):
        cis = [trip * unroll + cc for cc in range(unroll)]
        items = [(cis[cc], hh) for cc in range(unroll) for hh in range(hb)]
        lanes = [slice(hh * LANE, (hh + 1) * LANE) for _, hh in items]
        rows = [pl.ds(pl.multiple_of(ci * c, c), c) for ci, _ in items]

        def chunk_gates(ci):
            gt = gate_ref[pl.ds(pl.multiple_of(ci * c, c), c), :]
            return _sigmoid(gt), neg_a * _softplus(gt + dt_bias)

        gates = [chunk_gates(ci) for ci in cis]

        def stage_a(cc, hh, ln):
            ci = cis[cc]
            h = hg * hb + hh
            rw = pl.ds(pl.multiple_of(ci * c, c), c)
            q, k, v = q_ref[rw, ln], k_ref[rw, ln], v_ref[rw, ln]
            qn = q * lax.rsqrt(jnp.sum(q * q, axis=-1, keepdims=True) + RMS_EPS) * (GDN_D ** -0.5)
            kn = k * lax.rsqrt(jnp.sum(k * k, axis=-1, keepdims=True) + RMS_EPS)
            beta_all, g_all = gates[cc]
            beta = jnp.sum(jnp.where(lane == h, beta_all, 0.0), axis=-1, keepdims=True)
            g = jnp.sum(jnp.where(lane == h + GDN_HEADS, g_all, 0.0), axis=-1, keepdims=True)
            return qn, kn, v, beta, jnp.broadcast_to(g, (c, LANE))

        sa = [stage_a(cc, hh, slice(hh * LANE, (hh + 1) * LANE))
              for cc in range(unroll) for hh in range(hb)]
        qn = [x[0] for x in sa]
        kn = [x[1] for x in sa]
        v = [x[2] for x in sa]
        beta = [x[3] for x in sa]
        g_b = [x[4] for x in sa]
        yield
        gi = [_mm01(ltri, g) for g in g_b]
        gj = [_mm01(ones, jnp.where(row <= col, g[:, :c], 0.0)) for g in g_b]
        yield
        kb = [k * b for k, b in zip(kn, beta)]
        kk = [_mm(a, b, NT) for a, b in zip(kb, kn)]
        qk = [_mm(a, b, NT) for a, b in zip(qn, kn)]
        decay = [jnp.where(tril, jnp.exp(a[:, :c] - b), 0.0) for a, b in zip(gi, gj)]
        npow = [-jnp.where(strict, a * d, 0.0) for a, d in zip(kk, decay)]
        tinv = [eye + n for n in npow]
        for _ in range(5):
            yield
            npow = [_mm(n, n, NN) for n in npow]
            tinv = [t + _mm(t, n, NN) for t, n in zip(tinv, npow)]
        yield
        exp_g = [jnp.exp(g) for g in gi]
        rhs = [jnp.concatenate([vv * b, k * e], axis=1) for vv, b, k, e in zip(v, beta, kb, exp_g)]
        uw = [_mm(t, r, NN) for t, r in zip(tinv, rhs)]
        for i, ((ci, hh), rw) in enumerate(zip(items, rows)):
            g_last = jnp.broadcast_to(gi[i][c - 1:c, :], (c, LANE))
            u_ref[hh, rw, :] = uw[i][:, :LANE]
            w_ref[hh, rw, :] = uw[i][:, LANE:].astype(BF16)
            at_ref[hh, rw, :] = jnp.where(tril, qk[i] * decay[i], 0.0).astype(BF16)
            qd_ref[hh, rw, :] = (qn[i] * exp_g[i]).astype(BF16)
            kd_ref[hh, rw, :] = (kn[i] * jnp.exp(g_last - gi[i])).astype(BF16)
            eg_ref[hh, pl.ds(pl.multiple_of(ci * 8, 8), 8), :] = jnp.exp(g_last[0:8, :])

    norm_w = nw_ref[...]

    def scan_stages(trip, states, final):
        for cc in range(unroll):
            ci = trip * unroll + cc
            rw = pl.ds(pl.multiple_of(ci * c, c), c)
            lhs = [jnp.concatenate([w_ref[hh, rw, :], qd_ref[hh, rw, :]], axis=0)
                   for hh in range(hb)]
            ws_qs = [_mm(a, s, NN) for a, s in zip(lhs, states)]
            yield
            v_new = [u_ref[hh, rw, :] - ws_qs[hh][:c] for hh in range(hb)]
            kv = [_mm(kd_ref[hh, rw, :], v_new[hh], TN) for hh in range(hb)]
            av = [_mm(at_ref[hh, rw, :], v_new[hh], NN) for hh in range(hb)]
            new_states = []
            for hh in range(hb):
                eg = eg_ref[hh, pl.ds(pl.multiple_of(ci * 8, 8), 8), :]
                new_states.append(states[hh] * jnp.broadcast_to(eg[0:1, :], (GDN_D, GDN_D)) + kv[hh])
                o = ws_qs[hh][c:] + av[hh]
                ms = jnp.mean(o * o, axis=-1, keepdims=True)
                ln = slice(hh * LANE, (hh + 1) * LANE)
                y = o * lax.rsqrt(ms + RMS_EPS) * norm_w * _silu(z_ref[rw, ln])
                o_ref[rw, ln] = y.astype(o_ref.dtype)
            states = new_states
            yield
        final.extend(states)

    def drain(gen):
        for _ in gen:
            pass

    def fused(trip, states):
        final = []
        scan_gen = scan_stages(trip - 1, states, final)
        for _ in prep_stages(trip):
            next(scan_gen, None)
        drain(scan_gen)
        return tuple(final)

    n_trips = n_chunks // unroll
    drain(prep_stages(0))
    states = lax.fori_loop(1, n_trips, fused,
                           tuple(jnp.zeros((GDN_D, GDN_D), F32) for _ in range(hb)))
    drain(scan_stages(n_trips - 1, states, []))


def _gdn(proj, bsz, seq, a_log, dt_bias, norm_w, *, hb=3, unroll=4):
    t = proj.shape[0]
    unroll = math.gcd(unroll, seq // CHUNK)
    wd = hb * LANE
    q0 = COL_GDN_QKV // wd
    z0 = COL_GDN_Z // wd
    g0 = COL_GDN_GATE // LANE
    ng = GDN_HEADS // hb
    blk = lambda off: pl.BlockSpec((seq, wd), lambda b, h, off=off: (b, off + h))
    pad = lambda v: jnp.pad(v.astype(F32), (GDN_HEADS, LANE - 2 * GDN_HEADS))
    gpar = jnp.stack([pad(a_log), pad(dt_bias)])
    return pl.pallas_call(
        functools.partial(_gdn_kernel, hb=hb, unroll=unroll),
        grid=(bsz, ng),
        in_specs=[
            pl.BlockSpec((2, LANE), lambda b, h: (0, 0)),
            blk(q0), blk(q0 + ng), blk(q0 + 2 * ng), blk(z0),
            pl.BlockSpec((seq, LANE), lambda b, h: (b, g0)),
            pl.BlockSpec((1, LANE), lambda b, h: (0, 0)),
        ],
        out_specs=pl.BlockSpec((seq, wd), lambda b, h: (b, h)),
        out_shape=jax.ShapeDtypeStruct((t, GDN_QK), BF16),
        scratch_shapes=[
            pltpu.VMEM((hb, seq, LANE), F32),
            pltpu.VMEM((hb, seq, LANE), BF16),
            pltpu.VMEM((hb, seq, LANE), BF16),
            pltpu.VMEM((hb, seq, LANE), BF16),
            pltpu.VMEM((hb, seq, CHUNK), BF16),
            pltpu.VMEM((hb, seq // CHUNK * 8, LANE), F32),
        ],
        compiler_params=pltpu.CompilerParams(
            dimension_semantics=("parallel", "arbitrary"), vmem_limit_bytes=VMEM_LIMIT),
        name="gdn",
    )(gpar, proj, proj, proj, proj, proj, norm_w.reshape(1, LANE))


def _ssd_kernel(z_ref, x_ref, b_ref, c_ref, dt_ref,
                alog_ref, dtb_ref, dsk_ref, nw_ref,
                o_ref,
                s_ref, *, unroll):
    grp = pl.program_id(1)
    seq = x_ref.shape[0]
    n_chunks = seq // CHUNK
    c = CHUNK
    gw = SSD_GW

    row = _iota((c, c), 0)
    col = _iota((c, c), 1)
    ltri = (row >= col).astype(BF16)
    ones = jnp.ones((c, c), BF16)
    rowt = _iota((c, gw), 0)
    colt = jnp.bitwise_and(_iota((c, gw), 1), c - 1)
    tril_t = rowt >= colt
    upper_t = rowt <= colt
    expand = (_iota((LANE, gw), 0) == grp * SSD_HPG + lax.shift_right_logical(_iota((LANE, gw), 1), 6)).astype(BF16)
    last_row = (_iota((8, gw), 0) == 7)
    ones_8n = jnp.ones((8, SSD_N), BF16)

    neg_a = -jnp.exp(alog_ref[...])
    dt_bias = dtb_ref[...]
    d_skip = dsk_ref[...]
    norm_w = nw_ref[...]

    s_ref[...] = jnp.zeros((gw, SSD_N), F32)
    heads = [slice(hh * SSD_P, (hh + 1) * SSD_P) for hh in range(SSD_HPG)]

    def body(trip, carry):
        cis = [trip * unroll + cc for cc in range(unroll)]
        rows = [pl.ds(pl.multiple_of(ci * c, c), c) for ci in cis]
        xs = [x_ref[rw, :] for rw in rows]
        bm = [b_ref[rw, :] for rw in rows]
        cm = [c_ref[rw, :] for rw in rows]
        dt = [_softplus(_mm01_t(dt_ref[rw, :], expand) + dt_bias) for rw in rows]
        a = [d * neg_a for d in dt]
        acum = [_mm01(ltri, x) for x in a]
        acum_j = [_mm01(ones, jnp.where(upper_t, x, 0.0)) for x in a]
        cb = [_mm(x, y, NT) for x, y in zip(cm, bm)]
        lm = [jnp.where(tril_t, jnp.exp(x - y), 0.0) for x, y in zip(acum, acum_j)]
        xdt = [x * d for x, d in zip(xs, dt)]
        y_diag = [jnp.concatenate([_mm(cb[i] * lm[i][:, sl], xdt[i][:, sl], NN)
                                   for sl in heads], axis=1) for i in range(unroll)]
        a_last = [jnp.broadcast_to(x[c - 1:c, :], (c, gw)) for x in acum]
        states = [_mm(xdt[i] * jnp.exp(a_last[i] - acum[i]), bm[i], TN)
                  for i in range(unroll)]
        dec_col = [_mm01_t(jnp.where(last_row, jnp.exp(x[c - 8:, :]), 0.0), ones_8n, TN)
                   for x in acum]
        s_prev = s_ref[...]
        for i in range(unroll):
            y_off = _mm(cm[i], s_prev, NT) * jnp.exp(acum[i])
            s_prev = s_prev * dec_col[i] + states[i]
            y = y_diag[i] + y_off + d_skip * xs[i]
            y = y * _silu(z_ref[rows[i], :])
            ms = jnp.mean(y * y, axis=-1, keepdims=True)
            o_ref[rows[i], :] = (y * lax.rsqrt(ms + RMS_EPS) * norm_w).astype(o_ref.dtype)
        s_ref[...] = s_prev
        return carry

    lax.fori_loop(0, n_chunks // unroll, body, 0)


def _ssd(proj, bsz, seq, a_log, dt_bias, d_skip, norm_w, *, unroll=8):
    t = proj.shape[0]
    gw = SSD_GW
    unroll = math.gcd(unroll, seq // CHUNK)
    rep = lambda v: jnp.repeat(v.astype(F32), SSD_P).reshape(1, SSD_INNER)
    wide = lambda off: pl.BlockSpec((seq, gw), lambda b, g, off=off: (b, off + g))
    lane = lambda off: pl.BlockSpec((seq, LANE), lambda b, g, off=off: (b, off + g))
    chan = pl.BlockSpec((1, gw), lambda b, g: (0, g))
    return pl.pallas_call(
        functools.partial(_ssd_kernel, unroll=unroll),
        grid=(bsz, SSD_GROUPS),
        in_specs=[
            wide(COL_SSD_Z // gw), wide(COL_SSD_X // gw),
            lane(COL_SSD_B // LANE), lane(COL_SSD_C // LANE),
            pl.BlockSpec((seq, LANE), lambda b, g: (b, COL_SSD_DT // LANE)),
            chan, chan, chan, chan,
        ],
        out_specs=pl.BlockSpec((seq, gw), lambda b, g: (b, g)),
        out_shape=jax.ShapeDtypeStruct((t, SSD_INNER), BF16),
        scratch_shapes=[pltpu.VMEM((gw, SSD_N), F32)],
        compiler_params=pltpu.CompilerParams(
            dimension_semantics=("parallel", "arbitrary"), vmem_limit_bytes=VMEM_LIMIT),
        name="ssd",
    )(proj, proj, proj, proj, proj,
      rep(a_log), rep(dt_bias), rep(d_skip), norm_w.reshape(1, SSD_INNER))


def _s5_kernel(u_ref, wb_ref, wc_ref, are_ref, aim_ref, dsk_ref, gw_ref, gb_ref,
               o_ref, bu_ref, h_ref, *, bsz):
    ts = u_ref.shape[0]
    nt = S5_CH // LANE
    nb = S5_WIDTH // LANE
    per = nt // nb
    half = per * LANE

    @pl.when(pl.program_id(0) == 0)
    def _():
        h_ref[...] = jnp.zeros(h_ref.shape, F32)

    u = u_ref[...].reshape(ts * bsz, S5_WIDTH)
    u_bf = u.astype(BF16)
    for m in range(nb):
        res = _dot(u_bf[:, m * LANE:(m + 1) * LANE], wb_ref[m])
        for k in range(per):
            bu_ref[m * per + k] = res[:, k * LANE:(k + 1) * LANE]
            bu_ref[nt + m * per + k] = res[:, half + k * LANE:half + (k + 1) * LANE]

    def step(t, carry):
        rows = pl.ds(pl.multiple_of(t * bsz, bsz), bsz)
        new_re, new_im = [], []
        for k in range(nt):
            h_re, h_im = carry[k], carry[nt + k]
            a_re = are_ref[:, k * LANE:(k + 1) * LANE]
            a_im = aim_ref[:, k * LANE:(k + 1) * LANE]
            n_re = a_re * h_re - a_im * h_im + bu_ref[k, rows, :]
            n_im = a_re * h_im + a_im * h_re + bu_ref[nt + k, rows, :]
            bu_ref[k, rows, :] = n_re
            bu_ref[nt + k, rows, :] = n_im
            new_re.append(n_re)
            new_im.append(n_im)
        return tuple(new_re + new_im)

    h_last = lax.fori_loop(0, ts, step, tuple(h_ref[k] for k in range(2 * nt)))
    for k in range(2 * nt):
        h_ref[k] = h_last[k]

    y_parts = []
    for m in range(nb):
        hs = jnp.concatenate([bu_ref[m * per + k] for k in range(per)] +
                             [bu_ref[nt + m * per + k] for k in range(per)], axis=1)
        y_parts.append(_dot(hs.astype(BF16), wc_ref[m]))
    y = jnp.concatenate(y_parts, axis=1) + dsk_ref[...] * u
    g = 0.5 * y * (1.0 + jnp.tanh(math.sqrt(2.0 / math.pi) * (y + 0.044715 * (y * y * y))))
    out = g * _sigmoid(_dot(g.astype(BF16), gw_ref[...]) + gb_ref[...])
    o_ref[...] = out.reshape(ts, bsz * S5_WIDTH).astype(o_ref.dtype)


def _s5(u_tb, bsz, wb, wc, a_re, a_im, d_skip, glu_w, glu_b, *, ts=128):
    seq = u_tb.shape[0]
    blk = ts * bsz
    fixed = lambda shape: pl.BlockSpec(shape, lambda k: (0,) * len(shape))
    return pl.pallas_call(
        functools.partial(_s5_kernel, bsz=bsz),
        grid=(seq // ts,),
        in_specs=[
            pl.BlockSpec((ts, bsz * S5_WIDTH), lambda k: (k, 0)),
            fixed(wb.shape), fixed(wc.shape),
            fixed((1, S5_CH)), fixed((1, S5_CH)),
            fixed((1, S5_WIDTH)), fixed(glu_w.shape), fixed((1, S5_WIDTH)),
        ],
        out_specs=pl.BlockSpec((ts, bsz * S5_WIDTH), lambda k: (k, 0)),
        out_shape=jax.ShapeDtypeStruct((seq, bsz * S5_WIDTH), BF16),
        scratch_shapes=[
            pltpu.VMEM((2 * S5_CH // LANE, blk, LANE), F32),
            pltpu.VMEM((2 * S5_CH // LANE, bsz, LANE), F32),
        ],
        compiler_params=pltpu.CompilerParams(
            dimension_semantics=("arbitrary",), vmem_limit_bytes=VMEM_LIMIT),
        name="s5",
    )(u_tb, wb, wc, a_re.reshape(1, S5_CH), a_im.reshape(1, S5_CH),
      d_skip.reshape(1, S5_WIDTH), glu_w, glu_b.reshape(1, S5_WIDTH))


def _s5_params(a_re, a_im, b_re, b_im, c_re, c_im, log_dt):
    delta = jnp.exp(log_dt)[:, None]
    mag = jnp.exp(a_re * delta)
    ab_re, ab_im = mag * jnp.cos(a_im * delta), mag * jnp.sin(a_im * delta)
    den = a_re * a_re + a_im * a_im
    p_re, p_im = ab_re - 1.0, ab_im
    f_re = (p_re * a_re + p_im * a_im) / den
    f_im = (p_im * a_re - p_re * a_im) / den
    bb_re = f_re[..., None] * b_re - f_im[..., None] * b_im
    bb_im = f_re[..., None] * b_im + f_im[..., None] * b_re
    gpb = LANE // S5_GROUP
    nb = S5_GROUPS // gpb
    eye = jnp.eye(gpb, dtype=F32)
    emb_b = lambda bb: jnp.einsum('mgnj,gh->mgjhn', bb.reshape(nb, gpb, S5_STATE, S5_GROUP),
                                  eye).reshape(nb, LANE, gpb * S5_STATE)
    wb = jnp.concatenate([emb_b(bb_re), emb_b(bb_im)], axis=2)
    emb_c = lambda cc: jnp.einsum('mgin,gh->mgnhi', cc.reshape(nb, gpb, S5_GROUP, S5_STATE),
                                  eye).reshape(nb, gpb * S5_STATE, LANE)
    wc = jnp.concatenate([emb_c(c_re), -emb_c(c_im)], axis=1)
    return ab_re.reshape(-1), ab_im.reshape(-1), wb.astype(BF16), wc.astype(BF16)


def _w_in_segments():
    bc = SSD_GROUPS * SSD_N
    g_z = GDN_QKV
    g_gate = g_z + GDN_QK
    s_z = g_gate + 2 * GDN_HEADS
    s_x = s_z + SSD_INNER
    s_b = s_x + SSD_INNER
    s_c = s_b + bc
    s_dt = s_c + bc
    s5_u = s_dt + SSD_HEADS
    return ((0, GDN_QKV, COL_GDN_QKV), (g_z, GDN_QK, COL_GDN_Z),
            (g_gate, 2 * GDN_HEADS, COL_GDN_GATE), (s_z, SSD_INNER, COL_SSD_Z),
            (s_x, SSD_INNER, COL_SSD_X), (s_b, bc, COL_SSD_B), (s_c, bc, COL_SSD_C),
            (s_dt, SSD_HEADS, COL_SSD_DT), (s5_u, S5_WIDTH, COL_S5_U))


def _pack_conv(gdn_w, ssd_w, ssd_b):
    bc = SSD_GROUPS * SSD_N
    pieces = ((COL_GDN_QKV, gdn_w, None),
              (COL_SSD_X, ssd_w[:, :SSD_INNER], ssd_b[:SSD_INNER]),
              (COL_SSD_B, ssd_w[:, SSD_INNER:SSD_INNER + bc], ssd_b[SSD_INNER:SSD_INNER + bc]),
              (COL_SSD_C, ssd_w[:, SSD_INNER + bc:], ssd_b[SSD_INNER + bc:]))
    cw = jnp.zeros((CONV_K, P_PAD), F32)
    cb = jnp.zeros((1, P_PAD), F32)
    for col, w, b in pieces:
        cw = cw.at[:, col:col + w.shape[1]].set(w.astype(F32))
        if b is not None:
            cb = cb.at[0, col:col + b.shape[0]].set(b.astype(F32))
    return cw, cb


def _pack_kernel(x_ref, o_ref):
    for col in (COL_SSD_DT, COL_GDN_GATE):
        o_ref[:, col:col + LANE] = jnp.zeros((o_ref.shape[0], LANE), o_ref.dtype)
    for src, width, dst in _w_in_segments():
        if width % LANE == 0:
            for r in range(0, width, 2 * LANE):
                w = min(2 * LANE, width - r)
                o_ref[:, dst + r:dst + r + w] = x_ref[src + r:src + r + w, :].T.astype(o_ref.dtype)
        else:
            blk = x_ref[src:src + LANE, :].T
            o_ref[:, dst:dst + width] = blk[:, :width].astype(o_ref.dtype)


def _pack_w_in(w_in, *, kb=256):
    nl, d, c = w_in.shape
    return pl.pallas_call(
        _pack_kernel,
        grid=(nl, d // kb),
        in_specs=[pl.BlockSpec((None, c, kb), lambda l, i: (l, 0, i))],
        out_specs=pl.BlockSpec((None, kb, P_PAD), lambda l, i: (l, i, 0)),
        out_shape=jax.ShapeDtypeStruct((nl, d, P_PAD), BF16),
        compiler_params=pltpu.CompilerParams(
            dimension_semantics=("parallel", "parallel"), vmem_limit_bytes=VMEM_LIMIT),
        name="pack_w_in",
    )(jnp.swapaxes(w_in, 1, 2))


def kernel(x, ffn1_norm, ffn1_w_gate, ffn1_w_up, ffn1_w_down, mix_norm, w_in,
           gdn_conv_w, gdn_a_log, gdn_dt_bias, gdn_norm,
           ssd_conv_w, ssd_conv_b, ssd_a_log, ssd_dt_bias, ssd_d, ssd_norm,
           s5_a_re, s5_a_im, s5_b_re, s5_b_im, s5_c_re, s5_c_im, s5_d, s5_log_dt,
           s5_glu_w, s5_glu_b, w_out, ffn2_norm, ffn2_w_gate, ffn2_w_up, ffn2_w_down,
           final_norm):
    bsz, seq, d = x.shape
    depth = w_in.shape[0]
    h = x
    ffn1_f32 = (ffn1_w_gate, ffn1_w_up, ffn1_w_down)
    ffn2_f32 = (ffn2_w_gate, ffn2_w_up, ffn2_w_down)
    w_next = tuple(_to_bf16(w, 0) for w in ffn1_f32)
    w_in_packed = _pack_w_in(w_in)
    for i in range(depth):
        h, w_next = _ffn(h, ffn1_norm[i], *w_next, final_norm, final_norm=False,
                         cast_next=(*ffn2_f32, i))
        conv_w, conv_b = _pack_conv(gdn_conv_w[i], ssd_conv_w[i], ssd_conv_b[i])
        proj, u_s5 = _inproj(h, mix_norm[i], w_in_packed, i, conv_w, conv_b, bsz, seq)
        o_gdn = _gdn(proj, bsz, seq, gdn_a_log[i], gdn_dt_bias[i], gdn_norm[i])
        o_ssd = _ssd(proj, bsz, seq, ssd_a_log[i], ssd_dt_bias[i], ssd_d[i], ssd_norm[i])
        ab_re, ab_im, wb, wc = _s5_params(s5_a_re[i], s5_a_im[i], s5_b_re[i], s5_b_im[i],
                                          s5_c_re[i], s5_c_im[i], s5_log_dt[i])
        o_s5 = _s5(u_s5, bsz, wb, wc, ab_re, ab_im, s5_d[i], s5_glu_w[i].astype(BF16),
                   s5_glu_b[i])
        wo = w_out[i].astype(BF16)
        h = _outproj(h, o_gdn, o_ssd, o_s5, wo[0:GDN_QK], wo[GDN_QK:GDN_QK + SSD_INNER],
                     wo[GDN_QK + SSD_INNER:])
        last = i == depth - 1
        if last:
            h = _ffn(h, ffn2_norm[i], *w_next, final_norm, final_norm=True,
                     out_batched=(bsz, seq))
        else:
            h, w_next = _ffn(h, ffn2_norm[i], *w_next, final_norm, final_norm=False,
                             cast_next=(*ffn1_f32, i + 1))
    return h
```

```python
import functools
import math

import jax
import jax.numpy as jnp
from jax import lax
from jax.experimental import pallas as pl
from jax.experimental.pallas import tpu as pltpu

F32 = jnp.float32
BF16 = jnp.bfloat16

RMS_EPS = 1e-6
CHUNK = 64
CONV_K = 4

GDN_HEADS = 6
GDN_D = 128
GDN_QK = GDN_HEADS * GDN_D
GDN_QKV = 3 * GDN_QK

SSD_HEADS = 12
SSD_P = 64
SSD_GROUPS = 2
SSD_N = 128
SSD_HPG = SSD_HEADS // SSD_GROUPS
SSD_GW = SSD_HPG * SSD_P
SSD_INNER = SSD_HEADS * SSD_P

S5_WIDTH = 512
S5_GROUP = 16
S5_GROUPS = 32
S5_STATE = 64
S5_CH = S5_GROUPS * S5_STATE
S5_SCAN_UNROLL = 8

LANE = 128

COL_GDN_QKV = 0
COL_GDN_Z = 2304
COL_SSD_Z = 3072
COL_SSD_X = 3840
COL_SSD_B = 4608
COL_SSD_C = 4864
COL_S5_U = 5120
COL_SSD_DT = 5632
COL_GDN_GATE = 5760
P_PAD = 5888

VMEM_LIMIT = 60 * 1024 * 1024

NN = (((1,), (0,)), ((), ()))
NT = (((1,), (1,)), ((), ()))
TN = (((0,), (0,)), ((), ()))


def _dot(a, b, dims=NN):
    return lax.dot_general(a, b, dims, preferred_element_type=F32)


def _hi_lo(a):
    hi = a.astype(BF16)
    lo = (a - hi.astype(F32)).astype(BF16)
    return hi, lo


def _mm(a, b, dims=NN):
    return _dot(a.astype(BF16), b.astype(BF16), dims)


def _mm01(m01, x, dims=NN):
    hi, lo = _hi_lo(x)
    return _dot(m01, hi, dims) + _dot(m01, lo, dims)


def _mm01_t(x, m01, dims=NN):
    hi, lo = _hi_lo(x)
    return _dot(hi, m01, dims) + _dot(lo, m01, dims)


def _sigmoid(x):
    return 1.0 / (1.0 + jnp.exp(-x))


def _silu(x):
    return x * _sigmoid(x)


def _softplus(x):
    return jnp.maximum(x, 0.0) + jnp.log1p(jnp.exp(-jnp.abs(x)))


def _iota(shape, dim):
    return lax.broadcasted_iota(jnp.int32, shape, dim)


def _ffn_kernel(x_ref, nw_ref, wg_ref, wu_ref, wd_ref, fw_ref, *rest, final_norm, side_cast):
    if side_cast:
        cg_ref, cu_ref, cd_ref, o_ref, og_ref, ou_ref, od_ref, xn_ref = rest
        og_ref[...] = cg_ref[...].astype(BF16)
        ou_ref[...] = cu_ref[...].astype(BF16)
        od_ref[...] = cd_ref[...].astype(BF16)
    else:
        o_ref, xn_ref = rest
    j = pl.program_id(1)

    @pl.when(j == 0)
    def _():
        x = x_ref[...]
        ms = jnp.mean(x * x, axis=-1, keepdims=True)
        xn_ref[...] = (x * lax.rsqrt(ms + RMS_EPS) * nw_ref[...]).astype(BF16)
        o_ref[...] = x

    xn = xn_ref[...]
    g = _dot(xn, wg_ref[...])
    u = _dot(xn, wu_ref[...])
    a = (0.5 * _silu(g) * u).astype(BF16)
    o_ref[...] += _dot(a, wd_ref[...])

    if final_norm:
        @pl.when(j == pl.num_programs(1) - 1)
        def _():
            h = o_ref[...]
            ms = jnp.mean(h * h, axis=-1, keepdims=True)
            o_ref[...] = h * lax.rsqrt(ms + RMS_EPS) * fw_ref[...]


def _cast_kernel(x_ref, o_ref):
    o_ref[...] = x_ref[...].astype(o_ref.dtype)


def _to_bf16(w, layer, *, rows=256):
    _, r, c = w.shape
    return pl.pallas_call(
        _cast_kernel,
        grid=(r // rows,),
        in_specs=[pl.BlockSpec((None, rows, c), lambda i: (layer, i, 0))],
        out_specs=pl.BlockSpec((rows, c), lambda i: (i, 0)),
        out_shape=jax.ShapeDtypeStruct((r, c), BF16),
        compiler_params=pltpu.CompilerParams(
            dimension_semantics=("parallel",), vmem_limit_bytes=VMEM_LIMIT),
        name="to_bf16",
    )(w)


def _ffn(x, nw, wg, wu, wd, fw, *, final_norm, out_batched=None, cast_next=None,
         tm=1024, tf=512):
    d = x.shape[-1]
    t = x.size // d
    f = wg.shape[1]
    ni, nj = t // tm, f // tf
    if x.ndim == 3:
        nseg = x.shape[1] // tm
        x_spec = pl.BlockSpec((None, tm, d), lambda i, j: (i // nseg, i % nseg, 0))
    else:
        x_spec = pl.BlockSpec((tm, d), lambda i, j: (i, 0))
    if out_batched is None:
        out_spec = pl.BlockSpec((tm, d), lambda i, j: (i, 0))
        out_shape = jax.ShapeDtypeStruct((t, d), F32)
    else:
        oseg = out_batched[1] // tm
        out_spec = pl.BlockSpec((None, tm, d), lambda i, j: (i // oseg, i % oseg, 0))
        out_shape = jax.ShapeDtypeStruct((*out_batched, d), F32)
    in_specs = [
        x_spec,
        pl.BlockSpec((1, d), lambda i, j: (0, 0)),
        pl.BlockSpec((d, tf), lambda i, j: (0, j)),
        pl.BlockSpec((d, tf), lambda i, j: (0, j)),
        pl.BlockSpec((tf, d), lambda i, j: (j, 0)),
        pl.BlockSpec((1, d), lambda i, j: (0, 0)),
    ]
    operands = [x, nw.reshape(1, d), wg, wu, wd, fw.reshape(1, d)]
    out_specs, out_shapes = [out_spec], [out_shape]
    if cast_next is not None:
        cg, cu, cd, layer = cast_next
        dr = d // ni
        assert dr * ni == d and dr % LANE == 0, (d, ni)
        in_specs += [pl.BlockSpec((None, dr, tf), lambda i, j: (layer, i, j)),
                     pl.BlockSpec((None, dr, tf), lambda i, j: (layer, i, j)),
                     pl.BlockSpec((None, tf, dr), lambda i, j: (layer, j, i))]
        operands += [cg, cu, cd]
        out_specs += [pl.BlockSpec((dr, tf), lambda i, j: (i, j)),
                      pl.BlockSpec((dr, tf), lambda i, j: (i, j)),
                      pl.BlockSpec((tf, dr), lambda i, j: (j, i))]
        out_shapes += [jax.ShapeDtypeStruct((d, f), BF16), jax.ShapeDtypeStruct((d, f), BF16),
                       jax.ShapeDtypeStruct((f, d), BF16)]
    res = pl.pallas_call(
        functools.partial(_ffn_kernel, final_norm=final_norm, side_cast=cast_next is not None),
        grid=(ni, nj),
        in_specs=in_specs,
        out_specs=out_specs,
        out_shape=out_shapes,
        scratch_shapes=[pltpu.VMEM((tm, d), BF16)],
        compiler_params=pltpu.CompilerParams(
            dimension_semantics=("parallel", "arbitrary"), vmem_limit_bytes=VMEM_LIMIT),
        name="ffn",
    )(*operands)
    return (res[0], tuple(res[1:])) if cast_next is not None else res[0]


CONV_RANGES = ((COL_SSD_X, SSD_INNER), (COL_SSD_B, SSD_GROUPS * SSD_N),
               (COL_SSD_C, SSD_GROUPS * SSD_N), (COL_GDN_QKV, GDN_QKV))
INPROJ_PIECE = 256


def _inproj_kernel(x_ref, nw_ref, w_ref, cw_ref, cb_ref, o_ref, u5_ref, stage_ref, *, nseg):
    tm = x_ref.shape[0]

    @pl.when(pl.program_id(0) % nseg == 0)
    def _():
        stage_ref[:, 0:8, :] = jnp.zeros((stage_ref.shape[0], 8, LANE), F32)

    x = x_ref[...]
    ms = jnp.mean(x * x, axis=-1, keepdims=True)
    xn = (x * lax.rsqrt(ms + RMS_EPS) * nw_ref[...]).astype(BF16)
    starts = list(range(0, o_ref.shape[1], INPROJ_PIECE))
    piece = lambda col: _dot(xn, w_ref[:, col:col + INPROJ_PIECE])
    nxt = piece(starts[0])
    for idx, col in enumerate(starts):
        cols = slice(col, col + INPROJ_PIECE)
        res, nxt = nxt, (piece(starts[idx + 1]) if idx + 1 < len(starts) else None)
        if any(lo <= col < lo + width for lo, width in CONV_RANGES):
            for lt in range(INPROJ_PIECE // LANE):
                ti = col // LANE + lt
                ln = slice(lt * LANE, (lt + 1) * LANE)
                gl = slice(col + lt * LANE, col + (lt + 1) * LANE)
                stage_ref[ti, 8:, :] = res[:, ln]
                acc = None
                for k in range(CONV_K):
                    lo = k + 8 - (CONV_K - 1)
                    term = stage_ref[ti, pl.ds(lo, tm, stride=1), :] * cw_ref[k:k + 1, gl]
                    acc = term if acc is None else acc + term
                stage_ref[ti, 0:8, :] = res[tm - 8:, ln]
                o_ref[:, gl] = _silu(acc + cb_ref[:, gl])
        else:
            o_ref[:, cols] = res
        if COL_S5_U <= col < COL_S5_U + S5_WIDTH:
            u5_ref[:, col - COL_S5_U:col - COL_S5_U + INPROJ_PIECE] = res


def _inproj(x, nw, w, layer, conv_w, conv_b, bsz, seq, *, tm=256):
    t, d = x.shape
    n = w.shape[2]
    nseg = seq // tm
    assert nseg * tm == seq and n % INPROJ_PIECE == 0, (seq, tm, n)
    fixed = lambda i: (0, 0)
    return pl.pallas_call(
        functools.partial(_inproj_kernel, nseg=nseg),
        grid=(t // tm,),
        in_specs=[
            pl.BlockSpec((tm, d), lambda i: (i, 0)),
            pl.BlockSpec((1, d), fixed),
            pl.BlockSpec((None, d, n), lambda i: (layer, 0, 0), pipeline_mode=pl.Buffered(1)),
            pl.BlockSpec((CONV_K, n), fixed),
            pl.BlockSpec((1, n), fixed),
        ],
        out_specs=[
            pl.BlockSpec((tm, n), lambda i: (i, 0)),
            pl.BlockSpec((tm, S5_WIDTH), lambda i: (i % nseg, i // nseg)),
        ],
        out_shape=[jax.ShapeDtypeStruct((t, n), F32),
                   jax.ShapeDtypeStruct((seq, bsz * S5_WIDTH), F32)],
        scratch_shapes=[pltpu.VMEM((n // LANE, tm + 8, LANE), F32)],
        compiler_params=pltpu.CompilerParams(
            dimension_semantics=("arbitrary",), vmem_limit_bytes=VMEM_LIMIT),
        name="inproj",
    )(x, nw.reshape(1, d), w, conv_w, conv_b)


def _outproj_kernel(h_ref, a_ref, b_ref, c_ref, wa_ref, wb_ref, wc_ref, o_ref):
    acc = _dot(a_ref[...], wa_ref[...])
    acc += _dot(b_ref[...], wb_ref[...])
    acc += _dot(c_ref[...], wc_ref[...])
    o_ref[...] = h_ref[...] + acc


def _outproj(h, oa, ob, oc, wa, wb, wc, *, tm=1024):
    t, d = h.shape
    nseg = oc.shape[0] // tm
    row = lambda i: (i, 0)
    fixed = lambda i: (0, 0)
    return pl.pallas_call(
        _outproj_kernel,
        grid=(t // tm,),
        in_specs=[
            pl.BlockSpec((tm, d), row),
            pl.BlockSpec((tm, oa.shape[1]), row),
            pl.BlockSpec((tm, ob.shape[1]), row),
            pl.BlockSpec((tm, S5_WIDTH), lambda i: (i % nseg, i // nseg)),
            pl.BlockSpec(wa.shape, fixed, pipeline_mode=pl.Buffered(1)),
            pl.BlockSpec(wb.shape, fixed, pipeline_mode=pl.Buffered(1)),
            pl.BlockSpec(wc.shape, fixed, pipeline_mode=pl.Buffered(1)),
        ],
        out_specs=pl.BlockSpec((tm, d), row),
        out_shape=jax.ShapeDtypeStruct((t, d), F32),
        compiler_params=pltpu.CompilerParams(
            dimension_semantics=("parallel",), vmem_limit_bytes=VMEM_LIMIT),
        name="outproj",
    )(h, oa, ob, oc, wa, wb, wc)


def _gdn_kernel(gpar_ref,
                q_ref, k_ref, v_ref, z_ref, gate_ref,
                nw_ref,
                o_ref,
                u_ref, w_ref, qd_ref, kd_ref, at_ref, eg_ref,
                *, hb, unroll):
    hg = pl.program_id(1)
    seq = q_ref.shape[0]
    n_chunks = seq // CHUNK
    c = CHUNK

    row = _iota((c, c), 0)
    col = _iota((c, c), 1)
    tril = row >= col
    strict = row > col
    ltri = tril.astype(BF16)
    ones = jnp.ones((c, c), BF16)
    eye = (row == col).astype(F32)
    lane = _iota((c, LANE), 1)
    neg_a = -jnp.exp(gpar_ref[0:1, :])
    dt_bias = gpar_ref[1:2, :]

    def prep_stages(trip):
        cis = [trip * unroll + cc for cc in range(unroll)]
        items = [(cis[cc], hh) for cc in range(unroll) for hh in range(hb)]
        lanes = [slice(hh * LANE, (hh + 1) * LANE) for _, hh in items]
        rows = [pl.ds(pl.multiple_of(ci * c, c), c) for ci, _ in items]

        def chunk_gates(ci):
            gt = gate_ref[pl.ds(pl.multiple_of(ci * c, c), c), :]
            return _sigmoid(gt), neg_a * _softplus(gt + dt_bias)

        gates = [chunk_gates(ci) for ci in cis]

        def stage_a(cc, hh, ln):
            ci = cis[cc]
            h = hg * hb + hh
            rw = pl.ds(pl.multiple_of(ci * c, c), c)
            q, k, v = q_ref[rw, ln], k_ref[rw, ln], v_ref[rw, ln]
            qn = q * lax.rsqrt(jnp.sum(q * q, axis=-1, keepdims=True) + RMS_EPS) * (GDN_D ** -0.5)
            kn = k * lax.rsqrt(jnp.sum(k * k, axis=-1, keepdims=True) + RMS_EPS)
            beta_all, g_all = gates[cc]
            beta = jnp.sum(jnp.where(lane == h, beta_all, 0.0), axis=-1, keepdims=True)
            g = jnp.sum(jnp.where(lane == h + GDN_HEADS, g_all, 0.0), axis=-1, keepdims=True)
            return qn, kn, v, beta, jnp.broadcast_to(g, (c, LANE))

        sa = [stage_a(cc, hh, slice(hh * LANE, (hh + 1) * LANE))
              for cc in range(unroll) for hh in range(hb)]
        qn = [x[0] for x in sa]
        kn = [x[1] for x in sa]
        v = [x[2] for x in sa]
        beta = [x[3] for x in sa]
        g_b = [x[4] for x in sa]
        yield
        gi = [_mm01(ltri, g) for g in g_b]
        gj = [_mm01(ones, jnp.where(row <= col, g[:, :c], 0.0)) for g in g_b]
        yield
        kb = [k * b for k, b in zip(kn, beta)]
        kk = [_mm(a, b, NT) for a, b in zip(kb, kn)]
        qk = [_mm(a, b, NT) for a, b in zip(qn, kn)]
        decay = [jnp.where(tril, jnp.exp(a[:, :c] - b), 0.0) for a, b in zip(gi, gj)]
        npow = [-jnp.where(strict, a * d, 0.0) for a, d in zip(kk, decay)]
        tinv = [eye + n for n in npow]
        for _ in range(5):
            yield
            npow = [_mm(n, n, NN) for n in npow]
            tinv = [t + _mm(t, n, NN) for t, n in zip(tinv, npow)]
        yield
        exp_g = [jnp.exp(g) for g in gi]
        rhs = [jnp.concatenate([vv * b, k * e], axis=1) for vv, b, k, e in zip(v, beta, kb, exp_g)]
        uw = [_mm(t, r, NN) for t, r in zip(tinv, rhs)]
        for i, ((ci, hh), rw) in enumerate(zip(items, rows)):
            g_last = jnp.broadcast_to(gi[i][c - 1:c, :], (c, LANE))
            u_ref[hh, rw, :] = uw[i][:, :LANE]
            w_ref[hh, rw, :] = uw[i][:, LANE:].astype(BF16)
            at_ref[hh, rw, :] = jnp.where(tril, qk[i] * decay[i], 0.0).astype(BF16)
            qd_ref[hh, rw, :] = (qn[i] * exp_g[i]).astype(BF16)
            kd_ref[hh, rw, :] = (kn[i] * jnp.exp(g_last - gi[i])).astype(BF16)
            eg_ref[hh, pl.ds(pl.multiple_of(ci * 8, 8), 8), :] = jnp.exp(g_last[0:8, :])

    norm_w = nw_ref[...]

    def scan_stages(trip, states, final):
        for cc in range(unroll):
            ci = trip * unroll + cc
            rw = pl.ds(pl.multiple_of(ci * c, c), c)
            lhs = [jnp.concatenate([w_ref[hh, rw, :], qd_ref[hh, rw, :]], axis=0)
                   for hh in range(hb)]
            ws_qs = [_mm(a, s, NN) for a, s in zip(lhs, states)]
            yield
            v_new = [u_ref[hh, rw, :] - ws_qs[hh][:c] for hh in range(hb)]
            kv = [_mm(kd_ref[hh, rw, :], v_new[hh], TN) for hh in range(hb)]
            av = [_mm(at_ref[hh, rw, :], v_new[hh], NN) for hh in range(hb)]
            new_states = []
            for hh in range(hb):
                eg = eg_ref[hh, pl.ds(pl.multiple_of(ci * 8, 8), 8), :]
                new_states.append(states[hh] * jnp.broadcast_to(eg[0:1, :], (GDN_D, GDN_D)) + kv[hh])
                o = ws_qs[hh][c:] + av[hh]
                ms = jnp.mean(o * o, axis=-1, keepdims=True)
                ln = slice(hh * LANE, (hh + 1) * LANE)
                y = o * lax.rsqrt(ms + RMS_EPS) * norm_w * _silu(z_ref[rw, ln])
                o_ref[rw, ln] = y.astype(o_ref.dtype)
            states = new_states
            yield
        final.extend(states)

    def drain(gen):
        for _ in gen:
            pass

    def fused(trip, states):
        final = []
        scan_gen = scan_stages(trip - 1, states, final)
        for _ in prep_stages(trip):
            next(scan_gen, None)
        drain(scan_gen)
        return tuple(final)

    n_trips = n_chunks // unroll
    drain(prep_stages(0))
    states = lax.fori_loop(1, n_trips, fused,
                           tuple(jnp.zeros((GDN_D, GDN_D), F32) for _ in range(hb)))
    drain(scan_stages(n_trips - 1, states, []))


def _gdn(proj, bsz, seq, a_log, dt_bias, norm_w, *, hb=3, unroll=4):
    t = proj.shape[0]
    unroll = math.gcd(unroll, seq // CHUNK)
    wd = hb * LANE
    q0 = COL_GDN_QKV // wd
    z0 = COL_GDN_Z // wd
    g0 = COL_GDN_GATE // LANE
    ng = GDN_HEADS // hb
    blk = lambda off: pl.BlockSpec((seq, wd), lambda b, h, off=off: (b, off + h))
    pad = lambda v: jnp.pad(v.astype(F32), (GDN_HEADS, LANE - 2 * GDN_HEADS))
    gpar = jnp.stack([pad(a_log), pad(dt_bias)])
    return pl.pallas_call(
        functools.partial(_gdn_kernel, hb=hb, unroll=unroll),
        grid=(bsz, ng),
        in_specs=[
            pl.BlockSpec((2, LANE), lambda b, h: (0, 0)),
            blk(q0), blk(q0 + ng), blk(q0 + 2 * ng), blk(z0),
            pl.BlockSpec((seq, LANE), lambda b, h: (b, g0)),
            pl.BlockSpec((1, LANE), lambda b, h: (0, 0)),
        ],
        out_specs=pl.BlockSpec((seq, wd), lambda b, h: (b, h)),
        out_shape=jax.ShapeDtypeStruct((t, GDN_QK), BF16),
        scratch_shapes=[
            pltpu.VMEM((hb, seq, LANE), F32),
            pltpu.VMEM((hb, seq, LANE), BF16),
            pltpu.VMEM((hb, seq, LANE), BF16),
            pltpu.VMEM((hb, seq, LANE), BF16),
            pltpu.VMEM((hb, seq, CHUNK), BF16),
            pltpu.VMEM((hb, seq // CHUNK * 8, LANE), F32),
        ],
        compiler_params=pltpu.CompilerParams(
            dimension_semantics=("parallel", "arbitrary"), vmem_limit_bytes=VMEM_LIMIT),
        name="gdn",
    )(gpar, proj, proj, proj, proj, proj, norm_w.reshape(1, LANE))


def _ssd_kernel(z_ref, x_ref, b_ref, c_ref, dt_ref,
                alog_ref, dtb_ref, dsk_ref, nw_ref,
                o_ref,
                s_ref, *, unroll):
    grp = pl.program_id(1)
    seq = x_ref.shape[0]
    n_chunks = seq // CHUNK
    c = CHUNK
    gw = SSD_GW

    row = _iota((c, c), 0)
    col = _iota((c, c), 1)
    ltri = (row >= col).astype(BF16)
    ones = jnp.ones((c, c), BF16)
    rowt = _iota((c, gw), 0)
    colt = jnp.bitwise_and(_iota((c, gw), 1), c - 1)
    tril_t = rowt >= colt
    upper_t = rowt <= colt
    expand = (_iota((LANE, gw), 0) == grp * SSD_HPG + lax.shift_right_logical(_iota((LANE, gw), 1), 6)).astype(BF16)
    last_row = (_iota((8, gw), 0) == 7)
    ones_8n = jnp.ones((8, SSD_N), BF16)

    neg_a = -jnp.exp(alog_ref[...])
    dt_bias = dtb_ref[...]
    d_skip = dsk_ref[...]
    norm_w = nw_ref[...]

    s_ref[...] = jnp.zeros((gw, SSD_N), F32)
    heads = [slice(hh * SSD_P, (hh + 1) * SSD_P) for hh in range(SSD_HPG)]

    def body(trip, carry):
        cis = [trip * unroll + cc for cc in range(unroll)]
        rows = [pl.ds(pl.multiple_of(ci * c, c), c) for ci in cis]
        xs = [x_ref[rw, :] for rw in rows]
        bm = [b_ref[rw, :] for rw in rows]
        cm = [c_ref[rw, :] for rw in rows]
        dt = [_softplus(_mm01_t(dt_ref[rw, :], expand) + dt_bias) for rw in rows]
        a = [d * neg_a for d in dt]
        acum = [_mm01(ltri, x) for x in a]
        acum_j = [_mm01(ones, jnp.where(upper_t, x, 0.0)) for x in a]
        cb = [_mm(x, y, NT) for x, y in zip(cm, bm)]
        lm = [jnp.where(tril_t, jnp.exp(x - y), 0.0) for x, y in zip(acum, acum_j)]
        xdt = [x * d for x, d in zip(xs, dt)]
        y_diag = [jnp.concatenate([_mm(cb[i] * lm[i][:, sl], xdt[i][:, sl], NN)
                                   for sl in heads], axis=1) for i in range(unroll)]
        a_last = [jnp.broadcast_to(x[c - 1:c, :], (c, gw)) for x in acum]
        states = [_mm(xdt[i] * jnp.exp(a_last[i] - acum[i]), bm[i], TN)
                  for i in range(unroll)]
        dec_col = [_mm01_t(jnp.where(last_row, jnp.exp(x[c - 8:, :]), 0.0), ones_8n, TN)
                   for x in acum]
        s_prev = s_ref[...]
        for i in range(unroll):
            y_off = _mm(cm[i], s_prev, NT) * jnp.exp(acum[i])
            s_prev = s_prev * dec_col[i] + states[i]
            y = y_diag[i] + y_off + d_skip * xs[i]
            y = y * _silu(z_ref[rows[i], :])
            ms = jnp.mean(y * y, axis=-1, keepdims=True)
            o_ref[rows[i], :] = (y * lax.rsqrt(ms + RMS_EPS) * norm_w).astype(o_ref.dtype)
        s_ref[...] = s_prev
        return carry

    lax.fori_loop(0, n_chunks // unroll, body, 0)


def _ssd(proj, bsz, seq, a_log, dt_bias, d_skip, norm_w, *, unroll=8):
    t = proj.shape[0]
    gw = SSD_GW
    unroll = math.gcd(unroll, seq // CHUNK)
    rep = lambda v: jnp.repeat(v.astype(F32), SSD_P).reshape(1, SSD_INNER)
    wide = lambda off: pl.BlockSpec((seq, gw), lambda b, g, off=off: (b, off + g))
    lane = lambda off: pl.BlockSpec((seq, LANE), lambda b, g, off=off: (b, off + g))
    chan = pl.BlockSpec((1, gw), lambda b, g: (0, g))
    return pl.pallas_call(
        functools.partial(_ssd_kernel, unroll=unroll),
        grid=(bsz, SSD_GROUPS),
        in_specs=[
            wide(COL_SSD_Z // gw), wide(COL_SSD_X // gw),
            lane(COL_SSD_B // LANE), lane(COL_SSD_C // LANE),
            pl.BlockSpec((seq, LANE), lambda b, g: (b, COL_SSD_DT // LANE)),
            chan, chan, chan, chan,
        ],
        out_specs=pl.BlockSpec((seq, gw), lambda b, g: (b, g)),
        out_shape=jax.ShapeDtypeStruct((t, SSD_INNER), BF16),
        scratch_shapes=[pltpu.VMEM((gw, SSD_N), F32)],
        compiler_params=pltpu.CompilerParams(
            dimension_semantics=("parallel", "arbitrary"), vmem_limit_bytes=VMEM_LIMIT),
        name="ssd",
    )(proj, proj, proj, proj, proj,
      rep(a_log), rep(dt_bias), rep(d_skip), norm_w.reshape(1, SSD_INNER))


def _s5_kernel(u_ref, wb_ref, wc_ref, are_ref, aim_ref, dsk_ref, gw_ref, gb_ref,
               o_ref, bu_ref, h_ref, *, bsz):
    ts = u_ref.shape[0]
    nt = S5_CH // LANE
    nb = S5_WIDTH // LANE
    per = nt // nb
    half = per * LANE

    @pl.when(pl.program_id(0) == 0)
    def _():
        h_ref[...] = jnp.zeros(h_ref.shape, F32)

    u = u_ref[...].reshape(ts * bsz, S5_WIDTH)
    u_bf = u.astype(BF16)
    for m in range(nb):
        res = _dot(u_bf[:, m * LANE:(m + 1) * LANE], wb_ref[m])
        for k in range(per):
            bu_ref[m * per + k] = res[:, k * LANE:(k + 1) * LANE]
            bu_ref[nt + m * per + k] = res[:, half + k * LANE:half + (k + 1) * LANE]

    def step(t, carry):
        rows = pl.ds(pl.multiple_of(t * bsz, bsz), bsz)
        new_re, new_im = [], []
        for k in range(nt):
            h_re, h_im = carry[k], carry[nt + k]
            a_re = are_ref[:, k * LANE:(k + 1) * LANE]
            a_im = aim_ref[:, k * LANE:(k + 1) * LANE]
            n_re = a_re * h_re - a_im * h_im + bu_ref[k, rows, :]
            n_im = a_re * h_im + a_im * h_re + bu_ref[nt + k, rows, :]
            bu_ref[k, rows, :] = n_re
            bu_ref[nt + k, rows, :] = n_im
            new_re.append(n_re)
            new_im.append(n_im)
        return tuple(new_re + new_im)

    h_last = lax.fori_loop(0, ts, step, tuple(h_ref[k] for k in range(2 * nt)),
                           unroll=math.gcd(ts, S5_SCAN_UNROLL))
    for k in range(2 * nt):
        h_ref[k] = h_last[k]

    y_parts = []
    for m in range(nb):
        hs = jnp.concatenate([bu_ref[m * per + k] for k in range(per)] +
                             [bu_ref[nt + m * per + k] for k in range(per)], axis=1)
        y_parts.append(_dot(hs.astype(BF16), wc_ref[m]))
    y = jnp.concatenate(y_parts, axis=1) + dsk_ref[...] * u
    g = 0.5 * y * (1.0 + jnp.tanh(math.sqrt(2.0 / math.pi) * (y + 0.044715 * (y * y * y))))
    out = g * _sigmoid(_dot(g.astype(BF16), gw_ref[...]) + gb_ref[...])
    o_ref[...] = out.reshape(ts, bsz * S5_WIDTH).astype(o_ref.dtype)


def _s5(u_tb, bsz, wb, wc, a_re, a_im, d_skip, glu_w, glu_b, *, ts=128):
    seq = u_tb.shape[0]
    blk = ts * bsz
    fixed = lambda shape: pl.BlockSpec(shape, lambda k: (0,) * len(shape))
    return pl.pallas_call(
        functools.partial(_s5_kernel, bsz=bsz),
        grid=(seq // ts,),
        in_specs=[
            pl.BlockSpec((ts, bsz * S5_WIDTH), lambda k: (k, 0)),
            fixed(wb.shape), fixed(wc.shape),
            fixed((1, S5_CH)), fixed((1, S5_CH)),
            fixed((1, S5_WIDTH)), fixed(glu_w.shape), fixed((1, S5_WIDTH)),
        ],
        out_specs=pl.BlockSpec((ts, bsz * S5_WIDTH), lambda k: (k, 0)),
        out_shape=jax.ShapeDtypeStruct((seq, bsz * S5_WIDTH), BF16),
        scratch_shapes=[
            pltpu.VMEM((2 * S5_CH // LANE, blk, LANE), F32),
            pltpu.VMEM((2 * S5_CH // LANE, bsz, LANE), F32),
        ],
        compiler_params=pltpu.CompilerParams(
            dimension_semantics=("arbitrary",), vmem_limit_bytes=VMEM_LIMIT),
        name="s5",
    )(u_tb, wb, wc, a_re.reshape(1, S5_CH), a_im.reshape(1, S5_CH),
      d_skip.reshape(1, S5_WIDTH), glu_w, glu_b.reshape(1, S5_WIDTH))


def _s5_params(a_re, a_im, b_re, b_im, c_re, c_im, log_dt):
    delta = jnp.exp(log_dt)[:, None]
    mag = jnp.exp(a_re * delta)
    ab_re, ab_im = mag * jnp.cos(a_im * delta), mag * jnp.sin(a_im * delta)
    den = a_re * a_re + a_im * a_im
    p_re, p_im = ab_re - 1.0, ab_im
    f_re = (p_re * a_re + p_im * a_im) / den
    f_im = (p_im * a_re - p_re * a_im) / den
    bb_re = f_re[..., None] * b_re - f_im[..., None] * b_im
    bb_im = f_re[..., None] * b_im + f_im[..., None] * b_re
    gpb = LANE // S5_GROUP
    nb = S5_GROUPS // gpb
    eye = jnp.eye(gpb, dtype=F32)
    emb_b = lambda bb: jnp.einsum('mgnj,gh->mgjhn', bb.reshape(nb, gpb, S5_STATE, S5_GROUP),
                                  eye).reshape(nb, LANE, gpb * S5_STATE)
    wb = jnp.concatenate([emb_b(bb_re), emb_b(bb_im)], axis=2)
    emb_c = lambda cc: jnp.einsum('mgin,gh->mgnhi', cc.reshape(nb, gpb, S5_GROUP, S5_STATE),
                                  eye).reshape(nb, gpb * S5_STATE, LANE)
    wc = jnp.concatenate([emb_c(c_re), -emb_c(c_im)], axis=1)
    return ab_re.reshape(-1), ab_im.reshape(-1), wb.astype(BF16), wc.astype(BF16)


def _w_in_segments():
    bc = SSD_GROUPS * SSD_N
    g_z = GDN_QKV
    g_gate = g_z + GDN_QK
    s_z = g_gate + 2 * GDN_HEADS
    s_x = s_z + SSD_INNER
    s_b = s_x + SSD_INNER
    s_c = s_b + bc
    s_dt = s_c + bc
    s5_u = s_dt + SSD_HEADS
    return ((0, GDN_QKV, COL_GDN_QKV), (g_z, GDN_QK, COL_GDN_Z),
            (g_gate, 2 * GDN_HEADS, COL_GDN_GATE), (s_z, SSD_INNER, COL_SSD_Z),
            (s_x, SSD_INNER, COL_SSD_X), (s_b, bc, COL_SSD_B), (s_c, bc, COL_SSD_C),
            (s_dt, SSD_HEADS, COL_SSD_DT), (s5_u, S5_WIDTH, COL_S5_U))


def _pack_conv(gdn_w, ssd_w, ssd_b):
    bc = SSD_GROUPS * SSD_N
    pieces = ((COL_GDN_QKV, gdn_w, None),
              (COL_SSD_X, ssd_w[:, :SSD_INNER], ssd_b[:SSD_INNER]),
              (COL_SSD_B, ssd_w[:, SSD_INNER:SSD_INNER + bc], ssd_b[SSD_INNER:SSD_INNER + bc]),
              (COL_SSD_C, ssd_w[:, SSD_INNER + bc:], ssd_b[SSD_INNER + bc:]))
    cw = jnp.zeros((CONV_K, P_PAD), F32)
    cb = jnp.zeros((1, P_PAD), F32)
    for col, w, b in pieces:
        cw = cw.at[:, col:col + w.shape[1]].set(w.astype(F32))
        if b is not None:
            cb = cb.at[0, col:col + b.shape[0]].set(b.astype(F32))
    return cw, cb


def _pack_kernel(x_ref, o_ref):
    for col in (COL_SSD_DT, COL_GDN_GATE):
        o_ref[:, col:col + LANE] = jnp.zeros((o_ref.shape[0], LANE), o_ref.dtype)
    for src, width, dst in _w_in_segments():
        if width % LANE == 0:
            for r in range(0, width, 2 * LANE):
                w = min(2 * LANE, width - r)
                o_ref[:, dst + r:dst + r + w] = x_ref[src + r:src + r + w, :].T.astype(o_ref.dtype)
        else:
            blk = x_ref[src:src + LANE, :].T
            o_ref[:, dst:dst + width] = blk[:, :width].astype(o_ref.dtype)


def _pack_w_in(w_in, *, kb=256):
    nl, d, c = w_in.shape
    return pl.pallas_call(
        _pack_kernel,
        grid=(nl, d // kb),
        in_specs=[pl.BlockSpec((None, c, kb), lambda l, i: (l, 0, i))],
        out_specs=pl.BlockSpec((None, kb, P_PAD), lambda l, i: (l, i, 0)),
        out_shape=jax.ShapeDtypeStruct((nl, d, P_PAD), BF16),
        compiler_params=pltpu.CompilerParams(
            dimension_semantics=("parallel", "parallel"), vmem_limit_bytes=VMEM_LIMIT),
        name="pack_w_in",
    )(jnp.swapaxes(w_in, 1, 2))


def kernel(x, ffn1_norm, ffn1_w_gate, ffn1_w_up, ffn1_w_down, mix_norm, w_in,
           gdn_conv_w, gdn_a_log, gdn_dt_bias, gdn_norm,
           ssd_conv_w, ssd_conv_b, ssd_a_log, ssd_dt_bias, ssd_d, ssd_norm,
           s5_a_re, s5_a_im, s5_b_re, s5_b_im, s5_c_re, s5_c_im, s5_d, s5_log_dt,
           s5_glu_w, s5_glu_b, w_out, ffn2_norm, ffn2_w_gate, ffn2_w_up, ffn2_w_down,
           final_norm):
    bsz, seq, d = x.shape
    depth = w_in.shape[0]
    h = x
    ffn1_f32 = (ffn1_w_gate, ffn1_w_up, ffn1_w_down)
    ffn2_f32 = (ffn2_w_gate, ffn2_w_up, ffn2_w_down)
    w_next = tuple(_to_bf16(w, 0) for w in ffn1_f32)
    w_in_packed = _pack_w_in(w_in)
    for i in range(depth):
        h, w_next = _ffn(h, ffn1_norm[i], *w_next, final_norm, final_norm=False,
                         cast_next=(*ffn2_f32, i))
        conv_w, conv_b = _pack_conv(gdn_conv_w[i], ssd_conv_w[i], ssd_conv_b[i])
        proj, u_s5 = _inproj(h, mix_norm[i], w_in_packed, i, conv_w, conv_b, bsz, seq)
        o_gdn = _gdn(proj, bsz, seq, gdn_a_log[i], gdn_dt_bias[i], gdn_norm[i])
        o_ssd = _ssd(proj, bsz, seq, ssd_a_log[i], ssd_dt_bias[i], ssd_d[i], ssd_norm[i])
        ab_re, ab_im, wb, wc = _s5_params(s5_a_re[i], s5_a_im[i], s5_b_re[i], s5_b_im[i],
                                          s5_c_re[i], s5_c_im[i], s5_log_dt[i])
        o_s5 = _s5(u_s5, bsz, wb, wc, ab_re, ab_im, s5_d[i], s5_glu_w[i].astype(BF16),
                   s5_glu_b[i])
        wo = w_out[i].astype(BF16)
        h = _outproj(h, o_gdn, o_ssd, o_s5, wo[0:GDN_QK], wo[GDN_QK:GDN_QK + SSD_INNER],
                     wo[GDN_QK + SSD_INNER:])
        last = i == depth - 1
        if last:
            h = _ffn(h, ffn2_norm[i], *w_next, final_norm, final_norm=True,
                     out_batched=(bsz, seq))
        else:
            h, w_next = _ffn(h, ffn2_norm[i], *w_next, final_norm, final_norm=False,
                             cast_next=(*ffn1_f32, i + 1))
    return h
```

```python
import functools
import math

import jax
import jax.numpy as jnp
from jax import lax
from jax.experimental import pallas as pl
from jax.experimental.pallas import tpu as pltpu

F32 = jnp.float32
BF16 = jnp.bfloat16

RMS_EPS = 1e-6
CHUNK = 64
CONV_K = 4

GDN_HEADS = 6
GDN_D = 128
GDN_QK = GDN_HEADS * GDN_D
GDN_QKV = 3 * GDN_QK

SSD_HEADS = 12
SSD_P = 64
SSD_GROUPS = 2
SSD_N = 128
SSD_HPG = SSD_HEADS // SSD_GROUPS
SSD_GW = SSD_HPG * SSD_P
SSD_INNER = SSD_HEADS * SSD_P

S5_WIDTH = 512
S5_GROUP = 16
S5_GROUPS = 32
S5_STATE = 64
S5_CH = S5_GROUPS * S5_STATE
S5_SCAN_UNROLL = 8

LANE = 128

COL_GDN_QKV = 0
COL_GDN_Z = 2304
COL_SSD_Z = 3072
COL_SSD_X = 3840
COL_SSD_B = 4608
COL_SSD_C = 4864
COL_S5_U = 5120
COL_SSD_DT = 5632
COL_GDN_GATE = 5760
P_PAD = 5888

VMEM_LIMIT = 60 * 1024 * 1024

NN = (((1,), (0,)), ((), ()))
NT = (((1,), (1,)), ((), ()))
TN = (((0,), (0,)), ((), ()))


def _dot(a, b, dims=NN):
    return lax.dot_general(a, b, dims, preferred_element_type=F32)


def _hi_lo(a):
    hi = a.astype(BF16)
    lo = (a - hi.astype(F32)).astype(BF16)
    return hi, lo


def _mm(a, b, dims=NN):
    return _dot(a.astype(BF16), b.astype(BF16), dims)


def _mm01(m01, x, dims=NN):
    hi, lo = _hi_lo(x)
    return _dot(m01, hi, dims) + _dot(m01, lo, dims)


def _mm01_t(x, m01, dims=NN):
    hi, lo = _hi_lo(x)
    return _dot(hi, m01, dims) + _dot(lo, m01, dims)


def _sigmoid(x):
    return 1.0 / (1.0 + jnp.exp(-x))


def _silu(x):
    return x * _sigmoid(x)


def _softplus(x):
    return jnp.maximum(x, 0.0) + jnp.log1p(jnp.exp(-jnp.abs(x)))


def _iota(shape, dim):
    return lax.broadcasted_iota(jnp.int32, shape, dim)


def _ffn_kernel(x_ref, nw_ref, wg_ref, wu_ref, wd_ref, fw_ref, *rest, final_norm, n_step_casts,
                n_row_casts):
    n_side = n_step_casts + n_row_casts
    side_in, o_ref, side_out, xn_ref = rest[:n_side], rest[n_side], rest[n_side + 1:-1], rest[-1]
    for src, dst in zip(side_in[:n_step_casts], side_out[:n_step_casts]):
        dst[...] = src[...].astype(BF16)
    j = pl.program_id(1)

    @pl.when(j == 0)
    def _():
        x = x_ref[...]
        ms = jnp.mean(x * x, axis=-1, keepdims=True)
        xn_ref[...] = (x * lax.rsqrt(ms + RMS_EPS) * nw_ref[...]).astype(BF16)
        o_ref[...] = x
        for src, dst in zip(side_in[n_step_casts:], side_out[n_step_casts:]):
            dst[...] = src[...].astype(BF16)

    xn = xn_ref[...]
    g = _dot(xn, wg_ref[...])
    u = _dot(xn, wu_ref[...])
    a = (0.5 * _silu(g) * u).astype(BF16)
    o_ref[...] += _dot(a, wd_ref[...])

    if final_norm:
        @pl.when(j == pl.num_programs(1) - 1)
        def _():
            h = o_ref[...]
            ms = jnp.mean(h * h, axis=-1, keepdims=True)
            o_ref[...] = h * lax.rsqrt(ms + RMS_EPS) * fw_ref[...]


def _cast_kernel(x_ref, o_ref):
    o_ref[...] = x_ref[...].astype(o_ref.dtype)


def _to_bf16(w, layer, *, rows=256):
    _, r, c = w.shape
    return pl.pallas_call(
        _cast_kernel,
        grid=(r // rows,),
        in_specs=[pl.BlockSpec((None, rows, c), lambda i: (layer, i, 0))],
        out_specs=pl.BlockSpec((rows, c), lambda i: (i, 0)),
        out_shape=jax.ShapeDtypeStruct((r, c), BF16),
        compiler_params=pltpu.CompilerParams(
            dimension_semantics=("parallel",), vmem_limit_bytes=VMEM_LIMIT),
        name="to_bf16",
    )(w)


def _ffn(x, nw, wg, wu, wd, fw, *, final_norm, out_batched=None, cast_next=None,
         cast_rows=None, tm=1024, tf=512):
    d = x.shape[-1]
    t = x.size // d
    f = wg.shape[1]
    ni, nj = t // tm, f // tf
    if x.ndim == 3:
        nseg = x.shape[1] // tm
        x_spec = pl.BlockSpec((None, tm, d), lambda i, j: (i // nseg, i % nseg, 0))
    else:
        x_spec = pl.BlockSpec((tm, d), lambda i, j: (i, 0))
    if out_batched is None:
        out_spec = pl.BlockSpec((tm, d), lambda i, j: (i, 0))
        out_shape = jax.ShapeDtypeStruct((t, d), F32)
    else:
        oseg = out_batched[1] // tm
        out_spec = pl.BlockSpec((None, tm, d), lambda i, j: (i // oseg, i % oseg, 0))
        out_shape = jax.ShapeDtypeStruct((*out_batched, d), F32)
    in_specs = [
        x_spec,
        pl.BlockSpec((1, d), lambda i, j: (0, 0)),
        pl.BlockSpec((d, tf), lambda i, j: (0, j)),
        pl.BlockSpec((d, tf), lambda i, j: (0, j)),
        pl.BlockSpec((tf, d), lambda i, j: (j, 0)),
        pl.BlockSpec((1, d), lambda i, j: (0, 0)),
    ]
    operands = [x, nw.reshape(1, d), wg, wu, wd, fw.reshape(1, d)]
    out_specs, out_shapes = [out_spec], [out_shape]
    if cast_next is not None:
        cg, cu, cd, layer = cast_next
        dr = d // ni
        assert dr * ni == d and dr % LANE == 0, (d, ni)
        in_specs += [pl.BlockSpec((None, dr, tf), lambda i, j: (layer, i, j)),
                     pl.BlockSpec((None, dr, tf), lambda i, j: (layer, i, j)),
                     pl.BlockSpec((None, tf, dr), lambda i, j: (layer, j, i))]
        operands += [cg, cu, cd]
        out_specs += [pl.BlockSpec((dr, tf), lambda i, j: (i, j)),
                      pl.BlockSpec((dr, tf), lambda i, j: (i, j)),
                      pl.BlockSpec((tf, dr), lambda i, j: (j, i))]
        out_shapes += [jax.ShapeDtypeStruct((d, f), BF16), jax.ShapeDtypeStruct((d, f), BF16),
                       jax.ShapeDtypeStruct((f, d), BF16)]
    if cast_rows is not None:
        cw, wlayer = cast_rows
        dr = d // ni
        assert dr * ni == d and cw.shape[1:] == (d, d), (d, ni, cw.shape)
        in_specs += [pl.BlockSpec((None, dr, d), lambda i, j: (wlayer, i, 0))]
        operands += [cw]
        out_specs += [pl.BlockSpec((dr, d), lambda i, j: (i, 0))]
        out_shapes += [jax.ShapeDtypeStruct((d, d), BF16)]
    res = pl.pallas_call(
        functools.partial(_ffn_kernel, final_norm=final_norm,
                          n_step_casts=0 if cast_next is None else 3,
                          n_row_casts=0 if cast_rows is None else 1),
        grid=(ni, nj),
        in_specs=in_specs,
        out_specs=out_specs,
        out_shape=out_shapes,
        scratch_shapes=[pltpu.VMEM((tm, d), BF16)],
        compiler_params=pltpu.CompilerParams(
            dimension_semantics=("parallel", "arbitrary"), vmem_limit_bytes=VMEM_LIMIT),
        name="ffn",
    )(*operands)
    return tuple(res) if len(res) > 1 else res[0]


CONV_RANGES = ((COL_SSD_X, SSD_INNER), (COL_SSD_B, SSD_GROUPS * SSD_N),
               (COL_SSD_C, SSD_GROUPS * SSD_N), (COL_GDN_QKV, GDN_QKV))
INPROJ_PIECE = 256


def _inproj_kernel(x_ref, nw_ref, w_ref, cw_ref, cb_ref, o_ref, u5_ref, stage_ref, *, nseg):
    tm = x_ref.shape[0]

    @pl.when(pl.program_id(0) % nseg == 0)
    def _():
        stage_ref[:, 0:8, :] = jnp.zeros((stage_ref.shape[0], 8, LANE), F32)

    x = x_ref[...]
    ms = jnp.mean(x * x, axis=-1, keepdims=True)
    xn = (x * lax.rsqrt(ms + RMS_EPS) * nw_ref[...]).astype(BF16)
    starts = list(range(0, o_ref.shape[1], INPROJ_PIECE))
    piece = lambda col: _dot(xn, w_ref[:, col:col + INPROJ_PIECE])
    nxt = piece(starts[0])
    for idx, col in enumerate(starts):
        cols = slice(col, col + INPROJ_PIECE)
        res, nxt = nxt, (piece(starts[idx + 1]) if idx + 1 < len(starts) else None)
        if any(lo <= col < lo + width for lo, width in CONV_RANGES):
            for lt in range(INPROJ_PIECE // LANE):
                ti = col // LANE + lt
                ln = slice(lt * LANE, (lt + 1) * LANE)
                gl = slice(col + lt * LANE, col + (lt + 1) * LANE)
                stage_ref[ti, 8:, :] = res[:, ln]
                acc = None
                for k in range(CONV_K):
                    lo = k + 8 - (CONV_K - 1)
                    term = stage_ref[ti, pl.ds(lo, tm, stride=1), :] * cw_ref[k:k + 1, gl]
                    acc = term if acc is None else acc + term
                stage_ref[ti, 0:8, :] = res[tm - 8:, ln]
                o_ref[:, gl] = _silu(acc + cb_ref[:, gl])
        else:
            o_ref[:, cols] = res
        if COL_S5_U <= col < COL_S5_U + S5_WIDTH:
            u5_ref[:, col - COL_S5_U:col - COL_S5_U + INPROJ_PIECE] = res


def _inproj(x, nw, w, layer, conv_w, conv_b, bsz, seq, *, tm=256):
    t, d = x.shape
    n = w.shape[2]
    nseg = seq // tm
    assert nseg * tm == seq and n % INPROJ_PIECE == 0, (seq, tm, n)
    fixed = lambda i: (0, 0)
    return pl.pallas_call(
        functools.partial(_inproj_kernel, nseg=nseg),
        grid=(t // tm,),
        in_specs=[
            pl.BlockSpec((tm, d), lambda i: (i, 0)),
            pl.BlockSpec((1, d), fixed),
            pl.BlockSpec((None, d, n), lambda i: (layer, 0, 0), pipeline_mode=pl.Buffered(1)),
            pl.BlockSpec((CONV_K, n), fixed),
            pl.BlockSpec((1, n), fixed),
        ],
        out_specs=[
            pl.BlockSpec((tm, n), lambda i: (i, 0)),
            pl.BlockSpec((tm, S5_WIDTH), lambda i: (i % nseg, i // nseg)),
        ],
        out_shape=[jax.ShapeDtypeStruct((t, n), F32),
                   jax.ShapeDtypeStruct((seq, bsz * S5_WIDTH), F32)],
        scratch_shapes=[pltpu.VMEM((n // LANE, tm + 8, LANE), F32)],
        compiler_params=pltpu.CompilerParams(
            dimension_semantics=("arbitrary",), vmem_limit_bytes=VMEM_LIMIT),
        name="inproj",
    )(x, nw.reshape(1, d), w, conv_w, conv_b)


def _outproj_kernel(h_ref, a_ref, b_ref, c_ref, wa_ref, wb_ref, wc_ref, o_ref):
    acc = _dot(a_ref[...], wa_ref[...])
    acc += _dot(b_ref[...], wb_ref[...])
    acc += _dot(c_ref[...], wc_ref[...])
    o_ref[...] = h_ref[...] + acc


def _outproj(h, oa, ob, oc, w, *, tm=1024):
    t, d = h.shape
    nseg = oc.shape[0] // tm
    row = lambda i: (i, 0)
    na, nb, nc = oa.shape[1], ob.shape[1], S5_WIDTH
    assert na == nb and (na + nb) % nc == 0 and w.shape[0] == na + nb + nc, (na, nb, nc)
    wspec = lambda rows, blk: pl.BlockSpec((rows, d), lambda i: (blk, 0),
                                           pipeline_mode=pl.Buffered(1))
    return pl.pallas_call(
        _outproj_kernel,
        grid=(t // tm,),
        in_specs=[
            pl.BlockSpec((tm, d), row),
            pl.BlockSpec((tm, oa.shape[1]), row),
            pl.BlockSpec((tm, ob.shape[1]), row),
            pl.BlockSpec((tm, S5_WIDTH), lambda i: (i % nseg, i // nseg)),
            wspec(na, 0), wspec(nb, 1), wspec(nc, (na + nb) // nc),
        ],
        out_specs=pl.BlockSpec((tm, d), row),
        out_shape=jax.ShapeDtypeStruct((t, d), F32),
        compiler_params=pltpu.CompilerParams(
            dimension_semantics=("parallel",), vmem_limit_bytes=VMEM_LIMIT),
        name="outproj",
    )(h, oa, ob, oc, w, w, w)


def _gdn_kernel(gpar_ref,
                q_ref, k_ref, v_ref, z_ref, gate_ref,
                nw_ref,
                o_ref,
                u_ref, w_ref, qd_ref, kd_ref, at_ref, eg_ref,
                *, hb, unroll):
    hg = pl.program_id(1)
    seq = q_ref.shape[0]
    n_chunks = seq // CHUNK
    c = CHUNK

    row = _iota((c, c), 0)
    col = _iota((c, c), 1)
    tril = row >= col
    strict = row > col
    ltri = tril.astype(BF16)
    ones = jnp.ones((c, c), BF16)
    eye = (row == col).astype(F32)
    lane = _iota((c, LANE), 1)
    neg_a = -jnp.exp(gpar_ref[0:1, :])
    dt_bias = gpar_ref[1:2, :]

    def prep_stages(trip):
        cis = [trip * unroll + cc for cc in range(unroll)]
        items = [(cis[cc], hh) for cc in range(unroll) for hh in range(hb)]
        lanes = [slice(hh * LANE, (hh + 1) * LANE) for _, hh in items]
        rows = [pl.ds(pl.multiple_of(ci * c, c), c) for ci, _ in items]

        def chunk_gates(ci):
            gt = gate_ref[pl.ds(pl.multiple_of(ci * c, c), c), :]
            return _sigmoid(gt), neg_a * _softplus(gt + dt_bias)

        gates = [chunk_gates(ci) for ci in cis]

        def stage_a(cc, hh, ln):
            ci = cis[cc]
            h = hg * hb + hh
            rw = pl.ds(pl.multiple_of(ci * c, c), c)
            q, k, v = q_ref[rw, ln], k_ref[rw, ln], v_ref[rw, ln]
            qn = q * lax.rsqrt(jnp.sum(q * q, axis=-1, keepdims=True) + RMS_EPS) * (GDN_D ** -0.5)
            kn = k * lax.rsqrt(jnp.sum(k * k, axis=-1, keepdims=True) + RMS_EPS)
            beta_all, g_all = gates[cc]
            beta = jnp.sum(jnp.where(lane == h, beta_all, 0.0), axis=-1, keepdims=True)
            g = jnp.sum(jnp.where(lane == h + GDN_HEADS, g_all, 0.0), axis=-1, keepdims=True)
            return qn, kn, v, beta, jnp.broadcast_to(g, (c, LANE))

        sa = [stage_a(cc, hh, slice(hh * LANE, (hh + 1) * LANE))
              for cc in range(unroll) for hh in range(hb)]
        qn = [x[0] for x in sa]
        kn = [x[1] for x in sa]
        v = [x[2] for x in sa]
        beta = [x[3] for x in sa]
        g_b = [x[4] for x in sa]
        yield
        gi = [_mm01(ltri, g) for g in g_b]
        gj = [_mm01(ones, jnp.where(row <= col, g[:, :c], 0.0)) for g in g_b]
        yield
        kb = [k * b for k, b in zip(kn, beta)]
        kk = [_mm(a, b, NT) for a, b in zip(kb, kn)]
        qk = [_mm(a, b, NT) for a, b in zip(qn, kn)]
        decay = [jnp.where(tril, jnp.exp(a[:, :c] - b), 0.0) for a, b in zip(gi, gj)]
        npow = [-jnp.where(strict, a * d, 0.0) for a, d in zip(kk, decay)]
        tinv = [eye + n for n in npow]
        for _ in range(5):
            yield
            npow = [_mm(n, n, NN) for n in npow]
            tinv = [t + _mm(t, n, NN) for t, n in zip(tinv, npow)]
        yield
        exp_g = [jnp.exp(g) for g in gi]
        rhs = [jnp.concatenate([vv * b, k * e], axis=1) for vv, b, k, e in zip(v, beta, kb, exp_g)]
        uw = [_mm(t, r, NN) for t, r in zip(tinv, rhs)]
        for i, ((ci, hh), rw) in enumerate(zip(items, rows)):
            g_last = jnp.broadcast_to(gi[i][c - 1:c, :], (c, LANE))
            u_ref[hh, rw, :] = uw[i][:, :LANE]
            w_ref[hh, rw, :] = uw[i][:, LANE:].astype(BF16)
            at_ref[hh, rw, :] = jnp.where(tril, qk[i] * decay[i], 0.0).astype(BF16)
            qd_ref[hh, rw, :] = (qn[i] * exp_g[i]).astype(BF16)
            kd_ref[hh, rw, :] = (kn[i] * jnp.exp(g_last - gi[i])).astype(BF16)
            eg_ref[hh, pl.ds(pl.multiple_of(ci * 8, 8), 8), :] = jnp.exp(g_last[0:8, :])

    norm_w = nw_ref[...]

    def scan_stages(trip, states, final):
        for cc in range(unroll):
            ci = trip * unroll + cc
            rw = pl.ds(pl.multiple_of(ci * c, c), c)
            lhs = [jnp.concatenate([w_ref[hh, rw, :], qd_ref[hh, rw, :]], axis=0)
                   for hh in range(hb)]
            ws_qs = [_mm(a, s, NN) for a, s in zip(lhs, states)]
            yield
            v_new = [u_ref[hh, rw, :] - ws_qs[hh][:c] for hh in range(hb)]
            kv = [_mm(kd_ref[hh, rw, :], v_new[hh], TN) for hh in range(hb)]
            av = [_mm(at_ref[hh, rw, :], v_new[hh], NN) for hh in range(hb)]
            new_states = []
            for hh in range(hb):
                eg = eg_ref[hh, pl.ds(pl.multiple_of(ci * 8, 8), 8), :]
                new_states.append(states[hh] * jnp.broadcast_to(eg[0:1, :], (GDN_D, GDN_D)) + kv[hh])
                o = ws_qs[hh][c:] + av[hh]
                ms = jnp.mean(o * o, axis=-1, keepdims=True)
                ln = slice(hh * LANE, (hh + 1) * LANE)
                y = o * lax.rsqrt(ms + RMS_EPS) * norm_w * _silu(z_ref[rw, ln])
                o_ref[rw, ln] = y.astype(o_ref.dtype)
            states = new_states
            yield
        final.extend(states)

    def drain(gen):
        for _ in gen:
            pass

    def fused(trip, states):
        final = []
        scan_gen = scan_stages(trip - 1, states, final)
        for _ in prep_stages(trip):
            next(scan_gen, None)
        drain(scan_gen)
        return tuple(final)

    n_trips = n_chunks // unroll
    drain(prep_stages(0))
    states = lax.fori_loop(1, n_trips, fused,
                           tuple(jnp.zeros((GDN_D, GDN_D), F32) for _ in range(hb)))
    drain(scan_stages(n_trips - 1, states, []))


def _gdn(proj, bsz, seq, a_log, dt_bias, norm_w, *, hb=3, unroll=4):
    t = proj.shape[0]
    unroll = math.gcd(unroll, seq // CHUNK)
    wd = hb * LANE
    q0 = COL_GDN_QKV // wd
    z0 = COL_GDN_Z // wd
    g0 = COL_GDN_GATE // LANE
    ng = GDN_HEADS // hb
    blk = lambda off: pl.BlockSpec((seq, wd), lambda b, h, off=off: (b, off + h))
    pad = lambda v: jnp.pad(v.astype(F32), (GDN_HEADS, LANE - 2 * GDN_HEADS))
    gpar = jnp.stack([pad(a_log), pad(dt_bias)])
    return pl.pallas_call(
        functools.partial(_gdn_kernel, hb=hb, unroll=unroll),
        grid=(bsz, ng),
        in_specs=[
            pl.BlockSpec((2, LANE), lambda b, h: (0, 0)),
            blk(q0), blk(q0 + ng), blk(q0 + 2 * ng), blk(z0),
            pl.BlockSpec((seq, LANE), lambda b, h: (b, g0)),
            pl.BlockSpec((1, LANE), lambda b, h: (0, 0)),
        ],
        out_specs=pl.BlockSpec((seq, wd), lambda b, h: (b, h)),
        out_shape=jax.ShapeDtypeStruct((t, GDN_QK), BF16),
        scratch_shapes=[
            pltpu.VMEM((hb, seq, LANE), F32),
            pltpu.VMEM((hb, seq, LANE), BF16),
            pltpu.VMEM((hb, seq, LANE), BF16),
            pltpu.VMEM((hb, seq, LANE), BF16),
            pltpu.VMEM((hb, seq, CHUNK), BF16),
            pltpu.VMEM((hb, seq // CHUNK * 8, LANE), F32),
        ],
        compiler_params=pltpu.CompilerParams(
            dimension_semantics=("parallel", "arbitrary"), vmem_limit_bytes=VMEM_LIMIT),
        name="gdn",
    )(gpar, proj, proj, proj, proj, proj, norm_w.reshape(1, LANE))


def _ssd_kernel(z_ref, x_ref, b_ref, c_ref, dt_ref,
                alog_ref, dtb_ref, dsk_ref, nw_ref,
                o_ref,
                s_ref, *, unroll):
    grp = pl.program_id(1)
    seq = x_ref.shape[0]
    n_chunks = seq // CHUNK
    c = CHUNK
    gw = SSD_GW

    row = _iota((c, c), 0)
    col = _iota((c, c), 1)
    ltri = (row >= col).astype(BF16)
    ones = jnp.ones((c, c), BF16)
    rowt = _iota((c, gw), 0)
    colt = jnp.bitwise_and(_iota((c, gw), 1), c - 1)
    tril_t = rowt >= colt
    upper_t = rowt <= colt
    expand = (_iota((LANE, gw), 0) == grp * SSD_HPG + lax.shift_right_logical(_iota((LANE, gw), 1), 6)).astype(BF16)
    last_row = (_iota((8, gw), 0) == 7)
    ones_8n = jnp.ones((8, SSD_N), BF16)

    neg_a = -jnp.exp(alog_ref[...])
    dt_bias = dtb_ref[...]
    d_skip = dsk_ref[...]
    norm_w = nw_ref[...]

    s_ref[...] = jnp.zeros((gw, SSD_N), F32)
    heads = [slice(hh * SSD_P, (hh + 1) * SSD_P) for hh in range(SSD_HPG)]

    def body(trip, carry):
        cis = [trip * unroll + cc for cc in range(unroll)]
        rows = [pl.ds(pl.multiple_of(ci * c, c), c) for ci in cis]
        xs = [x_ref[rw, :] for rw in rows]
        bm = [b_ref[rw, :] for rw in rows]
        cm = [c_ref[rw, :] for rw in rows]
        dt = [_softplus(_mm01_t(dt_ref[rw, :], expand) + dt_bias) for rw in rows]
        a = [d * neg_a for d in dt]
        acum = [_mm01(ltri, x) for x in a]
        acum_j = [_mm01(ones, jnp.where(upper_t, x, 0.0)) for x in a]
        cb = [_mm(x, y, NT) for x, y in zip(cm, bm)]
        lm = [jnp.where(tril_t, jnp.exp(x - y), 0.0) for x, y in zip(acum, acum_j)]
        xdt = [x * d for x, d in zip(xs, dt)]
        y_diag = [jnp.concatenate([_mm(cb[i] * lm[i][:, sl], xdt[i][:, sl], NN)
                                   for sl in heads], axis=1) for i in range(unroll)]
        a_last = [jnp.broadcast_to(x[c - 1:c, :], (c, gw)) for x in acum]
        states = [_mm(xdt[i] * jnp.exp(a_last[i] - acum[i]), bm[i], TN)
                  for i in range(unroll)]
        dec_col = [_mm01_t(jnp.where(last_row, jnp.exp(x[c - 8:, :]), 0.0), ones_8n, TN)
                   for x in acum]
        s_prev = s_ref[...]
        for i in range(unroll):
            y_off = _mm(cm[i], s_prev, NT) * jnp.exp(acum[i])
            s_prev = s_prev * dec_col[i] + states[i]
            y = y_diag[i] + y_off + d_skip * xs[i]
            y = y * _silu(z_ref[rows[i], :])
            ms = jnp.mean(y * y, axis=-1, keepdims=True)
            o_ref[rows[i], :] = (y * lax.rsqrt(ms + RMS_EPS) * norm_w).astype(o_ref.dtype)
        s_ref[...] = s_prev
        return carry

    lax.fori_loop(0, n_chunks // unroll, body, 0)


def _ssd(proj, bsz, seq, a_log, dt_bias, d_skip, norm_w, *, unroll=8):
    t = proj.shape[0]
    gw = SSD_GW
    unroll = math.gcd(unroll, seq // CHUNK)
    rep = lambda v: jnp.repeat(v.astype(F32), SSD_P).reshape(1, SSD_INNER)
    wide = lambda off: pl.BlockSpec((seq, gw), lambda b, g, off=off: (b, off + g))
    lane = lambda off: pl.BlockSpec((seq, LANE), lambda b, g, off=off: (b, off + g))
    chan = pl.BlockSpec((1, gw), lambda b, g: (0, g))
    return pl.pallas_call(
        functools.partial(_ssd_kernel, unroll=unroll),
        grid=(bsz, SSD_GROUPS),
        in_specs=[
            wide(COL_SSD_Z // gw), wide(COL_SSD_X // gw),
            lane(COL_SSD_B // LANE), lane(COL_SSD_C // LANE),
            pl.BlockSpec((seq, LANE), lambda b, g: (b, COL_SSD_DT // LANE)),
            chan, chan, chan, chan,
        ],
        out_specs=pl.BlockSpec((seq, gw), lambda b, g: (b, g)),
        out_shape=jax.ShapeDtypeStruct((t, SSD_INNER), BF16),
        scratch_shapes=[pltpu.VMEM((gw, SSD_N), F32)],
        compiler_params=pltpu.CompilerParams(
            dimension_semantics=("parallel", "arbitrary"), vmem_limit_bytes=VMEM_LIMIT),
        name="ssd",
    )(proj, proj, proj, proj, proj,
      rep(a_log), rep(dt_bias), rep(d_skip), norm_w.reshape(1, SSD_INNER))


def _s5_kernel(u_ref, wb_ref, wc_ref, are_ref, aim_ref, dsk_ref, gw_ref, gb_ref,
               o_ref, bu_ref, h_ref, *, bsz):
    ts = u_ref.shape[0]
    nt = S5_CH // LANE
    nb = S5_WIDTH // LANE
    per = nt // nb
    half = per * LANE

    @pl.when(pl.program_id(0) == 0)
    def _():
        h_ref[...] = jnp.zeros(h_ref.shape, F32)

    u = u_ref[...].reshape(ts * bsz, S5_WIDTH)
    u_bf = u.astype(BF16)
    for m in range(nb):
        res = _dot(u_bf[:, m * LANE:(m + 1) * LANE], wb_ref[m])
        for k in range(per):
            bu_ref[m * per + k] = res[:, k * LANE:(k + 1) * LANE]
            bu_ref[nt + m * per + k] = res[:, half + k * LANE:half + (k + 1) * LANE]

    def step(t, carry):
        rows = pl.ds(pl.multiple_of(t * bsz, bsz), bsz)
        new_re, new_im = [], []
        for k in range(nt):
            h_re, h_im = carry[k], carry[nt + k]
            a_re = are_ref[:, k * LANE:(k + 1) * LANE]
            a_im = aim_ref[:, k * LANE:(k + 1) * LANE]
            n_re = a_re * h_re - a_im * h_im + bu_ref[k, rows, :]
            n_im = a_re * h_im + a_im * h_re + bu_ref[nt + k, rows, :]
            bu_ref[k, rows, :] = n_re
            bu_ref[nt + k, rows, :] = n_im
            new_re.append(n_re)
            new_im.append(n_im)
        return tuple(new_re + new_im)

    h_last = lax.fori_loop(0, ts, step, tuple(h_ref[k] for k in range(2 * nt)),
                           unroll=math.gcd(ts, S5_SCAN_UNROLL))
    for k in range(2 * nt):
        h_ref[k] = h_last[k]

    y_parts = []
    for m in range(nb):
        hs = jnp.concatenate([bu_ref[m * per + k] for k in range(per)] +
                             [bu_ref[nt + m * per + k] for k in range(per)], axis=1)
        y_parts.append(_dot(hs.astype(BF16), wc_ref[m]))
    y = jnp.concatenate(y_parts, axis=1) + dsk_ref[...] * u
    g = 0.5 * y * (1.0 + jnp.tanh(math.sqrt(2.0 / math.pi) * (y + 0.044715 * (y * y * y))))
    out = g * _sigmoid(_dot(g.astype(BF16), gw_ref[...]) + gb_ref[...])
    o_ref[...] = out.reshape(ts, bsz * S5_WIDTH).astype(o_ref.dtype)


def _s5(u_tb, bsz, wb, wc, a_re, a_im, d_skip, glu_w, glu_b, *, ts=128):
    seq = u_tb.shape[0]
    blk = ts * bsz
    fixed = lambda shape: pl.BlockSpec(shape, lambda k: (0,) * len(shape))
    return pl.pallas_call(
        functools.partial(_s5_kernel, bsz=bsz),
        grid=(seq // ts,),
        in_specs=[
            pl.BlockSpec((ts, bsz * S5_WIDTH), lambda k: (k, 0)),
            fixed(wb.shape), fixed(wc.shape),
            fixed((1, S5_CH)), fixed((1, S5_CH)),
            fixed((1, S5_WIDTH)), fixed(glu_w.shape), fixed((1, S5_WIDTH)),
        ],
        out_specs=pl.BlockSpec((ts, bsz * S5_WIDTH), lambda k: (k, 0)),
        out_shape=jax.ShapeDtypeStruct((seq, bsz * S5_WIDTH), BF16),
        scratch_shapes=[
            pltpu.VMEM((2 * S5_CH // LANE, blk, LANE), F32),
            pltpu.VMEM((2 * S5_CH // LANE, bsz, LANE), F32),
        ],
        compiler_params=pltpu.CompilerParams(
            dimension_semantics=("arbitrary",), vmem_limit_bytes=VMEM_LIMIT),
        name="s5",
    )(u_tb, wb, wc, a_re.reshape(1, S5_CH), a_im.reshape(1, S5_CH),
      d_skip.reshape(1, S5_WIDTH), glu_w, glu_b.reshape(1, S5_WIDTH))


def _s5_params(a_re, a_im, b_re, b_im, c_re, c_im, log_dt):
    delta = jnp.exp(log_dt)[:, None]
    mag = jnp.exp(a_re * delta)
    ab_re, ab_im = mag * jnp.cos(a_im * delta), mag * jnp.sin(a_im * delta)
    den = a_re * a_re + a_im * a_im
    p_re, p_im = ab_re - 1.0, ab_im
    f_re = (p_re * a_re + p_im * a_im) / den
    f_im = (p_im * a_re - p_re * a_im) / den
    bb_re = f_re[..., None] * b_re - f_im[..., None] * b_im
    bb_im = f_re[..., None] * b_im + f_im[..., None] * b_re
    gpb = LANE // S5_GROUP
    nb = S5_GROUPS // gpb
    eye = jnp.eye(gpb, dtype=F32)
    emb_b = lambda bb: jnp.einsum('mgnj,gh->mgjhn', bb.reshape(nb, gpb, S5_STATE, S5_GROUP),
                                  eye).reshape(nb, LANE, gpb * S5_STATE)
    wb = jnp.concatenate([emb_b(bb_re), emb_b(bb_im)], axis=2)
    emb_c = lambda cc: jnp.einsum('mgin,gh->mgnhi', cc.reshape(nb, gpb, S5_GROUP, S5_STATE),
                                  eye).reshape(nb, gpb * S5_STATE, LANE)
    wc = jnp.concatenate([emb_c(c_re), -emb_c(c_im)], axis=1)
    return ab_re.reshape(-1), ab_im.reshape(-1), wb.astype(BF16), wc.astype(BF16)


def _w_in_segments():
    bc = SSD_GROUPS * SSD_N
    g_z = GDN_QKV
    g_gate = g_z + GDN_QK
    s_z = g_gate + 2 * GDN_HEADS
    s_x = s_z + SSD_INNER
    s_b = s_x + SSD_INNER
    s_c = s_b + bc
    s_dt = s_c + bc
    s5_u = s_dt + SSD_HEADS
    return ((0, GDN_QKV, COL_GDN_QKV), (g_z, GDN_QK, COL_GDN_Z),
            (g_gate, 2 * GDN_HEADS, COL_GDN_GATE), (s_z, SSD_INNER, COL_SSD_Z),
            (s_x, SSD_INNER, COL_SSD_X), (s_b, bc, COL_SSD_B), (s_c, bc, COL_SSD_C),
            (s_dt, SSD_HEADS, COL_SSD_DT), (s5_u, S5_WIDTH, COL_S5_U))


def _pack_conv(gdn_w, ssd_w, ssd_b):
    bc = SSD_GROUPS * SSD_N
    pieces = ((COL_GDN_QKV, gdn_w, None),
              (COL_SSD_X, ssd_w[:, :SSD_INNER], ssd_b[:SSD_INNER]),
              (COL_SSD_B, ssd_w[:, SSD_INNER:SSD_INNER + bc], ssd_b[SSD_INNER:SSD_INNER + bc]),
              (COL_SSD_C, ssd_w[:, SSD_INNER + bc:], ssd_b[SSD_INNER + bc:]))
    cw = jnp.zeros((CONV_K, P_PAD), F32)
    cb = jnp.zeros((1, P_PAD), F32)
    for col, w, b in pieces:
        cw = cw.at[:, col:col + w.shape[1]].set(w.astype(F32))
        if b is not None:
            cb = cb.at[0, col:col + b.shape[0]].set(b.astype(F32))
    return cw, cb


def _pack_kernel(x_ref, o_ref):
    for col in (COL_SSD_DT, COL_GDN_GATE):
        o_ref[:, col:col + LANE] = jnp.zeros((o_ref.shape[0], LANE), o_ref.dtype)
    for src, width, dst in _w_in_segments():
        if width % LANE == 0:
            for r in range(0, width, 2 * LANE):
                w = min(2 * LANE, width - r)
                o_ref[:, dst + r:dst + r + w] = x_ref[src + r:src + r + w, :].T.astype(o_ref.dtype)
        else:
            blk = x_ref[src:src + LANE, :].T
            o_ref[:, dst:dst + width] = blk[:, :width].astype(o_ref.dtype)


def _pack_w_in(w_in, *, kb=256):
    nl, d, c = w_in.shape
    return pl.pallas_call(
        _pack_kernel,
        grid=(nl, d // kb),
        in_specs=[pl.BlockSpec((None, c, kb), lambda l, i: (l, 0, i))],
        out_specs=pl.BlockSpec((None, kb, P_PAD), lambda l, i: (l, i, 0)),
        out_shape=jax.ShapeDtypeStruct((nl, d, P_PAD), BF16),
        compiler_params=pltpu.CompilerParams(
            dimension_semantics=("parallel", "parallel"), vmem_limit_bytes=VMEM_LIMIT),
        name="pack_w_in",
    )(jnp.swapaxes(w_in, 1, 2))


def kernel(x, ffn1_norm, ffn1_w_gate, ffn1_w_up, ffn1_w_down, mix_norm, w_in,
           gdn_conv_w, gdn_a_log, gdn_dt_bias, gdn_norm,
           ssd_conv_w, ssd_conv_b, ssd_a_log, ssd_dt_bias, ssd_d, ssd_norm,
           s5_a_re, s5_a_im, s5_b_re, s5_b_im, s5_c_re, s5_c_im, s5_d, s5_log_dt,
           s5_glu_w, s5_glu_b, w_out, ffn2_norm, ffn2_w_gate, ffn2_w_up, ffn2_w_down,
           final_norm):
    bsz, seq, d = x.shape
    depth = w_in.shape[0]
    h = x
    ffn1_f32 = (ffn1_w_gate, ffn1_w_up, ffn1_w_down)
    ffn2_f32 = (ffn2_w_gate, ffn2_w_up, ffn2_w_down)
    w_next = tuple(_to_bf16(w, 0) for w in ffn1_f32)
    w_in_packed = _pack_w_in(w_in)
    for i in range(depth):
        h, *w_next, w_out_bf = _ffn(h, ffn1_norm[i], *w_next, final_norm, final_norm=False,
                                    cast_next=(*ffn2_f32, i), cast_rows=(w_out, i))
        conv_w, conv_b = _pack_conv(gdn_conv_w[i], ssd_conv_w[i], ssd_conv_b[i])
        proj, u_s5 = _inproj(h, mix_norm[i], w_in_packed, i, conv_w, conv_b, bsz, seq)
        o_gdn = _gdn(proj, bsz, seq, gdn_a_log[i], gdn_dt_bias[i], gdn_norm[i])
        o_ssd = _ssd(proj, bsz, seq, ssd_a_log[i], ssd_dt_bias[i], ssd_d[i], ssd_norm[i])
        ab_re, ab_im, wb, wc = _s5_params(s5_a_re[i], s5_a_im[i], s5_b_re[i], s5_b_im[i],
                                          s5_c_re[i], s5_c_im[i], s5_log_dt[i])
        o_s5 = _s5(u_s5, bsz, wb, wc, ab_re, ab_im, s5_d[i], s5_glu_w[i].astype(BF16),
                   s5_glu_b[i])
        h = _outproj(h, o_gdn, o_ssd, o_s5, w_out_bf)
        last = i == depth - 1
        if last:
            h = _ffn(h, ffn2_norm[i], *w_next, final_norm, final_norm=True,
                     out_batched=(bsz, seq))
        else:
            h, *w_next = _ffn(h, ffn2_norm[i], *w_next, final_norm, final_norm=False,
                              cast_next=(*ffn1_f32, i + 1))
    return h
```

```python
import functools
import math

import jax
import jax.numpy as jnp
from jax import lax
from jax.experimental import pallas as pl
from jax.experimental.pallas import tpu as pltpu

F32 = jnp.float32
BF16 = jnp.bfloat16

RMS_EPS = 1e-6
CHUNK = 64
CONV_K = 4

GDN_HEADS = 6
GDN_D = 128
GDN_QK = GDN_HEADS * GDN_D
GDN_QKV = 3 * GDN_QK

SSD_HEADS = 12
SSD_P = 64
SSD_GROUPS = 2
SSD_N = 128
SSD_HPG = SSD_HEADS // SSD_GROUPS
SSD_GW = SSD_HPG * SSD_P
SSD_INNER = SSD_HEADS * SSD_P

S5_WIDTH = 512
S5_GROUP = 16
S5_GROUPS = 32
S5_STATE = 64
S5_CH = S5_GROUPS * S5_STATE
S5_SCAN_UNROLL = 16

LANE = 128

COL_GDN_QKV = 0
COL_GDN_Z = 2304
COL_SSD_Z = 3072
COL_SSD_X = 3840
COL_SSD_B = 4608
COL_SSD_C = 4864
COL_S5_U = 5120
COL_SSD_DT = 5632
COL_GDN_GATE = 5760
P_PAD = 5888

VMEM_LIMIT = 60 * 1024 * 1024

NN = (((1,), (0,)), ((), ()))
NT = (((1,), (1,)), ((), ()))
TN = (((0,), (0,)), ((), ()))


def _dot(a, b, dims=NN):
    return lax.dot_general(a, b, dims, preferred_element_type=F32)


def _hi_lo(a):
    hi = a.astype(BF16)
    lo = (a - hi.astype(F32)).astype(BF16)
    return hi, lo


def _mm(a, b, dims=NN):
    return _dot(a.astype(BF16), b.astype(BF16), dims)


def _mm01(m01, x, dims=NN):
    hi, lo = _hi_lo(x)
    return _dot(m01, hi, dims) + _dot(m01, lo, dims)


def _mm01_t(x, m01, dims=NN):
    hi, lo = _hi_lo(x)
    return _dot(hi, m01, dims) + _dot(lo, m01, dims)


def _sigmoid(x):
    return 1.0 / (1.0 + jnp.exp(-x))


def _silu(x):
    return x * _sigmoid(x)


def _softplus(x):
    return jnp.maximum(x, 0.0) + jnp.log1p(jnp.exp(-jnp.abs(x)))


def _iota(shape, dim):
    return lax.broadcasted_iota(jnp.int32, shape, dim)


def _ffn_kernel(x_ref, nw_ref, wg_ref, wu_ref, wd_ref, fw_ref, *rest, final_norm, side_cast):
    if side_cast:
        cg_ref, cu_ref, cd_ref, o_ref, og_ref, ou_ref, od_ref, xn_ref = rest
        og_ref[...] = cg_ref[...].astype(BF16)
        ou_ref[...] = cu_ref[...].astype(BF16)
        od_ref[...] = cd_ref[...].astype(BF16)
    else:
        o_ref, xn_ref = rest
    j = pl.program_id(1)

    @pl.when(j == 0)
    def _():
        x = x_ref[...]
        ms = jnp.mean(x * x, axis=-1, keepdims=True)
        xn_ref[...] = (x * lax.rsqrt(ms + RMS_EPS) * nw_ref[...]).astype(BF16)
        o_ref[...] = x

    xn = xn_ref[...]
    g = _dot(xn, wg_ref[...])
    u = _dot(xn, wu_ref[...])
    a = (0.5 * _silu(g) * u).astype(BF16)
    o_ref[...] += _dot(a, wd_ref[...])

    if final_norm:
        @pl.when(j == pl.num_programs(1) - 1)
        def _():
            h = o_ref[...]
            ms = jnp.mean(h * h, axis=-1, keepdims=True)
            o_ref[...] = h * lax.rsqrt(ms + RMS_EPS) * fw_ref[...]


def _cast_kernel(x_ref, o_ref):
    o_ref[...] = x_ref[...].astype(o_ref.dtype)


def _to_bf16(w, layer, *, rows=256):
    _, r, c = w.shape
    return pl.pallas_call(
        _cast_kernel,
        grid=(r // rows,),
        in_specs=[pl.BlockSpec((None, rows, c), lambda i: (layer, i, 0))],
        out_specs=pl.BlockSpec((rows, c), lambda i: (i, 0)),
        out_shape=jax.ShapeDtypeStruct((r, c), BF16),
        compiler_params=pltpu.CompilerParams(
            dimension_semantics=("parallel",), vmem_limit_bytes=VMEM_LIMIT),
        name="to_bf16",
    )(w)


def _ffn(x, nw, wg, wu, wd, fw, *, final_norm, out_batched=None, cast_next=None,
         tm=1024, tf=512):
    d = x.shape[-1]
    t = x.size // d
    f = wg.shape[1]
    ni, nj = t // tm, f // tf
    if x.ndim == 3:
        nseg = x.shape[1] // tm
        x_spec = pl.BlockSpec((None, tm, d), lambda i, j: (i // nseg, i % nseg, 0))
    else:
        x_spec = pl.BlockSpec((tm, d), lambda i, j: (i, 0))
    if out_batched is None:
        out_spec = pl.BlockSpec((tm, d), lambda i, j: (i, 0))
        out_shape = jax.ShapeDtypeStruct((t, d), F32)
    else:
        oseg = out_batched[1] // tm
        out_spec = pl.BlockSpec((None, tm, d), lambda i, j: (i // oseg, i % oseg, 0))
        out_shape = jax.ShapeDtypeStruct((*out_batched, d), F32)
    in_specs = [
        x_spec,
        pl.BlockSpec((1, d), lambda i, j: (0, 0)),
        pl.BlockSpec((d, tf), lambda i, j: (0, j)),
        pl.BlockSpec((d, tf), lambda i, j: (0, j)),
        pl.BlockSpec((tf, d), lambda i, j: (j, 0)),
        pl.BlockSpec((1, d), lambda i, j: (0, 0)),
    ]
    operands = [x, nw.reshape(1, d), wg, wu, wd, fw.reshape(1, d)]
    out_specs, out_shapes = [out_spec], [out_shape]
    if cast_next is not None:
        cg, cu, cd, layer = cast_next
        dr = d // ni
        assert dr * ni == d and dr % LANE == 0, (d, ni)
        in_specs += [pl.BlockSpec((None, dr, tf), lambda i, j: (layer, i, j)),
                     pl.BlockSpec((None, dr, tf), lambda i, j: (layer, i, j)),
                     pl.BlockSpec((None, tf, dr), lambda i, j: (layer, j, i))]
        operands += [cg, cu, cd]
        out_specs += [pl.BlockSpec((dr, tf), lambda i, j: (i, j)),
                      pl.BlockSpec((dr, tf), lambda i, j: (i, j)),
                      pl.BlockSpec((tf, dr), lambda i, j: (j, i))]
        out_shapes += [jax.ShapeDtypeStruct((d, f), BF16), jax.ShapeDtypeStruct((d, f), BF16),
                       jax.ShapeDtypeStruct((f, d), BF16)]
    res = pl.pallas_call(
        functools.partial(_ffn_kernel, final_norm=final_norm, side_cast=cast_next is not None),
        grid=(ni, nj),
        in_specs=in_specs,
        out_specs=out_specs,
        out_shape=out_shapes,
        scratch_shapes=[pltpu.VMEM((tm, d), BF16)],
        compiler_params=pltpu.CompilerParams(
            dimension_semantics=("parallel", "arbitrary"), vmem_limit_bytes=VMEM_LIMIT),
        name="ffn",
    )(*operands)
    return (res[0], tuple(res[1:])) if cast_next is not None else res[0]


CONV_RANGES = ((COL_SSD_X, SSD_INNER), (COL_SSD_B, SSD_GROUPS * SSD_N),
               (COL_SSD_C, SSD_GROUPS * SSD_N), (COL_GDN_QKV, GDN_QKV))
INPROJ_PIECE = 256


def _inproj_kernel(x_ref, nw_ref, w_ref, cw_ref, cb_ref, o_ref, u5_ref, stage_ref, *, nseg):
    tm = x_ref.shape[0]

    @pl.when(pl.program_id(0) % nseg == 0)
    def _():
        stage_ref[:, 0:8, :] = jnp.zeros((stage_ref.shape[0], 8, LANE), F32)

    x = x_ref[...]
    ms = jnp.mean(x * x, axis=-1, keepdims=True)
    xn = (x * lax.rsqrt(ms + RMS_EPS) * nw_ref[...]).astype(BF16)
    starts = list(range(0, o_ref.shape[1], INPROJ_PIECE))
    piece = lambda col: _dot(xn, w_ref[:, col:col + INPROJ_PIECE])
    nxt = piece(starts[0])
    for idx, col in enumerate(starts):
        cols = slice(col, col + INPROJ_PIECE)
        res, nxt = nxt, (piece(starts[idx + 1]) if idx + 1 < len(starts) else None)
        if any(lo <= col < lo + width for lo, width in CONV_RANGES):
            for lt in range(INPROJ_PIECE // LANE):
                ti = col // LANE + lt
                ln = slice(lt * LANE, (lt + 1) * LANE)
                gl = slice(col + lt * LANE, col + (lt + 1) * LANE)
                stage_ref[ti, 8:, :] = res[:, ln]
                acc = None
                for k in range(CONV_K):
                    lo = k + 8 - (CONV_K - 1)
                    term = stage_ref[ti, pl.ds(lo, tm, stride=1), :] * cw_ref[k:k + 1, gl]
                    acc = term if acc is None else acc + term
                stage_ref[ti, 0:8, :] = res[tm - 8:, ln]
                o_ref[:, gl] = _silu(acc + cb_ref[:, gl])
        else:
            o_ref[:, cols] = res
        if COL_S5_U <= col < COL_S5_U + S5_WIDTH:
            u5_ref[:, col - COL_S5_U:col - COL_S5_U + INPROJ_PIECE] = res


def _inproj(x, nw, w, layer, conv_w, conv_b, bsz, seq, *, tm=256):
    t, d = x.shape
    n = w.shape[2]
    nseg = seq // tm
    assert nseg * tm == seq and n % INPROJ_PIECE == 0, (seq, tm, n)
    fixed = lambda i: (0, 0)
    return pl.pallas_call(
        functools.partial(_inproj_kernel, nseg=nseg),
        grid=(t // tm,),
        in_specs=[
            pl.BlockSpec((tm, d), lambda i: (i, 0)),
            pl.BlockSpec((1, d), fixed),
            pl.BlockSpec((None, d, n), lambda i: (layer, 0, 0), pipeline_mode=pl.Buffered(1)),
            pl.BlockSpec((CONV_K, n), fixed),
            pl.BlockSpec((1, n), fixed),
        ],
        out_specs=[
            pl.BlockSpec((tm, n), lambda i: (i, 0)),
            pl.BlockSpec((tm, S5_WIDTH), lambda i: (i % nseg, i // nseg)),
        ],
        out_shape=[jax.ShapeDtypeStruct((t, n), F32),
                   jax.ShapeDtypeStruct((seq, bsz * S5_WIDTH), F32)],
        scratch_shapes=[pltpu.VMEM((n // LANE, tm + 8, LANE), F32)],
        compiler_params=pltpu.CompilerParams(
            dimension_semantics=("arbitrary",), vmem_limit_bytes=VMEM_LIMIT),
        name="inproj",
    )(x, nw.reshape(1, d), w, conv_w, conv_b)


def _outproj_kernel(h_ref, a_ref, b_ref, c_ref, wa_ref, wb_ref, wc_ref, o_ref):
    acc = _dot(a_ref[...], wa_ref[...])
    acc += _dot(b_ref[...], wb_ref[...])
    acc += _dot(c_ref[...], wc_ref[...])
    o_ref[...] = h_ref[...] + acc


def _outproj(h, oa, ob, oc, wa, wb, wc, *, tm=1024):
    t, d = h.shape
    nseg = oc.shape[0] // tm
    row = lambda i: (i, 0)
    fixed = lambda i: (0, 0)
    return pl.pallas_call(
        _outproj_kernel,
        grid=(t // tm,),
        in_specs=[
            pl.BlockSpec((tm, d), row),
            pl.BlockSpec((tm, oa.shape[1]), row),
            pl.BlockSpec((tm, ob.shape[1]), row),
            pl.BlockSpec((tm, S5_WIDTH), lambda i: (i % nseg, i // nseg)),
            pl.BlockSpec(wa.shape, fixed, pipeline_mode=pl.Buffered(1)),
            pl.BlockSpec(wb.shape, fixed, pipeline_mode=pl.Buffered(1)),
            pl.BlockSpec(wc.shape, fixed, pipeline_mode=pl.Buffered(1)),
        ],
        out_specs=pl.BlockSpec((tm, d), row),
        out_shape=jax.ShapeDtypeStruct((t, d), F32),
        compiler_params=pltpu.CompilerParams(
            dimension_semantics=("parallel",), vmem_limit_bytes=VMEM_LIMIT),
        name="outproj",
    )(h, oa, ob, oc, wa, wb, wc)


def _gdn_kernel(gpar_ref,
                q_ref, k_ref, v_ref, z_ref, gate_ref,
                nw_ref,
                o_ref,
                u_ref, w_ref, qd_ref, kd_ref, at_ref, eg_ref,
                *, hb, unroll):
    hg = pl.program_id(1)
    seq = q_ref.shape[0]
    n_chunks = seq // CHUNK
    c = CHUNK

    row = _iota((c, c), 0)
    col = _iota((c, c), 1)
    tril = row >= col
    strict = row > col
    ltri = tril.astype(BF16)
    ones = jnp.ones((c, c), BF16)
    eye = (row == col).astype(F32)
    lane = _iota((c, LANE), 1)
    neg_a = -jnp.exp(gpar_ref[0:1, :])
    dt_bias = gpar_ref[1:2, :]

    def prep_stages(trip):
        cis = [trip * unroll + cc for cc in range(unroll)]
        items = [(cis[cc], hh) for cc in range(unroll) for hh in range(hb)]
        lanes = [slice(hh * LANE, (hh + 1) * LANE) for _, hh in items]
        rows = [pl.ds(pl.multiple_of(ci * c, c), c) for ci, _ in items]

        def chunk_gates(ci):
            gt = gate_ref[pl.ds(pl.multiple_of(ci * c, c), c), :]
            return _sigmoid(gt), neg_a * _softplus(gt + dt_bias)

        gates = [chunk_gates(ci) for ci in cis]

        def stage_a(cc, hh, ln):
            ci = cis[cc]
            h = hg * hb + hh
            rw = pl.ds(pl.multiple_of(ci * c, c), c)
            q, k, v = q_ref[rw, ln], k_ref[rw, ln], v_ref[rw, ln]
            qn = q * lax.rsqrt(jnp.sum(q * q, axis=-1, keepdims=True) + RMS_EPS) * (GDN_D ** -0.5)
            kn = k * lax.rsqrt(jnp.sum(k * k, axis=-1, keepdims=True) + RMS_EPS)
            beta_all, g_all = gates[cc]
            beta = jnp.sum(jnp.where(lane == h, beta_all, 0.0), axis=-1, keepdims=True)
            g = jnp.sum(jnp.where(lane == h + GDN_HEADS, g_all, 0.0), axis=-1, keepdims=True)
            return qn, kn, v, beta, jnp.broadcast_to(g, (c, LANE))

        sa = [stage_a(cc, hh, slice(hh * LANE, (hh + 1) * LANE))
              for cc in range(unroll) for hh in range(hb)]
        qn = [x[0] for x in sa]
        kn = [x[1] for x in sa]
        v = [x[2] for x in sa]
        beta = [x[3] for x in sa]
        g_b = [x[4] for x in sa]
        yield
        gi = [_mm01(ltri, g) for g in g_b]
        gj = [_mm01(ones, jnp.where(row <= col, g[:, :c], 0.0)) for g in g_b]
        yield
        kb = [k * b for k, b in zip(kn, beta)]
        kk = [_mm(a, b, NT) for a, b in zip(kb, kn)]
        qk = [_mm(a, b, NT) for a, b in zip(qn, kn)]
        decay = [jnp.where(tril, jnp.exp(a[:, :c] - b), 0.0) for a, b in zip(gi, gj)]
        npow = [-jnp.where(strict, a * d, 0.0) for a, d in zip(kk, decay)]
        tinv = [eye + n for n in npow]
        for _ in range(5):
            yield
            npow = [_mm(n, n, NN) for n in npow]
            tinv = [t + _mm(t, n, NN) for t, n in zip(tinv, npow)]
        yield
        exp_g = [jnp.exp(g) for g in gi]
        rhs = [jnp.concatenate([vv * b, k * e], axis=1) for vv, b, k, e in zip(v, beta, kb, exp_g)]
        uw = [_mm(t, r, NN) for t, r in zip(tinv, rhs)]
        for i, ((ci, hh), rw) in enumerate(zip(items, rows)):
            g_last = jnp.broadcast_to(gi[i][c - 1:c, :], (c, LANE))
            u_ref[hh, rw, :] = uw[i][:, :LANE]
            w_ref[hh, rw, :] = uw[i][:, LANE:].astype(BF16)
            at_ref[hh, rw, :] = jnp.where(tril, qk[i] * decay[i], 0.0).astype(BF16)
            qd_ref[hh, rw, :] = (qn[i] * exp_g[i]).astype(BF16)
            kd_ref[hh, rw, :] = (kn[i] * jnp.exp(g_last - gi[i])).astype(BF16)
            eg_ref[hh, pl.ds(pl.multiple_of(ci * 8, 8), 8), :] = jnp.exp(g_last[0:8, :])

    norm_w = nw_ref[...]

    def scan_stages(trip, states, final):
        for cc in range(unroll):
            ci = trip * unroll + cc
            rw = pl.ds(pl.multiple_of(ci * c, c), c)
            lhs = [jnp.concatenate([w_ref[hh, rw, :], qd_ref[hh, rw, :]], axis=0)
                   for hh in range(hb)]
            ws_qs = [_mm(a, s, NN) for a, s in zip(lhs, states)]
            yield
            v_new = [u_ref[hh, rw, :] - ws_qs[hh][:c] for hh in range(hb)]
            kv = [_mm(kd_ref[hh, rw, :], v_new[hh], TN) for hh in range(hb)]
            av = [_mm(at_ref[hh, rw, :], v_new[hh], NN) for hh in range(hb)]
            new_states = []
            for hh in range(hb):
                eg = eg_ref[hh, pl.ds(pl.multiple_of(ci * 8, 8), 8), :]
                new_states.append(states[hh] * jnp.broadcast_to(eg[0:1, :], (GDN_D, GDN_D)) + kv[hh])
                o = ws_qs[hh][c:] + av[hh]
                ms = jnp.mean(o * o, axis=-1, keepdims=True)
                ln = slice(hh * LANE, (hh + 1) * LANE)
                y = o * lax.rsqrt(ms + RMS_EPS) * norm_w * _silu(z_ref[rw, ln])
                o_ref[rw, ln] = y.astype(o_ref.dtype)
            states = new_states
            yield
        final.extend(states)

    def drain(gen):
        for _ in gen:
            pass

    def fused(trip, states):
        final = []
        scan_gen = scan_stages(trip - 1, states, final)
        for _ in prep_stages(trip):
            next(scan_gen, None)
        drain(scan_gen)
        return tuple(final)

    n_trips = n_chunks // unroll
    drain(prep_stages(0))
    states = lax.fori_loop(1, n_trips, fused,
                           tuple(jnp.zeros((GDN_D, GDN_D), F32) for _ in range(hb)))
    drain(scan_stages(n_trips - 1, states, []))


def _gdn(proj, bsz, seq, a_log, dt_bias, norm_w, *, hb=3, unroll=4):
    t = proj.shape[0]
    unroll = math.gcd(unroll, seq // CHUNK)
    wd = hb * LANE
    q0 = COL_GDN_QKV // wd
    z0 = COL_GDN_Z // wd
    g0 = COL_GDN_GATE // LANE
    ng = GDN_HEADS // hb
    blk = lambda off: pl.BlockSpec((seq, wd), lambda b, h, off=off: (b, off + h))
    pad = lambda v: jnp.pad(v.astype(F32), (GDN_HEADS, LANE - 2 * GDN_HEADS))
    gpar = jnp.stack([pad(a_log), pad(dt_bias)])
    return pl.pallas_call(
        functools.partial(_gdn_kernel, hb=hb, unroll=unroll),
        grid=(bsz, ng),
        in_specs=[
            pl.BlockSpec((2, LANE), lambda b, h: (0, 0)),
            blk(q0), blk(q0 + ng), blk(q0 + 2 * ng), blk(z0),
            pl.BlockSpec((seq, LANE), lambda b, h: (b, g0)),
            pl.BlockSpec((1, LANE), lambda b, h: (0, 0)),
        ],
        out_specs=pl.BlockSpec((seq, wd), lambda b, h: (b, h)),
        out_shape=jax.ShapeDtypeStruct((t, GDN_QK), BF16),
        scratch_shapes=[
            pltpu.VMEM((hb, seq, LANE), F32),
            pltpu.VMEM((hb, seq, LANE), BF16),
            pltpu.VMEM((hb, seq, LANE), BF16),
            pltpu.VMEM((hb, seq, LANE), BF16),
            pltpu.VMEM((hb, seq, CHUNK), BF16),
            pltpu.VMEM((hb, seq // CHUNK * 8, LANE), F32),
        ],
        compiler_params=pltpu.CompilerParams(
            dimension_semantics=("parallel", "arbitrary"), vmem_limit_bytes=VMEM_LIMIT),
        name="gdn",
    )(gpar, proj, proj, proj, proj, proj, norm_w.reshape(1, LANE))


def _ssd_kernel(z_ref, x_ref, b_ref, c_ref, dt_ref,
                alog_ref, dtb_ref, dsk_ref, nw_ref,
                o_ref,
                s_ref, *, unroll):
    grp = pl.program_id(1)
    seq = x_ref.shape[0]
    n_chunks = seq // CHUNK
    c = CHUNK
    gw = SSD_GW

    row = _iota((c, c), 0)
    col = _iota((c, c), 1)
    ltri = (row >= col).astype(BF16)
    ones = jnp.ones((c, c), BF16)
    rowt = _iota((c, gw), 0)
    colt = jnp.bitwise_and(_iota((c, gw), 1), c - 1)
    tril_t = rowt >= colt
    upper_t = rowt <= colt
    expand = (_iota((LANE, gw), 0) == grp * SSD_HPG + lax.shift_right_logical(_iota((LANE, gw), 1), 6)).astype(BF16)
    last_row = (_iota((8, gw), 0) == 7)
    ones_8n = jnp.ones((8, SSD_N), BF16)

    neg_a = -jnp.exp(alog_ref[...])
    dt_bias = dtb_ref[...]
    d_skip = dsk_ref[...]
    norm_w = nw_ref[...]

    s_ref[...] = jnp.zeros((gw, SSD_N), F32)
    heads = [slice(hh * SSD_P, (hh + 1) * SSD_P) for hh in range(SSD_HPG)]

    def body(trip, carry):
        cis = [trip * unroll + cc for cc in range(unroll)]
        rows = [pl.ds(pl.multiple_of(ci * c, c), c) for ci in cis]
        xs = [x_ref[rw, :] for rw in rows]
        bm = [b_ref[rw, :] for rw in rows]
        cm = [c_ref[rw, :] for rw in rows]
        dt = [_softplus(_mm01_t(dt_ref[rw, :], expand) + dt_bias) for rw in rows]
        a = [d * neg_a for d in dt]
        acum = [_mm01(ltri, x) for x in a]
        acum_j = [_mm01(ones, jnp.where(upper_t, x, 0.0)) for x in a]
        cb = [_mm(x, y, NT) for x, y in zip(cm, bm)]
        lm = [jnp.where(tril_t, jnp.exp(x - y), 0.0) for x, y in zip(acum, acum_j)]
        xdt = [x * d for x, d in zip(xs, dt)]
        y_diag = [jnp.concatenate([_mm(cb[i] * lm[i][:, sl], xdt[i][:, sl], NN)
                                   for sl in heads], axis=1) for i in range(unroll)]
        a_last = [jnp.broadcast_to(x[c - 1:c, :], (c, gw)) for x in acum]
        states = [_mm(xdt[i] * jnp.exp(a_last[i] - acum[i]), bm[i], TN)
                  for i in range(unroll)]
        dec_col = [_mm01_t(jnp.where(last_row, jnp.exp(x[c - 8:, :]), 0.0), ones_8n, TN)
                   for x in acum]
        s_prev = s_ref[...]
        for i in range(unroll):
            y_off = _mm(cm[i], s_prev, NT) * jnp.exp(acum[i])
            s_prev = s_prev * dec_col[i] + states[i]
            y = y_diag[i] + y_off + d_skip * xs[i]
            y = y * _silu(z_ref[rows[i], :])
            ms = jnp.mean(y * y, axis=-1, keepdims=True)
            o_ref[rows[i], :] = (y * lax.rsqrt(ms + RMS_EPS) * norm_w).astype(o_ref.dtype)
        s_ref[...] = s_prev
        return carry

    lax.fori_loop(0, n_chunks // unroll, body, 0)


def _ssd(proj, bsz, seq, a_log, dt_bias, d_skip, norm_w, *, unroll=16):
    t = proj.shape[0]
    gw = SSD_GW
    unroll = math.gcd(unroll, seq // CHUNK)
    rep = lambda v: jnp.repeat(v.astype(F32), SSD_P).reshape(1, SSD_INNER)
    wide = lambda off: pl.BlockSpec((seq, gw), lambda b, g, off=off: (b, off + g))
    lane = lambda off: pl.BlockSpec((seq, LANE), lambda b, g, off=off: (b, off + g))
    chan = pl.BlockSpec((1, gw), lambda b, g: (0, g))
    return pl.pallas_call(
        functools.partial(_ssd_kernel, unroll=unroll),
        grid=(bsz, SSD_GROUPS),
        in_specs=[
            wide(COL_SSD_Z // gw), wide(COL_SSD_X // gw),
            lane(COL_SSD_B // LANE), lane(COL_SSD_C // LANE),
            pl.BlockSpec((seq, LANE), lambda b, g: (b, COL_SSD_DT // LANE)),
            chan, chan, chan, chan,
        ],
        out_specs=pl.BlockSpec((seq, gw), lambda b, g: (b, g)),
        out_shape=jax.ShapeDtypeStruct((t, SSD_INNER), BF16),
        scratch_shapes=[pltpu.VMEM((gw, SSD_N), F32)],
        compiler_params=pltpu.CompilerParams(
            dimension_semantics=("parallel", "arbitrary"), vmem_limit_bytes=VMEM_LIMIT),
        name="ssd",
    )(proj, proj, proj, proj, proj,
      rep(a_log), rep(dt_bias), rep(d_skip), norm_w.reshape(1, SSD_INNER))


def _s5_kernel(u_ref, wb_ref, wc_ref, are_ref, aim_ref, dsk_ref, gw_ref, gb_ref,
               o_ref, bu_ref, h_ref, *, bsz):
    ts = u_ref.shape[0]
    nt = S5_CH // LANE
    nb = S5_WIDTH // LANE
    per = nt // nb
    half = per * LANE

    @pl.when(pl.program_id(0) == 0)
    def _():
        h_ref[...] = jnp.zeros(h_ref.shape, F32)

    u = u_ref[...].reshape(ts * bsz, S5_WIDTH)
    u_bf = u.astype(BF16)
    for m in range(nb):
        res = _dot(u_bf[:, m * LANE:(m + 1) * LANE], wb_ref[m])
        for k in range(per):
            bu_ref[m * per + k] = res[:, k * LANE:(k + 1) * LANE]
            bu_ref[nt + m * per + k] = res[:, half + k * LANE:half + (k + 1) * LANE]

    def step(t, carry):
        rows = pl.ds(pl.multiple_of(t * bsz, bsz), bsz)
        new_re, new_im = [], []
        for k in range(nt):
            h_re, h_im = carry[k], carry[nt + k]
            a_re = are_ref[:, k * LANE:(k + 1) * LANE]
            a_im = aim_ref[:, k * LANE:(k + 1) * LANE]
            n_re = a_re * h_re - a_im * h_im + bu_ref[k, rows, :]
            n_im = a_re * h_im + a_im * h_re + bu_ref[nt + k, rows, :]
            bu_ref[k, rows, :] = n_re
            bu_ref[nt + k, rows, :] = n_im
            new_re.append(n_re)
            new_im.append(n_im)
        return tuple(new_re + new_im)

    h_last = lax.fori_loop(0, ts, step, tuple(h_ref[k] for k in range(2 * nt)),
                           unroll=math.gcd(ts, S5_SCAN_UNROLL))
    for k in range(2 * nt):
        h_ref[k] = h_last[k]

    y_parts = []
    for m in range(nb):
        hs = jnp.concatenate([bu_ref[m * per + k] for k in range(per)] +
                             [bu_ref[nt + m * per + k] for k in range(per)], axis=1)
        y_parts.append(_dot(hs.astype(BF16), wc_ref[m]))
    y = jnp.concatenate(y_parts, axis=1) + dsk_ref[...] * u
    g = 0.5 * y * (1.0 + jnp.tanh(math.sqrt(2.0 / math.pi) * (y + 0.044715 * (y * y * y))))
    out = g * _sigmoid(_dot(g.astype(BF16), gw_ref[...]) + gb_ref[...])
    o_ref[...] = out.reshape(ts, bsz * S5_WIDTH).astype(o_ref.dtype)


def _s5(u_tb, bsz, wb, wc, a_re, a_im, d_skip, glu_w, glu_b, *, ts=128):
    seq = u_tb.shape[0]
    blk = ts * bsz
    fixed = lambda shape: pl.BlockSpec(shape, lambda k: (0,) * len(shape))
    return pl.pallas_call(
        functools.partial(_s5_kernel, bsz=bsz),
        grid=(seq // ts,),
        in_specs=[
            pl.BlockSpec((ts, bsz * S5_WIDTH), lambda k: (k, 0)),
            fixed(wb.shape), fixed(wc.shape),
            fixed((1, S5_CH)), fixed((1, S5_CH)),
            fixed((1, S5_WIDTH)), fixed(glu_w.shape), fixed((1, S5_WIDTH)),
        ],
        out_specs=pl.BlockSpec((ts, bsz * S5_WIDTH), lambda k: (k, 0)),
        out_shape=jax.ShapeDtypeStruct((seq, bsz * S5_WIDTH), BF16),
        scratch_shapes=[
            pltpu.VMEM((2 * S5_CH // LANE, blk, LANE), F32),
            pltpu.VMEM((2 * S5_CH // LANE, bsz, LANE), F32),
        ],
        compiler_params=pltpu.CompilerParams(
            dimension_semantics=("arbitrary",), vmem_limit_bytes=VMEM_LIMIT),
        name="s5",
    )(u_tb, wb, wc, a_re.reshape(1, S5_CH), a_im.reshape(1, S5_CH),
      d_skip.reshape(1, S5_WIDTH), glu_w, glu_b.reshape(1, S5_WIDTH))


def _s5_params(a_re, a_im, b_re, b_im, c_re, c_im, log_dt):
    delta = jnp.exp(log_dt)[:, None]
    mag = jnp.exp(a_re * delta)
    ab_re, ab_im = mag * jnp.cos(a_im * delta), mag * jnp.sin(a_im * delta)
    den = a_re * a_re + a_im * a_im
    p_re, p_im = ab_re - 1.0, ab_im
    f_re = (p_re * a_re + p_im * a_im) / den
    f_im = (p_im * a_re - p_re * a_im) / den
    bb_re = f_re[..., None] * b_re - f_im[..., None] * b_im
    bb_im = f_re[..., None] * b_im + f_im[..., None] * b_re
    gpb = LANE // S5_GROUP
    nb = S5_GROUPS // gpb
    eye = jnp.eye(gpb, dtype=F32)
    emb_b = lambda bb: jnp.einsum('mgnj,gh->mgjhn', bb.reshape(nb, gpb, S5_STATE, S5_GROUP),
                                  eye).reshape(nb, LANE, gpb * S5_STATE)
    wb = jnp.concatenate([emb_b(bb_re), emb_b(bb_im)], axis=2)
    emb_c = lambda cc: jnp.einsum('mgin,gh->mgnhi', cc.reshape(nb, gpb, S5_GROUP, S5_STATE),
                                  eye).reshape(nb, gpb * S5_STATE, LANE)
    wc = jnp.concatenate([emb_c(c_re), -emb_c(c_im)], axis=1)
    return ab_re.reshape(-1), ab_im.reshape(-1), wb.astype(BF16), wc.astype(BF16)


def _w_in_segments():
    bc = SSD_GROUPS * SSD_N
    g_z = GDN_QKV
    g_gate = g_z + GDN_QK
    s_z = g_gate + 2 * GDN_HEADS
    s_x = s_z + SSD_INNER
    s_b = s_x + SSD_INNER
    s_c = s_b + bc
    s_dt = s_c + bc
    s5_u = s_dt + SSD_HEADS
    return ((0, GDN_QKV, COL_GDN_QKV), (g_z, GDN_QK, COL_GDN_Z),
            (g_gate, 2 * GDN_HEADS, COL_GDN_GATE), (s_z, SSD_INNER, COL_SSD_Z),
            (s_x, SSD_INNER, COL_SSD_X), (s_b, bc, COL_SSD_B), (s_c, bc, COL_SSD_C),
            (s_dt, SSD_HEADS, COL_SSD_DT), (s5_u, S5_WIDTH, COL_S5_U))


def _pack_conv(gdn_w, ssd_w, ssd_b):
    bc = SSD_GROUPS * SSD_N
    pieces = ((COL_GDN_QKV, gdn_w, None),
              (COL_SSD_X, ssd_w[:, :SSD_INNER], ssd_b[:SSD_INNER]),
              (COL_SSD_B, ssd_w[:, SSD_INNER:SSD_INNER + bc], ssd_b[SSD_INNER:SSD_INNER + bc]),
              (COL_SSD_C, ssd_w[:, SSD_INNER + bc:], ssd_b[SSD_INNER + bc:]))
    cw = jnp.zeros((CONV_K, P_PAD), F32)
    cb = jnp.zeros((1, P_PAD), F32)
    for col, w, b in pieces:
        cw = cw.at[:, col:col + w.shape[1]].set(w.astype(F32))
        if b is not None:
            cb = cb.at[0, col:col + b.shape[0]].set(b.astype(F32))
    return cw, cb


def _pack_kernel(x_ref, o_ref):
    for col in (COL_SSD_DT, COL_GDN_GATE):
        o_ref[:, col:col + LANE] = jnp.zeros((o_ref.shape[0], LANE), o_ref.dtype)
    for src, width, dst in _w_in_segments():
        if width % LANE == 0:
            for r in range(0, width, 2 * LANE):
                w = min(2 * LANE, width - r)
                o_ref[:, dst + r:dst + r + w] = x_ref[src + r:src + r + w, :].T.astype(o_ref.dtype)
        else:
            blk = x_ref[src:src + LANE, :].T
            o_ref[:, dst:dst + width] = blk[:, :width].astype(o_ref.dtype)


def _pack_w_in(w_in, *, kb=256):
    nl, d, c = w_in.shape
    return pl.pallas_call(
        _pack_kernel,
        grid=(nl, d // kb),
        in_specs=[pl.BlockSpec((None, c, kb), lambda l, i: (l, 0, i))],
        out_specs=pl.BlockSpec((None, kb, P_PAD), lambda l, i: (l, i, 0)),
        out_shape=jax.ShapeDtypeStruct((nl, d, P_PAD), BF16),
        compiler_params=pltpu.CompilerParams(
            dimension_semantics=("parallel", "parallel"), vmem_limit_bytes=VMEM_LIMIT),
        name="pack_w_in",
    )(jnp.swapaxes(w_in, 1, 2))


def kernel(x, ffn1_norm, ffn1_w_gate, ffn1_w_up, ffn1_w_down, mix_norm, w_in,
           gdn_conv_w, gdn_a_log, gdn_dt_bias, gdn_norm,
           ssd_conv_w, ssd_conv_b, ssd_a_log, ssd_dt_bias, ssd_d, ssd_norm,
           s5_a_re, s5_a_im, s5_b_re, s5_b_im, s5_c_re, s5_c_im, s5_d, s5_log_dt,
           s5_glu_w, s5_glu_b, w_out, ffn2_norm, ffn2_w_gate, ffn2_w_up, ffn2_w_down,
           final_norm):
    bsz, seq, d = x.shape
    depth = w_in.shape[0]
    h = x
    ffn1_f32 = (ffn1_w_gate, ffn1_w_up, ffn1_w_down)
    ffn2_f32 = (ffn2_w_gate, ffn2_w_up, ffn2_w_down)
    w_next = tuple(_to_bf16(w, 0) for w in ffn1_f32)
    w_in_packed = _pack_w_in(w_in)
    for i in range(depth):
        h, w_next = _ffn(h, ffn1_norm[i], *w_next, final_norm, final_norm=False,
                         cast_next=(*ffn2_f32, i))
        conv_w, conv_b = _pack_conv(gdn_conv_w[i], ssd_conv_w[i], ssd_conv_b[i])
        proj, u_s5 = _inproj(h, mix_norm[i], w_in_packed, i, conv_w, conv_b, bsz, seq)
        o_gdn = _gdn(proj, bsz, seq, gdn_a_log[i], gdn_dt_bias[i], gdn_norm[i])
        o_ssd = _ssd(proj, bsz, seq, ssd_a_log[i], ssd_dt_bias[i], ssd_d[i], ssd_norm[i])
        ab_re, ab_im, wb, wc = _s5_params(s5_a_re[i], s5_a_im[i], s5_b_re[i], s5_b_im[i],
                                          s5_c_re[i], s5_c_im[i], s5_log_dt[i])
        o_s5 = _s5(u_s5, bsz, wb, wc, ab_re, ab_im, s5_d[i], s5_glu_w[i].astype(BF16),
                   s5_glu_b[i])
        wo = w_out[i].astype(BF16)
        h = _outproj(h, o_gdn, o_ssd, o_s5, wo[0:GDN_QK], wo[GDN_QK:GDN_QK + SSD_INNER],
                     wo[GDN_QK + SSD_INNER:])
        last = i == depth - 1
        if last:
            h = _ffn(h, ffn2_norm[i], *w_next, final_norm, final_norm=True,
                     out_batched=(bsz, seq))
        else:
            h, w_next = _ffn(h, ffn2_norm[i], *w_next, final_norm, final_norm=False,
                             cast_next=(*ffn1_f32, i + 1))
    return h
```

```python
import functools
import math

import jax
import jax.numpy as jnp
from jax import lax
from jax.experimental import pallas as pl
from jax.experimental.pallas import tpu as pltpu

F32 = jnp.float32
BF16 = jnp.bfloat16

RMS_EPS = 1e-6
CHUNK = 64
CONV_K = 4

GDN_HEADS = 6
GDN_D = 128
GDN_QK = GDN_HEADS * GDN_D
GDN_QKV = 3 * GDN_QK

SSD_HEADS = 12
SSD_P = 64
SSD_GROUPS = 2
SSD_N = 128
SSD_HPG = SSD_HEADS // SSD_GROUPS
SSD_GW = SSD_HPG * SSD_P
SSD_INNER = SSD_HEADS * SSD_P

S5_WIDTH = 512
S5_GROUP = 16
S5_GROUPS = 32
S5_STATE = 64
S5_CH = S5_GROUPS * S5_STATE
S5_SCAN_UNROLL = 16

LANE = 128

COL_GDN_QKV = 0
COL_GDN_Z = 2304
COL_SSD_Z = 3072
COL_SSD_X = 3840
COL_SSD_B = 4608
COL_SSD_C = 4864
COL_S5_U = 5120
COL_SSD_DT = 5632
COL_GDN_GATE = 5760
P_PAD = 5888

VMEM_LIMIT = 60 * 1024 * 1024

NN = (((1,), (0,)), ((), ()))
NT = (((1,), (1,)), ((), ()))
TN = (((0,), (0,)), ((), ()))


def _dot(a, b, dims=NN):
    return lax.dot_general(a, b, dims, preferred_element_type=F32)


def _hi_lo(a):
    hi = a.astype(BF16)
    lo = (a - hi.astype(F32)).astype(BF16)
    return hi, lo


def _mm(a, b, dims=NN):
    return _dot(a.astype(BF16), b.astype(BF16), dims)


def _mm01(m01, x, dims=NN):
    hi, lo = _hi_lo(x)
    return _dot(m01, hi, dims) + _dot(m01, lo, dims)


def _mm01_t(x, m01, dims=NN):
    hi, lo = _hi_lo(x)
    return _dot(hi, m01, dims) + _dot(lo, m01, dims)


def _sigmoid(x):
    return 1.0 / (1.0 + jnp.exp(-x))


def _silu(x):
    return x * _sigmoid(x)


def _softplus(x):
    return jnp.maximum(x, 0.0) + jnp.log1p(jnp.exp(-jnp.abs(x)))


def _iota(shape, dim):
    return lax.broadcasted_iota(jnp.int32, shape, dim)


def _ffn_kernel(x_ref, nw_ref, wg_ref, wu_ref, wd_ref, fw_ref, *rest, final_norm, side_cast):
    if side_cast:
        cg_ref, cu_ref, cd_ref, o_ref, og_ref, ou_ref, od_ref, xn_ref = rest
        og_ref[...] = cg_ref[...].astype(BF16)
        ou_ref[...] = cu_ref[...].astype(BF16)
        od_ref[...] = cd_ref[...].astype(BF16)
    else:
        o_ref, xn_ref = rest
    j = pl.program_id(1)

    @pl.when(j == 0)
    def _():
        x = x_ref[...]
        ms = jnp.mean(x * x, axis=-1, keepdims=True)
        xn_ref[...] = (x * lax.rsqrt(ms + RMS_EPS) * nw_ref[...]).astype(BF16)
        o_ref[...] = x

    xn = xn_ref[...]
    g = _dot(xn, wg_ref[...])
    u = _dot(xn, wu_ref[...])
    a = (0.5 * _silu(g) * u).astype(BF16)
    o_ref[...] += _dot(a, wd_ref[...])

    if final_norm:
        @pl.when(j == pl.num_programs(1) - 1)
        def _():
            h = o_ref[...]
            ms = jnp.mean(h * h, axis=-1, keepdims=True)
            o_ref[...] = h * lax.rsqrt(ms + RMS_EPS) * fw_ref[...]


def _cast_kernel(x_ref, o_ref):
    o_ref[...] = x_ref[...].astype(o_ref.dtype)


def _to_bf16(w, layer, *, rows=256):
    _, r, c = w.shape
    return pl.pallas_call(
        _cast_kernel,
        grid=(r // rows,),
        in_specs=[pl.BlockSpec((None, rows, c), lambda i: (layer, i, 0))],
        out_specs=pl.BlockSpec((rows, c), lambda i: (i, 0)),
        out_shape=jax.ShapeDtypeStruct((r, c), BF16),
        compiler_params=pltpu.CompilerParams(
            dimension_semantics=("parallel",), vmem_limit_bytes=VMEM_LIMIT),
        name="to_bf16",
    )(w)


def _ffn(x, nw, wg, wu, wd, fw, *, final_norm, out_batched=None, cast_next=None,
         tm=1024, tf=512):
    d = x.shape[-1]
    t = x.size // d
    f = wg.shape[1]
    ni, nj = t // tm, f // tf
    if x.ndim == 3:
        nseg = x.shape[1] // tm
        x_spec = pl.BlockSpec((None, tm, d), lambda i, j: (i // nseg, i % nseg, 0))
    else:
        x_spec = pl.BlockSpec((tm, d), lambda i, j: (i, 0))
    if out_batched is None:
        out_spec = pl.BlockSpec((tm, d), lambda i, j: (i, 0))
        out_shape = jax.ShapeDtypeStruct((t, d), F32)
    else:
        oseg = out_batched[1] // tm
        out_spec = pl.BlockSpec((None, tm, d), lambda i, j: (i // oseg, i % oseg, 0))
        out_shape = jax.ShapeDtypeStruct((*out_batched, d), F32)
    in_specs = [
        x_spec,
        pl.BlockSpec((1, d), lambda i, j: (0, 0)),
        pl.BlockSpec((d, tf), lambda i, j: (0, j)),
        pl.BlockSpec((d, tf), lambda i, j: (0, j)),
        pl.BlockSpec((tf, d), lambda i, j: (j, 0)),
        pl.BlockSpec((1, d), lambda i, j: (0, 0)),
    ]
    operands = [x, nw.reshape(1, d), wg, wu, wd, fw.reshape(1, d)]
    out_specs, out_shapes = [out_spec], [out_shape]
    if cast_next is not None:
        cg, cu, cd, layer = cast_next
        dr = d // ni
        assert dr * ni == d and dr % LANE == 0, (d, ni)
        in_specs += [pl.BlockSpec((None, dr, tf), lambda i, j: (layer, i, j)),
                     pl.BlockSpec((None, dr, tf), lambda i, j: (layer, i, j)),
                     pl.BlockSpec((None, tf, dr), lambda i, j: (layer, j, i))]
        operands += [cg, cu, cd]
        out_specs += [pl.BlockSpec((dr, tf), lambda i, j: (i, j)),
                      pl.BlockSpec((dr, tf), lambda i, j: (i, j)),
                      pl.BlockSpec((tf, dr), lambda i, j: (j, i))]
        out_shapes += [jax.ShapeDtypeStruct((d, f), BF16), jax.ShapeDtypeStruct((d, f), BF16),
                       jax.ShapeDtypeStruct((f, d), BF16)]
    res = pl.pallas_call(
        functools.partial(_ffn_kernel, final_norm=final_norm, side_cast=cast_next is not None),
        grid=(ni, nj),
        in_specs=in_specs,
        out_specs=out_specs,
        out_shape=out_shapes,
        scratch_shapes=[pltpu.VMEM((tm, d), BF16)],
        compiler_params=pltpu.CompilerParams(
            dimension_semantics=("parallel", "arbitrary"), vmem_limit_bytes=VMEM_LIMIT),
        name="ffn",
    )(*operands)
    return (res[0], tuple(res[1:])) if cast_next is not None else res[0]


CONV_RANGES = ((COL_SSD_X, SSD_INNER), (COL_SSD_B, SSD_GROUPS * SSD_N),
               (COL_SSD_C, SSD_GROUPS * SSD_N), (COL_GDN_QKV, GDN_QKV))
INPROJ_PIECE = 256


def _inproj_kernel(x_ref, nw_ref, w_ref, cw_ref, cb_ref, o_ref, u5_ref, stage_ref, *, nseg):
    tm = x_ref.shape[0]

    @pl.when(pl.program_id(0) % nseg == 0)
    def _():
        stage_ref[:, 0:8, :] = jnp.zeros((stage_ref.shape[0], 8, LANE), F32)

    x = x_ref[...]
    ms = jnp.mean(x * x, axis=-1, keepdims=True)
    xn = (x * lax.rsqrt(ms + RMS_EPS) * nw_ref[...]).astype(BF16)
    starts = list(range(0, o_ref.shape[1], INPROJ_PIECE))
    piece = lambda col: _dot(xn, w_ref[:, col:col + INPROJ_PIECE])
    nxt = piece(starts[0])
    for idx, col in enumerate(starts):
        cols = slice(col, col + INPROJ_PIECE)
        res, nxt = nxt, (piece(starts[idx + 1]) if idx + 1 < len(starts) else None)
        if any(lo <= col < lo + width for lo, width in CONV_RANGES):
            for lt in range(INPROJ_PIECE // LANE):
                ti = col // LANE + lt
                ln = slice(lt * LANE, (lt + 1) * LANE)
                gl = slice(col + lt * LANE, col + (lt + 1) * LANE)
                stage_ref[ti, 8:, :] = res[:, ln]
                acc = None
                for k in range(CONV_K):
                    lo = k + 8 - (CONV_K - 1)
                    term = stage_ref[ti, pl.ds(lo, tm, stride=1), :] * cw_ref[k:k + 1, gl]
                    acc = term if acc is None else acc + term
                stage_ref[ti, 0:8, :] = res[tm - 8:, ln]
                o_ref[:, gl] = _silu(acc + cb_ref[:, gl])
        else:
            o_ref[:, cols] = res
        if COL_S5_U <= col < COL_S5_U + S5_WIDTH:
            u5_ref[:, col - COL_S5_U:col - COL_S5_U + INPROJ_PIECE] = res


def _inproj(x, nw, w, layer, conv_w, conv_b, bsz, seq, *, tm=256):
    t, d = x.shape
    n = w.shape[2]
    nseg = seq // tm
    assert nseg * tm == seq and n % INPROJ_PIECE == 0, (seq, tm, n)
    fixed = lambda i: (0, 0)
    return pl.pallas_call(
        functools.partial(_inproj_kernel, nseg=nseg),
        grid=(t // tm,),
        in_specs=[
            pl.BlockSpec((tm, d), lambda i: (i, 0)),
            pl.BlockSpec((1, d), fixed),
            pl.BlockSpec((None, d, n), lambda i: (layer, 0, 0), pipeline_mode=pl.Buffered(1)),
            pl.BlockSpec((CONV_K, n), fixed),
            pl.BlockSpec((1, n), fixed),
        ],
        out_specs=[
            pl.BlockSpec((tm, n), lambda i: (i, 0)),
            pl.BlockSpec((tm, S5_WIDTH), lambda i: (i % nseg, i // nseg)),
        ],
        out_shape=[jax.ShapeDtypeStruct((t, n), F32),
                   jax.ShapeDtypeStruct((seq, bsz * S5_WIDTH), F32)],
        scratch_shapes=[pltpu.VMEM((n // LANE, tm + 8, LANE), F32)],
        compiler_params=pltpu.CompilerParams(
            dimension_semantics=("arbitrary",), vmem_limit_bytes=VMEM_LIMIT),
        name="inproj",
    )(x, nw.reshape(1, d), w, conv_w, conv_b)


def _outproj_kernel(h_ref, a_ref, b_ref, c_ref, wa_ref, wb_ref, wc_ref, o_ref):
    acc = _dot(a_ref[...], wa_ref[...])
    acc += _dot(b_ref[...], wb_ref[...])
    acc += _dot(c_ref[...], wc_ref[...])
    o_ref[...] = h_ref[...] + acc


def _outproj(h, oa, ob, oc, wa, wb, wc, *, tm=1024):
    t, d = h.shape
    nseg = oc.shape[0] // tm
    row = lambda i: (i, 0)
    fixed = lambda i: (0, 0)
    return pl.pallas_call(
        _outproj_kernel,
        grid=(t // tm,),
        in_specs=[
            pl.BlockSpec((tm, d), row),
            pl.BlockSpec((tm, oa.shape[1]), row),
            pl.BlockSpec((tm, ob.shape[1]), row),
            pl.BlockSpec((tm, S5_WIDTH), lambda i: (i % nseg, i // nseg)),
            pl.BlockSpec(wa.shape, fixed, pipeline_mode=pl.Buffered(1)),
            pl.BlockSpec(wb.shape, fixed, pipeline_mode=pl.Buffered(1)),
            pl.BlockSpec(wc.shape, fixed, pipeline_mode=pl.Buffered(1)),
        ],
        out_specs=pl.BlockSpec((tm, d), row),
        out_shape=jax.ShapeDtypeStruct((t, d), F32),
        compiler_params=pltpu.CompilerParams(
            dimension_semantics=("parallel",), vmem_limit_bytes=VMEM_LIMIT),
        name="outproj",
    )(h, oa, ob, oc, wa, wb, wc)


def _gdn_kernel(gpar_ref,
                q_ref, k_ref, v_ref, z_ref, gate_ref,
                nw_ref,
                o_ref,
                u_ref, w_ref, qd_ref, kd_ref, at_ref, eg_ref,
                *, hb, unroll):
    hg = pl.program_id(1)
    seq = q_ref.shape[0]
    n_chunks = seq // CHUNK
    c = CHUNK

    row = _iota((c, c), 0)
    col = _iota((c, c), 1)
    tril = row >= col
    strict = row > col
    ltri = tril.astype(BF16)
    ones = jnp.ones((c, c), BF16)
    eye = (row == col).astype(F32)
    lane = _iota((c, LANE), 1)
    neg_a = -jnp.exp(gpar_ref[0:1, :])
    dt_bias = gpar_ref[1:2, :]

    def prep_stages(trip):
        cis = [trip * unroll + cc for cc in range(unroll)]
        items = [(cis[cc], hh) for cc in range(unroll) for hh in range(hb)]
        lanes = [slice(hh * LANE, (hh + 1) * LANE) for _, hh in items]
        rows = [pl.ds(pl.multiple_of(ci * c, c), c) for ci, _ in items]

        def chunk_gates(ci):
            gt = gate_ref[pl.ds(pl.multiple_of(ci * c, c), c), :]
            return _sigmoid(gt), neg_a * _softplus(gt + dt_bias)

        gates = [chunk_gates(ci) for ci in cis]

        def stage_a(cc, hh, ln):
            ci = cis[cc]
            h = hg * hb + hh
            rw = pl.ds(pl.multiple_of(ci * c, c), c)
            q, k, v = q_ref[rw, ln], k_ref[rw, ln], v_ref[rw, ln]
            qn = q * lax.rsqrt(jnp.sum(q * q, axis=-1, keepdims=True) + RMS_EPS) * (GDN_D ** -0.5)
            kn = k * lax.rsqrt(jnp.sum(k * k, axis=-1, keepdims=True) + RMS_EPS)
            beta_all, g_all = gates[cc]
            beta = jnp.sum(jnp.where(lane == h, beta_all, 0.0), axis=-1, keepdims=True)
            g = jnp.sum(jnp.where(lane == h + GDN_HEADS, g_all, 0.0), axis=-1, keepdims=True)
            return qn, kn, v, beta, jnp.broadcast_to(g, (c, LANE))

        sa = [stage_a(cc, hh, slice(hh * LANE, (hh + 1) * LANE))
              for cc in range(unroll) for hh in range(hb)]
        qn = [x[0] for x in sa]
        kn = [x[1] for x in sa]
        v = [x[2] for x in sa]
        beta = [x[3] for x in sa]
        g_b = [x[4] for x in sa]
        yield
        gi = [_mm01(ltri, g) for g in g_b]
        gj = [_mm01(ones, jnp.where(row <= col, g[:, :c], 0.0)) for g in g_b]
        yield
        kb = [k * b for k, b in zip(kn, beta)]
        kk = [_mm(a, b, NT) for a, b in zip(kb, kn)]
        qk = [_mm(a, b, NT) for a, b in zip(qn, kn)]
        decay = [jnp.where(tril, jnp.exp(a[:, :c] - b), 0.0) for a, b in zip(gi, gj)]
        npow = [-jnp.where(strict, a * d, 0.0) for a, d in zip(kk, decay)]
        tinv = [eye + n for n in npow]
        for _ in range(5):
            yield
            npow = [_mm(n, n, NN) for n in npow]
            tinv = [t + _mm(t, n, NN) for t, n in zip(tinv, npow)]
        yield
        exp_g = [jnp.exp(g) for g in gi]
        rhs = [jnp.concatenate([vv * b, k * e], axis=1) for vv, b, k, e in zip(v, beta, kb, exp_g)]
        uw = [_mm(t, r, NN) for t, r in zip(tinv, rhs)]
        for i, ((ci, hh), rw) in enumerate(zip(items, rows)):
            g_last = jnp.broadcast_to(gi[i][c - 1:c, :], (c, LANE))
            u_ref[hh, rw, :] = uw[i][:, :LANE]
            w_ref[hh, rw, :] = uw[i][:, LANE:].astype(BF16)
            at_ref[hh, rw, :] = jnp.where(tril, qk[i] * decay[i], 0.0).astype(BF16)
            qd_ref[hh, rw, :] = (qn[i] * exp_g[i]).astype(BF16)
            kd_ref[hh, rw, :] = (kn[i] * jnp.exp(g_last - gi[i])).astype(BF16)
            eg_ref[hh, pl.ds(pl.multiple_of(ci * 8, 8), 8), :] = jnp.exp(g_last[0:8, :])

    norm_w = nw_ref[...]

    def scan_stages(trip, states, final):
        for cc in range(unroll):
            ci = trip * unroll + cc
            rw = pl.ds(pl.multiple_of(ci * c, c), c)
            lhs = [jnp.concatenate([w_ref[hh, rw, :], qd_ref[hh, rw, :]], axis=0)
                   for hh in range(hb)]
            ws_qs = [_mm(a, s, NN) for a, s in zip(lhs, states)]
            yield
            v_new = [u_ref[hh, rw, :] - ws_qs[hh][:c] for hh in range(hb)]
            kv = [_mm(kd_ref[hh, rw, :], v_new[hh], TN) for hh in range(hb)]
            av = [_mm(at_ref[hh, rw, :], v_new[hh], NN) for hh in range(hb)]
            new_states = []
            for hh in range(hb):
                eg = eg_ref[hh, pl.ds(pl.multiple_of(ci * 8, 8), 8), :]
                new_states.append(states[hh] * jnp.broadcast_to(eg[0:1, :], (GDN_D, GDN_D)) + kv[hh])
                o = ws_qs[hh][c:] + av[hh]
                ms = jnp.mean(o * o, axis=-1, keepdims=True)
                ln = slice(hh * LANE, (hh + 1) * LANE)
                y = o * lax.rsqrt(ms + RMS_EPS) * norm_w * _silu(z_ref[rw, ln])
                o_ref[rw, ln] = y.astype(o_ref.dtype)
            states = new_states
            yield
        final.extend(states)

    def drain(gen):
        for _ in gen:
            pass

    def fused(trip, states):
        final = []
        scan_gen = scan_stages(trip - 1, states, final)
        for _ in prep_stages(trip):
            next(scan_gen, None)
        drain(scan_gen)
        return tuple(final)

    n_trips = n_chunks // unroll
    drain(prep_stages(0))
    states = lax.fori_loop(1, n_trips, fused,
                           tuple(jnp.zeros((GDN_D, GDN_D), F32) for _ in range(hb)))
    drain(scan_stages(n_trips - 1, states, []))


def _gdn(proj, bsz, seq, a_log, dt_bias, norm_w, *, hb=3, unroll=4):
    t = proj.shape[0]
    unroll = math.gcd(unroll, seq // CHUNK)
    wd = hb * LANE
    q0 = COL_GDN_QKV // wd
    z0 = COL_GDN_Z // wd
    g0 = COL_GDN_GATE // LANE
    ng = GDN_HEADS // hb
    blk = lambda off: pl.BlockSpec((seq, wd), lambda b, h, off=off: (b, off + h))
    pad = lambda v: jnp.pad(v.astype(F32), (GDN_HEADS, LANE - 2 * GDN_HEADS))
    gpar = jnp.stack([pad(a_log), pad(dt_bias)])
    return pl.pallas_call(
        functools.partial(_gdn_kernel, hb=hb, unroll=unroll),
        grid=(bsz, ng),
        in_specs=[
            pl.BlockSpec((2, LANE), lambda b, h: (0, 0)),
            blk(q0), blk(q0 + ng), blk(q0 + 2 * ng), blk(z0),
            pl.BlockSpec((seq, LANE), lambda b, h: (b, g0)),
            pl.BlockSpec((1, LANE), lambda b, h: (0, 0)),
        ],
        out_specs=pl.BlockSpec((seq, wd), lambda b, h: (b, h)),
        out_shape=jax.ShapeDtypeStruct((t, GDN_QK), BF16),
        scratch_shapes=[
            pltpu.VMEM((hb, seq, LANE), F32),
            pltpu.VMEM((hb, seq, LANE), BF16),
            pltpu.VMEM((hb, seq, LANE), BF16),
            pltpu.VMEM((hb, seq, LANE), BF16),
            pltpu.VMEM((hb, seq, CHUNK), BF16),
            pltpu.VMEM((hb, seq // CHUNK * 8, LANE), F32),
        ],
        compiler_params=pltpu.CompilerParams(
            dimension_semantics=("parallel", "arbitrary"), vmem_limit_bytes=VMEM_LIMIT),
        name="gdn",
    )(gpar, proj, proj, proj, proj, proj, norm_w.reshape(1, LANE))


def _ssd_kernel(z_ref, x_ref, b_ref, c_ref, dt_ref,
                alog_ref, dtb_ref, dsk_ref, nw_ref,
                o_ref,
                s_ref, *, unroll):
    grp = pl.program_id(1)
    seq = x_ref.shape[0]
    n_chunks = seq // CHUNK
    c = CHUNK
    gw = SSD_GW

    row = _iota((c, c), 0)
    col = _iota((c, c), 1)
    ltri = (row >= col).astype(BF16)
    ones = jnp.ones((c, c), BF16)
    rowt = _iota((c, gw), 0)
    colt = jnp.bitwise_and(_iota((c, gw), 1), c - 1)
    tril_t = rowt >= colt
    upper_t = rowt <= colt
    expand = (_iota((LANE, gw), 0) == grp * SSD_HPG + lax.shift_right_logical(_iota((LANE, gw), 1), 6)).astype(BF16)
    last_row = (_iota((8, gw), 0) == 7)
    ones_8n = jnp.ones((8, SSD_N), BF16)

    neg_a = -jnp.exp(alog_ref[...])
    dt_bias = dtb_ref[...]
    d_skip = dsk_ref[...]
    norm_w = nw_ref[...]

    s_ref[...] = jnp.zeros((gw, SSD_N), F32)
    heads = [slice(hh * SSD_P, (hh + 1) * SSD_P) for hh in range(SSD_HPG)]

    def body(trip, carry):
        cis = [trip * unroll + cc for cc in range(unroll)]
        rows = [pl.ds(pl.multiple_of(ci * c, c), c) for ci in cis]
        xs = [x_ref[rw, :] for rw in rows]
        bm = [b_ref[rw, :] for rw in rows]
        cm = [c_ref[rw, :] for rw in rows]
        dt = [_softplus(_mm01_t(dt_ref[rw, :], expand) + dt_bias) for rw in rows]
        a = [d * neg_a for d in dt]
        acum = [_mm01(ltri, x) for x in a]
        acum_j = [_mm01(ones, jnp.where(upper_t, x, 0.0)) for x in a]
        cb = [_mm(x, y, NT) for x, y in zip(cm, bm)]
        lm = [jnp.where(tril_t, jnp.exp(x - y), 0.0) for x, y in zip(acum, acum_j)]
        xdt = [x * d for x, d in zip(xs, dt)]
        y_diag = [jnp.concatenate([_mm(cb[i] * lm[i][:, sl], xdt[i][:, sl], NN)
                                   for sl in heads], axis=1) for i in range(unroll)]
        a_last = [jnp.broadcast_to(x[c - 1:c, :], (c, gw)) for x in acum]
        states = [_mm(xdt[i] * jnp.exp(a_last[i] - acum[i]), bm[i], TN)
                  for i in range(unroll)]
        dec_col = [_mm01_t(jnp.where(last_row, jnp.exp(x[c - 8:, :]), 0.0), ones_8n, TN)
                   for x in acum]
        s_prev = s_ref[...]
        for i in range(unroll):
            y_off = _mm(cm[i], s_prev, NT) * jnp.exp(acum[i])
            s_prev = s_prev * dec_col[i] + states[i]
            y = y_diag[i] + y_off + d_skip * xs[i]
            y = y * _silu(z_ref[rows[i], :])
            ms = jnp.mean(y * y, axis=-1, keepdims=True)
            o_ref[rows[i], :] = (y * lax.rsqrt(ms + RMS_EPS) * norm_w).astype(o_ref.dtype)
        s_ref[...] = s_prev
        return carry

    lax.fori_loop(0, n_chunks // unroll, body, 0)


def _ssd(proj, bsz, seq, a_log, dt_bias, d_skip, norm_w, *, unroll=16):
    t = proj.shape[0]
    gw = SSD_GW
    unroll = math.gcd(unroll, seq // CHUNK)
    rep = lambda v: jnp.repeat(v.astype(F32), SSD_P).reshape(1, SSD_INNER)
    wide = lambda off: pl.BlockSpec((seq, gw), lambda b, g, off=off: (b, off + g))
    lane = lambda off: pl.BlockSpec((seq, LANE), lambda b, g, off=off: (b, off + g))
    chan = pl.BlockSpec((1, gw), lambda b, g: (0, g))
    return pl.pallas_call(
        functools.partial(_ssd_kernel, unroll=unroll),
        grid=(bsz, SSD_GROUPS),
        in_specs=[
            wide(COL_SSD_Z // gw), wide(COL_SSD_X // gw),
            lane(COL_SSD_B // LANE), lane(COL_SSD_C // LANE),
            pl.BlockSpec((seq, LANE), lambda b, g: (b, COL_SSD_DT // LANE)),
            chan, chan, chan, chan,
        ],
        out_specs=pl.BlockSpec((seq, gw), lambda b, g: (b, g)),
        out_shape=jax.ShapeDtypeStruct((t, SSD_INNER), BF16),
        scratch_shapes=[pltpu.VMEM((gw, SSD_N), F32)],
        compiler_params=pltpu.CompilerParams(
            dimension_semantics=("parallel", "arbitrary"), vmem_limit_bytes=VMEM_LIMIT),
        name="ssd",
    )(proj, proj, proj, proj, proj,
      rep(a_log), rep(dt_bias), rep(d_skip), norm_w.reshape(1, SSD_INNER))


def _s5_kernel(u_ref, wb_ref, wc_ref, are_ref, aim_ref, dsk_ref, gw_ref, gb_ref,
               o_ref, bu_ref, h_ref, *, bsz):
    ts = u_ref.shape[0]
    nt = S5_CH // LANE
    nb = S5_WIDTH // LANE
    per = nt // nb
    half = per * LANE

    @pl.when(pl.program_id(0) == 0)
    def _():
        h_ref[...] = jnp.zeros(h_ref.shape, F32)

    u = u_ref[...].reshape(ts * bsz, S5_WIDTH)
    u_bf = u.astype(BF16)
    for m in range(nb):
        res = _dot(u_bf[:, m * LANE:(m + 1) * LANE], wb_ref[m])
        for k in range(per):
            bu_ref[m * per + k] = res[:, k * LANE:(k + 1) * LANE]
            bu_ref[nt + m * per + k] = res[:, half + k * LANE:half + (k + 1) * LANE]

    def step(t, carry):
        rows = pl.ds(pl.multiple_of(t * bsz, bsz), bsz)
        new_re, new_im = [], []
        for k in range(nt):
            h_re, h_im = carry[k], carry[nt + k]
            a_re = are_ref[:, k * LANE:(k + 1) * LANE]
            a_im = aim_ref[:, k * LANE:(k + 1) * LANE]
            n_re = a_re * h_re - a_im * h_im + bu_ref[k, rows, :]
            n_im = a_re * h_im + a_im * h_re + bu_ref[nt + k, rows, :]
            bu_ref[k, rows, :] = n_re
            bu_ref[nt + k, rows, :] = n_im
            new_re.append(n_re)
            new_im.append(n_im)
        return tuple(new_re + new_im)

    h_last = lax.fori_loop(0, ts, step, tuple(h_ref[k] for k in range(2 * nt)),
                           unroll=math.gcd(ts, S5_SCAN_UNROLL))
    for k in range(2 * nt):
        h_ref[k] = h_last[k]

    y_parts = []
    for m in range(nb):
        hs = jnp.concatenate([bu_ref[m * per + k] for k in range(per)] +
                             [bu_ref[nt + m * per + k] for k in range(per)], axis=1)
        y_parts.append(_dot(hs.astype(BF16), wc_ref[m]))
    y = jnp.concatenate(y_parts, axis=1) + dsk_ref[...] * u
    g = 0.5 * y * (1.0 + jnp.tanh(math.sqrt(2.0 / math.pi) * (y + 0.044715 * (y * y * y))))
    out = g * _sigmoid(_dot(g.astype(BF16), gw_ref[...]) + gb_ref[...])
    o_ref[...] = out.reshape(ts, bsz * S5_WIDTH).astype(o_ref.dtype)


def _s5(u_tb, bsz, wb, wc, a_re, a_im, d_skip, glu_w, glu_b, *, ts=256):
    seq = u_tb.shape[0]
    blk = ts * bsz
    fixed = lambda shape: pl.BlockSpec(shape, lambda k: (0,) * len(shape))
    return pl.pallas_call(
        functools.partial(_s5_kernel, bsz=bsz),
        grid=(seq // ts,),
        in_specs=[
            pl.BlockSpec((ts, bsz * S5_WIDTH), lambda k: (k, 0)),
            fixed(wb.shape), fixed(wc.shape),
            fixed((1, S5_CH)), fixed((1, S5_CH)),
            fixed((1, S5_WIDTH)), fixed(glu_w.shape), fixed((1, S5_WIDTH)),
        ],
        out_specs=pl.BlockSpec((ts, bsz * S5_WIDTH), lambda k: (k, 0)),
        out_shape=jax.ShapeDtypeStruct((seq, bsz * S5_WIDTH), BF16),
        scratch_shapes=[
            pltpu.VMEM((2 * S5_CH // LANE, blk, LANE), F32),
            pltpu.VMEM((2 * S5_CH // LANE, bsz, LANE), F32),
        ],
        compiler_params=pltpu.CompilerParams(
            dimension_semantics=("arbitrary",), vmem_limit_bytes=VMEM_LIMIT),
        name="s5",
    )(u_tb, wb, wc, a_re.reshape(1, S5_CH), a_im.reshape(1, S5_CH),
      d_skip.reshape(1, S5_WIDTH), glu_w, glu_b.reshape(1, S5_WIDTH))


def _s5_params(a_re, a_im, b_re, b_im, c_re, c_im, log_dt):
    delta = jnp.exp(log_dt)[:, None]
    mag = jnp.exp(a_re * delta)
    ab_re, ab_im = mag * jnp.cos(a_im * delta), mag * jnp.sin(a_im * delta)
    den = a_re * a_re + a_im * a_im
    p_re, p_im = ab_re - 1.0, ab_im
    f_re = (p_re * a_re + p_im * a_im) / den
    f_im = (p_im * a_re - p_re * a_im) / den
    bb_re = f_re[..., None] * b_re - f_im[..., None] * b_im
    bb_im = f_re[..., None] * b_im + f_im[..., None] * b_re
    gpb = LANE // S5_GROUP
    nb = S5_GROUPS // gpb
    eye = jnp.eye(gpb, dtype=F32)
    emb_b = lambda bb: jnp.einsum('mgnj,gh->mgjhn', bb.reshape(nb, gpb, S5_STATE, S5_GROUP),
                                  eye).reshape(nb, LANE, gpb * S5_STATE)
    wb = jnp.concatenate([emb_b(bb_re), emb_b(bb_im)], axis=2)
    emb_c = lambda cc: jnp.einsum('mgin,gh->mgnhi', cc.reshape(nb, gpb, S5_GROUP, S5_STATE),
                                  eye).reshape(nb, gpb * S5_STATE, LANE)
    wc = jnp.concatenate([emb_c(c_re), -emb_c(c_im)], axis=1)
    return ab_re.reshape(-1), ab_im.reshape(-1), wb.astype(BF16), wc.astype(BF16)


def _w_in_segments():
    bc = SSD_GROUPS * SSD_N
    g_z = GDN_QKV
    g_gate = g_z + GDN_QK
    s_z = g_gate + 2 * GDN_HEADS
    s_x = s_z + SSD_INNER
    s_b = s_x + SSD_INNER
    s_c = s_b + bc
    s_dt = s_c + bc
    s5_u = s_dt + SSD_HEADS
    return ((0, GDN_QKV, COL_GDN_QKV), (g_z, GDN_QK, COL_GDN_Z),
            (g_gate, 2 * GDN_HEADS, COL_GDN_GATE), (s_z, SSD_INNER, COL_SSD_Z),
            (s_x, SSD_INNER, COL_SSD_X), (s_b, bc, COL_SSD_B), (s_c, bc, COL_SSD_C),
            (s_dt, SSD_HEADS, COL_SSD_DT), (s5_u, S5_WIDTH, COL_S5_U))


def _pack_conv(gdn_w, ssd_w, ssd_b):
    bc = SSD_GROUPS * SSD_N
    pieces = ((COL_GDN_QKV, gdn_w, None),
              (COL_SSD_X, ssd_w[:, :SSD_INNER], ssd_b[:SSD_INNER]),
              (COL_SSD_B, ssd_w[:, SSD_INNER:SSD_INNER + bc], ssd_b[SSD_INNER:SSD_INNER + bc]),
              (COL_SSD_C, ssd_w[:, SSD_INNER + bc:], ssd_b[SSD_INNER + bc:]))
    cw = jnp.zeros((CONV_K, P_PAD), F32)
    cb = jnp.zeros((1, P_PAD), F32)
    for col, w, b in pieces:
        cw = cw.at[:, col:col + w.shape[1]].set(w.astype(F32))
        if b is not None:
            cb = cb.at[0, col:col + b.shape[0]].set(b.astype(F32))
    return cw, cb


def _pack_kernel(x_ref, o_ref):
    for col in (COL_SSD_DT, COL_GDN_GATE):
        o_ref[:, col:col + LANE] = jnp.zeros((o_ref.shape[0], LANE), o_ref.dtype)
    for src, width, dst in _w_in_segments():
        if width % LANE == 0:
            for r in range(0, width, 2 * LANE):
                w = min(2 * LANE, width - r)
                o_ref[:, dst + r:dst + r + w] = x_ref[src + r:src + r + w, :].T.astype(o_ref.dtype)
        else:
            blk = x_ref[src:src + LANE, :].T
            o_ref[:, dst:dst + width] = blk[:, :width].astype(o_ref.dtype)


def _pack_w_in(w_in, *, kb=256):
    nl, d, c = w_in.shape
    return pl.pallas_call(
        _pack_kernel,
        grid=(nl, d // kb),
        in_specs=[pl.BlockSpec((None, c, kb), lambda l, i: (l, 0, i))],
        out_specs=pl.BlockSpec((None, kb, P_PAD), lambda l, i: (l, i, 0)),
        out_shape=jax.ShapeDtypeStruct((nl, d, P_PAD), BF16),
        compiler_params=pltpu.CompilerParams(
            dimension_semantics=("parallel", "parallel"), vmem_limit_bytes=VMEM_LIMIT),
        name="pack_w_in",
    )(jnp.swapaxes(w_in, 1, 2))


def kernel(x, ffn1_norm, ffn1_w_gate, ffn1_w_up, ffn1_w_down, mix_norm, w_in,
           gdn_conv_w, gdn_a_log, gdn_dt_bias, gdn_norm,
           ssd_conv_w, ssd_conv_b, ssd_a_log, ssd_dt_bias, ssd_d, ssd_norm,
           s5_a_re, s5_a_im, s5_b_re, s5_b_im, s5_c_re, s5_c_im, s5_d, s5_log_dt,
           s5_glu_w, s5_glu_b, w_out, ffn2_norm, ffn2_w_gate, ffn2_w_up, ffn2_w_down,
           final_norm):
    bsz, seq, d = x.shape
    depth = w_in.shape[0]
    h = x
    ffn1_f32 = (ffn1_w_gate, ffn1_w_up, ffn1_w_down)
    ffn2_f32 = (ffn2_w_gate, ffn2_w_up, ffn2_w_down)
    w_next = tuple(_to_bf16(w, 0) for w in ffn1_f32)
    w_in_packed = _pack_w_in(w_in)
    for i in range(depth):
        h, w_next = _ffn(h, ffn1_norm[i], *w_next, final_norm, final_norm=False,
                         cast_next=(*ffn2_f32, i))
        conv_w, conv_b = _pack_conv(gdn_conv_w[i], ssd_conv_w[i], ssd_conv_b[i])
        proj, u_s5 = _inproj(h, mix_norm[i], w_in_packed, i, conv_w, conv_b, bsz, seq)
        o_gdn = _gdn(proj, bsz, seq, gdn_a_log[i], gdn_dt_bias[i], gdn_norm[i])
        o_ssd = _ssd(proj, bsz, seq, ssd_a_log[i], ssd_dt_bias[i], ssd_d[i], ssd_norm[i])
        ab_re, ab_im, wb, wc = _s5_params(s5_a_re[i], s5_a_im[i], s5_b_re[i], s5_b_im[i],
                                          s5_c_re[i], s5_c_im[i], s5_log_dt[i])
        o_s5 = _s5(u_s5, bsz, wb, wc, ab_re, ab_im, s5_d[i], s5_glu_w[i].astype(BF16),
                   s5_glu_b[i])
        wo = w_out[i].astype(BF16)
        h = _outproj(h, o_gdn, o_ssd, o_s5, wo[0:GDN_QK], wo[GDN_QK:GDN_QK + SSD_INNER],
                     wo[GDN_QK + SSD_INNER:])
        last = i == depth - 1
        if last:
            h = _ffn(h, ffn2_norm[i], *w_next, final_norm, final_norm=True,
                     out_batched=(bsz, seq))
        else:
            h, w_next = _ffn(h, ffn2_norm[i], *w_next, final_norm, final_norm=False,
                             cast_next=(*ffn1_f32, i + 1))
    return h
```
